```python
import jax, jax.numpy as jnp
from jax import lax
import numpy as np

D_MODEL = 1024
BATCH = 8
SEQ = 2048
DEPTH = 2
DEC_BATCH = 128
DEC_SEQ = 4
PAST_LEN = 16384
PAGE_SIZE = 128

BRANCH_W = D_MODEL // 2
N_BRANCH = 3
LRU_W = BRANCH_W
LRU_BLOCKS = 8
LRU_BW = LRU_W // LRU_BLOCKS
CONV_W = 4
LRU_C = 8.0
HG_HEADS = 4
HG_DK = BRANCH_W // HG_HEADS
HG_DV = BRANCH_W // HG_HEADS
HG_CHUNK = 64
HG_F_MIN = 1e-20
RW_HD = 64
RW_HEADS = BRANCH_W // RW_HD
RW_LORA_W = 64
RW_LORA_A = 64
RW_LORA_G = 128
RW_GN_EPS = 64e-5
RW_COLS = 3 * BRANCH_W + RW_LORA_W + RW_LORA_A + RW_LORA_G
D_FF = ((8 * D_MODEL // 3 + 255) // 256) * 256
PLE_DIM = 256
EPS = 1e-6
IN_SPLIT = (LRU_W, LRU_W, HG_HEADS * HG_DK, HG_HEADS * HG_DK, HG_HEADS * HG_DV, HG_HEADS * HG_DV, RW_COLS, N_BRANCH * D_MODEL)
IN_COLS = sum(IN_SPLIT)

kernel_name = 'hybrid_rglru_hgrn2_rwkv7_step'


def _f32(t):
    return t.astype(jnp.float32)


def _split(t, sizes):
    idx = [int(i) for i in np.cumsum(sizes)[:-1]]
    return jnp.split(t, idx, axis=-1)


def rmsnorm(x, g):
    xf = _f32(x)
    y = xf * lax.rsqrt(jnp.mean(xf * xf, axis=-1, keepdims=True) + EPS)
    return (y * _f32(g)).astype(x.dtype)


def rglru_branch(xa, ga, conv_state, h0, pos, conv_w, conv_b, wa, ba, wx, bx, lam):
    B, S, W = xa.shape
    xcat = jnp.concatenate([_f32(conv_state), _f32(xa)], axis=1)
    cw = _f32(conv_w)
    xc = _f32(conv_b) + xcat[:, 0:S] * cw[0]
    for j in range(1, CONV_W):
        xc = xc + xcat[:, j:j + S] * cw[j]
    new_conv = xcat[:, S:]
    xb = xc.reshape(B, S, LRU_BLOCKS, LRU_BW)
    r = jax.nn.sigmoid(jnp.einsum('bshi,hij->bshj', xb, _f32(wa)).reshape(B, S, W) + _f32(ba))
    i = jax.nn.sigmoid(jnp.einsum('bshi,hij->bshj', xb, _f32(wx)).reshape(B, S, W) + _f32(bx))
    log_a = -LRU_C * r * jax.nn.softplus(-_f32(lam))
    a = jnp.exp(log_a)
    mult = jnp.where((pos == 0)[None, :, None], 1.0, jnp.sqrt(jnp.maximum(-jnp.expm1(2.0 * log_a), 0.0)))
    b = xc * i * mult
    b = b.at[:, 0].add(a[:, 0] * _f32(h0))

    def comb(e, l):
        return (e[0] * l[0], l[0] * e[1] + l[1])

    _, h = lax.associative_scan(comb, (a, b), axis=1)
    y = h * jax.nn.gelu(_f32(ga), approximate=True)
    return y, new_conv, h[:, -1]


def hgrn2_branch(q, f_pre, v, g, s0, lb, norm_g):
    B, S, _ = q.shape
    C = HG_CHUNK if S % HG_CHUNK == 0 else S
    N = S // C
    q = jax.nn.silu(_f32(q))
    lb = _f32(lb)
    fp = _f32(f_pre)
    f = lb + (1.0 - lb) * jax.nn.sigmoid(fp)
    k = (1.0 - lb) * jax.nn.sigmoid(-fp)
    logf = jnp.log(jnp.maximum(f, HG_F_MIN))

    def chunks(t, d):
        return jnp.moveaxis(t.reshape(B, N, C, HG_HEADS, d), 1, 0)

    xs = (chunks(q, HG_DK), chunks(k, HG_DK), chunks(_f32(v), HG_DV), chunks(logf, HG_DK))
    causal = jnp.tril(jnp.ones((C, C), bool))

    def step(st, inp):
        qc, kc, vc, lc = inp
        bc = jnp.cumsum(lc, axis=1)
        diff = bc[:, :, None] - bc[:, None, :]
        dec = jnp.where(causal[None, :, :, None, None], jnp.exp(jnp.minimum(diff, 0.0)), 0.0)
        A = jnp.einsum('bthk,bshk,btshk->bhts', qc, kc, dec)
        o = jnp.einsum('bhts,bshv->bthv', A, vc) + jnp.einsum('bthk,bhkv->bthv', qc * jnp.exp(bc), st)
        btot = bc[:, -1]
        st = jnp.exp(btot)[..., None] * st + jnp.einsum('bshk,bshv->bhkv', kc * jnp.exp(btot[:, None] - bc), vc)
        return st, o

    s_new, o = lax.scan(step, _f32(s0), xs)
    o = jnp.moveaxis(o, 0, 1).reshape(B, S, HG_HEADS, HG_DV)
    o = o * lax.rsqrt(jnp.mean(o * o, axis=-1, keepdims=True) + EPS) * _f32(norm_g).reshape(HG_HEADS, HG_DV)
    y = o.reshape(B, S, HG_HEADS * HG_DV) * jax.nn.silu(_f32(g))
    return y, s_new


def rwkv7_branch(cblk, shift0, s0, mu, w0, w_up, a0, a_up, g_up, k_k, k_a, r_k, ln_g, ln_b):
    B, S, _ = cblk.shape
    c = _f32(cblk)
    prev = jnp.concatenate([_f32(shift0)[:, None], c[:, :-1]], axis=1)
    xm = c + (prev - c) * _f32(mu)
    new_shift = c[:, -1]
    r, k, v, wl, al, gl = _split(xm, (BRANCH_W, BRANCH_W, BRANCH_W, RW_LORA_W, RW_LORA_A, RW_LORA_G))
    w = -jax.nn.softplus(-(_f32(w0) + jnp.tanh(wl) @ _f32(w_up))) - 0.5
    decay = jnp.exp(-jnp.exp(w))
    a = jax.nn.sigmoid(_f32(a0) + al @ _f32(a_up))
    g = jax.nn.sigmoid(gl) @ _f32(g_up)

    def heads(t):
        return t.reshape(B, S, RW_HEADS, RW_HD)

    kk = heads(k * _f32(k_k))
    kk = kk / jnp.maximum(jnp.sqrt(jnp.sum(kk * kk, axis=-1, keepdims=True)), 1e-12)
    k = k * (1.0 + (a - 1.0) * _f32(k_a))
    rh, kh, vh, ah, dh = heads(r), heads(k), heads(v), heads(a), heads(decay)

    def step(st, inp):
        rt, wt, kt, vt, kkt, at = inp
        sk = jnp.einsum('bhvk,bhk->bhv', st, kkt)
        st = st * wt[:, :, None, :] - sk[..., None] * (kkt * at)[:, :, None, :] + vt[..., None] * kt[:, :, None, :]
        o = jnp.einsum('bhvk,bhk->bhv', st, rt)
        return st, o

    xs = tuple(jnp.moveaxis(t, 1, 0) for t in (rh, dh, kh, vh, kk, ah))
    s_new, o = lax.scan(step, _f32(s0), xs)
    o = jnp.moveaxis(o, 0, 1)
    mean = jnp.mean(o, axis=-1, keepdims=True)
    var = jnp.mean((o - mean) ** 2, axis=-1, keepdims=True)
    o = (o - mean) * lax.rsqrt(var + RW_GN_EPS) * _f32(ln_g).reshape(RW_HEADS, RW_HD) + _f32(ln_b).reshape(RW_HEADS, RW_HD)
    o = o + jnp.sum(rh * kh * _f32(r_k), axis=-1, keepdims=True) * vh
    y = o.reshape(B, S, BRANCH_W) * g
    return y, s_new, new_shift


def decoder_layer(x, p, st, pos, lb, L):
    conv0, lru0, hg0, rw0, sh0 = st
    B, S, _ = x.shape
    dt = x.dtype
    h = rmsnorm(x, L['norm_pre_mix'])
    proj = jnp.einsum('bsd,dc->bsc', h, L['w_in'])
    xa, ga, bq, bf, bi, bg, cblk, gts = _split(proj, IN_SPLIT)
    yA, nconv, nlru = rglru_branch(xa, ga, conv0, lru0, pos, L['conv_w'], L['conv_b'], L['lru_wa'], L['lru_ba'], L['lru_wx'], L['lru_bx'], L['lru_lambda'])
    yB, nhg = hgrn2_branch(bq, bf, bi, bg, hg0, lb, L['hg_norm_g'])
    yC, nrw, nsh = rwkv7_branch(cblk, sh0, rw0, L['rw_mu'], L['rw_w0'], L['rw_w_up'], L['rw_a0'], L['rw_a_up'], L['rw_g_up'], L['rw_k_k'], L['rw_k_a'], L['rw_r_k'], L['rw_ln_g'], L['rw_ln_b'])
    br = jnp.stack([yA, yB, yC], axis=2).astype(dt)
    up = jnp.einsum('bsnw,nwd->bsnd', br, L['w_branch'])
    gates = jax.nn.sigmoid(gts.reshape(B, S, N_BRANCH, D_MODEL))
    mix = jnp.einsum('bsd,de->bse', jnp.sum(gates * up, axis=2), L['w_out'])
    x = x + rmsnorm(mix, L['norm_post_mix'])
    h = rmsnorm(x, L['norm_pre_ffn'])
    ff = (jax.nn.silu(h @ L['w_ffn_gate']) * (h @ L['w_ffn_up'])) @ L['w_ffn_down']
    x = x + rmsnorm(ff, L['norm_post_ffn'])
    ple = (p @ L['w_ple']) * jax.nn.sigmoid(x @ L['w_ple_gate'])
    x = x + rmsnorm(ple, L['norm_ple'])
    return x, (nconv.astype(conv0.dtype), nlru.astype(lru0.dtype), nhg.astype(hg0.dtype), nrw.astype(rw0.dtype), nsh.astype(sh0.dtype))


def run_trunk(x, p, states, pos, lb_all, W):
    new = ([], [], [], [], [])
    for i in range(DEPTH):
        Li = {name: w[i] for name, w in W.items()}
        x, ns = decoder_layer(x, p[i], tuple(s[i] for s in states), pos, lb_all[i], Li)
        for lst, t in zip(new, ns):
            lst.append(t)
    return x, tuple(jnp.stack(l) for l in new)


def setup_inputs(seed: int = 0) -> dict:
    key = jax.random.key(seed)
    ks = iter(jax.random.split(key, 64))
    f32 = jnp.float32

    def nrm(shape, scale):
        return scale * jax.random.normal(next(ks), shape, f32)

    def gain(shape):
        return 1.0 + 0.05 * jax.random.normal(next(ks), shape, f32)

    def unif(shape, lo, hi):
        return jax.random.uniform(next(ks), shape, f32, lo, hi)

    u = unif((DEPTH, LRU_W), 0.9, 0.999)
    s = u ** (1.0 / LRU_C)
    lam = jnp.log(s) - jnp.log1p(-s)
    return {
        'x_prompt': nrm((BATCH, SEQ, D_MODEL), 1.0),
        'x_sample': nrm((DEC_BATCH, DEC_SEQ, D_MODEL), 1.0),
        'p_prompt': nrm((DEPTH, BATCH, SEQ, PLE_DIM), 1.0),
        'p_sample': nrm((DEPTH, DEC_BATCH, DEC_SEQ, PLE_DIM), 1.0),
        'state_conv_a': nrm((DEPTH, DEC_BATCH, CONV_W - 1, LRU_W), 1.0),
        'state_lru_a': nrm((DEPTH, DEC_BATCH, LRU_W), 0.5),
        'state_hgrn': nrm((DEPTH, DEC_BATCH, HG_HEADS, HG_DK, HG_DV), 0.5),
        'state_rwkv': nrm((DEPTH, DEC_BATCH, RW_HEADS, RW_HD, RW_HD), 0.3),
        'state_shift_c': nrm((DEPTH, DEC_BATCH, RW_COLS), 1.0),
        'norm_pre_mix': gain((DEPTH, D_MODEL)),
        'w_in': nrm((DEPTH, D_MODEL, IN_COLS), D_MODEL ** -0.5),
        'conv_w': nrm((DEPTH, CONV_W, LRU_W), CONV_W ** -0.5),
        'conv_b': nrm((DEPTH, LRU_W), 0.02),
        'lru_wa': nrm((DEPTH, LRU_BLOCKS, LRU_BW, LRU_BW), LRU_BW ** -0.5),
        'lru_ba': nrm((DEPTH, LRU_W), 0.02),
        'lru_wx': nrm((DEPTH, LRU_BLOCKS, LRU_BW, LRU_BW), LRU_BW ** -0.5),
        'lru_bx': nrm((DEPTH, LRU_W), 0.02),
        'lru_lambda': lam,
        'hg_lower_bounds': nrm((DEPTH, HG_HEADS * HG_DK), 1.0),
        'hg_norm_g': gain((DEPTH, HG_HEADS * HG_DV)),
        'rw_mu': unif((DEPTH, RW_COLS), 0.0, 1.0),
        'rw_w0': unif((DEPTH, BRANCH_W), -6.0, 1.0),
        'rw_w_up': nrm((DEPTH, RW_LORA_W, BRANCH_W), 0.5 * RW_LORA_W ** -0.5),
        'rw_a0': nrm((DEPTH, BRANCH_W), 0.5),
        'rw_a_up': nrm((DEPTH, RW_LORA_A, BRANCH_W), RW_LORA_A ** -0.5),
        'rw_g_up': nrm((DEPTH, RW_LORA_G, BRANCH_W), RW_LORA_G ** -0.5),
        'rw_k_k': 0.85 + nrm((DEPTH, BRANCH_W), 0.05),
        'rw_k_a': gain((DEPTH, BRANCH_W)),
        'rw_r_k': nrm((DEPTH, RW_HEADS, RW_HD), 0.1),
        'rw_ln_g': gain((DEPTH, BRANCH_W)),
        'rw_ln_b': nrm((DEPTH, BRANCH_W), 0.02),
        'w_branch': nrm((DEPTH, N_BRANCH, BRANCH_W, D_MODEL), BRANCH_W ** -0.5),
        'w_out': nrm((DEPTH, D_MODEL, D_MODEL), D_MODEL ** -0.5),
        'norm_post_mix': gain((DEPTH, D_MODEL)),
        'norm_pre_ffn': gain((DEPTH, D_MODEL)),
        'w_ffn_gate': nrm((DEPTH, D_MODEL, D_FF), D_MODEL ** -0.5),
        'w_ffn_up': nrm((DEPTH, D_MODEL, D_FF), D_MODEL ** -0.5),
        'w_ffn_down': nrm((DEPTH, D_FF, D_MODEL), D_FF ** -0.5),
        'norm_post_ffn': gain((DEPTH, D_MODEL)),
        'w_ple': nrm((DEPTH, PLE_DIM, D_MODEL), PLE_DIM ** -0.5),
        'w_ple_gate': nrm((DEPTH, D_MODEL, D_MODEL), D_MODEL ** -0.5),
        'norm_ple': gain((DEPTH, D_MODEL)),
    }


def reference(x_prompt, x_sample, p_prompt, p_sample, state_conv_a, state_lru_a, state_hgrn, state_rwkv, state_shift_c,
              norm_pre_mix, w_in, conv_w, conv_b, lru_wa, lru_ba, lru_wx, lru_bx, lru_lambda, hg_lower_bounds, hg_norm_g,
              rw_mu, rw_w0, rw_w_up, rw_a0, rw_a_up, rw_g_up, rw_k_k, rw_k_a, rw_r_k, rw_ln_g, rw_ln_b,
              w_branch, w_out, norm_post_mix, norm_pre_ffn, w_ffn_gate, w_ffn_up, w_ffn_down, norm_post_ffn,
              w_ple, w_ple_gate, norm_ple):
    W = dict(norm_pre_mix=norm_pre_mix, w_in=w_in, conv_w=conv_w, conv_b=conv_b, lru_wa=lru_wa, lru_ba=lru_ba,
             lru_wx=lru_wx, lru_bx=lru_bx, lru_lambda=lru_lambda, hg_norm_g=hg_norm_g, rw_mu=rw_mu, rw_w0=rw_w0,
             rw_w_up=rw_w_up, rw_a0=rw_a0, rw_a_up=rw_a_up, rw_g_up=rw_g_up, rw_k_k=rw_k_k, rw_k_a=rw_k_a,
             rw_r_k=rw_r_k, rw_ln_g=rw_ln_g, rw_ln_b=rw_ln_b, w_branch=w_branch, w_out=w_out,
             norm_post_mix=norm_post_mix, norm_pre_ffn=norm_pre_ffn, w_ffn_gate=w_ffn_gate, w_ffn_up=w_ffn_up,
             w_ffn_down=w_ffn_down, norm_post_ffn=norm_post_ffn, w_ple=w_ple, w_ple_gate=w_ple_gate, norm_ple=norm_ple)
    sm = jax.nn.softmax(_f32(hg_lower_bounds), axis=0)
    lb_all = jnp.cumsum(sm, axis=0) - sm[0]
    dt = x_prompt.dtype
    bp = x_prompt.shape[0]
    zero_states = (jnp.zeros((DEPTH, bp, CONV_W - 1, LRU_W), dt), jnp.zeros((DEPTH, bp, LRU_W), dt),
                   jnp.zeros((DEPTH, bp, HG_HEADS, HG_DK, HG_DV), dt), jnp.zeros((DEPTH, bp, RW_HEADS, RW_HD, RW_HD), dt),
                   jnp.zeros((DEPTH, bp, RW_COLS), dt))
    pos_p = jnp.arange(x_prompt.shape[1])
    pos_s = PAST_LEN + jnp.arange(x_sample.shape[1])
    y_prompt, (conv_p, lru_p, hgrn_p, rwkv_p, shift_p) = run_trunk(x_prompt, p_prompt, zero_states, pos_p, lb_all, W)
    y_sample, (conv_s, lru_s, hgrn_s, rwkv_s, shift_s) = run_trunk(
        x_sample, p_sample, (state_conv_a, state_lru_a, state_hgrn, state_rwkv, state_shift_c), pos_s, lb_all, W)
    return (y_prompt, y_sample, conv_p, lru_p, hgrn_p, rwkv_p, shift_p, conv_s, lru_s, hgrn_s, rwkv_s, shift_s)
```

```python
import functools
import math

import jax
import jax.numpy as jnp
from jax import lax
from jax.experimental import pallas as pl
from jax.experimental.pallas import tpu as pltpu

F32 = jnp.float32
BF16 = jnp.bfloat16

D_MODEL = 1024
BRANCH_W = 512
N_BRANCH = 3
LRU_BLOCKS = 8
LRU_BW = BRANCH_W // LRU_BLOCKS
CONV_W = 4
LRU_C = 8.0
HG_HEADS = 4
HG_D = BRANCH_W // HG_HEADS
HG_F_MIN = 1e-20
RW_HD = 64
RW_HEADS = BRANCH_W // RW_HD
RW_LORA_W = 64
RW_LORA_A = 64
RW_LORA_G = 128
RW_LORA = RW_LORA_W + RW_LORA_A + RW_LORA_G
RW_GN_EPS = 64e-5
RW_COLS = 3 * BRANCH_W + RW_LORA
PLE_DIM = 256
EPS = 1e-6

SUBLANES = 8
LANES = 128
MXU_DIM = 256
VMEM_LIMIT = 56 * 1024 * 1024

PROJ_COLS = 8192
OFF_GATES = 0
OFF_LRU = N_BRANCH * D_MODEL
OFF_HG = OFF_LRU + 2 * BRANCH_W
OFF_RW = OFF_HG + 4 * BRANCH_W
RW_BLOCK = PROJ_COLS - OFF_RW

RW_GROUP = MXU_DIM // RW_HD
RW_GW = RW_GROUP * RW_HD


def _params(sem):
    return pltpu.CompilerParams(dimension_semantics=sem, vmem_limit_bytes=VMEM_LIMIT)


def _rms(x, g):
    return x * lax.rsqrt(jnp.mean(x * x, axis=-1, keepdims=True) + EPS) * g


def _sigmoid(x):
    return 1.0 / (1.0 + jnp.exp(-x))


def _softplus(x):
    return jnp.maximum(x, 0.0) + jnp.log1p(jnp.exp(-jnp.abs(x)))


def _dot(a, b):
    return jnp.dot(a.astype(BF16), b.astype(BF16), preferred_element_type=F32)


def _dot_nt(a, b):
    return lax.dot_general(a.astype(BF16), b.astype(BF16), (((1,), (1,)), ((), ())),
                           preferred_element_type=F32)


def _dot_tn(a, b):
    return lax.dot_general(a.astype(BF16), b.astype(BF16), (((0,), (0,)), ((), ())),
                           preferred_element_type=F32)


def _split3(x):
    hi = x.astype(BF16)
    r1 = x - hi.astype(F32)
    mid = r1.astype(BF16)
    lo = (r1 - mid.astype(F32)).astype(BF16)
    return hi, mid, lo


def _dot_exact_lhs(m01, x):
    hi, mid, lo = _split3(x)
    m = m01.astype(BF16)
    out = jnp.dot(m, lo, preferred_element_type=F32)
    out = out + jnp.dot(m, mid, preferred_element_type=F32)
    return out + jnp.dot(m, hi, preferred_element_type=F32)


def _dot_exact_rhs(x, m01):
    hi, mid, lo = _split3(x)
    m = m01.astype(BF16)
    out = jnp.dot(lo, m, preferred_element_type=F32)
    out = out + jnp.dot(mid, m, preferred_element_type=F32)
    return out + jnp.dot(hi, m, preferred_element_type=F32)


def _tri_incl(n):
    i = lax.broadcasted_iota(jnp.int32, (n, n), 0)
    j = lax.broadcasted_iota(jnp.int32, (n, n), 1)
    return (i >= j).astype(F32)


def _seg_ones(width, seg):
    i = lax.broadcasted_iota(jnp.int32, (width, width), 0)
    j = lax.broadcasted_iota(jnp.int32, (width, width), 1)
    return (i // seg == j // seg).astype(F32)


def _in_proj_kernel(x_ref, g_ref, w_ref, o_ref, h_scr):
    @pl.when(pl.program_id(1) == 0)
    def _():
        h_scr[...] = _rms(x_ref[...], g_ref[...]).astype(BF16)

    o_ref[...] = jnp.dot(h_scr[...], w_ref[...], preferred_element_type=F32)


def _in_proj(x, g, w, tm, tn):
    t = x.shape[0]
    return pl.pallas_call(
        _in_proj_kernel,
        grid=(t // tm, PROJ_COLS // tn),
        in_specs=[
            pl.BlockSpec((tm, D_MODEL), lambda i, j: (i, 0)),
            pl.BlockSpec((1, D_MODEL), lambda i, j: (0, 0)),
            pl.BlockSpec((D_MODEL, tn), lambda i, j: (0, j)),
        ],
        out_specs=pl.BlockSpec((tm, tn), lambda i, j: (i, j)),
        out_shape=jax.ShapeDtypeStruct((t, PROJ_COLS), F32),
        scratch_shapes=[pltpu.VMEM((tm, D_MODEL), BF16)],
        compiler_params=_params(("parallel", "arbitrary")),
        name="in_proj",
    )(x, g, w)


def _lru_kernel(xa_ref, ga_ref, prev8_ref, h0_ref, cw_ref, cb_ref, w_ref, bab_ref, lam_ref,
                y_ref, hout_ref, prev_scr, h_scr, a_scr, b_scr, hh_scr,
                *, nb, tt_len, s_valid, pos0_is_zero):
    tt = pl.program_id(0)
    rows = nb * tt_len

    @pl.when(tt == 0)
    def _():
        prev_scr[...] = prev8_ref[...]
        h_scr[...] = h0_ref[...]

    xa = xa_ref[...]
    prev8 = prev_scr[...]
    t8 = lax.broadcasted_iota(jnp.int32, (1, SUBLANES, 1), 1)
    cw = cw_ref[...]
    xc = cb_ref[...][None] + cw[CONV_W - 1][None, None] * xa
    for j in range(1, CONV_W):
        rolled = pltpu.roll(xa, j, 1)
        head = jnp.where(t8 < j, pltpu.roll(prev8, j, 1), rolled[:, :SUBLANES])
        shifted = head if tt_len == SUBLANES else jnp.concatenate([head, rolled[:, SUBLANES:]], axis=1)
        xc = xc + cw[CONV_W - 1 - j][None, None] * shifted
    prev_scr[...] = xa[:, tt_len - SUBLANES:, :]

    xc2 = xc.reshape(rows, BRANCH_W)
    z = _dot(xc2, w_ref[...])
    bab = bab_ref[...]
    r = _sigmoid(z[:, :BRANCH_W] + bab[0:1])
    i = _sigmoid(z[:, BRANCH_W:] + bab[1:2])
    log_a = (-LRU_C) * r * _softplus(-lam_ref[...])
    a = jnp.exp(log_a)
    th = jnp.tanh(log_a)
    mult = jnp.sqrt(jnp.maximum(-2.0 * th / (1.0 - th), 0.0))
    t_in = lax.broadcasted_iota(jnp.int32, (rows, 1), 0) % tt_len
    if pos0_is_zero:
        mult = jnp.where(jnp.logical_and(tt == 0, t_in == 0), 1.0, mult)
    b = xc2 * i * mult
    valid = (tt * tt_len + t_in) < s_valid
    a = jnp.where(valid, a, 1.0)
    b = jnp.where(valid, b, 0.0)
    n_lt = BRANCH_W // LANES
    for l in range(n_lt):
        a_scr[l] = a[:, l * LANES:(l + 1) * LANES]
        b_scr[l] = b[:, l * LANES:(l + 1) * LANES]

    def step(t, hs):
        out = []
        for l in range(n_lt):
            h = a_scr[l, pl.ds(t, nb, stride=tt_len), :] * hs[l] + b_scr[l, pl.ds(t, nb, stride=tt_len), :]
            hh_scr[l, pl.ds(t, nb, stride=tt_len), :] = h
            out.append(h)
        return tuple(out)

    h0 = h_scr[...]
    hs = lax.fori_loop(0, tt_len, step, tuple(h0[:, l * LANES:(l + 1) * LANES] for l in range(n_lt)))
    h = jnp.concatenate(hs, axis=-1)
    h_scr[...] = h
    hout_ref[...] = h

    ga = ga_ref[...].reshape(rows, BRANCH_W)
    gelu = 0.5 * ga * (1.0 + jnp.tanh(math.sqrt(2.0 / math.pi) * (ga + 0.044715 * ga * ga * ga)))
    hh = jnp.concatenate([hh_scr[l] for l in range(n_lt)], axis=-1)
    y_ref[...] = (hh * gelu).reshape(nb, tt_len, BRANCH_W)


def _lru(proj3, prev8, h0, cw, cb, w, bab, lam, *, tt_len, s_valid, pos0_is_zero):
    nb, s_len, _ = proj3.shape
    blk = OFF_LRU // BRANCH_W
    kern = functools.partial(_lru_kernel, nb=nb, tt_len=tt_len, s_valid=s_valid, pos0_is_zero=pos0_is_zero)
    full2 = lambda t: (0, 0)
    return pl.pallas_call(
        kern,
        grid=(s_len // tt_len,),
        in_specs=[
            pl.BlockSpec((nb, tt_len, BRANCH_W), lambda t: (0, t, blk)),
            pl.BlockSpec((nb, tt_len, BRANCH_W), lambda t: (0, t, blk + 1)),
            pl.BlockSpec((nb, SUBLANES, BRANCH_W), lambda t: (0, 0, 0)),
            pl.BlockSpec((nb, BRANCH_W), full2),
            pl.BlockSpec((CONV_W, BRANCH_W), full2),
            pl.BlockSpec((1, BRANCH_W), full2),
            pl.BlockSpec((BRANCH_W, 2 * BRANCH_W), full2),
            pl.BlockSpec((2, BRANCH_W), full2),
            pl.BlockSpec((1, BRANCH_W), full2),
        ],
        out_specs=[
            pl.BlockSpec((nb, tt_len, BRANCH_W), lambda t: (0, t, 0)),
            pl.BlockSpec((nb, BRANCH_W), full2),
        ],
        out_shape=[
            jax.ShapeDtypeStruct((nb, s_len, BRANCH_W), F32),
            jax.ShapeDtypeStruct((nb, BRANCH_W), F32),
        ],
        scratch_shapes=[
            pltpu.VMEM((nb, SUBLANES, BRANCH_W), F32),
            pltpu.VMEM((nb, BRANCH_W), F32),
        ] + [pltpu.VMEM((BRANCH_W // LANES, nb * tt_len, LANES), F32)] * 3,
        compiler_params=_params(("arbitrary",)),
        name="rglru",
    )(proj3, proj3, prev8, h0, cw, cb, w, bab, lam)


def _hg_diag_blocks(qh, kh, vh, bch):
    c = qh.shape[0]
    nblk = c // SUBLANES
    q3 = qh.reshape(nblk, SUBLANES, HG_D)
    k3 = kh.reshape(nblk, SUBLANES, HG_D)
    v3 = vh.reshape(nblk, SUBLANES, HG_D)
    b3 = bch.reshape(nblk, SUBLANES, HG_D)
    tin = lax.broadcasted_iota(jnp.int32, (1, SUBLANES, 1), 1)
    o3 = jnp.zeros((nblk, SUBLANES, HG_D), F32)
    for s in range(SUBLANES):
        dec = jnp.exp(jnp.minimum(b3 - b3[:, s:s + 1, :], 0.0))
        w = jnp.sum(q3 * k3[:, s:s + 1, :] * dec, axis=-1, keepdims=True)
        w = jnp.where(tin >= s, w, 0.0)
        o3 = o3 + w * v3[:, s:s + 1, :]
    return o3.reshape(c, HG_D)


def _hg_level_refs(bch, h, c):
    gq, gk = [], []
    zero = jnp.zeros((h, HG_D), F32)
    for j in range(c // h):
        if j % 2 == 1:
            gq.append(jnp.broadcast_to(bch[j * h - 1:j * h, :], (h, HG_D)))
            gk.append(zero)
        else:
            gq.append(zero)
            gk.append(jnp.broadcast_to(bch[(j + 1) * h - 1:(j + 1) * h, :], (h, HG_D)))
    return jnp.concatenate(gq, axis=0), jnp.concatenate(gk, axis=0)


def _hgrn_kernel(q_ref, f_ref, v_ref, g_ref, s0_ref, lbraw_ref, ng_ref, y_ref, sout_ref,
                 st_scr, qs_scr, k_scr, lf_scr, o_scr, *, nbb, tt_len, chunk, s_valid, layer):
    tt = pl.program_id(1)
    n_t = pl.num_programs(1)
    c = chunk

    raw = lbraw_ref[...]
    ex = jnp.exp(raw - jnp.max(raw, axis=0, keepdims=True))
    sm = ex / jnp.sum(ex, axis=0, keepdims=True)
    lb = jnp.zeros((1, BRANCH_W), F32)
    for l in range(1, layer + 1):
        lb = lb + sm[l:l + 1]

    tri = _tri_incl(c)
    ti = lax.broadcasted_iota(jnp.int32, (c, 1), 0)
    ii = lax.broadcasted_iota(jnp.int32, (c, c), 0)
    jj = lax.broadcasted_iota(jnp.int32, (c, c), 1)
    levels = []
    h = c // 2
    while h >= SUBLANES:
        odd = (ti // h) % 2 == 1
        pair = jnp.logical_and(ii // (2 * h) == jj // (2 * h),
                               jnp.logical_and((ii // h) % 2 == 1, (jj // h) % 2 == 0))
        levels.append((h, odd, pair))
        h //= 2
    t_rows = lax.broadcasted_iota(jnp.int32, (tt_len, 1), 0)
    valid = (tt * tt_len + t_rows) < s_valid

    def per_batch(bb, carry):
        @pl.when(tt == 0)
        def _():
            for hd in range(HG_HEADS):
                st_scr[bb, hd] = s0_ref[bb, hd].T

        q = q_ref[bb]
        fp = f_ref[bb]
        qs_scr[...] = q * _sigmoid(q)
        sg = _sigmoid(fp)
        f = lb + (1.0 - lb) * sg
        k_scr[...] = jnp.where(valid, (1.0 - lb) * _sigmoid(-fp), 0.0)
        lf_scr[...] = jnp.where(valid, jnp.log(jnp.maximum(f, HG_F_MIN)), 0.0)

        def per_chunk(ci, carry2):
            r0 = pl.multiple_of(ci * c, c)
            bc = _dot_exact_lhs(tri, lf_scr[pl.ds(r0, c), :])
            for hd in range(HG_HEADS):
                ln = slice(hd * HG_D, (hd + 1) * HG_D)
                qh = qs_scr[pl.ds(r0, c), ln]
                kh = k_scr[pl.ds(r0, c), ln]
                vh = v_ref[bb, pl.ds(r0, c), ln]
                bch = bc[:, ln]
                st = st_scr[bb, hd]
                o = _dot_nt(qh * jnp.exp(bch), st)
                if levels:
                    amat = jnp.zeros((c, c), F32)
                    for (h, odd, pair) in levels:
                        gq, gk = _hg_level_refs(bch, h, c)
                        qt = jnp.where(odd, qh * jnp.exp(jnp.where(odd, bch - gq, 0.0)), 0.0)
                        kt = jnp.where(odd, 0.0, kh * jnp.exp(jnp.where(odd, 0.0, gk - bch)))
                        amat = amat + jnp.where(pair, _dot_nt(qt, kt), 0.0)
                    o = o + _dot(amat, vh)
                o = o + _hg_diag_blocks(qh, kh, vh, bch)
                btot = bch[c - 1:c, :]
                khat = kh * jnp.exp(btot - bch)
                st_scr[bb, hd] = st * jnp.exp(btot) + _dot_tn(vh, khat)
                o_scr[pl.ds(r0, c), ln] = o
            return carry2

        lax.fori_loop(0, tt_len // c, per_chunk, 0)

        g = g_ref[bb]
        ng = ng_ref[...]
        outs = []
        for hd in range(HG_HEADS):
            ln = slice(hd * HG_D, (hd + 1) * HG_D)
            o = o_scr[:, ln]
            outs.append(o * lax.rsqrt(jnp.mean(o * o, axis=-1, keepdims=True) + EPS) * ng[:, ln])
        y_ref[bb] = jnp.concatenate(outs, axis=-1) * (g * _sigmoid(g))

        @pl.when(tt == n_t - 1)
        def _():
            for hd in range(HG_HEADS):
                sout_ref[bb, hd] = st_scr[bb, hd].T
        return carry

    if nbb == 1:
        per_batch(0, 0)
    else:
        lax.fori_loop(0, nbb, per_batch, 0)


def _hgrn(proj3, s0, lbraw, ng, *, nbb, tt_len, chunk, s_valid, layer):
    nb, s_len, _ = proj3.shape
    blk = OFF_HG // BRANCH_W
    kern = functools.partial(_hgrn_kernel, nbb=nbb, tt_len=tt_len, chunk=chunk, s_valid=s_valid, layer=layer)
    seq = lambda k: pl.BlockSpec((nbb, tt_len, BRANCH_W), lambda b, t, k=k: (b, t, blk + k))
    st_spec = pl.BlockSpec((nbb, HG_HEADS, HG_D, HG_D), lambda b, t: (b, 0, 0, 0))
    return pl.pallas_call(
        kern,
        grid=(nb // nbb, s_len // tt_len),
        in_specs=[seq(0), seq(1), seq(2), seq(3), st_spec,
                  pl.BlockSpec(lbraw.shape, lambda b, t: (0, 0)),
                  pl.BlockSpec((1, BRANCH_W), lambda b, t: (0, 0))],
        out_specs=[pl.BlockSpec((nbb, tt_len, BRANCH_W), lambda b, t: (b, t, 0)), st_spec],
        out_shape=[jax.ShapeDtypeStruct((nb, s_len, BRANCH_W), F32),
                   jax.ShapeDtypeStruct(s0.shape, F32)],
        scratch_shapes=[pltpu.VMEM((nbb, HG_HEADS, HG_D, HG_D), F32)]
        + [pltpu.VMEM((tt_len, BRANCH_W), F32)] * 4,
        compiler_params=_params(("parallel", "arbitrary")),
        name="hgrn2",
    )(proj3, proj3, proj3, proj3, s0, lbraw, ng)


def _rwkv_kernel(c_ref, sh0_ref, s0_ref, mu_ref, wl_ref, vec_ref, y_ref, sout_ref,
                 sbd_scr, carry_scr, r_scr, kb_scr, v_scr, kap_scr, b_scr, ld_scr, o_scr,
                 *, nbb, tt_len, chunk, s_valid):
    tt = pl.program_id(1)
    n_t = pl.num_programs(1)
    c = chunk
    c4 = RW_GROUP * c
    n_groups = RW_HEADS // RW_GROUP

    vec = vec_ref[...]
    w0, a0, k_k, k_a, r_k, ln_g, ln_b = [vec[i:i + 1] for i in range(7)]
    seg = _seg_ones(BRANCH_W, RW_HD)
    tri = _tri_incl(c)
    t_rows = lax.broadcasted_iota(jnp.int32, (tt_len, 1), 0)
    valid = (tt * tt_len + t_rows) < s_valid
    lane_l = lax.broadcasted_iota(jnp.int32, (1, RW_LORA), 1)

    si = lax.broadcasted_iota(jnp.int32, (c4, RW_GW), 0)
    sj = lax.broadcasted_iota(jnp.int32, (c4, RW_GW), 1)
    head_rows = si // c == sj // RW_HD
    qi = lax.broadcasted_iota(jnp.int32, (c4, c4), 0)
    qj = lax.broadcasted_iota(jnp.int32, (c4, c4), 1)
    same = qi // c == qj // c
    strict = jnp.logical_and(same, qi % c > qj % c)
    incl = jnp.logical_and(same, qi % c >= qj % c)
    eye = (qi == qj).astype(F32)
    gi_ = lax.broadcasted_iota(jnp.int32, (RW_GW, RW_GW), 0)
    gj_ = lax.broadcasted_iota(jnp.int32, (RW_GW, RW_GW), 1)
    bd_state = gi_ // RW_HD == gj_ // RW_HD

    def stack(x):
        return jnp.where(head_rows, jnp.concatenate([x] * RW_GROUP, axis=0), 0.0)

    def unstack(x):
        out = x[0:c]
        for h in range(1, RW_GROUP):
            out = out + x[h * c:(h + 1) * c]
        return out

    def per_batch(bb, carry):
        @pl.when(tt == 0)
        def _():
            carry_scr[bb] = sh0_ref[bb]
            for g in range(n_groups):
                rows = jnp.concatenate([s0_ref[bb, g * RW_GROUP + h] for h in range(RW_GROUP)], axis=0)
                sbd_scr[bb, g] = jnp.where(bd_state, jnp.concatenate([rows] * RW_GROUP, axis=1), 0.0)

        cc = c_ref[bb][:, :RW_COLS]
        prev = jnp.where(t_rows == 0, carry_scr[bb], pltpu.roll(cc, 1, 0))
        carry_scr[bb] = cc[tt_len - 1:tt_len, :]
        xm = cc + (prev - cc) * mu_ref[...]
        r = xm[:, 0:BRANCH_W]
        k = xm[:, BRANCH_W:2 * BRANCH_W]
        v = xm[:, 2 * BRANCH_W:3 * BRANCH_W]
        lo = xm[:, 3 * BRANCH_W:]
        act = jnp.where(lane_l < RW_LORA_W, jnp.tanh(lo),
                        jnp.where(lane_l < RW_LORA_W + RW_LORA_A, lo, _sigmoid(lo)))
        z = _dot(act, wl_ref[...])
        w = -_softplus(-(w0 + z[:, 0:BRANCH_W])) - 0.5
        ld = -jnp.exp(w)
        a = _sigmoid(a0 + z[:, BRANCH_W:2 * BRANCH_W])
        gate = z[:, 2 * BRANCH_W:]
        kk = k * k_k
        nrm = jnp.sqrt(_dot_exact_rhs(kk * kk, seg))
        kap = kk / jnp.maximum(nrm, 1e-12)
        kbar = k * (1.0 + (a - 1.0) * k_a)
        bonus = _dot_exact_rhs(r * kbar * r_k, seg)
        r_scr[...] = r
        v_scr[...] = v
        kb_scr[...] = jnp.where(valid, kbar, 0.0)
        kap_scr[...] = jnp.where(valid, kap, 0.0)
        b_scr[...] = jnp.where(valid, kap * a, 0.0)
        ld_scr[...] = jnp.where(valid, ld, 0.0)

        def per_chunk(ci, carry2):
            r0 = pl.multiple_of(ci * c, c)
            for g in range(n_groups):
                ln = slice(g * RW_GW, (g + 1) * RW_GW)
                ldc = ld_scr[pl.ds(r0, c), ln]
                lw = _dot_exact_lhs(tri, ldc)
                e_in = jnp.exp(lw)
                e_neg = jnp.exp(-lw)
                pc = e_in[c - 1:c, :]
                at = -kap_scr[pl.ds(r0, c), ln] * jnp.exp(lw - ldc)
                rt = r_scr[pl.ds(r0, c), ln] * e_in
                bt = b_scr[pl.ds(r0, c), ln] * e_neg
                kt = kb_scr[pl.ds(r0, c), ln] * e_neg
                vc = v_scr[pl.ds(r0, c), ln]
                sbd = sbd_scr[bb, g]

                lhs = jnp.concatenate([stack(at), stack(rt)], axis=0)
                rhs = jnp.concatenate([bt] * RW_GROUP + [kt] * RW_GROUP, axis=0)
                quad = _dot_nt(lhs, rhs)
                nmat = jnp.where(strict, quad[:c4, :c4], 0.0)
                aak = jnp.where(strict, quad[:c4, c4:], 0.0)
                arb = jnp.where(incl, quad[c4:, :c4], 0.0)
                ark = jnp.where(incl, quad[c4:, c4:], 0.0)

                v_bd = stack(vc)
                wmat = _dot_nt(at, sbd) + _dot(unstack(aak), v_bd)
                tinv = eye + nmat
                npow = nmat
                span = 1
                while 2 * span < c:
                    npow = _dot(npow, npow)
                    span *= 2
                    tinv = tinv + _dot(npow, tinv)
                u = _dot(unstack(tinv), stack(wmat))
                o = _dot_nt(rt, sbd) + _dot(unstack(arb), stack(u)) + _dot(unstack(ark), v_bd)
                upd = _dot_tn(u, bt * pc) + _dot_tn(vc, kt * pc)
                sbd_scr[bb, g] = sbd * pc + jnp.where(bd_state, upd, 0.0)
                o_scr[pl.ds(r0, c), ln] = o
            return carry2

        lax.fori_loop(0, tt_len // c, per_chunk, 0)

        o = o_scr[...]
        inv_n = 1.0 / RW_HD
        mean = _dot_exact_rhs(o, seg) * inv_n
        cen = o - mean
        var = _dot_exact_rhs(cen * cen, seg) * inv_n
        on = cen * lax.rsqrt(var + RW_GN_EPS) * ln_g + ln_b
        y_ref[bb] = (on + bonus * v) * gate

        @pl.when(tt == n_t - 1)
        def _():
            for g in range(n_groups):
                sbd = sbd_scr[bb, g]
                for h in range(RW_GROUP):
                    sout_ref[bb, g * RW_GROUP + h] = sbd[h * RW_HD:(h + 1) * RW_HD, h * RW_HD:(h + 1) * RW_HD]
        return carry

    if nbb == 1:
        per_batch(0, 0)
    else:
        lax.fori_loop(0, nbb, per_batch, 0)


def _rwkv(proj3, sh0, s0, mu, wl, vec, *, nbb, tt_len, chunk, s_valid):
    nb, s_len, _ = proj3.shape
    kern = functools.partial(_rwkv_kernel, nbb=nbb, tt_len=tt_len, chunk=chunk, s_valid=s_valid)
    st_spec = pl.BlockSpec((nbb, RW_HEADS, RW_HD, RW_HD), lambda b, t: (b, 0, 0, 0))
    full2 = lambda b, t: (0, 0)
    return pl.pallas_call(
        kern,
        grid=(nb // nbb, s_len // tt_len),
        in_specs=[
            pl.BlockSpec((nbb, tt_len, RW_BLOCK), lambda b, t: (b, t, OFF_RW // RW_BLOCK)),
            pl.BlockSpec((nbb, 1, RW_COLS), lambda b, t: (b, 0, 0)),
            st_spec,
            pl.BlockSpec((1, RW_COLS), full2),
            pl.BlockSpec((RW_LORA, 3 * BRANCH_W), full2),
            pl.BlockSpec((SUBLANES, BRANCH_W), full2),
        ],
        out_specs=[pl.BlockSpec((nbb, tt_len, BRANCH_W), lambda b, t: (b, t, 0)), st_spec],
        out_shape=[jax.ShapeDtypeStruct((nb, s_len, BRANCH_W), F32),
                   jax.ShapeDtypeStruct(s0.shape, F32)],
        scratch_shapes=[
            pltpu.VMEM((nbb, RW_HEADS // RW_GROUP, RW_GW, RW_GW), F32),
            pltpu.VMEM((nbb, 1, RW_COLS), F32),
        ] + [pltpu.VMEM((tt_len, BRANCH_W), F32)] * 7,
        compiler_params=_params(("parallel", "arbitrary")),
        name="rwkv7",
    )(proj3, sh0, s0, mu, wl, vec)


def _mix_kernel(x_ref, gts_ref, ya_ref, yb_ref, yc_ref, wb_ref, wo_ref, g_ref, o_ref):
    acc = None
    for n, y_ref in enumerate((ya_ref, yb_ref, yc_ref)):
        up = _dot(y_ref[...], wb_ref[n])
        term = _sigmoid(gts_ref[:, n * D_MODEL:(n + 1) * D_MODEL]) * up
        acc = term if acc is None else acc + term
    mix = _dot(acc, wo_ref[...])
    o_ref[...] = x_ref[...] + _rms(mix, g_ref[...])


def _mix(x, proj, ya, yb, yc, wb, wo, g, tm):
    t = x.shape[0]
    row = lambda w: pl.BlockSpec((tm, w), lambda i: (i, 0))
    return pl.pallas_call(
        _mix_kernel,
        grid=(t // tm,),
        in_specs=[row(D_MODEL), row(N_BRANCH * D_MODEL), row(BRANCH_W), row(BRANCH_W), row(BRANCH_W),
                  pl.BlockSpec((N_BRANCH, BRANCH_W, D_MODEL), lambda i: (0, 0, 0)),
                  pl.BlockSpec((D_MODEL, D_MODEL), lambda i: (0, 0)),
                  pl.BlockSpec((1, D_MODEL), lambda i: (0, 0))],
        out_specs=row(D_MODEL),
        out_shape=jax.ShapeDtypeStruct((t, D_MODEL), F32),
        compiler_params=_params(("parallel",)),
        name="branch_mix",
    )(x, proj, ya, yb, yc, wb, wo, g)


def _ffn_kernel(x_ref, p_ref, gpre_ref, wg_ref, wu_ref, wd_ref, gpost_ref, wple_ref, wpg_ref, gple_ref,
                o_ref, h_scr, acc_scr):
    j = pl.program_id(1)

    @pl.when(j == 0)
    def _():
        h_scr[...] = _rms(x_ref[...], gpre_ref[...]).astype(BF16)
        acc_scr[...] = jnp.zeros_like(acc_scr)

    h = h_scr[...]
    gt = jnp.dot(h, wg_ref[...], preferred_element_type=F32)
    up = jnp.dot(h, wu_ref[...], preferred_element_type=F32)
    acc_scr[...] += _dot(gt * _sigmoid(gt) * up, wd_ref[...])

    @pl.when(j == pl.num_programs(1) - 1)
    def _():
        x2 = x_ref[...] + _rms(acc_scr[...], gpost_ref[...])
        ple = _dot(p_ref[...], wple_ref[...]) * _sigmoid(_dot(x2, wpg_ref[...]))
        o_ref[...] = x2 + _rms(ple, gple_ref[...])


def _ffn(x, p, gpre, wg, wu, wd, gpost, wple, wpg, gple, tm, n_split):
    t = x.shape[0]
    d_ff = wg.shape[1]
    tf = d_ff // n_split
    vecspec = pl.BlockSpec((1, D_MODEL), lambda i, j: (0, 0))
    return pl.pallas_call(
        _ffn_kernel,
        grid=(t // tm, n_split),
        in_specs=[
            pl.BlockSpec((tm, D_MODEL), lambda i, j: (i, 0)),
            pl.BlockSpec((tm, PLE_DIM), lambda i, j: (i, 0)),
            vecspec,
            pl.BlockSpec((D_MODEL, tf), lambda i, j: (0, j)),
            pl.BlockSpec((D_MODEL, tf), lambda i, j: (0, j)),
            pl.BlockSpec((tf, D_MODEL), lambda i, j: (j, 0)),
            vecspec,
            pl.BlockSpec((PLE_DIM, D_MODEL), lambda i, j: (0, 0)),
            pl.BlockSpec((D_MODEL, D_MODEL), lambda i, j: (0, 0)),
            vecspec,
        ],
        out_specs=pl.BlockSpec((tm, D_MODEL), lambda i, j: (i, 0)),
        out_shape=jax.ShapeDtypeStruct((t, D_MODEL), F32),
        scratch_shapes=[pltpu.VMEM((tm, D_MODEL), BF16), pltpu.VMEM((tm, D_MODEL), F32)],
        compiler_params=_params(("parallel", "arbitrary")),
        name="ffn_ple",
    )(x, p, gpre, wg, wu, wd, gpost, wple, wpg, gple)


def _block_diag(w):
    n, r, c = w.shape
    eye = jnp.eye(n, dtype=w.dtype)
    return (eye[:, None, :, None] * w[:, :, None, :]).reshape(n * r, n * c)


def _prep_layer(i, W):
    win = W["w_in"][i]
    segs = jnp.split(win, [2 * BRANCH_W, 2 * BRANCH_W + 4 * BRANCH_W, 2 * BRANCH_W + 4 * BRANCH_W + RW_COLS], axis=1)
    w_lru, w_hg, w_rw, w_gates = segs
    pad = jnp.zeros((D_MODEL, PROJ_COLS - OFF_RW - RW_COLS), win.dtype)
    w_in = jnp.concatenate([w_gates, w_lru, w_hg, w_rw, pad], axis=1).astype(BF16)
    lora = jnp.zeros((RW_LORA, 3 * BRANCH_W), F32)
    lora = lora.at[0:RW_LORA_W, 0:BRANCH_W].set(W["rw_w_up"][i])
    lora = lora.at[RW_LORA_W:RW_LORA_W + RW_LORA_A, BRANCH_W:2 * BRANCH_W].set(W["rw_a_up"][i])
    lora = lora.at[RW_LORA_W + RW_LORA_A:, 2 * BRANCH_W:].set(W["rw_g_up"][i])
    vec = jnp.stack([W["rw_w0"][i], W["rw_a0"][i], W["rw_k_k"][i], W["rw_k_a"][i],
                     W["rw_r_k"][i].reshape(BRANCH_W), W["rw_ln_g"][i], W["rw_ln_b"][i],
                     jnp.zeros((BRANCH_W,), F32)])
    row = lambda name: W[name][i].reshape(1, -1)
    return dict(
        w_in=w_in, norm_pre_mix=row("norm_pre_mix"),
        conv_w=W["conv_w"][i], conv_b=row("conv_b"),
        lru_w=jnp.concatenate([_block_diag(W["lru_wa"][i]), _block_diag(W["lru_wx"][i])], axis=1).astype(BF16),
        lru_b=jnp.stack([W["lru_ba"][i], W["lru_bx"][i]]), lru_lambda=row("lru_lambda"),
        hg_norm_g=row("hg_norm_g"),
        rw_mu=row("rw_mu"), rw_lora=lora.astype(BF16), rw_vec=vec,
        w_branch=W["w_branch"][i].astype(BF16), w_out=W["w_out"][i].astype(BF16),
        norm_post_mix=row("norm_post_mix"), norm_pre_ffn=row("norm_pre_ffn"),
        w_ffn_gate=W["w_ffn_gate"][i].astype(BF16), w_ffn_up=W["w_ffn_up"][i].astype(BF16),
        w_ffn_down=W["w_ffn_down"][i].astype(BF16), norm_post_ffn=row("norm_post_ffn"),
        w_ple=W["w_ple"][i].astype(BF16), w_ple_gate=W["w_ple_gate"][i].astype(BF16), norm_ple=row("norm_ple"),
    )


def _tiles(nb, s_len):
    t = nb * s_len
    tm_in = min(t, 1024)
    tm_tok = min(t, 512)
    if s_len >= 512:
        return dict(tm_in=tm_in, tm_tok=tm_tok, lru_tt=128, nbb=1, mix_tt=512, chunk=64)
    return dict(tm_in=tm_in, tm_tok=tm_tok, lru_tt=s_len, nbb=min(nb, 16), mix_tt=s_len, chunk=s_len)


def _run_trunk(x3, p4, states, layers, lbraw, *, s_valid, pos0_is_zero):
    nb, s_len, _ = x3.shape
    t = nb * s_len
    plan = _tiles(nb, s_len)
    conv0, lru0, hg0, rw0, sh0 = states
    x = x3.reshape(t, D_MODEL)
    new = ([], [], [], [], [])
    for i, L in enumerate(layers):
        proj = _in_proj(x, L["norm_pre_mix"], L["w_in"], plan["tm_in"], 1024)
        proj3 = proj.reshape(nb, s_len, PROJ_COLS)
        prev8 = jnp.pad(conv0[i], ((0, 0), (SUBLANES - (CONV_W - 1), 0), (0, 0)))
        ya, nlru = _lru(proj3, prev8, lru0[i], L["conv_w"], L["conv_b"], L["lru_w"], L["lru_b"], L["lru_lambda"],
                        tt_len=plan["lru_tt"], s_valid=s_valid, pos0_is_zero=pos0_is_zero)
        yb, nhg = _hgrn(proj3, hg0[i], lbraw, L["hg_norm_g"], nbb=plan["nbb"], tt_len=plan["mix_tt"],
                        chunk=plan["chunk"], s_valid=s_valid, layer=i)
        yc, nrw = _rwkv(proj3, sh0[i].reshape(nb, 1, RW_COLS), rw0[i], L["rw_mu"], L["rw_lora"], L["rw_vec"],
                        nbb=plan["nbb"], tt_len=plan["mix_tt"], chunk=plan["chunk"], s_valid=s_valid)
        x = _mix(x, proj, ya.reshape(t, BRANCH_W), yb.reshape(t, BRANCH_W), yc.reshape(t, BRANCH_W),
                 L["w_branch"], L["w_out"], L["norm_post_mix"], plan["tm_tok"])
        x = _ffn(x, p4[i].reshape(t, PLE_DIM), L["norm_pre_ffn"], L["w_ffn_gate"], L["w_ffn_up"], L["w_ffn_down"],
                 L["norm_post_ffn"], L["w_ple"], L["w_ple_gate"], L["norm_ple"], plan["tm_tok"], 2)
        nconv = proj3[:, s_valid - (CONV_W - 1):s_valid, OFF_LRU:OFF_LRU + BRANCH_W]
        nsh = proj3[:, s_valid - 1, OFF_RW:OFF_RW + RW_COLS]
        for lst, val in zip(new, (nconv, nlru, nhg, nrw, nsh)):
            lst.append(val)
    return x.reshape(nb, s_len, D_MODEL), tuple(jnp.stack(l) for l in new)


def kernel(x_prompt, x_sample, p_prompt, p_sample, state_conv_a, state_lru_a, state_hgrn, state_rwkv, state_shift_c, norm_pre_mix, w_in, conv_w, conv_b, lru_wa, lru_ba, lru_wx, lru_bx, lru_lambda, hg_lower_bounds, hg_norm_g, rw_mu, rw_w0, rw_w_up, rw_a0, rw_a_up, rw_g_up, rw_k_k, rw_k_a, rw_r_k, rw_ln_g, rw_ln_b, w_branch, w_out, norm_post_mix, norm_pre_ffn, w_ffn_gate, w_ffn_up, w_ffn_down, norm_post_ffn, w_ple, w_ple_gate, norm_ple):
    W = dict(norm_pre_mix=norm_pre_mix, w_in=w_in, conv_w=conv_w, conv_b=conv_b, lru_wa=lru_wa, lru_ba=lru_ba,
             lru_wx=lru_wx, lru_bx=lru_bx, lru_lambda=lru_lambda, hg_norm_g=hg_norm_g, rw_mu=rw_mu, rw_w0=rw_w0,
             rw_w_up=rw_w_up, rw_a0=rw_a0, rw_a_up=rw_a_up, rw_g_up=rw_g_up, rw_k_k=rw_k_k, rw_k_a=rw_k_a,
             rw_r_k=rw_r_k, rw_ln_g=rw_ln_g, rw_ln_b=rw_ln_b, w_branch=w_branch, w_out=w_out,
             norm_post_mix=norm_post_mix, norm_pre_ffn=norm_pre_ffn, w_ffn_gate=w_ffn_gate, w_ffn_up=w_ffn_up,
             w_ffn_down=w_ffn_down, norm_post_ffn=norm_post_ffn, w_ple=w_ple, w_ple_gate=w_ple_gate, norm_ple=norm_ple)
    depth = w_in.shape[0]
    layers = [_prep_layer(i, W) for i in range(depth)]
    lbraw = hg_lower_bounds.astype(F32)

    bp, sp, _ = x_prompt.shape
    zeros = lambda *shape: jnp.zeros((depth, bp) + shape, F32)
    zero_states = (zeros(CONV_W - 1, BRANCH_W), zeros(BRANCH_W), zeros(HG_HEADS, HG_D, HG_D),
                   zeros(RW_HEADS, RW_HD, RW_HD), zeros(RW_COLS))
    y_prompt, st_p = _run_trunk(x_prompt, p_prompt, zero_states, layers, lbraw, s_valid=sp, pos0_is_zero=True)

    bs, ss, _ = x_sample.shape
    ss_pad = -(-ss // SUBLANES) * SUBLANES
    xs = jnp.pad(x_sample, ((0, 0), (0, ss_pad - ss), (0, 0)))
    ps = jnp.pad(p_sample, ((0, 0), (0, 0), (0, ss_pad - ss), (0, 0)))
    y_sample, st_s = _run_trunk(xs, ps, (state_conv_a, state_lru_a, state_hgrn, state_rwkv, state_shift_c),
                                layers, lbraw, s_valid=ss, pos0_is_zero=False)
    return (y_prompt, y_sample[:, :ss]) + st_p + st_s
```

```python
import functools
import math

import jax
import jax.numpy as jnp
from jax import lax
from jax.experimental import pallas as pl
from jax.experimental.pallas import tpu as pltpu

F32 = jnp.float32
BF16 = jnp.bfloat16

D_MODEL = 1024
BRANCH_W = 512
N_BRANCH = 3
LRU_BLOCKS = 8
LRU_BW = BRANCH_W // LRU_BLOCKS
CONV_W = 4
LRU_C = 8.0
HG_HEADS = 4
HG_D = BRANCH_W // HG_HEADS
HG_F_MIN = 1e-20
RW_HD = 64
RW_HEADS = BRANCH_W // RW_HD
RW_LORA_W = 64
RW_LORA_A = 64
RW_LORA_G = 128
RW_LORA = RW_LORA_W + RW_LORA_A + RW_LORA_G
RW_GN_EPS = 64e-5
RW_COLS = 3 * BRANCH_W + RW_LORA
PLE_DIM = 256
EPS = 1e-6

SUBLANES = 8
LANES = 128
MXU_DIM = 256
VMEM_LIMIT = 56 * 1024 * 1024

PROJ_COLS = 8192
OFF_GATES = 0
OFF_LRU = N_BRANCH * D_MODEL
OFF_HG = OFF_LRU + 2 * BRANCH_W
OFF_RW = OFF_HG + 4 * BRANCH_W
RW_BLOCK = PROJ_COLS - OFF_RW

RW_GROUP = MXU_DIM // RW_HD
RW_GW = RW_GROUP * RW_HD


def _params(sem):
    return pltpu.CompilerParams(dimension_semantics=sem, vmem_limit_bytes=VMEM_LIMIT)


def _rms(x, g):
    return x * lax.rsqrt(jnp.mean(x * x, axis=-1, keepdims=True) + EPS) * g


def _sigmoid(x):
    return 1.0 / (1.0 + jnp.exp(-x))


def _softplus(x):
    return jnp.maximum(x, 0.0) + jnp.log1p(jnp.exp(-jnp.abs(x)))


def _dot(a, b):
    return jnp.dot(a.astype(BF16), b.astype(BF16), preferred_element_type=F32)


def _dot_nt(a, b):
    return lax.dot_general(a.astype(BF16), b.astype(BF16), (((1,), (1,)), ((), ())),
                           preferred_element_type=F32)


def _dot_tn(a, b):
    return lax.dot_general(a.astype(BF16), b.astype(BF16), (((0,), (0,)), ((), ())),
                           preferred_element_type=F32)


def _split3(x):
    hi = x.astype(BF16)
    r1 = x - hi.astype(F32)
    mid = r1.astype(BF16)
    lo = (r1 - mid.astype(F32)).astype(BF16)
    return hi, mid, lo


def _dot_exact_lhs(m01, x):
    hi, mid, lo = _split3(x)
    m = m01.astype(BF16)
    out = jnp.dot(m, lo, preferred_element_type=F32)
    out = out + jnp.dot(m, mid, preferred_element_type=F32)
    return out + jnp.dot(m, hi, preferred_element_type=F32)


def _split2(x):
    hi = x.astype(BF16)
    return hi, (x - hi.astype(F32)).astype(BF16)


def _dot_seg(x, m01):
    hi, lo = _split2(x)
    m = m01.astype(BF16)
    return jnp.dot(lo, m, preferred_element_type=F32) + jnp.dot(hi, m, preferred_element_type=F32)


def _dot_cum(m01, x):
    hi, lo = _split2(x)
    m = m01.astype(BF16)
    return jnp.dot(m, lo, preferred_element_type=F32) + jnp.dot(m, hi, preferred_element_type=F32)


def _tri_incl(n):
    i = lax.broadcasted_iota(jnp.int32, (n, n), 0)
    j = lax.broadcasted_iota(jnp.int32, (n, n), 1)
    return (i >= j).astype(F32)


def _seg_ones(width, seg):
    i = lax.broadcasted_iota(jnp.int32, (width, width), 0)
    j = lax.broadcasted_iota(jnp.int32, (width, width), 1)
    return (i // seg == j // seg).astype(F32)


def _in_proj_kernel(x_ref, g_ref, w_ref, o_ref, h_scr):
    @pl.when(pl.program_id(1) == 0)
    def _():
        h_scr[...] = _rms(x_ref[...], g_ref[...]).astype(BF16)

    o_ref[...] = jnp.dot(h_scr[...], w_ref[...], preferred_element_type=F32)


def _in_proj(x, g, w, tm, tn):
    t = x.shape[0]
    return pl.pallas_call(
        _in_proj_kernel,
        grid=(t // tm, PROJ_COLS // tn),
        in_specs=[
            pl.BlockSpec((tm, D_MODEL), lambda i, j: (i, 0)),
            pl.BlockSpec((1, D_MODEL), lambda i, j: (0, 0)),
            pl.BlockSpec((D_MODEL, tn), lambda i, j: (0, j)),
        ],
        out_specs=pl.BlockSpec((tm, tn), lambda i, j: (i, j)),
        out_shape=jax.ShapeDtypeStruct((t, PROJ_COLS), F32),
        scratch_shapes=[pltpu.VMEM((tm, D_MODEL), BF16)],
        compiler_params=_params(("parallel", "arbitrary")),
        name="in_proj",
    )(x, g, w)


def _lru_kernel(xa_ref, ga_ref, prev8_ref, h0_ref, cw_ref, cb_ref, w_ref, bab_ref, lam_ref,
                y_ref, hout_ref, prev_scr, h_scr, a_scr, b_scr, hh_scr,
                *, nb, tt_len, s_valid, pos0_is_zero):
    tt = pl.program_id(0)
    rows = nb * tt_len

    @pl.when(tt == 0)
    def _():
        prev_scr[...] = prev8_ref[...]
        h_scr[...] = h0_ref[...]

    xa = xa_ref[...]
    prev8 = prev_scr[...]
    t8 = lax.broadcasted_iota(jnp.int32, (1, SUBLANES, 1), 1)
    cw = cw_ref[...]
    xc = cb_ref[...][None] + cw[CONV_W - 1][None, None] * xa
    for j in range(1, CONV_W):
        rolled = pltpu.roll(xa, j, 1)
        head = jnp.where(t8 < j, pltpu.roll(prev8, j, 1), rolled[:, :SUBLANES])
        shifted = head if tt_len == SUBLANES else jnp.concatenate([head, rolled[:, SUBLANES:]], axis=1)
        xc = xc + cw[CONV_W - 1 - j][None, None] * shifted
    prev_scr[...] = xa[:, tt_len - SUBLANES:, :]

    xc2 = xc.reshape(rows, BRANCH_W)
    z = _dot(xc2, w_ref[...])
    bab = bab_ref[...]
    r = _sigmoid(z[:, :BRANCH_W] + bab[0:1])
    i = _sigmoid(z[:, BRANCH_W:] + bab[1:2])
    log_a = (-LRU_C) * r * _softplus(-lam_ref[...])
    a = jnp.exp(log_a)
    th = jnp.tanh(log_a)
    mult = jnp.sqrt(jnp.maximum(-2.0 * th / (1.0 - th), 0.0))
    t_in = lax.broadcasted_iota(jnp.int32, (rows, 1), 0) % tt_len
    if pos0_is_zero:
        mult = jnp.where(jnp.logical_and(tt == 0, t_in == 0), 1.0, mult)
    b = xc2 * i * mult
    valid = (tt * tt_len + t_in) < s_valid
    a = jnp.where(valid, a, 1.0)
    b = jnp.where(valid, b, 0.0)
    n_lt = BRANCH_W // LANES
    for l in range(n_lt):
        a_scr[l] = a[:, l * LANES:(l + 1) * LANES]
        b_scr[l] = b[:, l * LANES:(l + 1) * LANES]

    def step(t, hs):
        out = []
        for l in range(n_lt):
            h = a_scr[l, pl.ds(t, nb, stride=tt_len), :] * hs[l] + b_scr[l, pl.ds(t, nb, stride=tt_len), :]
            hh_scr[l, pl.ds(t, nb, stride=tt_len), :] = h
            out.append(h)
        return tuple(out)

    h0 = h_scr[...]
    hs = lax.fori_loop(0, tt_len, step, tuple(h0[:, l * LANES:(l + 1) * LANES] for l in range(n_lt)))
    h = jnp.concatenate(hs, axis=-1)
    h_scr[...] = h
    hout_ref[...] = h

    ga = ga_ref[...].reshape(rows, BRANCH_W)
    gelu = 0.5 * ga * (1.0 + jnp.tanh(math.sqrt(2.0 / math.pi) * (ga + 0.044715 * ga * ga * ga)))
    hh = jnp.concatenate([hh_scr[l] for l in range(n_lt)], axis=-1)
    y_ref[...] = (hh * gelu).reshape(nb, tt_len, BRANCH_W)


def _lru(proj3, prev8, h0, cw, cb, w, bab, lam, *, tt_len, s_valid, pos0_is_zero):
    nb, s_len, _ = proj3.shape
    blk = OFF_LRU // BRANCH_W
    kern = functools.partial(_lru_kernel, nb=nb, tt_len=tt_len, s_valid=s_valid, pos0_is_zero=pos0_is_zero)
    full2 = lambda t: (0, 0)
    return pl.pallas_call(
        kern,
        grid=(s_len // tt_len,),
        in_specs=[
            pl.BlockSpec((nb, tt_len, BRANCH_W), lambda t: (0, t, blk)),
            pl.BlockSpec((nb, tt_len, BRANCH_W), lambda t: (0, t, blk + 1)),
            pl.BlockSpec((nb, SUBLANES, BRANCH_W), lambda t: (0, 0, 0)),
            pl.BlockSpec((nb, BRANCH_W), full2),
            pl.BlockSpec((CONV_W, BRANCH_W), full2),
            pl.BlockSpec((1, BRANCH_W), full2),
            pl.BlockSpec((BRANCH_W, 2 * BRANCH_W), full2),
            pl.BlockSpec((2, BRANCH_W), full2),
            pl.BlockSpec((1, BRANCH_W), full2),
        ],
        out_specs=[
            pl.BlockSpec((nb, tt_len, BRANCH_W), lambda t: (0, t, 0)),
            pl.BlockSpec((nb, BRANCH_W), full2),
        ],
        out_shape=[
            jax.ShapeDtypeStruct((nb, s_len, BRANCH_W), F32),
            jax.ShapeDtypeStruct((nb, BRANCH_W), F32),
        ],
        scratch_shapes=[
            pltpu.VMEM((nb, SUBLANES, BRANCH_W), F32),
            pltpu.VMEM((nb, BRANCH_W), F32),
        ] + [pltpu.VMEM((BRANCH_W // LANES, nb * tt_len, LANES), F32)] * 3,
        compiler_params=_params(("arbitrary",)),
        name="rglru",
    )(proj3, proj3, prev8, h0, cw, cb, w, bab, lam)


def _hg_diag_blocks(qh, kh, vh, bch):
    c = qh.shape[0]
    nblk = c // SUBLANES
    q3 = qh.reshape(nblk, SUBLANES, HG_D)
    k3 = kh.reshape(nblk, SUBLANES, HG_D)
    v3 = vh.reshape(nblk, SUBLANES, HG_D)
    b3 = bch.reshape(nblk, SUBLANES, HG_D)
    tin = lax.broadcasted_iota(jnp.int32, (1, SUBLANES, 1), 1)
    o3 = jnp.zeros((nblk, SUBLANES, HG_D), F32)
    for s in range(SUBLANES):
        dec = jnp.exp(jnp.minimum(b3 - b3[:, s:s + 1, :], 0.0))
        w = jnp.sum(q3 * k3[:, s:s + 1, :] * dec, axis=-1, keepdims=True)
        w = jnp.where(tin >= s, w, 0.0)
        o3 = o3 + w * v3[:, s:s + 1, :]
    return o3.reshape(c, HG_D)


def _hg_level_refs(bch, h, c):
    gq, gk = [], []
    zero = jnp.zeros((h, HG_D), F32)
    for j in range(c // h):
        if j % 2 == 1:
            gq.append(jnp.broadcast_to(bch[j * h - 1:j * h, :], (h, HG_D)))
            gk.append(zero)
        else:
            gq.append(zero)
            gk.append(jnp.broadcast_to(bch[(j + 1) * h - 1:(j + 1) * h, :], (h, HG_D)))
    return jnp.concatenate(gq, axis=0), jnp.concatenate(gk, axis=0)


def _hgrn_kernel(q_ref, f_ref, v_ref, g_ref, s0_ref, lbraw_ref, ng_ref, y_ref, sout_ref,
                 st_scr, qs_scr, k_scr, lf_scr, o_scr, *, nbb, tt_len, chunk, s_valid, layer):
    tt = pl.program_id(1)
    n_t = pl.num_programs(1)
    c = chunk

    raw = lbraw_ref[...]
    ex = jnp.exp(raw - jnp.max(raw, axis=0, keepdims=True))
    sm = ex / jnp.sum(ex, axis=0, keepdims=True)
    lb = jnp.zeros((1, BRANCH_W), F32)
    for l in range(1, layer + 1):
        lb = lb + sm[l:l + 1]

    tri = _tri_incl(c)
    ti = lax.broadcasted_iota(jnp.int32, (c, 1), 0)
    ii = lax.broadcasted_iota(jnp.int32, (c, c), 0)
    jj = lax.broadcasted_iota(jnp.int32, (c, c), 1)
    levels = []
    h = c // 2
    while h >= SUBLANES:
        odd = (ti // h) % 2 == 1
        pair = jnp.logical_and(ii // (2 * h) == jj // (2 * h),
                               jnp.logical_and((ii // h) % 2 == 1, (jj // h) % 2 == 0))
        levels.append((h, odd, pair))
        h //= 2
    t_rows = lax.broadcasted_iota(jnp.int32, (tt_len, 1), 0)
    valid = (tt * tt_len + t_rows) < s_valid

    def per_batch(bb, carry):
        @pl.when(tt == 0)
        def _():
            for hd in range(HG_HEADS):
                st_scr[bb, hd] = s0_ref[bb, hd].T

        q = q_ref[bb]
        fp = f_ref[bb]
        qs_scr[...] = q * _sigmoid(q)
        sg = _sigmoid(fp)
        f = lb + (1.0 - lb) * sg
        k_scr[...] = jnp.where(valid, (1.0 - lb) * _sigmoid(-fp), 0.0)
        lf_scr[...] = jnp.where(valid, jnp.log(jnp.maximum(f, HG_F_MIN)), 0.0)

        def per_chunk(ci, carry2):
            r0 = pl.multiple_of(ci * c, c)
            bc = _dot_exact_lhs(tri, lf_scr[pl.ds(r0, c), :])
            for hd in range(HG_HEADS):
                ln = slice(hd * HG_D, (hd + 1) * HG_D)
                qh = qs_scr[pl.ds(r0, c), ln]
                kh = k_scr[pl.ds(r0, c), ln]
                vh = v_ref[bb, pl.ds(r0, c), ln]
                bch = bc[:, ln]
                st = st_scr[bb, hd]
                o = _dot_nt(qh * jnp.exp(bch), st)
                if levels:
                    amat = jnp.zeros((c, c), F32)
                    for (h, odd, pair) in levels:
                        gq, gk = _hg_level_refs(bch, h, c)
                        qt = jnp.where(odd, qh * jnp.exp(jnp.where(odd, bch - gq, 0.0)), 0.0)
                        kt = jnp.where(odd, 0.0, kh * jnp.exp(jnp.where(odd, 0.0, gk - bch)))
                        amat = amat + jnp.where(pair, _dot_nt(qt, kt), 0.0)
                    o = o + _dot(amat, vh)
                o = o + _hg_diag_blocks(qh, kh, vh, bch)
                btot = bch[c - 1:c, :]
                khat = kh * jnp.exp(btot - bch)
                st_scr[bb, hd] = st * jnp.exp(btot) + _dot_tn(vh, khat)
                o_scr[pl.ds(r0, c), ln] = o
            return carry2

        lax.fori_loop(0, tt_len // c, per_chunk, 0)

        g = g_ref[bb]
        ng = ng_ref[...]
        outs = []
        for hd in range(HG_HEADS):
            ln = slice(hd * HG_D, (hd + 1) * HG_D)
            o = o_scr[:, ln]
            outs.append(o * lax.rsqrt(jnp.mean(o * o, axis=-1, keepdims=True) + EPS) * ng[:, ln])
        y_ref[bb] = jnp.concatenate(outs, axis=-1) * (g * _sigmoid(g))

        @pl.when(tt == n_t - 1)
        def _():
            for hd in range(HG_HEADS):
                sout_ref[bb, hd] = st_scr[bb, hd].T
        return carry

    if nbb == 1:
        per_batch(0, 0)
    else:
        lax.fori_loop(0, nbb, per_batch, 0)


def _hgrn(proj3, s0, lbraw, ng, *, nbb, tt_len, chunk, s_valid, layer):
    nb, s_len, _ = proj3.shape
    blk = OFF_HG // BRANCH_W
    kern = functools.partial(_hgrn_kernel, nbb=nbb, tt_len=tt_len, chunk=chunk, s_valid=s_valid, layer=layer)
    seq = lambda k: pl.BlockSpec((nbb, tt_len, BRANCH_W), lambda b, t, k=k: (b, t, blk + k))
    st_spec = pl.BlockSpec((nbb, HG_HEADS, HG_D, HG_D), lambda b, t: (b, 0, 0, 0))
    return pl.pallas_call(
        kern,
        grid=(nb // nbb, s_len // tt_len),
        in_specs=[seq(0), seq(1), seq(2), seq(3), st_spec,
                  pl.BlockSpec(lbraw.shape, lambda b, t: (0, 0)),
                  pl.BlockSpec((1, BRANCH_W), lambda b, t: (0, 0))],
        out_specs=[pl.BlockSpec((nbb, tt_len, BRANCH_W), lambda b, t: (b, t, 0)), st_spec],
        out_shape=[jax.ShapeDtypeStruct((nb, s_len, BRANCH_W), F32),
                   jax.ShapeDtypeStruct(s0.shape, F32)],
        scratch_shapes=[pltpu.VMEM((nbb, HG_HEADS, HG_D, HG_D), F32)]
        + [pltpu.VMEM((tt_len, BRANCH_W), F32)] * 4,
        compiler_params=_params(("parallel", "arbitrary")),
        name="hgrn2",
    )(proj3, proj3, proj3, proj3, s0, lbraw, ng)


def _rwkv_kernel(c_ref, sh0_ref, s0_ref, mu_ref, wl_ref, vec_ref, y_ref, sout_ref,
                 sbd_scr, carry_scr, at_scr, rt_scr, bt_scr, kt_scr, bp_scr, kp_scr, v_scr, pc_scr,
                 bonus_scr, gate_scr, o_scr, tcat_scr, aak_scr, arb_scr, ark_scr,
                 *, nbb, tt_len, chunk, s_valid, ua, ub):
    tt = pl.program_id(1)
    n_t = pl.num_programs(1)
    c = chunk
    c4 = RW_GROUP * c
    n_groups = RW_HEADS // RW_GROUP
    rows = nbb * tt_len
    n_cb = tt_len // c
    n_ch = rows // c

    vec = vec_ref[...]
    w0, a0, k_k, k_a, r_k, ln_g, ln_b = [vec[i:i + 1] for i in range(7)]
    seg = _seg_ones(BRANCH_W, RW_HD)
    gi_ = lax.broadcasted_iota(jnp.int32, (RW_GW, RW_GW), 0)
    gj_ = lax.broadcasted_iota(jnp.int32, (RW_GW, RW_GW), 1)
    bd_state = gi_ // RW_HD == gj_ // RW_HD

    @pl.when(tt == 0)
    def _():
        carry_scr[...] = sh0_ref[...]

        def init(bb, carry):
            for g in range(n_groups):
                blk = jnp.concatenate([s0_ref[bb, g * RW_GROUP + h] for h in range(RW_GROUP)], axis=0)
                sbd_scr[bb, g] = jnp.where(bd_state, jnp.concatenate([blk] * RW_GROUP, axis=1), 0.0)
            return carry

        lax.fori_loop(0, nbb, init, 0)

    cc3 = c_ref[:, :, :RW_COLS]
    t3 = lax.broadcasted_iota(jnp.int32, (1, tt_len, 1), 1)
    prev3 = jnp.where(t3 == 0, carry_scr[...], pltpu.roll(cc3, 1, 1))
    carry_scr[...] = cc3[:, tt_len - 1:tt_len, :]
    xm = (cc3 + (prev3 - cc3) * mu_ref[...][None]).reshape(rows, RW_COLS)
    t_in = lax.broadcasted_iota(jnp.int32, (rows, 1), 0) % tt_len
    valid = (tt * tt_len + t_in) < s_valid
    r = xm[:, 0:BRANCH_W]
    k = xm[:, BRANCH_W:2 * BRANCH_W]
    v = xm[:, 2 * BRANCH_W:3 * BRANCH_W]
    lo = xm[:, 3 * BRANCH_W:]
    lane_l = lax.broadcasted_iota(jnp.int32, (1, RW_LORA), 1)
    act = jnp.where(lane_l < RW_LORA_W, jnp.tanh(lo),
                    jnp.where(lane_l < RW_LORA_W + RW_LORA_A, lo, _sigmoid(lo)))
    z = _dot(act, wl_ref[...])
    w = -_softplus(-(w0 + z[:, 0:BRANCH_W])) - 0.5
    ld = jnp.where(valid, -jnp.exp(w), 0.0)
    a = _sigmoid(a0 + z[:, BRANCH_W:2 * BRANCH_W])
    kk = k * k_k
    kbar = k * (1.0 + (a - 1.0) * k_a)
    sums = _dot_seg(jnp.concatenate([kk * kk, r * kbar * r_k], axis=0), seg)
    kap = jnp.where(valid, kk / jnp.maximum(jnp.sqrt(sums[:rows]), 1e-12), 0.0)
    kbar = jnp.where(valid, kbar, 0.0)
    ri = lax.broadcasted_iota(jnp.int32, (rows, rows), 0)
    rj = lax.broadcasted_iota(jnp.int32, (rows, rows), 1)
    same_chunk = ri // c == rj // c
    cum = jnp.concatenate([jnp.logical_and(same_chunk, ri >= rj).astype(F32), same_chunk.astype(F32)], axis=0)
    lws = _dot_cum(cum, ld)
    lw = lws[:rows]
    back = lws[rows:] - lw
    e_in = jnp.exp(lw)
    e_neg = jnp.exp(-lw)
    e_back = jnp.exp(back)
    at_scr[...] = -kap * jnp.exp(lw - ld)
    rt_scr[...] = r * e_in
    bt_scr[...] = kap * a * e_neg
    kt_scr[...] = kbar * e_neg
    bp_scr[...] = kap * a * e_back
    kp_scr[...] = kbar * e_back
    v_scr[...] = v
    pc_scr[...] = jnp.exp(lws[rows:])
    bonus_scr[...] = sums[rows:]
    gate_scr[...] = z[:, 2 * BRANCH_W:]

    si = lax.broadcasted_iota(jnp.int32, (c4, RW_GW), 0)
    sj = lax.broadcasted_iota(jnp.int32, (c4, RW_GW), 1)
    head_rows = si // c == sj // RW_HD
    qi = lax.broadcasted_iota(jnp.int32, (c4, c4), 0)
    qj = lax.broadcasted_iota(jnp.int32, (c4, c4), 1)
    same = qi // c == qj // c
    strict = jnp.logical_and(same, qi % c > qj % c)
    incl = jnp.logical_and(same, qi % c >= qj % c)
    eye = (qi == qj).astype(F32)
    fuse_sq = c4 % LANES == 0

    def stack(x):
        return jnp.where(head_rows, jnp.concatenate([x] * RW_GROUP, axis=0), 0.0)

    def unstack(x):
        out = x[0:c]
        for h in range(1, RW_GROUP):
            out = out + x[h * c:(h + 1) * c]
        return out

    def phase_a(it, carry):
        chains = [(it * ua + u, g) for u in range(ua) for g in range(n_groups)]
        nmats = []
        for ch, g in chains:
            r0 = pl.multiple_of(ch * c, c)
            ln = slice(g * RW_GW, (g + 1) * RW_GW)
            lhs = jnp.concatenate([stack(at_scr[pl.ds(r0, c), ln]), stack(rt_scr[pl.ds(r0, c), ln])], axis=0)
            rhs = jnp.concatenate([bt_scr[pl.ds(r0, c), ln]] * RW_GROUP + [kt_scr[pl.ds(r0, c), ln]] * RW_GROUP,
                                  axis=0)
            quad = _dot_nt(lhs, rhs)
            nmats.append(jnp.where(strict, quad[:c4, :c4], 0.0))
            aak_scr[ch, g] = unstack(jnp.where(strict, quad[:c4, c4:], 0.0))
            arb_scr[ch, g] = unstack(jnp.where(incl, quad[c4:, :c4], 0.0))
            ark_scr[ch, g] = unstack(jnp.where(incl, quad[c4:, c4:], 0.0))
        tinvs = [eye + n for n in nmats]
        npows = [_dot(n, n) for n in nmats]
        span = 2
        while 2 * span < c:
            if fuse_sq:
                boths = [_dot(p, jnp.concatenate([p, t], axis=1)) for p, t in zip(npows, tinvs)]
                tinvs = [t + bo[:, c4:] for t, bo in zip(tinvs, boths)]
                npows = [bo[:, :c4] for bo in boths]
            else:
                tinvs = [t + _dot(p, t) for p, t in zip(npows, tinvs)]
                npows = [_dot(p, p) for p in npows]
            span *= 2
        tinvs = [t + _dot(p, t) for p, t in zip(npows, tinvs)]
        for (ch, g), t in zip(chains, tinvs):
            tcat_scr[ch, g] = unstack(t)
        return carry

    if n_ch // ua == 1:
        phase_a(0, 0)
    else:
        lax.fori_loop(0, n_ch // ua, phase_a, 0)

    def phase_b(it, carry):
        cb = it // (nbb // ub)
        b0 = (it % (nbb // ub)) * ub
        chains = [(b0 + u, g) for u in range(ub) for g in range(n_groups)]
        ops = []
        for bb, g in chains:
            ch = bb * n_cb + cb
            r0 = pl.multiple_of(ch * c, c)
            ln = slice(g * RW_GW, (g + 1) * RW_GW)
            ops.append(dict(bb=bb, g=g, ch=ch, r0=r0, ln=ln, sbd=sbd_scr[bb, g],
                            v_bd=stack(v_scr[pl.ds(r0, c), ln])))
        wmats = [_dot_nt(at_scr[pl.ds(q["r0"], c), q["ln"]], q["sbd"]) + _dot(aak_scr[q["ch"], q["g"]], q["v_bd"])
                 for q in ops]
        us = [_dot(tcat_scr[q["ch"], q["g"]], stack(w)) for q, w in zip(ops, wmats)]
        for q, u in zip(ops, us):
            r0, ln = q["r0"], q["ln"]
            o_scr[pl.ds(r0, c), ln] = (_dot_nt(rt_scr[pl.ds(r0, c), ln], q["sbd"])
                                       + _dot(arb_scr[q["ch"], q["g"]], stack(u))
                                       + _dot(ark_scr[q["ch"], q["g"]], q["v_bd"]))
        for q, u in zip(ops, us):
            r0, ln = q["r0"], q["ln"]
            upd = _dot_tn(u, bp_scr[pl.ds(r0, c), ln]) + _dot_tn(v_scr[pl.ds(r0, c), ln], kp_scr[pl.ds(r0, c), ln])
            sbd_scr[q["bb"], q["g"]] = q["sbd"] * pc_scr[pl.ds(r0, 1), ln] + jnp.where(bd_state, upd, 0.0)
        return carry

    n_it = n_cb * (nbb // ub)
    if n_it == 1:
        phase_b(0, 0)
    else:
        lax.fori_loop(0, n_it, phase_b, 0)

    o = o_scr[...]
    inv_n = 1.0 / RW_HD
    mean = _dot_seg(o, seg) * inv_n
    cen = o - mean
    var = _dot_seg(cen * cen, seg) * inv_n
    on = cen * lax.rsqrt(var + RW_GN_EPS) * ln_g + ln_b
    y_ref[...] = ((on + bonus_scr[...] * v_scr[...]) * gate_scr[...]).reshape(nbb, tt_len, BRANCH_W)

    @pl.when(tt == n_t - 1)
    def _():
        def fin(bb, carry):
            for g in range(n_groups):
                sbd = sbd_scr[bb, g]
                for h in range(RW_GROUP):
                    sout_ref[bb, g * RW_GROUP + h] = sbd[h * RW_HD:(h + 1) * RW_HD, h * RW_HD:(h + 1) * RW_HD]
            return carry

        lax.fori_loop(0, nbb, fin, 0)


def _rwkv(proj3, sh0, s0, mu, wl, vec, *, nbb, tt_len, chunk, s_valid, ua, ub):
    nb, s_len, _ = proj3.shape
    kern = functools.partial(_rwkv_kernel, nbb=nbb, tt_len=tt_len, chunk=chunk, s_valid=s_valid, ua=ua, ub=ub)
    st_spec = pl.BlockSpec((nbb, RW_HEADS, RW_HD, RW_HD), lambda b, t: (b, 0, 0, 0))
    full2 = lambda b, t: (0, 0)
    rows = nbb * tt_len
    n_groups = RW_HEADS // RW_GROUP
    mats = pltpu.VMEM((rows // chunk, n_groups, chunk, RW_GROUP * chunk), F32)
    return pl.pallas_call(
        kern,
        grid=(nb // nbb, s_len // tt_len),
        in_specs=[
            pl.BlockSpec((nbb, tt_len, RW_BLOCK), lambda b, t: (b, t, OFF_RW // RW_BLOCK)),
            pl.BlockSpec((nbb, 1, RW_COLS), lambda b, t: (b, 0, 0)),
            st_spec,
            pl.BlockSpec((1, RW_COLS), full2),
            pl.BlockSpec((RW_LORA, 3 * BRANCH_W), full2),
            pl.BlockSpec((SUBLANES, BRANCH_W), full2),
        ],
        out_specs=[pl.BlockSpec((nbb, tt_len, BRANCH_W), lambda b, t: (b, t, 0)), st_spec],
        out_shape=[jax.ShapeDtypeStruct((nb, s_len, BRANCH_W), F32),
                   jax.ShapeDtypeStruct(s0.shape, F32)],
        scratch_shapes=[
            pltpu.VMEM((nbb, n_groups, RW_GW, RW_GW), F32),
            pltpu.VMEM((nbb, 1, RW_COLS), F32),
        ] + [pltpu.VMEM((rows, BRANCH_W), F32)] * 11 + [mats] * 4,
        compiler_params=_params(("parallel", "arbitrary")),
        name="rwkv7",
    )(proj3, sh0, s0, mu, wl, vec)


def _mix_kernel(x_ref, gts_ref, ya_ref, yb_ref, yc_ref, wb_ref, wo_ref, g_ref, o_ref):
    acc = None
    for n, y_ref in enumerate((ya_ref, yb_ref, yc_ref)):
        up = _dot(y_ref[...], wb_ref[n])
        term = _sigmoid(gts_ref[:, n * D_MODEL:(n + 1) * D_MODEL]) * up
        acc = term if acc is None else acc + term
    mix = _dot(acc, wo_ref[...])
    o_ref[...] = x_ref[...] + _rms(mix, g_ref[...])


def _mix(x, proj, ya, yb, yc, wb, wo, g, tm):
    t = x.shape[0]
    row = lambda w: pl.BlockSpec((tm, w), lambda i: (i, 0))
    return pl.pallas_call(
        _mix_kernel,
        grid=(t // tm,),
        in_specs=[row(D_MODEL), row(N_BRANCH * D_MODEL), row(BRANCH_W), row(BRANCH_W), row(BRANCH_W),
                  pl.BlockSpec((N_BRANCH, BRANCH_W, D_MODEL), lambda i: (0, 0, 0)),
                  pl.BlockSpec((D_MODEL, D_MODEL), lambda i: (0, 0)),
                  pl.BlockSpec((1, D_MODEL), lambda i: (0, 0))],
        out_specs=row(D_MODEL),
        out_shape=jax.ShapeDtypeStruct((t, D_MODEL), F32),
        compiler_params=_params(("parallel",)),
        name="branch_mix",
    )(x, proj, ya, yb, yc, wb, wo, g)


def _ffn_kernel(x_ref, p_ref, gpre_ref, wg_ref, wu_ref, wd_ref, gpost_ref, wple_ref, wpg_ref, gple_ref,
                o_ref, h_scr, acc_scr):
    j = pl.program_id(1)

    @pl.when(j == 0)
    def _():
        h_scr[...] = _rms(x_ref[...], gpre_ref[...]).astype(BF16)
        acc_scr[...] = jnp.zeros_like(acc_scr)

    h = h_scr[...]
    gt = jnp.dot(h, wg_ref[...], preferred_element_type=F32)
    up = jnp.dot(h, wu_ref[...], preferred_element_type=F32)
    acc_scr[...] += _dot(gt * _sigmoid(gt) * up, wd_ref[...])

    @pl.when(j == pl.num_programs(1) - 1)
    def _():
        x2 = x_ref[...] + _rms(acc_scr[...], gpost_ref[...])
        ple = _dot(p_ref[...], wple_ref[...]) * _sigmoid(_dot(x2, wpg_ref[...]))
        o_ref[...] = x2 + _rms(ple, gple_ref[...])


def _ffn(x, p, gpre, wg, wu, wd, gpost, wple, wpg, gple, tm, n_split):
    t = x.shape[0]
    d_ff = wg.shape[1]
    tf = d_ff // n_split
    vecspec = pl.BlockSpec((1, D_MODEL), lambda i, j: (0, 0))
    return pl.pallas_call(
        _ffn_kernel,
        grid=(t // tm, n_split),
        in_specs=[
            pl.BlockSpec((tm, D_MODEL), lambda i, j: (i, 0)),
            pl.BlockSpec((tm, PLE_DIM), lambda i, j: (i, 0)),
            vecspec,
            pl.BlockSpec((D_MODEL, tf), lambda i, j: (0, j)),
            pl.BlockSpec((D_MODEL, tf), lambda i, j: (0, j)),
            pl.BlockSpec((tf, D_MODEL), lambda i, j: (j, 0)),
            vecspec,
            pl.BlockSpec((PLE_DIM, D_MODEL), lambda i, j: (0, 0)),
            pl.BlockSpec((D_MODEL, D_MODEL), lambda i, j: (0, 0)),
            vecspec,
        ],
        out_specs=pl.BlockSpec((tm, D_MODEL), lambda i, j: (i, 0)),
        out_shape=jax.ShapeDtypeStruct((t, D_MODEL), F32),
        scratch_shapes=[pltpu.VMEM((tm, D_MODEL), BF16), pltpu.VMEM((tm, D_MODEL), F32)],
        compiler_params=_params(("parallel", "arbitrary")),
        name="ffn_ple",
    )(x, p, gpre, wg, wu, wd, gpost, wple, wpg, gple)


def _block_diag(w):
    n, r, c = w.shape
    eye = jnp.eye(n, dtype=w.dtype)
    return (eye[:, None, :, None] * w[:, :, None, :]).reshape(n * r, n * c)


def _prep_layer(i, W):
    win = W["w_in"][i]
    segs = jnp.split(win, [2 * BRANCH_W, 2 * BRANCH_W + 4 * BRANCH_W, 2 * BRANCH_W + 4 * BRANCH_W + RW_COLS], axis=1)
    w_lru, w_hg, w_rw, w_gates = segs
    pad = jnp.zeros((D_MODEL, PROJ_COLS - OFF_RW - RW_COLS), win.dtype)
    w_in = jnp.concatenate([w_gates, w_lru, w_hg, w_rw, pad], axis=1).astype(BF16)
    lora = jnp.zeros((RW_LORA, 3 * BRANCH_W), F32)
    lora = lora.at[0:RW_LORA_W, 0:BRANCH_W].set(W["rw_w_up"][i])
    lora = lora.at[RW_LORA_W:RW_LORA_W + RW_LORA_A, BRANCH_W:2 * BRANCH_W].set(W["rw_a_up"][i])
    lora = lora.at[RW_LORA_W + RW_LORA_A:, 2 * BRANCH_W:].set(W["rw_g_up"][i])
    vec = jnp.stack([W["rw_w0"][i], W["rw_a0"][i], W["rw_k_k"][i], W["rw_k_a"][i],
                     W["rw_r_k"][i].reshape(BRANCH_W), W["rw_ln_g"][i], W["rw_ln_b"][i],
                     jnp.zeros((BRANCH_W,), F32)])
    row = lambda name: W[name][i].reshape(1, -1)
    return dict(
        w_in=w_in, norm_pre_mix=row("norm_pre_mix"),
        conv_w=W["conv_w"][i], conv_b=row("conv_b"),
        lru_w=jnp.concatenate([_block_diag(W["lru_wa"][i]), _block_diag(W["lru_wx"][i])], axis=1).astype(BF16),
        lru_b=jnp.stack([W["lru_ba"][i], W["lru_bx"][i]]), lru_lambda=row("lru_lambda"),
        hg_norm_g=row("hg_norm_g"),
        rw_mu=row("rw_mu"), rw_lora=lora.astype(BF16), rw_vec=vec,
        w_branch=W["w_branch"][i].astype(BF16), w_out=W["w_out"][i].astype(BF16),
        norm_post_mix=row("norm_post_mix"), norm_pre_ffn=row("norm_pre_ffn"),
        w_ffn_gate=W["w_ffn_gate"][i].astype(BF16), w_ffn_up=W["w_ffn_up"][i].astype(BF16),
        w_ffn_down=W["w_ffn_down"][i].astype(BF16), norm_post_ffn=row("norm_post_ffn"),
        w_ple=W["w_ple"][i].astype(BF16), w_ple_gate=W["w_ple_gate"][i].astype(BF16), norm_ple=row("norm_ple"),
    )


def _tiles(nb, s_len):
    t = nb * s_len
    tm_in = min(t, 1024)
    tm_tok = min(t, 512)
    if s_len >= 512:
        return dict(tm_in=tm_in, tm_tok=tm_tok, lru_tt=128, nbb=1, mix_tt=512, chunk=64,
                    rw=dict(nbb=4, tt_len=128, chunk=64, ua=4, ub=4))
    return dict(tm_in=tm_in, tm_tok=tm_tok, lru_tt=s_len, nbb=min(nb, 16), mix_tt=s_len, chunk=s_len,
                rw=dict(nbb=min(nb, 16), tt_len=s_len, chunk=s_len, ua=8, ub=8))


def _run_trunk(x3, p4, states, layers, lbraw, *, s_valid, pos0_is_zero):
    nb, s_len, _ = x3.shape
    t = nb * s_len
    plan = _tiles(nb, s_len)
    conv0, lru0, hg0, rw0, sh0 = states
    x = x3.reshape(t, D_MODEL)
    new = ([], [], [], [], [])
    for i, L in enumerate(layers):
        proj = _in_proj(x, L["norm_pre_mix"], L["w_in"], plan["tm_in"], 1024)
        proj3 = proj.reshape(nb, s_len, PROJ_COLS)
        prev8 = jnp.pad(conv0[i], ((0, 0), (SUBLANES - (CONV_W - 1), 0), (0, 0)))
        ya, nlru = _lru(proj3, prev8, lru0[i], L["conv_w"], L["conv_b"], L["lru_w"], L["lru_b"], L["lru_lambda"],
                        tt_len=plan["lru_tt"], s_valid=s_valid, pos0_is_zero=pos0_is_zero)
        yb, nhg = _hgrn(proj3, hg0[i], lbraw, L["hg_norm_g"], nbb=plan["nbb"], tt_len=plan["mix_tt"],
                        chunk=plan["chunk"], s_valid=s_valid, layer=i)
        yc, nrw = _rwkv(proj3, sh0[i].reshape(nb, 1, RW_COLS), rw0[i], L["rw_mu"], L["rw_lora"], L["rw_vec"],
                        s_valid=s_valid, **plan["rw"])
        x = _mix(x, proj, ya.reshape(t, BRANCH_W), yb.reshape(t, BRANCH_W), yc.reshape(t, BRANCH_W),
                 L["w_branch"], L["w_out"], L["norm_post_mix"], plan["tm_tok"])
        x = _ffn(x, p4[i].reshape(t, PLE_DIM), L["norm_pre_ffn"], L["w_ffn_gate"], L["w_ffn_up"], L["w_ffn_down"],
                 L["norm_post_ffn"], L["w_ple"], L["w_ple_gate"], L["norm_ple"], plan["tm_tok"], 2)
        nconv = proj3[:, s_valid - (CONV_W - 1):s_valid, OFF_LRU:OFF_LRU + BRANCH_W]
        nsh = proj3[:, s_valid - 1, OFF_RW:OFF_RW + RW_COLS]
        for lst, val in zip(new, (nconv, nlru, nhg, nrw, nsh)):
            lst.append(val)
    return x.reshape(nb, s_len, D_MODEL), tuple(jnp.stack(l) for l in new)


def kernel(x_prompt, x_sample, p_prompt, p_sample, state_conv_a, state_lru_a, state_hgrn, state_rwkv, state_shift_c, norm_pre_mix, w_in, conv_w, conv_b, lru_wa, lru_ba, lru_wx, lru_bx, lru_lambda, hg_lower_bounds, hg_norm_g, rw_mu, rw_w0, rw_w_up, rw_a0, rw_a_up, rw_g_up, rw_k_k, rw_k_a, rw_r_k, rw_ln_g, rw_ln_b, w_branch, w_out, norm_post_mix, norm_pre_ffn, w_ffn_gate, w_ffn_up, w_ffn_down, norm_post_ffn, w_ple, w_ple_gate, norm_ple):
    W = dict(norm_pre_mix=norm_pre_mix, w_in=w_in, conv_w=conv_w, conv_b=conv_b, lru_wa=lru_wa, lru_ba=lru_ba,
             lru_wx=lru_wx, lru_bx=lru_bx, lru_lambda=lru_lambda, hg_norm_g=hg_norm_g, rw_mu=rw_mu, rw_w0=rw_w0,
             rw_w_up=rw_w_up, rw_a0=rw_a0, rw_a_up=rw_a_up, rw_g_up=rw_g_up, rw_k_k=rw_k_k, rw_k_a=rw_k_a,
             rw_r_k=rw_r_k, rw_ln_g=rw_ln_g, rw_ln_b=rw_ln_b, w_branch=w_branch, w_out=w_out,
             norm_post_mix=norm_post_mix, norm_pre_ffn=norm_pre_ffn, w_ffn_gate=w_ffn_gate, w_ffn_up=w_ffn_up,
             w_ffn_down=w_ffn_down, norm_post_ffn=norm_post_ffn, w_ple=w_ple, w_ple_gate=w_ple_gate, norm_ple=norm_ple)
    depth = w_in.shape[0]
    layers = [_prep_layer(i, W) for i in range(depth)]
    lbraw = hg_lower_bounds.astype(F32)

    bp, sp, _ = x_prompt.shape
    zeros = lambda *shape: jnp.zeros((depth, bp) + shape, F32)
    zero_states = (zeros(CONV_W - 1, BRANCH_W), zeros(BRANCH_W), zeros(HG_HEADS, HG_D, HG_D),
                   zeros(RW_HEADS, RW_HD, RW_HD), zeros(RW_COLS))
    y_prompt, st_p = _run_trunk(x_prompt, p_prompt, zero_states, layers, lbraw, s_valid=sp, pos0_is_zero=True)

    bs, ss, _ = x_sample.shape
    ss_pad = -(-ss // SUBLANES) * SUBLANES
    xs = jnp.pad(x_sample, ((0, 0), (0, ss_pad - ss), (0, 0)))
    ps = jnp.pad(p_sample, ((0, 0), (0, 0), (0, ss_pad - ss), (0, 0)))
    y_sample, st_s = _run_trunk(xs, ps, (state_conv_a, state_lru_a, state_hgrn, state_rwkv, state_shift_c),
                                layers, lbraw, s_valid=ss, pos0_is_zero=False)
    return (y_prompt, y_sample[:, :ss]) + st_p + st_s
```

```python
import functools
import math

import jax
import jax.numpy as jnp
from jax import lax
from jax.experimental import pallas as pl
from jax.experimental.pallas import tpu as pltpu

F32 = jnp.float32
BF16 = jnp.bfloat16

D_MODEL = 1024
BRANCH_W = 512
N_BRANCH = 3
LRU_BLOCKS = 8
LRU_BW = BRANCH_W // LRU_BLOCKS
CONV_W = 4
LRU_C = 8.0
HG_HEADS = 4
HG_D = BRANCH_W // HG_HEADS
HG_F_MIN = 1e-20
RW_HD = 64
RW_HEADS = BRANCH_W // RW_HD
RW_LORA_W = 64
RW_LORA_A = 64
RW_LORA_G = 128
RW_LORA = RW_LORA_W + RW_LORA_A + RW_LORA_G
RW_GN_EPS = 64e-5
RW_COLS = 3 * BRANCH_W + RW_LORA
PLE_DIM = 256
EPS = 1e-6

SUBLANES = 8
LANES = 128
MXU_DIM = 256
VMEM_LIMIT = 56 * 1024 * 1024

PROJ_COLS = 8192
OFF_GATES = 0
OFF_LRU = N_BRANCH * D_MODEL
OFF_HG = OFF_LRU + 2 * BRANCH_W
OFF_RW = OFF_HG + 4 * BRANCH_W
RW_BLOCK = PROJ_COLS - OFF_RW

RW_GROUP = MXU_DIM // RW_HD
RW_GW = RW_GROUP * RW_HD


def _params(sem):
    return pltpu.CompilerParams(dimension_semantics=sem, vmem_limit_bytes=VMEM_LIMIT)


def _rms(x, g):
    return x * lax.rsqrt(jnp.mean(x * x, axis=-1, keepdims=True) + EPS) * g


def _sigmoid(x):
    return 1.0 / (1.0 + jnp.exp(-x))


def _sigmoid_t(x):
    return 0.5 * jnp.tanh(0.5 * x) + 0.5


def _softplus(x):
    return jnp.maximum(x, 0.0) + jnp.log1p(jnp.exp(-jnp.abs(x)))


def _softplus_abs(x):
    return jnp.maximum(x, 0.0) + jnp.log(1.0 + jnp.exp(-jnp.abs(x)))


def _dot(a, b):
    return jnp.dot(a.astype(BF16), b.astype(BF16), preferred_element_type=F32)


def _dot_nt(a, b):
    return lax.dot_general(a.astype(BF16), b.astype(BF16), (((1,), (1,)), ((), ())),
                           preferred_element_type=F32)


def _dot_tn(a, b):
    return lax.dot_general(a.astype(BF16), b.astype(BF16), (((0,), (0,)), ((), ())),
                           preferred_element_type=F32)


def _split2(x):
    hi = x.astype(BF16)
    return hi, (x - hi.astype(F32)).astype(BF16)


def _dot_seg(x, m01):
    hi, lo = _split2(x)
    m = m01.astype(BF16)
    return jnp.dot(lo, m, preferred_element_type=F32) + jnp.dot(hi, m, preferred_element_type=F32)


def _dot_cum(m01, x):
    hi, lo = _split2(x)
    m = m01.astype(BF16)
    return jnp.dot(m, lo, preferred_element_type=F32) + jnp.dot(m, hi, preferred_element_type=F32)


def _seg_ones(width, seg):
    i = lax.broadcasted_iota(jnp.int32, (width, width), 0)
    j = lax.broadcasted_iota(jnp.int32, (width, width), 1)
    return (i // seg == j // seg).astype(F32)


def _in_proj_kernel(x_ref, g_ref, w_ref, o_ref, h_scr):
    @pl.when(pl.program_id(1) == 0)
    def _():
        h_scr[...] = _rms(x_ref[...], g_ref[...]).astype(BF16)

    o_ref[...] = jnp.dot(h_scr[...], w_ref[...], preferred_element_type=F32).astype(o_ref.dtype)


def _in_proj(x, g, w, tm, tn, out_dtype):
    t = x.shape[0]
    return pl.pallas_call(
        _in_proj_kernel,
        grid=(t // tm, PROJ_COLS // tn),
        in_specs=[
            pl.BlockSpec((tm, D_MODEL), lambda i, j: (i, 0)),
            pl.BlockSpec((1, D_MODEL), lambda i, j: (0, 0)),
            pl.BlockSpec((D_MODEL, tn), lambda i, j: (0, j)),
        ],
        out_specs=pl.BlockSpec((tm, tn), lambda i, j: (i, j)),
        out_shape=jax.ShapeDtypeStruct((t, PROJ_COLS), out_dtype),
        scratch_shapes=[pltpu.VMEM((tm, D_MODEL), BF16)],
        compiler_params=_params(("parallel", "arbitrary")),
        name="in_proj",
    )(x, g, w)


def _lru_kernel(xa_ref, ga_ref, prev8_ref, h0_ref, cw_ref, cb_ref, w_ref, bab_ref, lam_ref,
                y_ref, hout_ref, prev_scr, h_scr, *, nb, tt_len, s_valid, pos0_is_zero):
    tt = pl.program_id(0)
    rows = nb * tt_len

    @pl.when(tt == 0)
    def _():
        prev_scr[...] = prev8_ref[...]
        h_scr[...] = h0_ref[...]

    xa = xa_ref[...].astype(F32)
    prev8 = prev_scr[...]
    t8 = lax.broadcasted_iota(jnp.int32, (1, SUBLANES, 1), 1)
    cw = cw_ref[...]
    xc = cb_ref[...][None] + cw[CONV_W - 1][None, None] * xa
    for j in range(1, CONV_W):
        rolled = pltpu.roll(xa, j, 1)
        head = jnp.where(t8 < j, pltpu.roll(prev8, j, 1), rolled[:, :SUBLANES])
        shifted = head if tt_len == SUBLANES else jnp.concatenate([head, rolled[:, SUBLANES:]], axis=1)
        xc = xc + cw[CONV_W - 1 - j][None, None] * shifted
    prev_scr[...] = xa[:, tt_len - SUBLANES:, :]

    xc2 = xc.reshape(rows, BRANCH_W)
    z = _dot(xc2, w_ref[...])
    bab = bab_ref[...]
    r = _sigmoid_t(z[:, :BRANCH_W] + bab[0:1])
    i = _sigmoid_t(z[:, BRANCH_W:] + bab[1:2])
    log_a = (-LRU_C) * r * _softplus(-lam_ref[...])
    a = jnp.exp(log_a)
    mult = jnp.sqrt(jnp.maximum(1.0 - a * a, 0.0))
    t_in = lax.broadcasted_iota(jnp.int32, (rows, 1), 0) % tt_len
    if pos0_is_zero:
        mult = jnp.where(jnp.logical_and(tt == 0, t_in == 0), 1.0, mult)
    b = xc2 * i * mult
    valid = (tt * tt_len + t_in) < s_valid
    a3 = jnp.where(valid, a, 1.0).reshape(nb, tt_len, BRANCH_W)
    b3 = jnp.where(valid, b, 0.0).reshape(nb, tt_len, BRANCH_W)

    tmod = lax.broadcasted_iota(jnp.int32, (1, tt_len, 1), 1) % SUBLANES
    d = 1
    while d < SUBLANES:
        keep = tmod >= d
        b3 = a3 * jnp.where(keep, pltpu.roll(b3, d, 1), 0.0) + b3
        a3 = a3 * jnp.where(keep, pltpu.roll(a3, d, 1), 1.0)
        d *= 2
    h_in = h_scr[...][:, None, :]
    blocks = []
    for blk in range(tt_len // SUBLANES):
        rows8 = slice(blk * SUBLANES, (blk + 1) * SUBLANES)
        hb = a3[:, rows8, :] * h_in + b3[:, rows8, :]
        blocks.append(hb)
        h_in = hb[:, SUBLANES - 1:SUBLANES, :]
    hh = blocks[0] if len(blocks) == 1 else jnp.concatenate(blocks, axis=1)
    h = h_in.reshape(nb, BRANCH_W)
    h_scr[...] = h
    hout_ref[...] = h

    ga = ga_ref[...].astype(F32)
    gelu = 0.5 * ga * (1.0 + jnp.tanh(math.sqrt(2.0 / math.pi) * (ga + 0.044715 * ga * ga * ga)))
    y_ref[...] = hh * gelu


def _lru(proj3, prev8, h0, cw, cb, w, bab, lam, *, tt_len, s_valid, pos0_is_zero):
    nb, s_len, _ = proj3.shape
    blk = OFF_LRU // BRANCH_W
    kern = functools.partial(_lru_kernel, nb=nb, tt_len=tt_len, s_valid=s_valid, pos0_is_zero=pos0_is_zero)
    full2 = lambda t: (0, 0)
    return pl.pallas_call(
        kern,
        grid=(s_len // tt_len,),
        in_specs=[
            pl.BlockSpec((nb, tt_len, BRANCH_W), lambda t: (0, t, blk)),
            pl.BlockSpec((nb, tt_len, BRANCH_W), lambda t: (0, t, blk + 1)),
            pl.BlockSpec((nb, SUBLANES, BRANCH_W), lambda t: (0, 0, 0)),
            pl.BlockSpec((nb, BRANCH_W), full2),
            pl.BlockSpec((CONV_W, BRANCH_W), full2),
            pl.BlockSpec((1, BRANCH_W), full2),
            pl.BlockSpec((BRANCH_W, 2 * BRANCH_W), full2),
            pl.BlockSpec((2, BRANCH_W), full2),
            pl.BlockSpec((1, BRANCH_W), full2),
        ],
        out_specs=[
            pl.BlockSpec((nb, tt_len, BRANCH_W), lambda t: (0, t, 0)),
            pl.BlockSpec((nb, BRANCH_W), full2),
        ],
        out_shape=[
            jax.ShapeDtypeStruct((nb, s_len, BRANCH_W), F32),
            jax.ShapeDtypeStruct((nb, BRANCH_W), F32),
        ],
        scratch_shapes=[
            pltpu.VMEM((nb, SUBLANES, BRANCH_W), F32),
            pltpu.VMEM((nb, BRANCH_W), F32),
        ],
        compiler_params=_params(("arbitrary",)),
        name="rglru",
    )(proj3, proj3, prev8, h0, cw, cb, w, bab, lam)


def _hg_diag_blocks(qh, kh, vh, bch):
    c = qh.shape[0]
    nblk = c // SUBLANES
    q3 = qh.reshape(nblk, SUBLANES, HG_D)
    k3 = kh.reshape(nblk, SUBLANES, HG_D)
    v3 = vh.reshape(nblk, SUBLANES, HG_D)
    b3 = bch.reshape(nblk, SUBLANES, HG_D)
    tin = lax.broadcasted_iota(jnp.int32, (1, SUBLANES, 1), 1)
    o3 = jnp.zeros((nblk, SUBLANES, HG_D), F32)
    for s in range(SUBLANES):
        dec = jnp.exp(jnp.minimum(b3 - b3[:, s:s + 1, :], 0.0))
        w = jnp.sum(q3 * k3[:, s:s + 1, :] * dec, axis=-1, keepdims=True)
        w = jnp.where(tin >= s, w, 0.0)
        o3 = o3 + w * v3[:, s:s + 1, :]
    return o3.reshape(c, HG_D)


def _hg_level_refs(bch, h, c):
    gq, gk = [], []
    zero = jnp.zeros((h, HG_D), F32)
    for j in range(c // h):
        if j % 2 == 1:
            gq.append(jnp.broadcast_to(bch[j * h - 1:j * h, :], (h, HG_D)))
            gk.append(zero)
        else:
            gq.append(zero)
            gk.append(jnp.broadcast_to(bch[(j + 1) * h - 1:(j + 1) * h, :], (h, HG_D)))
    return jnp.concatenate(gq, axis=0), jnp.concatenate(gk, axis=0)


def _hgrn_kernel(q_ref, f_ref, v_ref, g_ref, s0_ref, lbraw_ref, ng_ref, cum_ref, y_ref, sout_ref,
                 st_scr, qs_scr, k_scr, v_scr, bc_scr, qe_scr, kh_scr, et_scr, o_scr,
                 *, nbb, tt_len, chunk, s_valid, s_len, layer, ub):
    tt = pl.program_id(1)
    n_t = pl.num_programs(1)
    c = chunk
    rows = nbb * tt_len
    n_cb = tt_len // c

    raw = lbraw_ref[...]
    ex = jnp.exp(raw - jnp.max(raw, axis=0, keepdims=True))
    sm = ex / jnp.sum(ex, axis=0, keepdims=True)
    lb = jnp.zeros((1, BRANCH_W), F32)
    for l in range(1, layer + 1):
        lb = lb + sm[l:l + 1]

    @pl.when(tt == 0)
    def _():
        def init(bb, carry):
            for hd in range(HG_HEADS):
                st_scr[bb, hd] = s0_ref[bb, hd].T
            return carry

        lax.fori_loop(0, nbb, init, 0)

    q = q_ref[...].astype(F32).reshape(rows, BRANCH_W)
    fp = f_ref[...].astype(F32).reshape(rows, BRANCH_W)
    sg = _sigmoid(fp)
    f = lb + (1.0 - lb) * sg
    k = (1.0 - lb) * (1.0 - sg)
    logf = jnp.log(jnp.maximum(f, HG_F_MIN))
    if s_valid < s_len:
        t_in = lax.broadcasted_iota(jnp.int32, (rows, 1), 0) % tt_len
        valid = (tt * tt_len + t_in) < s_valid
        k = jnp.where(valid, k, 0.0)
        logf = jnp.where(valid, logf, 0.0)
    lws = _dot_cum(cum_ref[...], logf)
    bc = lws[:rows]
    qs = q * _sigmoid_t(q)
    qs_scr[...] = qs
    k_scr[...] = k
    v_scr[...] = v_ref[...].astype(F32).reshape(rows, BRANCH_W)
    bc_scr[...] = bc
    qe_scr[...] = qs * jnp.exp(bc)
    kh_scr[...] = k * jnp.exp(lws[rows:] - bc)
    et_scr[...] = jnp.exp(lws[rows:])

    ti = lax.broadcasted_iota(jnp.int32, (c, 1), 0)
    ii = lax.broadcasted_iota(jnp.int32, (c, c), 0)
    jj = lax.broadcasted_iota(jnp.int32, (c, c), 1)
    levels = []
    h = c // 2
    while h >= SUBLANES:
        odd = (ti // h) % 2 == 1
        pair = jnp.logical_and(ii // (2 * h) == jj // (2 * h),
                               jnp.logical_and((ii // h) % 2 == 1, (jj // h) % 2 == 0))
        levels.append((h, odd, pair))
        h //= 2

    def step(it, carry):
        cb = it // (nbb // ub)
        b0 = (it % (nbb // ub)) * ub
        chains = []
        for u in range(ub):
            r0 = pl.multiple_of((b0 + u) * tt_len + cb * c, c)
            for hd in range(HG_HEADS):
                ln = slice(hd * HG_D, (hd + 1) * HG_D)
                chains.append(dict(bb=b0 + u, hd=hd, r0=r0, ln=ln, st=st_scr[b0 + u, hd],
                                   qh=qs_scr[pl.ds(r0, c), ln], kh=k_scr[pl.ds(r0, c), ln],
                                   vh=v_scr[pl.ds(r0, c), ln], bch=bc_scr[pl.ds(r0, c), ln]))
        outs = [_dot_nt(qe_scr[pl.ds(x["r0"], c), x["ln"]], x["st"]) for x in chains]
        if levels:
            amats = []
            for x in chains:
                amat = None
                for (h, odd, pair) in levels:
                    gq, gk = _hg_level_refs(x["bch"], h, c)
                    qt = jnp.where(odd, x["qh"] * jnp.exp(jnp.where(odd, x["bch"] - gq, 0.0)), 0.0)
                    kt = jnp.where(odd, 0.0, x["kh"] * jnp.exp(jnp.where(odd, 0.0, gk - x["bch"])))
                    term = jnp.where(pair, _dot_nt(qt, kt), 0.0)
                    amat = term if amat is None else amat + term
                amats.append(amat)
            outs = [o + _dot(a, x["vh"]) for o, a, x in zip(outs, amats, chains)]
        for o, x in zip(outs, chains):
            o_scr[pl.ds(x["r0"], c), x["ln"]] = o + _hg_diag_blocks(x["qh"], x["kh"], x["vh"], x["bch"])
        for x in chains:
            upd = _dot_tn(x["vh"], kh_scr[pl.ds(x["r0"], c), x["ln"]])
            st_scr[x["bb"], x["hd"]] = x["st"] * et_scr[pl.ds(x["r0"], 1), x["ln"]] + upd
        return carry

    n_it = n_cb * (nbb // ub)
    if n_it == 1:
        step(0, 0)
    else:
        lax.fori_loop(0, n_it, step, 0)

    g = g_ref[...].astype(F32).reshape(rows, BRANCH_W)
    ng = ng_ref[...]
    outs = []
    for hd in range(HG_HEADS):
        ln = slice(hd * HG_D, (hd + 1) * HG_D)
        o = o_scr[:, ln]
        outs.append(o * lax.rsqrt(jnp.mean(o * o, axis=-1, keepdims=True) + EPS) * ng[:, ln])
    y_ref[...] = (jnp.concatenate(outs, axis=-1) * (g * _sigmoid_t(g))).reshape(nbb, tt_len, BRANCH_W)

    @pl.when(tt == n_t - 1)
    def _():
        def fin(bb, carry):
            for hd in range(HG_HEADS):
                sout_ref[bb, hd] = st_scr[bb, hd].T
            return carry

        lax.fori_loop(0, nbb, fin, 0)


def _chunk_cum_matrix(rows, c):
    i = jnp.arange(rows)[:, None]
    j = jnp.arange(rows)[None, :]
    same = i // c == j // c
    return jnp.concatenate([same & (i >= j), same], axis=0).astype(BF16)


def _hgrn(proj3, s0, lbraw, ng, *, nbb, tt_len, chunk, s_valid, layer, ub):
    nb, s_len, _ = proj3.shape
    blk = OFF_HG // BRANCH_W
    rows = nbb * tt_len
    kern = functools.partial(_hgrn_kernel, nbb=nbb, tt_len=tt_len, chunk=chunk, s_valid=s_valid, s_len=s_len,
                             layer=layer, ub=ub)
    seq = lambda k: pl.BlockSpec((nbb, tt_len, BRANCH_W), lambda b, t, k=k: (b, t, blk + k))
    st_spec = pl.BlockSpec((nbb, HG_HEADS, HG_D, HG_D), lambda b, t: (b, 0, 0, 0))
    return pl.pallas_call(
        kern,
        grid=(nb // nbb, s_len // tt_len),
        in_specs=[seq(0), seq(1), seq(2), seq(3), st_spec,
                  pl.BlockSpec(lbraw.shape, lambda b, t: (0, 0)),
                  pl.BlockSpec((1, BRANCH_W), lambda b, t: (0, 0)),
                  pl.BlockSpec((2 * rows, rows), lambda b, t: (0, 0))],
        out_specs=[pl.BlockSpec((nbb, tt_len, BRANCH_W), lambda b, t: (b, t, 0)), st_spec],
        out_shape=[jax.ShapeDtypeStruct((nb, s_len, BRANCH_W), F32),
                   jax.ShapeDtypeStruct(s0.shape, F32)],
        scratch_shapes=[pltpu.VMEM((nbb, HG_HEADS, HG_D, HG_D), F32)]
        + [pltpu.VMEM((rows, BRANCH_W), F32)] * 8,
        compiler_params=_params(("parallel", "arbitrary")),
        name="hgrn2",
    )(proj3, proj3, proj3, proj3, s0, lbraw, ng, _chunk_cum_matrix(rows, chunk))


def _rwkv_kernel(c_ref, sh0_ref, s0_ref, mu_ref, wl_ref, vec_ref, cum_ref, y_ref, sout_ref,
                 sbd_scr, carry_scr, at_scr, rt_scr, bt_scr, kt_scr, bp_scr, kp_scr, v_scr, pc_scr,
                 bonus_scr, gate_scr, o_scr, tcat_scr, aak_scr, arb_scr, ark_scr,
                 *, nbb, tt_len, chunk, s_valid, s_len, ua, ub):
    tt = pl.program_id(1)
    n_t = pl.num_programs(1)
    c = chunk
    c4 = RW_GROUP * c
    n_groups = RW_HEADS // RW_GROUP
    rows = nbb * tt_len
    n_cb = tt_len // c
    n_ch = rows // c

    vec = vec_ref[...]
    w0, a0, k_k, k_a, r_k, ln_g, ln_b = [vec[i:i + 1] for i in range(7)]
    seg = _seg_ones(BRANCH_W, RW_HD)
    gi_ = lax.broadcasted_iota(jnp.int32, (RW_GW, RW_GW), 0)
    gj_ = lax.broadcasted_iota(jnp.int32, (RW_GW, RW_GW), 1)
    bd_state = gi_ // RW_HD == gj_ // RW_HD

    @pl.when(tt == 0)
    def _():
        carry_scr[...] = sh0_ref[...]

        def init(bb, carry):
            for g in range(n_groups):
                blk = jnp.concatenate([s0_ref[bb, g * RW_GROUP + h] for h in range(RW_GROUP)], axis=0)
                sbd_scr[bb, g] = jnp.where(bd_state, jnp.concatenate([blk] * RW_GROUP, axis=1), 0.0)
            return carry

        lax.fori_loop(0, nbb, init, 0)

    cc3 = c_ref[:, :, :RW_COLS].astype(F32)
    t3 = lax.broadcasted_iota(jnp.int32, (1, tt_len, 1), 1)
    prev3 = jnp.where(t3 == 0, carry_scr[...], pltpu.roll(cc3, 1, 1))
    carry_scr[...] = cc3[:, tt_len - 1:tt_len, :]
    xm = (cc3 + (prev3 - cc3) * mu_ref[...][None]).reshape(rows, RW_COLS)
    r = xm[:, 0:BRANCH_W]
    k = xm[:, BRANCH_W:2 * BRANCH_W]
    v = xm[:, 2 * BRANCH_W:3 * BRANCH_W]
    lo = xm[:, 3 * BRANCH_W:]
    lane_l = lax.broadcasted_iota(jnp.int32, (1, RW_LORA), 1)
    act = jnp.where(lane_l < RW_LORA_W, jnp.tanh(lo),
                    jnp.where(lane_l < RW_LORA_W + RW_LORA_A, lo, _sigmoid_t(lo)))
    z = _dot(act, wl_ref[...])
    w = -_softplus_abs(-(w0 + z[:, 0:BRANCH_W])) - 0.5
    ld = -jnp.exp(w)
    a = _sigmoid_t(a0 + z[:, BRANCH_W:2 * BRANCH_W])
    kk = k * k_k
    kbar = k * (1.0 + (a - 1.0) * k_a)
    sums = _dot_seg(jnp.concatenate([kk * kk, r * kbar * r_k], axis=0), seg)
    kap = kk / jnp.maximum(jnp.sqrt(sums[:rows]), 1e-12)
    if s_valid < s_len:
        t_in = lax.broadcasted_iota(jnp.int32, (rows, 1), 0) % tt_len
        valid = (tt * tt_len + t_in) < s_valid
        ld = jnp.where(valid, ld, 0.0)
        kap = jnp.where(valid, kap, 0.0)
        kbar = jnp.where(valid, kbar, 0.0)
    lws = _dot_cum(cum_ref[...], ld)
    lw = lws[:rows]
    back = lws[rows:] - lw
    e_in = jnp.exp(lw)
    e_neg = jnp.exp(-lw)
    e_back = jnp.exp(back)
    at_scr[...] = -kap * jnp.exp(lw - ld)
    rt_scr[...] = r * e_in
    bt_scr[...] = kap * a * e_neg
    kt_scr[...] = kbar * e_neg
    bp_scr[...] = kap * a * e_back
    kp_scr[...] = kbar * e_back
    v_scr[...] = v
    pc_scr[...] = jnp.exp(lws[rows:])
    bonus_scr[...] = sums[rows:]
    gate_scr[...] = z[:, 2 * BRANCH_W:]

    si = lax.broadcasted_iota(jnp.int32, (c4, RW_GW), 0)
    sj = lax.broadcasted_iota(jnp.int32, (c4, RW_GW), 1)
    head_rows = si // c == sj // RW_HD
    qi = lax.broadcasted_iota(jnp.int32, (c4, c4), 0)
    qj = lax.broadcasted_iota(jnp.int32, (c4, c4), 1)
    same = qi // c == qj // c
    strict = jnp.logical_and(same, qi % c > qj % c)
    incl = jnp.logical_and(same, qi % c >= qj % c)
    eye = (qi == qj).astype(F32)
    fuse_sq = c4 % LANES == 0

    def stack(x):
        return jnp.where(head_rows, jnp.concatenate([x] * RW_GROUP, axis=0), 0.0)

    def unstack(x):
        out = x[0:c]
        for h in range(1, RW_GROUP):
            out = out + x[h * c:(h + 1) * c]
        return out

    def phase_a(it, carry):
        chains = [(it * ua + u, g) for u in range(ua) for g in range(n_groups)]
        nmats = []
        for ch, g in chains:
            r0 = pl.multiple_of(ch * c, c)
            ln = slice(g * RW_GW, (g + 1) * RW_GW)
            lhs = jnp.concatenate([stack(at_scr[pl.ds(r0, c), ln]), stack(rt_scr[pl.ds(r0, c), ln])], axis=0)
            rhs = jnp.concatenate([bt_scr[pl.ds(r0, c), ln]] * RW_GROUP + [kt_scr[pl.ds(r0, c), ln]] * RW_GROUP,
                                  axis=0)
            quad = _dot_nt(lhs, rhs)
            nmats.append(jnp.where(strict, quad[:c4, :c4], 0.0))
            aak_scr[ch, g] = unstack(jnp.where(strict, quad[:c4, c4:], 0.0))
            arb_scr[ch, g] = unstack(jnp.where(incl, quad[c4:, :c4], 0.0))
            ark_scr[ch, g] = unstack(jnp.where(incl, quad[c4:, c4:], 0.0))
        tinvs = [eye + n for n in nmats]
        npows = [_dot(n, n) for n in nmats]
        span = 2
        while 2 * span < c:
            if fuse_sq:
                boths = [_dot(p, jnp.concatenate([p, t], axis=1)) for p, t in zip(npows, tinvs)]
                tinvs = [t + bo[:, c4:] for t, bo in zip(tinvs, boths)]
                npows = [bo[:, :c4] for bo in boths]
            else:
                tinvs = [t + _dot(p, t) for p, t in zip(npows, tinvs)]
                npows = [_dot(p, p) for p in npows]
            span *= 2
        tinvs = [t + _dot(p, t) for p, t in zip(npows, tinvs)]
        for (ch, g), t in zip(chains, tinvs):
            tcat_scr[ch, g] = unstack(t)
        return carry

    if n_ch // ua == 1:
        phase_a(0, 0)
    else:
        lax.fori_loop(0, n_ch // ua, phase_a, 0)

    def phase_b(it, carry):
        cb = it // (nbb // ub)
        b0 = (it % (nbb // ub)) * ub
        chains = [(b0 + u, g) for u in range(ub) for g in range(n_groups)]
        ops = []
        for bb, g in chains:
            ch = bb * n_cb + cb
            r0 = pl.multiple_of(ch * c, c)
            ln = slice(g * RW_GW, (g + 1) * RW_GW)
            ops.append(dict(bb=bb, g=g, ch=ch, r0=r0, ln=ln, sbd=sbd_scr[bb, g],
                            v_bd=stack(v_scr[pl.ds(r0, c), ln])))
        wmats = [_dot_nt(at_scr[pl.ds(q["r0"], c), q["ln"]], q["sbd"]) + _dot(aak_scr[q["ch"], q["g"]], q["v_bd"])
                 for q in ops]
        us = [_dot(tcat_scr[q["ch"], q["g"]], stack(w)) for q, w in zip(ops, wmats)]
        for q, u in zip(ops, us):
            r0, ln = q["r0"], q["ln"]
            o_scr[pl.ds(r0, c), ln] = (_dot_nt(rt_scr[pl.ds(r0, c), ln], q["sbd"])
                                       + _dot(arb_scr[q["ch"], q["g"]], stack(u))
                                       + _dot(ark_scr[q["ch"], q["g"]], q["v_bd"]))
        for q, u in zip(ops, us):
            r0, ln = q["r0"], q["ln"]
            upd = _dot_tn(u, bp_scr[pl.ds(r0, c), ln]) + _dot_tn(v_scr[pl.ds(r0, c), ln], kp_scr[pl.ds(r0, c), ln])
            sbd_scr[q["bb"], q["g"]] = q["sbd"] * pc_scr[pl.ds(r0, 1), ln] + jnp.where(bd_state, upd, 0.0)
        return carry

    n_it = n_cb * (nbb // ub)
    if n_it == 1:
        phase_b(0, 0)
    else:
        lax.fori_loop(0, n_it, phase_b, 0)

    o = o_scr[...]
    inv_n = 1.0 / RW_HD
    mean = _dot_seg(o, seg) * inv_n
    cen = o - mean
    var = _dot_seg(cen * cen, seg) * inv_n
    on = cen * lax.rsqrt(var + RW_GN_EPS) * ln_g + ln_b
    y_ref[...] = ((on + bonus_scr[...] * v_scr[...]) * gate_scr[...]).reshape(nbb, tt_len, BRANCH_W)

    @pl.when(tt == n_t - 1)
    def _():
        def fin(bb, carry):
            for g in range(n_groups):
                sbd = sbd_scr[bb, g]
                for h in range(RW_GROUP):
                    sout_ref[bb, g * RW_GROUP + h] = sbd[h * RW_HD:(h + 1) * RW_HD, h * RW_HD:(h + 1) * RW_HD]
            return carry

        lax.fori_loop(0, nbb, fin, 0)


def _rwkv(proj3, sh0, s0, mu, wl, vec, *, nbb, tt_len, chunk, s_valid, ua, ub):
    nb, s_len, _ = proj3.shape
    kern = functools.partial(_rwkv_kernel, nbb=nbb, tt_len=tt_len, chunk=chunk, s_valid=s_valid, s_len=s_len,
                             ua=ua, ub=ub)
    st_spec = pl.BlockSpec((nbb, RW_HEADS, RW_HD, RW_HD), lambda b, t: (b, 0, 0, 0))
    full2 = lambda b, t: (0, 0)
    rows = nbb * tt_len
    n_groups = RW_HEADS // RW_GROUP
    mats = pltpu.VMEM((rows // chunk, n_groups, chunk, RW_GROUP * chunk), F32)
    return pl.pallas_call(
        kern,
        grid=(nb // nbb, s_len // tt_len),
        in_specs=[
            pl.BlockSpec((nbb, tt_len, RW_BLOCK), lambda b, t: (b, t, OFF_RW // RW_BLOCK)),
            pl.BlockSpec((nbb, 1, RW_COLS), lambda b, t: (b, 0, 0)),
            st_spec,
            pl.BlockSpec((1, RW_COLS), full2),
            pl.BlockSpec((RW_LORA, 3 * BRANCH_W), full2),
            pl.BlockSpec((SUBLANES, BRANCH_W), full2),
            pl.BlockSpec((2 * rows, rows), full2),
        ],
        out_specs=[pl.BlockSpec((nbb, tt_len, BRANCH_W), lambda b, t: (b, t, 0)), st_spec],
        out_shape=[jax.ShapeDtypeStruct((nb, s_len, BRANCH_W), F32),
                   jax.ShapeDtypeStruct(s0.shape, F32)],
        scratch_shapes=[
            pltpu.VMEM((nbb, n_groups, RW_GW, RW_GW), F32),
            pltpu.VMEM((nbb, 1, RW_COLS), F32),
        ] + [pltpu.VMEM((rows, BRANCH_W), F32)] * 11 + [mats] * 4,
        compiler_params=_params(("parallel", "arbitrary")),
        name="rwkv7",
    )(proj3, sh0, s0, mu, wl, vec, _chunk_cum_matrix(rows, chunk))


def _mix_kernel(x_ref, gts_ref, ya_ref, yb_ref, yc_ref, wb_ref, wo_ref, g_ref, o_ref):
    acc = None
    for n, y_ref in enumerate((ya_ref, yb_ref, yc_ref)):
        up = _dot(y_ref[...], wb_ref[n])
        term = _sigmoid_t(gts_ref[:, n * D_MODEL:(n + 1) * D_MODEL].astype(F32)) * up
        acc = term if acc is None else acc + term
    mix = _dot(acc, wo_ref[...])
    o_ref[...] = x_ref[...] + _rms(mix, g_ref[...])


def _mix(x, proj, ya, yb, yc, wb, wo, g, tm):
    t = x.shape[0]
    row = lambda w: pl.BlockSpec((tm, w), lambda i: (i, 0))
    return pl.pallas_call(
        _mix_kernel,
        grid=(t // tm,),
        in_specs=[row(D_MODEL), row(N_BRANCH * D_MODEL), row(BRANCH_W), row(BRANCH_W), row(BRANCH_W),
                  pl.BlockSpec((N_BRANCH, BRANCH_W, D_MODEL), lambda i: (0, 0, 0)),
                  pl.BlockSpec((D_MODEL, D_MODEL), lambda i: (0, 0)),
                  pl.BlockSpec((1, D_MODEL), lambda i: (0, 0))],
        out_specs=row(D_MODEL),
        out_shape=jax.ShapeDtypeStruct((t, D_MODEL), F32),
        compiler_params=_params(("parallel",)),
        name="branch_mix",
    )(x, proj, ya, yb, yc, wb, wo, g)


def _ffn_kernel(x_ref, p_ref, gpre_ref, wg_ref, wu_ref, wd_ref, gpost_ref, wple_ref, wpg_ref, gple_ref,
                o_ref, h_scr, acc_scr):
    j = pl.program_id(1)

    @pl.when(j == 0)
    def _():
        h_scr[...] = _rms(x_ref[...], gpre_ref[...]).astype(BF16)
        acc_scr[...] = jnp.zeros_like(acc_scr)

    h = h_scr[...]
    gt = jnp.dot(h, wg_ref[...], preferred_element_type=F32)
    up = jnp.dot(h, wu_ref[...], preferred_element_type=F32)
    acc_scr[...] += _dot(gt * _sigmoid_t(gt) * up, wd_ref[...])

    @pl.when(j == pl.num_programs(1) - 1)
    def _():
        x2 = x_ref[...] + _rms(acc_scr[...], gpost_ref[...])
        ple = _dot(p_ref[...], wple_ref[...]) * _sigmoid_t(_dot(x2, wpg_ref[...]))
        o_ref[...] = x2 + _rms(ple, gple_ref[...])


def _ffn(x, p, gpre, wg, wu, wd, gpost, wple, wpg, gple, tm, n_split):
    t = x.shape[0]
    d_ff = wg.shape[1]
    tf = d_ff // n_split
    vecspec = pl.BlockSpec((1, D_MODEL), lambda i, j: (0, 0))
    return pl.pallas_call(
        _ffn_kernel,
        grid=(t // tm, n_split),
        in_specs=[
            pl.BlockSpec((tm, D_MODEL), lambda i, j: (i, 0)),
            pl.BlockSpec((tm, PLE_DIM), lambda i, j: (i, 0)),
            vecspec,
            pl.BlockSpec((D_MODEL, tf), lambda i, j: (0, j)),
            pl.BlockSpec((D_MODEL, tf), lambda i, j: (0, j)),
            pl.BlockSpec((tf, D_MODEL), lambda i, j: (j, 0)),
            vecspec,
            pl.BlockSpec((PLE_DIM, D_MODEL), lambda i, j: (0, 0)),
            pl.BlockSpec((D_MODEL, D_MODEL), lambda i, j: (0, 0)),
            vecspec,
        ],
        out_specs=pl.BlockSpec((tm, D_MODEL), lambda i, j: (i, 0)),
        out_shape=jax.ShapeDtypeStruct((t, D_MODEL), F32),
        scratch_shapes=[pltpu.VMEM((tm, D_MODEL), BF16), pltpu.VMEM((tm, D_MODEL), F32)],
        compiler_params=_params(("parallel", "arbitrary")),
        name="ffn_ple",
    )(x, p, gpre, wg, wu, wd, gpost, wple, wpg, gple)


def _block_diag(w):
    n, r, c = w.shape
    eye = jnp.eye(n, dtype=w.dtype)
    return (eye[:, None, :, None] * w[:, :, None, :]).reshape(n * r, n * c)


def _prep_layer(i, W):
    win = W["w_in"][i]
    segs = jnp.split(win, [2 * BRANCH_W, 2 * BRANCH_W + 4 * BRANCH_W, 2 * BRANCH_W + 4 * BRANCH_W + RW_COLS], axis=1)
    w_lru, w_hg, w_rw, w_gates = segs
    pad = jnp.zeros((D_MODEL, PROJ_COLS - OFF_RW - RW_COLS), win.dtype)
    w_in = jnp.concatenate([w_gates, w_lru, w_hg, w_rw, pad], axis=1).astype(BF16)
    lora = jnp.zeros((RW_LORA, 3 * BRANCH_W), F32)
    lora = lora.at[0:RW_LORA_W, 0:BRANCH_W].set(W["rw_w_up"][i])
    lora = lora.at[RW_LORA_W:RW_LORA_W + RW_LORA_A, BRANCH_W:2 * BRANCH_W].set(W["rw_a_up"][i])
    lora = lora.at[RW_LORA_W + RW_LORA_A:, 2 * BRANCH_W:].set(W["rw_g_up"][i])
    vec = jnp.stack([W["rw_w0"][i], W["rw_a0"][i], W["rw_k_k"][i], W["rw_k_a"][i],
                     W["rw_r_k"][i].reshape(BRANCH_W), W["rw_ln_g"][i], W["rw_ln_b"][i],
                     jnp.zeros((BRANCH_W,), F32)])
    row = lambda name: W[name][i].reshape(1, -1)
    return dict(
        w_in=w_in, norm_pre_mix=row("norm_pre_mix"),
        conv_w=W["conv_w"][i], conv_b=row("conv_b"),
        lru_w=jnp.concatenate([_block_diag(W["lru_wa"][i]), _block_diag(W["lru_wx"][i])], axis=1).astype(BF16),
        lru_b=jnp.stack([W["lru_ba"][i], W["lru_bx"][i]]), lru_lambda=row("lru_lambda"),
        hg_norm_g=row("hg_norm_g"),
        rw_mu=row("rw_mu"), rw_lora=lora.astype(BF16), rw_vec=vec,
        w_branch=W["w_branch"][i].astype(BF16), w_out=W["w_out"][i].astype(BF16),
        norm_post_mix=row("norm_post_mix"), norm_pre_ffn=row("norm_pre_ffn"),
        w_ffn_gate=W["w_ffn_gate"][i].astype(BF16), w_ffn_up=W["w_ffn_up"][i].astype(BF16),
        w_ffn_down=W["w_ffn_down"][i].astype(BF16), norm_post_ffn=row("norm_post_ffn"),
        w_ple=W["w_ple"][i].astype(BF16), w_ple_gate=W["w_ple_gate"][i].astype(BF16), norm_ple=row("norm_ple"),
    )


def _tiles(nb, s_len):
    t = nb * s_len
    tm_in = min(t, 1024)
    tm_tok = min(t, 512)
    if s_len >= 512:
        return dict(tm_in=tm_in, tn_in=2048, proj_dtype=BF16, tm_mix=tm_tok, tm_ffn=tm_tok, ffn_split=2,
                    lru_tt=128,
                    hg=dict(nbb=2, tt_len=256, chunk=64, ub=2),
                    rw=dict(nbb=4, tt_len=128, chunk=64, ua=4, ub=4))
    return dict(tm_in=tm_in, tn_in=2048, proj_dtype=F32, tm_mix=tm_tok, tm_ffn=tm_tok, ffn_split=2,
                lru_tt=s_len,
                hg=dict(nbb=min(nb, 16), tt_len=s_len, chunk=s_len, ub=4),
                rw=dict(nbb=min(nb, 16), tt_len=s_len, chunk=s_len, ua=8, ub=8))


def _run_trunk(x3, p4, states, layers, lbraw, *, s_valid, pos0_is_zero):
    nb, s_len, _ = x3.shape
    t = nb * s_len
    plan = _tiles(nb, s_len)
    conv0, lru0, hg0, rw0, sh0 = states
    x = x3.reshape(t, D_MODEL)
    new = ([], [], [], [], [])
    for i, L in enumerate(layers):
        proj = _in_proj(x, L["norm_pre_mix"], L["w_in"], plan["tm_in"], plan["tn_in"], plan["proj_dtype"])
        proj3 = proj.reshape(nb, s_len, PROJ_COLS)
        prev8 = jnp.pad(conv0[i], ((0, 0), (SUBLANES - (CONV_W - 1), 0), (0, 0)))
        ya, nlru = _lru(proj3, prev8, lru0[i], L["conv_w"], L["conv_b"], L["lru_w"], L["lru_b"], L["lru_lambda"],
                        tt_len=plan["lru_tt"], s_valid=s_valid, pos0_is_zero=pos0_is_zero)
        yb, nhg = _hgrn(proj3, hg0[i], lbraw, L["hg_norm_g"], s_valid=s_valid, layer=i, **plan["hg"])
        yc, nrw = _rwkv(proj3, sh0[i].reshape(nb, 1, RW_COLS), rw0[i], L["rw_mu"], L["rw_lora"], L["rw_vec"],
                        s_valid=s_valid, **plan["rw"])
        x = _mix(x, proj, ya.reshape(t, BRANCH_W), yb.reshape(t, BRANCH_W), yc.reshape(t, BRANCH_W),
                 L["w_branch"], L["w_out"], L["norm_post_mix"], plan["tm_mix"])
        x = _ffn(x, p4[i].reshape(t, PLE_DIM), L["norm_pre_ffn"], L["w_ffn_gate"], L["w_ffn_up"], L["w_ffn_down"],
                 L["norm_post_ffn"], L["w_ple"], L["w_ple_gate"], L["norm_ple"], plan["tm_ffn"], plan["ffn_split"])
        nconv = proj3[:, s_valid - (CONV_W - 1):s_valid, OFF_LRU:OFF_LRU + BRANCH_W].astype(F32)
        nsh = proj3[:, s_valid - 1, OFF_RW:OFF_RW + RW_COLS].astype(F32)
        for lst, val in zip(new, (nconv, nlru, nhg, nrw, nsh)):
            lst.append(val)
    return x.reshape(nb, s_len, D_MODEL), tuple(jnp.stack(l) for l in new)


def kernel(x_prompt, x_sample, p_prompt, p_sample, state_conv_a, state_lru_a, state_hgrn, state_rwkv, state_shift_c, norm_pre_mix, w_in, conv_w, conv_b, lru_wa, lru_ba, lru_wx, lru_bx, lru_lambda, hg_lower_bounds, hg_norm_g, rw_mu, rw_w0, rw_w_up, rw_a0, rw_a_up, rw_g_up, rw_k_k, rw_k_a, rw_r_k, rw_ln_g, rw_ln_b, w_branch, w_out, norm_post_mix, norm_pre_ffn, w_ffn_gate, w_ffn_up, w_ffn_down, norm_post_ffn, w_ple, w_ple_gate, norm_ple):
    W = dict(norm_pre_mix=norm_pre_mix, w_in=w_in, conv_w=conv_w, conv_b=conv_b, lru_wa=lru_wa, lru_ba=lru_ba,
             lru_wx=lru_wx, lru_bx=lru_bx, lru_lambda=lru_lambda, hg_norm_g=hg_norm_g, rw_mu=rw_mu, rw_w0=rw_w0,
             rw_w_up=rw_w_up, rw_a0=rw_a0, rw_a_up=rw_a_up, rw_g_up=rw_g_up, rw_k_k=rw_k_k, rw_k_a=rw_k_a,
             rw_r_k=rw_r_k, rw_ln_g=rw_ln_g, rw_ln_b=rw_ln_b, w_branch=w_branch, w_out=w_out,
             norm_post_mix=norm_post_mix, norm_pre_ffn=norm_pre_ffn, w_ffn_gate=w_ffn_gate, w_ffn_up=w_ffn_up,
             w_ffn_down=w_ffn_down, norm_post_ffn=norm_post_ffn, w_ple=w_ple, w_ple_gate=w_ple_gate, norm_ple=norm_ple)
    depth = w_in.shape[0]
    layers = [_prep_layer(i, W) for i in range(depth)]
    lbraw = hg_lower_bounds.astype(F32)

    bp, sp, _ = x_prompt.shape
    zeros = lambda *shape: jnp.zeros((depth, bp) + shape, F32)
    zero_states = (zeros(CONV_W - 1, BRANCH_W), zeros(BRANCH_W), zeros(HG_HEADS, HG_D, HG_D),
                   zeros(RW_HEADS, RW_HD, RW_HD), zeros(RW_COLS))
    y_prompt, st_p = _run_trunk(x_prompt, p_prompt, zero_states, layers, lbraw, s_valid=sp, pos0_is_zero=True)

    bs, ss, _ = x_sample.shape
    ss_pad = -(-ss // SUBLANES) * SUBLANES
    xs = jnp.pad(x_sample, ((0, 0), (0, ss_pad - ss), (0, 0)))
    ps = jnp.pad(p_sample, ((0, 0), (0, 0), (0, ss_pad - ss), (0, 0)))
    y_sample, st_s = _run_trunk(xs, ps, (state_conv_a, state_lru_a, state_hgrn, state_rwkv, state_shift_c),
                                layers, lbraw, s_valid=ss, pos0_is_zero=False)
    return (y_prompt, y_sample[:, :ss]) + st_p + st_s
```

```python
import functools
import math

import jax
import jax.numpy as jnp
from jax import lax
from jax.experimental import pallas as pl
from jax.experimental.pallas import tpu as pltpu

F32 = jnp.float32
BF16 = jnp.bfloat16

D_MODEL = 1024
BRANCH_W = 512
N_BRANCH = 3
LRU_BLOCKS = 8
LRU_BW = BRANCH_W // LRU_BLOCKS
CONV_W = 4
LRU_C = 8.0
HG_HEADS = 4
HG_D = BRANCH_W // HG_HEADS
HG_F_MIN = 1e-20
RW_HD = 64
RW_HEADS = BRANCH_W // RW_HD
RW_LORA_W = 64
RW_LORA_A = 64
RW_LORA_G = 128
RW_LORA = RW_LORA_W + RW_LORA_A + RW_LORA_G
RW_GN_EPS = 64e-5
RW_COLS = 3 * BRANCH_W + RW_LORA
PLE_DIM = 256
EPS = 1e-6

SUBLANES = 8
LANES = 128
MXU_DIM = 256
VMEM_LIMIT = 56 * 1024 * 1024

PROJ_COLS = 8192
OFF_GATES = 0
OFF_LRU = N_BRANCH * D_MODEL
OFF_HG = OFF_LRU + 2 * BRANCH_W
OFF_RW = OFF_HG + 4 * BRANCH_W
RW_BLOCK = PROJ_COLS - OFF_RW

RW_GROUP = LANES // RW_HD
RW_GW = RW_GROUP * RW_HD


def _params(sem):
    return pltpu.CompilerParams(dimension_semantics=sem, vmem_limit_bytes=VMEM_LIMIT)


def _rms(x, g):
    return x * lax.rsqrt(jnp.mean(x * x, axis=-1, keepdims=True) + EPS) * g


def _sigmoid(x):
    return 1.0 / (1.0 + jnp.exp(-x))


def _sigmoid_t(x):
    return 0.5 * jnp.tanh(0.5 * x) + 0.5


def _softplus(x):
    return jnp.maximum(x, 0.0) + jnp.log1p(jnp.exp(-jnp.abs(x)))


def _softplus_abs(x):
    return jnp.maximum(x, 0.0) + jnp.log(1.0 + jnp.exp(-jnp.abs(x)))


def _dot(a, b):
    return jnp.dot(a.astype(BF16), b.astype(BF16), preferred_element_type=F32)


def _dot_nt(a, b):
    return lax.dot_general(a.astype(BF16), b.astype(BF16), (((1,), (1,)), ((), ())),
                           preferred_element_type=F32)


def _dot_tn(a, b):
    return lax.dot_general(a.astype(BF16), b.astype(BF16), (((0,), (0,)), ((), ())),
                           preferred_element_type=F32)


def _split2(x):
    hi = x.astype(BF16)
    return hi, (x - hi.astype(F32)).astype(BF16)


def _dot_seg(x, m01):
    hi, lo = _split2(x)
    m = m01.astype(BF16)
    return jnp.dot(lo, m, preferred_element_type=F32) + jnp.dot(hi, m, preferred_element_type=F32)


def _dot_cum(m01, x):
    hi, lo = _split2(x)
    m = m01.astype(BF16)
    return jnp.dot(m, lo, preferred_element_type=F32) + jnp.dot(m, hi, preferred_element_type=F32)


def _seg_ones(width, seg):
    i = lax.broadcasted_iota(jnp.int32, (width, width), 0)
    j = lax.broadcasted_iota(jnp.int32, (width, width), 1)
    return (i // seg == j // seg).astype(F32)


def _in_proj_kernel(x_ref, g_ref, w_ref, o_ref, h_scr):
    @pl.when(pl.program_id(1) == 0)
    def _():
        h_scr[...] = _rms(x_ref[...], g_ref[...]).astype(BF16)

    o_ref[...] = jnp.dot(h_scr[...], w_ref[...], preferred_element_type=F32).astype(o_ref.dtype)


def _in_proj(x, g, w, tm, tn, out_dtype):
    t = x.shape[0]
    return pl.pallas_call(
        _in_proj_kernel,
        grid=(t // tm, PROJ_COLS // tn),
        in_specs=[
            pl.BlockSpec((tm, D_MODEL), lambda i, j: (i, 0)),
            pl.BlockSpec((1, D_MODEL), lambda i, j: (0, 0)),
            pl.BlockSpec((D_MODEL, tn), lambda i, j: (0, j)),
        ],
        out_specs=pl.BlockSpec((tm, tn), lambda i, j: (i, j)),
        out_shape=jax.ShapeDtypeStruct((t, PROJ_COLS), out_dtype),
        scratch_shapes=[pltpu.VMEM((tm, D_MODEL), BF16)],
        compiler_params=_params(("parallel", "arbitrary")),
        name="in_proj",
    )(x, g, w)


def _lru_kernel(xa_ref, ga_ref, prev8_ref, h0_ref, cw_ref, cb_ref, w_ref, bab_ref, lam_ref,
                y_ref, hout_ref, prev_scr, h_scr, *, nb, tt_len, s_valid, pos0_is_zero):
    tt = pl.program_id(0)
    rows = nb * tt_len

    @pl.when(tt == 0)
    def _():
        prev_scr[...] = prev8_ref[...]
        h_scr[...] = h0_ref[...]

    xa = xa_ref[...].astype(F32)
    prev8 = prev_scr[...]
    t8 = lax.broadcasted_iota(jnp.int32, (1, SUBLANES, 1), 1)
    cw = cw_ref[...]
    xc = cb_ref[...][None] + cw[CONV_W - 1][None, None] * xa
    for j in range(1, CONV_W):
        rolled = pltpu.roll(xa, j, 1)
        head = jnp.where(t8 < j, pltpu.roll(prev8, j, 1), rolled[:, :SUBLANES])
        shifted = head if tt_len == SUBLANES else jnp.concatenate([head, rolled[:, SUBLANES:]], axis=1)
        xc = xc + cw[CONV_W - 1 - j][None, None] * shifted
    prev_scr[...] = xa[:, tt_len - SUBLANES:, :]

    xc2 = xc.reshape(rows, BRANCH_W)
    z = _dot(xc2, w_ref[...])
    bab = bab_ref[...]
    r = _sigmoid_t(z[:, :BRANCH_W] + bab[0:1])
    i = _sigmoid_t(z[:, BRANCH_W:] + bab[1:2])
    log_a = (-LRU_C) * r * _softplus(-lam_ref[...])
    a = jnp.exp(log_a)
    mult = jnp.sqrt(jnp.maximum(1.0 - a * a, 0.0))
    t_in = lax.broadcasted_iota(jnp.int32, (rows, 1), 0) % tt_len
    if pos0_is_zero:
        mult = jnp.where(jnp.logical_and(tt == 0, t_in == 0), 1.0, mult)
    b = xc2 * i * mult
    valid = (tt * tt_len + t_in) < s_valid
    a3 = jnp.where(valid, a, 1.0).reshape(nb, tt_len, BRANCH_W)
    b3 = jnp.where(valid, b, 0.0).reshape(nb, tt_len, BRANCH_W)

    tmod = lax.broadcasted_iota(jnp.int32, (1, tt_len, 1), 1) % SUBLANES
    d = 1
    while d < SUBLANES:
        keep = tmod >= d
        b3 = a3 * jnp.where(keep, pltpu.roll(b3, d, 1), 0.0) + b3
        a3 = a3 * jnp.where(keep, pltpu.roll(a3, d, 1), 1.0)
        d *= 2
    h_in = h_scr[...][:, None, :]
    blocks = []
    for blk in range(tt_len // SUBLANES):
        rows8 = slice(blk * SUBLANES, (blk + 1) * SUBLANES)
        hb = a3[:, rows8, :] * h_in + b3[:, rows8, :]
        blocks.append(hb)
        h_in = hb[:, SUBLANES - 1:SUBLANES, :]
    hh = blocks[0] if len(blocks) == 1 else jnp.concatenate(blocks, axis=1)
    h = h_in.reshape(nb, BRANCH_W)
    h_scr[...] = h
    hout_ref[...] = h

    ga = ga_ref[...].astype(F32)
    gelu = 0.5 * ga * (1.0 + jnp.tanh(math.sqrt(2.0 / math.pi) * (ga + 0.044715 * ga * ga * ga)))
    y_ref[...] = (hh * gelu).astype(y_ref.dtype)


def _lru(proj3, prev8, h0, cw, cb, w, bab, lam, *, tt_len, s_valid, pos0_is_zero):
    nb, s_len, _ = proj3.shape
    blk = OFF_LRU // BRANCH_W
    kern = functools.partial(_lru_kernel, nb=nb, tt_len=tt_len, s_valid=s_valid, pos0_is_zero=pos0_is_zero)
    full2 = lambda t: (0, 0)
    return pl.pallas_call(
        kern,
        grid=(s_len // tt_len,),
        in_specs=[
            pl.BlockSpec((nb, tt_len, BRANCH_W), lambda t: (0, t, blk)),
            pl.BlockSpec((nb, tt_len, BRANCH_W), lambda t: (0, t, blk + 1)),
            pl.BlockSpec((nb, SUBLANES, BRANCH_W), lambda t: (0, 0, 0)),
            pl.BlockSpec((nb, BRANCH_W), full2),
            pl.BlockSpec((CONV_W, BRANCH_W), full2),
            pl.BlockSpec((1, BRANCH_W), full2),
            pl.BlockSpec((BRANCH_W, 2 * BRANCH_W), full2),
            pl.BlockSpec((2, BRANCH_W), full2),
            pl.BlockSpec((1, BRANCH_W), full2),
        ],
        out_specs=[
            pl.BlockSpec((nb, tt_len, BRANCH_W), lambda t: (0, t, 0)),
            pl.BlockSpec((nb, BRANCH_W), full2),
        ],
        out_shape=[
            jax.ShapeDtypeStruct((nb, s_len, BRANCH_W), proj3.dtype),
            jax.ShapeDtypeStruct((nb, BRANCH_W), F32),
        ],
        scratch_shapes=[
            pltpu.VMEM((nb, SUBLANES, BRANCH_W), F32),
            pltpu.VMEM((nb, BRANCH_W), F32),
        ],
        compiler_params=_params(("arbitrary",)),
        name="rglru",
    )(proj3, proj3, prev8, h0, cw, cb, w, bab, lam)


def _hg_diag_blocks(qh, kh, vh, bch):
    c = qh.shape[0]
    nblk = c // SUBLANES
    q3 = qh.reshape(nblk, SUBLANES, HG_D)
    k3 = kh.reshape(nblk, SUBLANES, HG_D)
    v3 = vh.reshape(nblk, SUBLANES, HG_D)
    b3 = bch.reshape(nblk, SUBLANES, HG_D)
    tin = lax.broadcasted_iota(jnp.int32, (1, SUBLANES, 1), 1)
    o3 = jnp.zeros((nblk, SUBLANES, HG_D), F32)
    for s in range(SUBLANES):
        dec = jnp.exp(jnp.minimum(b3 - b3[:, s:s + 1, :], 0.0))
        w = jnp.sum(q3 * k3[:, s:s + 1, :] * dec, axis=-1, keepdims=True)
        w = jnp.where(tin >= s, w, 0.0)
        o3 = o3 + w * v3[:, s:s + 1, :]
    return o3.reshape(c, HG_D)


def _hg_level_refs(bch, h, c):
    gq, gk = [], []
    zero = jnp.zeros((h, HG_D), F32)
    for j in range(c // h):
        if j % 2 == 1:
            gq.append(jnp.broadcast_to(bch[j * h - 1:j * h, :], (h, HG_D)))
            gk.append(zero)
        else:
            gq.append(zero)
            gk.append(jnp.broadcast_to(bch[(j + 1) * h - 1:(j + 1) * h, :], (h, HG_D)))
    return jnp.concatenate(gq, axis=0), jnp.concatenate(gk, axis=0)


def _hgrn_kernel(q_ref, f_ref, v_ref, g_ref, s0_ref, lbraw_ref, ng_ref, cum_ref, sbuf_ref, y_ref, sout_ref,
                 st_scr, qs_scr, k_scr, v_scr, bc_scr, qe_scr, kh_scr, et_scr, o_scr,
                 *, nbb, tt_len, chunk, s_valid, s_len, layer, ub):
    tt = pl.program_id(1)
    n_t = pl.num_programs(1)
    c = chunk
    rows = nbb * tt_len
    n_cb = tt_len // c

    raw = lbraw_ref[...]
    ex = jnp.exp(raw - jnp.max(raw, axis=0, keepdims=True))
    sm = ex / jnp.sum(ex, axis=0, keepdims=True)
    lb = jnp.zeros((1, BRANCH_W), F32)
    for l in range(1, layer + 1):
        lb = lb + sm[l:l + 1]

    @pl.when(tt == 0)
    def _():
        def init(bb, carry):
            for hd in range(HG_HEADS):
                st_scr[bb, hd] = s0_ref[bb, hd].T
            return carry

        lax.fori_loop(0, nbb, init, 0)

    q = q_ref[...].astype(F32).reshape(rows, BRANCH_W)
    fp = f_ref[...].astype(F32).reshape(rows, BRANCH_W)
    sg = _sigmoid(fp)
    f = lb + (1.0 - lb) * sg
    k = (1.0 - lb) * (1.0 - sg)
    logf = jnp.log(jnp.maximum(f, HG_F_MIN))
    if s_valid < s_len:
        t_in = lax.broadcasted_iota(jnp.int32, (rows, 1), 0) % tt_len
        valid = (tt * tt_len + t_in) < s_valid
        k = jnp.where(valid, k, 0.0)
        logf = jnp.where(valid, logf, 0.0)
    lws = _dot_cum(cum_ref[...], logf)
    bc = lws[:rows]
    qs = q * _sigmoid_t(q)
    qs_scr[...] = qs
    k_scr[...] = k
    v_scr[...] = v_ref[...].astype(F32).reshape(rows, BRANCH_W)
    bc_scr[...] = bc
    qe_scr[...] = qs * jnp.exp(bc)
    kh_scr[...] = k * jnp.exp(lws[rows:] - bc)
    et_scr[...] = jnp.exp(lws[rows:])

    ti = lax.broadcasted_iota(jnp.int32, (c, 1), 0)
    ii = lax.broadcasted_iota(jnp.int32, (c, c), 0)
    jj = lax.broadcasted_iota(jnp.int32, (c, c), 1)
    levels = []
    h = c // 2
    while h >= SUBLANES:
        odd = (ti // h) % 2 == 1
        pair = jnp.logical_and(ii // (2 * h) == jj // (2 * h),
                               jnp.logical_and((ii // h) % 2 == 1, (jj // h) % 2 == 0))
        levels.append((h, odd, pair))
        h //= 2

    def step(it, carry):
        cb = it // (nbb // ub)
        b0 = (it % (nbb // ub)) * ub
        chains = []
        for u in range(ub):
            r0 = pl.multiple_of((b0 + u) * tt_len + cb * c, c)
            for hd in range(HG_HEADS):
                ln = slice(hd * HG_D, (hd + 1) * HG_D)
                chains.append(dict(bb=b0 + u, hd=hd, r0=r0, ln=ln, st=st_scr[b0 + u, hd],
                                   qh=qs_scr[pl.ds(r0, c), ln], kh=k_scr[pl.ds(r0, c), ln],
                                   vh=v_scr[pl.ds(r0, c), ln], bch=bc_scr[pl.ds(r0, c), ln]))
        outs = [_dot_nt(qe_scr[pl.ds(x["r0"], c), x["ln"]], x["st"]) for x in chains]
        if levels:
            amats = []
            for x in chains:
                amat = None
                for (h, odd, pair) in levels:
                    gq, gk = _hg_level_refs(x["bch"], h, c)
                    qt = jnp.where(odd, x["qh"] * jnp.exp(jnp.where(odd, x["bch"] - gq, 0.0)), 0.0)
                    kt = jnp.where(odd, 0.0, x["kh"] * jnp.exp(jnp.where(odd, 0.0, gk - x["bch"])))
                    term = jnp.where(pair, _dot_nt(qt, kt), 0.0)
                    amat = term if amat is None else amat + term
                amats.append(amat)
            outs = [o + _dot(a, x["vh"]) for o, a, x in zip(outs, amats, chains)]
        for o, x in zip(outs, chains):
            o_scr[pl.ds(x["r0"], c), x["ln"]] = o + _hg_diag_blocks(x["qh"], x["kh"], x["vh"], x["bch"])
        for x in chains:
            upd = _dot_tn(x["vh"], kh_scr[pl.ds(x["r0"], c), x["ln"]])
            st_scr[x["bb"], x["hd"]] = x["st"] * et_scr[pl.ds(x["r0"], 1), x["ln"]] + upd
        return carry

    n_it = n_cb * (nbb // ub)
    if n_it == 1:
        step(0, 0)
    else:
        lax.fori_loop(0, n_it, step, 0)

    g = g_ref[...].astype(F32).reshape(rows, BRANCH_W)
    ng = ng_ref[...]
    outs = []
    for hd in range(HG_HEADS):
        ln = slice(hd * HG_D, (hd + 1) * HG_D)
        o = o_scr[:, ln]
        outs.append(o * lax.rsqrt(jnp.mean(o * o, axis=-1, keepdims=True) + EPS) * ng[:, ln])
    y = jnp.concatenate(outs, axis=-1) * (g * _sigmoid_t(g))
    y_ref[...] = y.reshape(nbb, tt_len, BRANCH_W).astype(y_ref.dtype)

    @pl.when(tt == n_t - 1)
    def _():
        def fin(bb, carry):
            for hd in range(HG_HEADS):
                sout_ref[bb, hd] = st_scr[bb, hd].T
            return carry

        lax.fori_loop(0, nbb, fin, 0)


def _chunk_cum_matrix(rows, c):
    i = jnp.arange(rows)[:, None]
    j = jnp.arange(rows)[None, :]
    same = i // c == j // c
    return jnp.concatenate([same & (i >= j), same], axis=0).astype(BF16)


def _hgrn(proj3, s0, sbuf, lbraw, ng, *, nbb, tt_len, chunk, s_valid, layer, ub):
    nb, s_len, _ = proj3.shape
    blk = OFF_HG // BRANCH_W
    rows = nbb * tt_len
    kern = functools.partial(_hgrn_kernel, nbb=nbb, tt_len=tt_len, chunk=chunk, s_valid=s_valid, s_len=s_len,
                             layer=layer, ub=ub)
    seq = lambda k: pl.BlockSpec((nbb, tt_len, BRANCH_W), lambda b, t, k=k: (b, t, blk + k))
    st_spec = pl.BlockSpec((None, nbb, HG_HEADS, HG_D, HG_D), lambda b, t: (layer, b, 0, 0, 0))
    return pl.pallas_call(
        kern,
        grid=(nb // nbb, s_len // tt_len),
        in_specs=[seq(0), seq(1), seq(2), seq(3), st_spec,
                  pl.BlockSpec(lbraw.shape, lambda b, t: (0, 0)),
                  pl.BlockSpec((1, BRANCH_W), lambda b, t: (0, 0)),
                  pl.BlockSpec((2 * rows, rows), lambda b, t: (0, 0)),
                  pl.BlockSpec(memory_space=pl.ANY)],
        out_specs=[pl.BlockSpec((nbb, tt_len, BRANCH_W), lambda b, t: (b, t, 0)), st_spec],
        out_shape=[jax.ShapeDtypeStruct((nb, s_len, BRANCH_W), proj3.dtype),
                   jax.ShapeDtypeStruct(sbuf.shape, F32)],
        input_output_aliases={8: 1},
        scratch_shapes=[pltpu.VMEM((nbb, HG_HEADS, HG_D, HG_D), F32)]
        + [pltpu.VMEM((rows, BRANCH_W), F32)] * 8,
        compiler_params=_params(("parallel", "arbitrary")),
        name="hgrn2",
    )(proj3, proj3, proj3, proj3, s0, lbraw, ng, _chunk_cum_matrix(rows, chunk), sbuf)


def _rwkv_kernel(c_ref, sh0_ref, s0_ref, mu_ref, wl_ref, vec_ref, cum_ref, sbuf_ref, y_ref, sout_ref,
                 sbd_scr, carry_scr, at_scr, rt_scr, bt_scr, kt_scr, bp_scr, kp_scr, v_scr, pc_scr,
                 bonus_scr, gate_scr, o_scr, tcat_scr, aak_scr, arb_scr, ark_scr,
                 *, nbb, tt_len, chunk, s_valid, s_len, ua, ub):
    tt = pl.program_id(1)
    n_t = pl.num_programs(1)
    c = chunk
    c4 = RW_GROUP * c
    n_groups = RW_HEADS // RW_GROUP
    rows = nbb * tt_len
    n_cb = tt_len // c
    n_ch = rows // c

    vec = vec_ref[...]
    w0, a0, k_k, k_a, r_k, ln_g, ln_b = [vec[i:i + 1] for i in range(7)]
    seg = _seg_ones(BRANCH_W, RW_HD)
    gi_ = lax.broadcasted_iota(jnp.int32, (RW_GW, RW_GW), 0)
    gj_ = lax.broadcasted_iota(jnp.int32, (RW_GW, RW_GW), 1)
    bd_state = gi_ // RW_HD == gj_ // RW_HD

    @pl.when(tt == 0)
    def _():
        carry_scr[...] = sh0_ref[...]

        def init(bb, carry):
            for g in range(n_groups):
                blk = jnp.concatenate([s0_ref[bb, g * RW_GROUP + h] for h in range(RW_GROUP)], axis=0)
                sbd_scr[bb, g] = jnp.where(bd_state, jnp.concatenate([blk] * RW_GROUP, axis=1), 0.0)
            return carry

        lax.fori_loop(0, nbb, init, 0)

    cc3 = c_ref[:, :, :RW_COLS].astype(F32)
    t3 = lax.broadcasted_iota(jnp.int32, (1, tt_len, 1), 1)
    prev3 = jnp.where(t3 == 0, carry_scr[...], pltpu.roll(cc3, 1, 1))
    carry_scr[...] = cc3[:, tt_len - 1:tt_len, :]
    xm = (cc3 + (prev3 - cc3) * mu_ref[...][None]).reshape(rows, RW_COLS)
    r = xm[:, 0:BRANCH_W]
    k = xm[:, BRANCH_W:2 * BRANCH_W]
    v = xm[:, 2 * BRANCH_W:3 * BRANCH_W]
    lo = xm[:, 3 * BRANCH_W:]
    lane_l = lax.broadcasted_iota(jnp.int32, (1, RW_LORA), 1)
    act = jnp.where(lane_l < RW_LORA_W, jnp.tanh(lo),
                    jnp.where(lane_l < RW_LORA_W + RW_LORA_A, lo, _sigmoid_t(lo)))
    z = _dot(act, wl_ref[...])
    w = -_softplus_abs(-(w0 + z[:, 0:BRANCH_W])) - 0.5
    ld = -jnp.exp(w)
    a = _sigmoid_t(a0 + z[:, BRANCH_W:2 * BRANCH_W])
    kk = k * k_k
    kbar = k * (1.0 + (a - 1.0) * k_a)
    sums = _dot_seg(jnp.concatenate([kk * kk, r * kbar * r_k], axis=0), seg)
    kap = kk / jnp.maximum(jnp.sqrt(sums[:rows]), 1e-12)
    if s_valid < s_len:
        t_in = lax.broadcasted_iota(jnp.int32, (rows, 1), 0) % tt_len
        valid = (tt * tt_len + t_in) < s_valid
        ld = jnp.where(valid, ld, 0.0)
        kap = jnp.where(valid, kap, 0.0)
        kbar = jnp.where(valid, kbar, 0.0)
    lws = _dot_cum(cum_ref[...], ld)
    lw = lws[:rows]
    back = lws[rows:] - lw
    e_in = jnp.exp(lw)
    e_neg = jnp.exp(-lw)
    e_back = jnp.exp(back)
    at_scr[...] = -kap * jnp.exp(lw - ld)
    rt_scr[...] = r * e_in
    bt_scr[...] = kap * a * e_neg
    kt_scr[...] = kbar * e_neg
    bp_scr[...] = kap * a * e_back
    kp_scr[...] = kbar * e_back
    v_scr[...] = v
    pc_scr[...] = jnp.exp(lws[rows:])
    bonus_scr[...] = sums[rows:]
    gate_scr[...] = z[:, 2 * BRANCH_W:]

    si = lax.broadcasted_iota(jnp.int32, (c4, RW_GW), 0)
    sj = lax.broadcasted_iota(jnp.int32, (c4, RW_GW), 1)
    head_rows = si // c == sj // RW_HD
    qi = lax.broadcasted_iota(jnp.int32, (c4, c4), 0)
    qj = lax.broadcasted_iota(jnp.int32, (c4, c4), 1)
    same = qi // c == qj // c
    strict = jnp.logical_and(same, qi % c > qj % c)
    incl = jnp.logical_and(same, qi % c >= qj % c)
    eye = (qi == qj).astype(F32)
    fuse_sq = c4 % LANES == 0

    def stack(x):
        return jnp.where(head_rows, jnp.concatenate([x] * RW_GROUP, axis=0), 0.0)

    def unstack(x):
        out = x[0:c]
        for h in range(1, RW_GROUP):
            out = out + x[h * c:(h + 1) * c]
        return out

    def phase_a(it, carry):
        chains = [(it * ua + u, g) for u in range(ua) for g in range(n_groups)]
        nmats = []
        for ch, g in chains:
            r0 = pl.multiple_of(ch * c, c)
            ln = slice(g * RW_GW, (g + 1) * RW_GW)
            lhs = jnp.concatenate([stack(at_scr[pl.ds(r0, c), ln]), stack(rt_scr[pl.ds(r0, c), ln])], axis=0)
            rhs = jnp.concatenate([bt_scr[pl.ds(r0, c), ln]] * RW_GROUP + [kt_scr[pl.ds(r0, c), ln]] * RW_GROUP,
                                  axis=0)
            quad = _dot_nt(lhs, rhs)
            nmats.append(jnp.where(strict, quad[:c4, :c4], 0.0))
            aak_scr[ch, g] = unstack(jnp.where(strict, quad[:c4, c4:], 0.0))
            arb_scr[ch, g] = unstack(jnp.where(incl, quad[c4:, :c4], 0.0))
            ark_scr[ch, g] = unstack(jnp.where(incl, quad[c4:, c4:], 0.0))
        tinvs = [eye + n for n in nmats]
        npows = [_dot(n, n) for n in nmats]
        span = 2
        while 2 * span < c:
            if fuse_sq:
                boths = [_dot(p, jnp.concatenate([p, t], axis=1)) for p, t in zip(npows, tinvs)]
                tinvs = [t + bo[:, c4:] for t, bo in zip(tinvs, boths)]
                npows = [bo[:, :c4] for bo in boths]
            else:
                tinvs = [t + _dot(p, t) for p, t in zip(npows, tinvs)]
                npows = [_dot(p, p) for p in npows]
            span *= 2
        tinvs = [t + _dot(p, t) for p, t in zip(npows, tinvs)]
        for (ch, g), t in zip(chains, tinvs):
            tcat_scr[ch, g] = unstack(t)
        return carry

    if n_ch // ua == 1:
        phase_a(0, 0)
    else:
        lax.fori_loop(0, n_ch // ua, phase_a, 0)

    def phase_b(it, carry):
        cb = it // (nbb // ub)
        b0 = (it % (nbb // ub)) * ub
        chains = [(b0 + u, g) for u in range(ub) for g in range(n_groups)]
        ops = []
        for bb, g in chains:
            ch = bb * n_cb + cb
            r0 = pl.multiple_of(ch * c, c)
            ln = slice(g * RW_GW, (g + 1) * RW_GW)
            ops.append(dict(bb=bb, g=g, ch=ch, r0=r0, ln=ln, sbd=sbd_scr[bb, g],
                            v_bd=stack(v_scr[pl.ds(r0, c), ln])))
        wmats = [_dot_nt(at_scr[pl.ds(q["r0"], c), q["ln"]], q["sbd"]) + _dot(aak_scr[q["ch"], q["g"]], q["v_bd"])
                 for q in ops]
        us = [_dot(tcat_scr[q["ch"], q["g"]], stack(w)) for q, w in zip(ops, wmats)]
        for q, u in zip(ops, us):
            r0, ln = q["r0"], q["ln"]
            o_scr[pl.ds(r0, c), ln] = (_dot_nt(rt_scr[pl.ds(r0, c), ln], q["sbd"])
                                       + _dot(arb_scr[q["ch"], q["g"]], stack(u))
                                       + _dot(ark_scr[q["ch"], q["g"]], q["v_bd"]))
        for q, u in zip(ops, us):
            r0, ln = q["r0"], q["ln"]
            upd = _dot_tn(u, bp_scr[pl.ds(r0, c), ln]) + _dot_tn(v_scr[pl.ds(r0, c), ln], kp_scr[pl.ds(r0, c), ln])
            sbd_scr[q["bb"], q["g"]] = q["sbd"] * pc_scr[pl.ds(r0, 1), ln] + jnp.where(bd_state, upd, 0.0)
        return carry

    n_it = n_cb * (nbb // ub)
    if n_it == 1:
        phase_b(0, 0)
    else:
        lax.fori_loop(0, n_it, phase_b, 0)

    o = o_scr[...]
    inv_n = 1.0 / RW_HD
    mean = _dot_seg(o, seg) * inv_n
    cen = o - mean
    var = _dot_seg(cen * cen, seg) * inv_n
    on = cen * lax.rsqrt(var + RW_GN_EPS) * ln_g + ln_b
    y = (on + bonus_scr[...] * v_scr[...]) * gate_scr[...]
    y_ref[...] = y.reshape(nbb, tt_len, BRANCH_W).astype(y_ref.dtype)

    @pl.when(tt == n_t - 1)
    def _():
        def fin(bb, carry):
            for g in range(n_groups):
                sbd = sbd_scr[bb, g]
                for h in range(RW_GROUP):
                    sout_ref[bb, g * RW_GROUP + h] = sbd[h * RW_HD:(h + 1) * RW_HD, h * RW_HD:(h + 1) * RW_HD]
            return carry

        lax.fori_loop(0, nbb, fin, 0)


def _rwkv(proj3, sh0, s0, sbuf, mu, wl, vec, *, nbb, tt_len, chunk, s_valid, layer, ua, ub):
    nb, s_len, _ = proj3.shape
    kern = functools.partial(_rwkv_kernel, nbb=nbb, tt_len=tt_len, chunk=chunk, s_valid=s_valid, s_len=s_len,
                             ua=ua, ub=ub)
    st_spec = pl.BlockSpec((None, nbb, RW_HEADS, RW_HD, RW_HD), lambda b, t: (layer, b, 0, 0, 0))
    full2 = lambda b, t: (0, 0)
    rows = nbb * tt_len
    n_groups = RW_HEADS // RW_GROUP
    mats = pltpu.VMEM((rows // chunk, n_groups, chunk, RW_GROUP * chunk), F32)
    return pl.pallas_call(
        kern,
        grid=(nb // nbb, s_len // tt_len),
        in_specs=[
            pl.BlockSpec((nbb, tt_len, RW_BLOCK), lambda b, t: (b, t, OFF_RW // RW_BLOCK)),
            pl.BlockSpec((nbb, 1, RW_COLS), lambda b, t: (b, 0, 0)),
            st_spec,
            pl.BlockSpec((1, RW_COLS), full2),
            pl.BlockSpec((RW_LORA, 3 * BRANCH_W), full2),
            pl.BlockSpec((SUBLANES, BRANCH_W), full2),
            pl.BlockSpec((2 * rows, rows), full2),
            pl.BlockSpec(memory_space=pl.ANY),
        ],
        out_specs=[pl.BlockSpec((nbb, tt_len, BRANCH_W), lambda b, t: (b, t, 0)), st_spec],
        out_shape=[jax.ShapeDtypeStruct((nb, s_len, BRANCH_W), proj3.dtype),
                   jax.ShapeDtypeStruct(sbuf.shape, F32)],
        input_output_aliases={7: 1},
        scratch_shapes=[
            pltpu.VMEM((nbb, n_groups, RW_GW, RW_GW), F32),
            pltpu.VMEM((nbb, 1, RW_COLS), F32),
        ] + [pltpu.VMEM((rows, BRANCH_W), F32)] * 11 + [mats] * 4,
        compiler_params=_params(("parallel", "arbitrary")),
        name="rwkv7",
    )(proj3, sh0, s0, mu, wl, vec, _chunk_cum_matrix(rows, chunk), sbuf)


def _mix_kernel(x_ref, gts_ref, ya_ref, yb_ref, yc_ref, wb_ref, wo_ref, g_ref, o_ref):
    acc = None
    for n, y_ref in enumerate((ya_ref, yb_ref, yc_ref)):
        up = _dot(y_ref[...], wb_ref[n])
        term = _sigmoid_t(gts_ref[:, n * D_MODEL:(n + 1) * D_MODEL].astype(F32)) * up
        acc = term if acc is None else acc + term
    mix = _dot(acc, wo_ref[...])
    o_ref[...] = x_ref[...] + _rms(mix, g_ref[...])


def _mix(x, proj, ya, yb, yc, wb, wo, g, tm):
    t = x.shape[0]
    row = lambda w: pl.BlockSpec((tm, w), lambda i: (i, 0))
    return pl.pallas_call(
        _mix_kernel,
        grid=(t // tm,),
        in_specs=[row(D_MODEL), row(N_BRANCH * D_MODEL), row(BRANCH_W), row(BRANCH_W), row(BRANCH_W),
                  pl.BlockSpec((N_BRANCH, BRANCH_W, D_MODEL), lambda i: (0, 0, 0)),
                  pl.BlockSpec((D_MODEL, D_MODEL), lambda i: (0, 0)),
                  pl.BlockSpec((1, D_MODEL), lambda i: (0, 0))],
        out_specs=row(D_MODEL),
        out_shape=jax.ShapeDtypeStruct((t, D_MODEL), F32),
        compiler_params=_params(("parallel",)),
        name="branch_mix",
    )(x, proj, ya, yb, yc, wb, wo, g)


def _ffn_kernel(x_ref, p_ref, gpre_ref, wg_ref, wu_ref, wd_ref, gpost_ref, wple_ref, wpg_ref, gple_ref,
                o_ref, h_scr, acc_scr):
    j = pl.program_id(1)

    @pl.when(j == 0)
    def _():
        h_scr[...] = _rms(x_ref[...], gpre_ref[...]).astype(BF16)
        acc_scr[...] = jnp.zeros_like(acc_scr)

    h = h_scr[...]
    gt = jnp.dot(h, wg_ref[...], preferred_element_type=F32)
    up = jnp.dot(h, wu_ref[...], preferred_element_type=F32)
    acc_scr[...] += _dot(gt * _sigmoid_t(gt) * up, wd_ref[...])

    @pl.when(j == pl.num_programs(1) - 1)
    def _():
        x2 = x_ref[...] + _rms(acc_scr[...], gpost_ref[...])
        ple = _dot(p_ref[...], wple_ref[...]) * _sigmoid_t(_dot(x2, wpg_ref[...]))
        o_ref[...] = x2 + _rms(ple, gple_ref[...])


def _ffn(x, p, gpre, wg, wu, wd, gpost, wple, wpg, gple, tm, n_split):
    t = x.shape[0]
    d_ff = wg.shape[1]
    tf = d_ff // n_split
    vecspec = pl.BlockSpec((1, D_MODEL), lambda i, j: (0, 0))
    return pl.pallas_call(
        _ffn_kernel,
        grid=(t // tm, n_split),
        in_specs=[
            pl.BlockSpec((tm, D_MODEL), lambda i, j: (i, 0)),
            pl.BlockSpec((tm, PLE_DIM), lambda i, j: (i, 0)),
            vecspec,
            pl.BlockSpec((D_MODEL, tf), lambda i, j: (0, j)),
            pl.BlockSpec((D_MODEL, tf), lambda i, j: (0, j)),
            pl.BlockSpec((tf, D_MODEL), lambda i, j: (j, 0)),
            vecspec,
            pl.BlockSpec((PLE_DIM, D_MODEL), lambda i, j: (0, 0)),
            pl.BlockSpec((D_MODEL, D_MODEL), lambda i, j: (0, 0)),
            vecspec,
        ],
        out_specs=pl.BlockSpec((tm, D_MODEL), lambda i, j: (i, 0)),
        out_shape=jax.ShapeDtypeStruct((t, D_MODEL), F32),
        scratch_shapes=[pltpu.VMEM((tm, D_MODEL), BF16), pltpu.VMEM((tm, D_MODEL), F32)],
        compiler_params=_params(("parallel", "arbitrary")),
        name="ffn_ple",
    )(x, p, gpre, wg, wu, wd, gpost, wple, wpg, gple)


def _cast_kernel(x_ref, o_ref):
    o_ref[...] = x_ref[...].astype(o_ref.dtype)


def _to_bf16(w, tr):
    d, r, c = w.shape
    spec = pl.BlockSpec((None, tr, c), lambda l, i: (l, i, 0))
    return pl.pallas_call(
        _cast_kernel, grid=(d, r // tr), in_specs=[spec], out_specs=spec,
        out_shape=jax.ShapeDtypeStruct(w.shape, BF16),
        compiler_params=_params(("parallel", "parallel")), name="to_bf16",
    )(w)


W_IN_TILE = 256


def _w_in_kernel(x_ref, o_ref):
    is_pad = pl.program_id(1) >= (OFF_RW + RW_COLS) // W_IN_TILE
    o_ref[...] = jnp.where(is_pad, 0.0, x_ref[...]).astype(o_ref.dtype)


def _w_in_layout(w_in):
    d = w_in.shape[0]
    n_gate = N_BRANCH * D_MODEL // W_IN_TILE
    n_rest = (OFF_RW + RW_COLS) // W_IN_TILE - n_gate

    def src(l, j):
        return l, 0, jnp.where(j < n_gate, j + n_rest, jnp.where(j < n_gate + n_rest, j - n_gate, 0))

    return pl.pallas_call(
        _w_in_kernel, grid=(d, PROJ_COLS // W_IN_TILE),
        in_specs=[pl.BlockSpec((None, D_MODEL, W_IN_TILE), src)],
        out_specs=pl.BlockSpec((None, D_MODEL, W_IN_TILE), lambda l, j: (l, 0, j)),
        out_shape=jax.ShapeDtypeStruct((d, D_MODEL, PROJ_COLS), BF16),
        compiler_params=_params(("parallel", "parallel")), name="w_in_layout",
    )(w_in)


def _block_diag(w):
    n, r, c = w.shape
    eye = jnp.eye(n, dtype=w.dtype)
    return (eye[:, None, :, None] * w[:, :, None, :]).reshape(n * r, n * c)


def _prep_layer(i, W, big):
    lora = jnp.zeros((RW_LORA, 3 * BRANCH_W), F32)
    lora = lora.at[0:RW_LORA_W, 0:BRANCH_W].set(W["rw_w_up"][i])
    lora = lora.at[RW_LORA_W:RW_LORA_W + RW_LORA_A, BRANCH_W:2 * BRANCH_W].set(W["rw_a_up"][i])
    lora = lora.at[RW_LORA_W + RW_LORA_A:, 2 * BRANCH_W:].set(W["rw_g_up"][i])
    vec = jnp.stack([W["rw_w0"][i], W["rw_a0"][i], W["rw_k_k"][i], W["rw_k_a"][i],
                     W["rw_r_k"][i].reshape(BRANCH_W), W["rw_ln_g"][i], W["rw_ln_b"][i],
                     jnp.zeros((BRANCH_W,), F32)])
    row = lambda name: W[name][i].reshape(1, -1)
    return dict(
        w_in=big["w_in"][i], norm_pre_mix=row("norm_pre_mix"),
        conv_w=W["conv_w"][i], conv_b=row("conv_b"),
        lru_w=jnp.concatenate([_block_diag(W["lru_wa"][i]), _block_diag(W["lru_wx"][i])], axis=1).astype(BF16),
        lru_b=jnp.stack([W["lru_ba"][i], W["lru_bx"][i]]), lru_lambda=row("lru_lambda"),
        hg_norm_g=row("hg_norm_g"),
        rw_mu=row("rw_mu"), rw_lora=lora.astype(BF16), rw_vec=vec,
        w_branch=big["w_branch"][i].reshape(N_BRANCH, BRANCH_W, D_MODEL), w_out=big["w_out"][i],
        norm_post_mix=row("norm_post_mix"), norm_pre_ffn=row("norm_pre_ffn"),
        w_ffn_gate=big["w_ffn_gate"][i], w_ffn_up=big["w_ffn_up"][i],
        w_ffn_down=big["w_ffn_down"][i], norm_post_ffn=row("norm_post_ffn"),
        w_ple=big["w_ple"][i], w_ple_gate=big["w_ple_gate"][i], norm_ple=row("norm_ple"),
    )


def _tiles(nb, s_len):
    t = nb * s_len
    tm_in = min(t, 1024)
    tm_tok = min(t, 512)
    if s_len >= 512:
        return dict(tm_in=tm_in, tn_in=2048, proj_dtype=BF16, tm_mix=tm_tok, tm_ffn=tm_tok, ffn_split=2,
                    lru_tt=128,
                    hg=dict(nbb=2, tt_len=256, chunk=64, ub=2),
                    rw=dict(nbb=4, tt_len=128, chunk=64, ua=4, ub=4))
    return dict(tm_in=tm_in, tn_in=2048, proj_dtype=F32, tm_mix=tm_tok, tm_ffn=tm_tok, ffn_split=2,
                lru_tt=s_len,
                hg=dict(nbb=min(nb, 16), tt_len=s_len, chunk=s_len, ub=4),
                rw=dict(nbb=min(nb, 16), tt_len=s_len, chunk=s_len, ua=8, ub=8))


def _run_trunk(x3, p4, states, layers, lbraw, *, s_valid, pos0_is_zero):
    nb, s_len, _ = x3.shape
    t = nb * s_len
    plan = _tiles(nb, s_len)
    conv0, lru0, hg0, rw0, sh0 = states
    x = x3.reshape(t, D_MODEL)
    new = ([], [], [])
    nhg = jnp.zeros(hg0.shape, F32)
    nrw = jnp.zeros(rw0.shape, F32)
    for i, L in enumerate(layers):
        proj = _in_proj(x, L["norm_pre_mix"], L["w_in"], plan["tm_in"], plan["tn_in"], plan["proj_dtype"])
        proj3 = proj.reshape(nb, s_len, PROJ_COLS)
        prev8 = jnp.pad(conv0[i], ((0, 0), (SUBLANES - (CONV_W - 1), 0), (0, 0)))
        ya, nlru = _lru(proj3, prev8, lru0[i], L["conv_w"], L["conv_b"], L["lru_w"], L["lru_b"], L["lru_lambda"],
                        tt_len=plan["lru_tt"], s_valid=s_valid, pos0_is_zero=pos0_is_zero)
        yb, nhg = _hgrn(proj3, hg0, nhg, lbraw, L["hg_norm_g"], s_valid=s_valid, layer=i, **plan["hg"])
        yc, nrw = _rwkv(proj3, sh0[i].reshape(nb, 1, RW_COLS), rw0, nrw, L["rw_mu"], L["rw_lora"], L["rw_vec"],
                        s_valid=s_valid, layer=i, **plan["rw"])
        x = _mix(x, proj, ya.reshape(t, BRANCH_W), yb.reshape(t, BRANCH_W), yc.reshape(t, BRANCH_W),
                 L["w_branch"], L["w_out"], L["norm_post_mix"], plan["tm_mix"])
        x = _ffn(x, p4[i].reshape(t, PLE_DIM), L["norm_pre_ffn"], L["w_ffn_gate"], L["w_ffn_up"], L["w_ffn_down"],
                 L["norm_post_ffn"], L["w_ple"], L["w_ple_gate"], L["norm_ple"], plan["tm_ffn"], plan["ffn_split"])
        nconv = proj3[:, s_valid - (CONV_W - 1):s_valid, OFF_LRU:OFF_LRU + BRANCH_W].astype(F32)
        nsh = proj3[:, s_valid - 1, OFF_RW:OFF_RW + RW_COLS].astype(F32)
        for lst, val in zip(new, (nconv, nlru, nsh)):
            lst.append(val)
    nconv, nlru, nsh = (jnp.stack(l) for l in new)
    return x.reshape(nb, s_len, D_MODEL), (nconv, nlru, nhg, nrw, nsh)


def kernel(x_prompt, x_sample, p_prompt, p_sample, state_conv_a, state_lru_a, state_hgrn, state_rwkv, state_shift_c, norm_pre_mix, w_in, conv_w, conv_b, lru_wa, lru_ba, lru_wx, lru_bx, lru_lambda, hg_lower_bounds, hg_norm_g, rw_mu, rw_w0, rw_w_up, rw_a0, rw_a_up, rw_g_up, rw_k_k, rw_k_a, rw_r_k, rw_ln_g, rw_ln_b, w_branch, w_out, norm_post_mix, norm_pre_ffn, w_ffn_gate, w_ffn_up, w_ffn_down, norm_post_ffn, w_ple, w_ple_gate, norm_ple):
    W = dict(norm_pre_mix=norm_pre_mix, w_in=w_in, conv_w=conv_w, conv_b=conv_b, lru_wa=lru_wa, lru_ba=lru_ba,
             lru_wx=lru_wx, lru_bx=lru_bx, lru_lambda=lru_lambda, hg_norm_g=hg_norm_g, rw_mu=rw_mu, rw_w0=rw_w0,
             rw_w_up=rw_w_up, rw_a0=rw_a0, rw_a_up=rw_a_up, rw_g_up=rw_g_up, rw_k_k=rw_k_k, rw_k_a=rw_k_a,
             rw_r_k=rw_r_k, rw_ln_g=rw_ln_g, rw_ln_b=rw_ln_b, w_branch=w_branch, w_out=w_out,
             norm_post_mix=norm_post_mix, norm_pre_ffn=norm_pre_ffn, w_ffn_gate=w_ffn_gate, w_ffn_up=w_ffn_up,
             w_ffn_down=w_ffn_down, norm_post_ffn=norm_post_ffn, w_ple=w_ple, w_ple_gate=w_ple_gate, norm_ple=norm_ple)
    depth = w_in.shape[0]
    d_ff = w_ffn_gate.shape[2]
    big = dict(
        w_in=_w_in_layout(w_in),
        w_branch=_to_bf16(w_branch.reshape(depth, N_BRANCH * BRANCH_W, D_MODEL), N_BRANCH * BRANCH_W // 2),
        w_out=_to_bf16(w_out, D_MODEL),
        w_ffn_gate=_to_bf16(w_ffn_gate, D_MODEL // 2), w_ffn_up=_to_bf16(w_ffn_up, D_MODEL // 2),
        w_ffn_down=_to_bf16(w_ffn_down, d_ff // 2),
        w_ple=_to_bf16(w_ple, PLE_DIM), w_ple_gate=_to_bf16(w_ple_gate, D_MODEL),
    )
    layers = [_prep_layer(i, W, big) for i in range(depth)]
    lbraw = hg_lower_bounds.astype(F32)

    bp, sp, _ = x_prompt.shape
    zeros = lambda *shape: jnp.zeros((depth, bp) + shape, F32)
    zero_states = (zeros(CONV_W - 1, BRANCH_W), zeros(BRANCH_W), zeros(HG_HEADS, HG_D, HG_D),
                   zeros(RW_HEADS, RW_HD, RW_HD), zeros(RW_COLS))
    y_prompt, st_p = _run_trunk(x_prompt, p_prompt, zero_states, layers, lbraw, s_valid=sp, pos0_is_zero=True)

    bs, ss, _ = x_sample.shape
    ss_pad = -(-ss // SUBLANES) * SUBLANES
    xs = jnp.pad(x_sample, ((0, 0), (0, ss_pad - ss), (0, 0)))
    ps = jnp.pad(p_sample, ((0, 0), (0, 0), (0, ss_pad - ss), (0, 0)))
    y_sample, st_s = _run_trunk(xs, ps, (state_conv_a, state_lru_a, state_hgrn, state_rwkv, state_shift_c),
                                layers, lbraw, s_valid=ss, pos0_is_zero=False)
    return (y_prompt, y_sample[:, :ss]) + st_p + st_s
```

```python
import functools
import math

import jax
import jax.numpy as jnp
from jax import lax
from jax.experimental import pallas as pl
from jax.experimental.pallas import tpu as pltpu

F32 = jnp.float32
BF16 = jnp.bfloat16

D_MODEL = 1024
BRANCH_W = 512
N_BRANCH = 3
LRU_BLOCKS = 8
LRU_BW = BRANCH_W // LRU_BLOCKS
CONV_W = 4
LRU_C = 8.0
HG_HEADS = 4
HG_D = BRANCH_W // HG_HEADS
HG_F_MIN = 1e-20
RW_HD = 64
RW_HEADS = BRANCH_W // RW_HD
RW_LORA_W = 64
RW_LORA_A = 64
RW_LORA_G = 128
RW_LORA = RW_LORA_W + RW_LORA_A + RW_LORA_G
RW_GN_EPS = 64e-5
RW_COLS = 3 * BRANCH_W + RW_LORA
PLE_DIM = 256
EPS = 1e-6

SUBLANES = 8
LANES = 128
MXU_DIM = 256
VMEM_LIMIT = 56 * 1024 * 1024

PROJ_COLS = 8192
OFF_GATES = 0
OFF_LRU = N_BRANCH * D_MODEL
OFF_HG = OFF_LRU + 2 * BRANCH_W
OFF_RW = OFF_HG + 4 * BRANCH_W
RW_BLOCK = PROJ_COLS - OFF_RW

RW_GROUP = LANES // RW_HD
RW_GW = RW_GROUP * RW_HD


def _params(sem):
    return pltpu.CompilerParams(dimension_semantics=sem, vmem_limit_bytes=VMEM_LIMIT)


def _rms(x, g):
    return x * lax.rsqrt(jnp.mean(x * x, axis=-1, keepdims=True) + EPS) * g


def _sigmoid(x):
    return 1.0 / (1.0 + jnp.exp(-x))


def _sigmoid_t(x):
    return 0.5 * jnp.tanh(0.5 * x) + 0.5


def _softplus(x):
    return jnp.maximum(x, 0.0) + jnp.log1p(jnp.exp(-jnp.abs(x)))


def _dot(a, b):
    return jnp.dot(a.astype(BF16), b.astype(BF16), preferred_element_type=F32)


def _dot_nt(a, b):
    return lax.dot_general(a.astype(BF16), b.astype(BF16), (((1,), (1,)), ((), ())),
                           preferred_element_type=F32)


def _dot_tn(a, b):
    return lax.dot_general(a.astype(BF16), b.astype(BF16), (((0,), (0,)), ((), ())),
                           preferred_element_type=F32)


def _split2(x):
    hi = x.astype(BF16)
    return hi, (x - hi.astype(F32)).astype(BF16)


def _seg_sum(x, seg):
    i = lax.broadcasted_iota(jnp.int32, (MXU_DIM, MXU_DIM), 0)
    j = lax.broadcasted_iota(jnp.int32, (MXU_DIM, MXU_DIM), 1)
    ones = (i // seg == j // seg).astype(BF16)
    xb = x.astype(BF16)
    tiles = [jnp.dot(xb[:, l:l + MXU_DIM], ones, preferred_element_type=F32)
             for l in range(0, x.shape[1], MXU_DIM)]
    return jnp.concatenate(tiles, axis=1)


def _chunk_sums(cum, x, nbb, tt_len):
    hi, lo = _split2(x)
    m = cum.astype(BF16)
    pre, tot = [], []
    for b in range(nbb):
        rs = slice(b * tt_len, (b + 1) * tt_len)
        both = jnp.dot(m, lo[rs], preferred_element_type=F32) + jnp.dot(m, hi[rs], preferred_element_type=F32)
        pre.append(both[:tt_len])
        tot.append(both[tt_len:])
    if nbb == 1:
        return pre[0], tot[0]
    return jnp.concatenate(pre, axis=0), jnp.concatenate(tot, axis=0)


def _in_proj_kernel(x_ref, g_ref, w_ref, o_ref, h_scr):
    @pl.when(pl.program_id(1) == 0)
    def _():
        h_scr[...] = _rms(x_ref[...], g_ref[...]).astype(BF16)

    o_ref[...] = jnp.dot(h_scr[...], w_ref[...], preferred_element_type=F32).astype(o_ref.dtype)


def _in_proj(x, g, w, tm, tn, out_dtype):
    t = x.shape[0]
    return pl.pallas_call(
        _in_proj_kernel,
        grid=(t // tm, PROJ_COLS // tn),
        in_specs=[
            pl.BlockSpec((tm, D_MODEL), lambda i, j: (i, 0)),
            pl.BlockSpec((1, D_MODEL), lambda i, j: (0, 0)),
            pl.BlockSpec((D_MODEL, tn), lambda i, j: (0, j)),
        ],
        out_specs=pl.BlockSpec((tm, tn), lambda i, j: (i, j)),
        out_shape=jax.ShapeDtypeStruct((t, PROJ_COLS), out_dtype),
        scratch_shapes=[pltpu.VMEM((tm, D_MODEL), BF16)],
        compiler_params=_params(("parallel", "arbitrary")),
        name="in_proj",
    )(x, g, w)


def _lru_kernel(xa_ref, ga_ref, prev8_ref, h0_ref, cw_ref, cb_ref, w_ref, bab_ref, lam_ref,
                y_ref, hout_ref, prev_scr, h_scr, *, nb, tt_len, s_valid, pos0_is_zero):
    tt = pl.program_id(0)
    rows = nb * tt_len

    @pl.when(tt == 0)
    def _():
        prev_scr[...] = prev8_ref[...]
        h_scr[...] = h0_ref[...]

    xa = xa_ref[...].astype(F32)
    prev8 = prev_scr[...]
    t8 = lax.broadcasted_iota(jnp.int32, (1, SUBLANES, 1), 1)
    cw = cw_ref[...]
    xc = cb_ref[...][None] + cw[CONV_W - 1][None, None] * xa
    for j in range(1, CONV_W):
        rolled = pltpu.roll(xa, j, 1)
        head = jnp.where(t8 < j, pltpu.roll(prev8, j, 1), rolled[:, :SUBLANES])
        shifted = head if tt_len == SUBLANES else jnp.concatenate([head, rolled[:, SUBLANES:]], axis=1)
        xc = xc + cw[CONV_W - 1 - j][None, None] * shifted
    prev_scr[...] = xa[:, tt_len - SUBLANES:, :]

    xc2 = xc.reshape(rows, BRANCH_W)
    z = _dot(xc2, w_ref[...])
    bab = bab_ref[...]
    r = _sigmoid_t(z[:, :BRANCH_W] + bab[0:1])
    i = _sigmoid_t(z[:, BRANCH_W:] + bab[1:2])
    log_a = (-LRU_C) * r * _softplus(-lam_ref[...])
    a = jnp.exp(log_a)
    mult = jnp.sqrt(jnp.maximum(1.0 - a * a, 0.0))
    t_in = lax.broadcasted_iota(jnp.int32, (rows, 1), 0) % tt_len
    if pos0_is_zero:
        mult = jnp.where(jnp.logical_and(tt == 0, t_in == 0), 1.0, mult)
    b = xc2 * i * mult
    valid = (tt * tt_len + t_in) < s_valid
    a3 = jnp.where(valid, a, 1.0).reshape(nb, tt_len, BRANCH_W)
    b3 = jnp.where(valid, b, 0.0).reshape(nb, tt_len, BRANCH_W)

    tmod = lax.broadcasted_iota(jnp.int32, (1, tt_len, 1), 1) % SUBLANES
    d = 1
    while d < SUBLANES:
        keep = tmod >= d
        b3 = a3 * jnp.where(keep, pltpu.roll(b3, d, 1), 0.0) + b3
        a3 = a3 * jnp.where(keep, pltpu.roll(a3, d, 1), 1.0)
        d *= 2
    h_in = h_scr[...][:, None, :]
    blocks = []
    for blk in range(tt_len // SUBLANES):
        rows8 = slice(blk * SUBLANES, (blk + 1) * SUBLANES)
        hb = a3[:, rows8, :] * h_in + b3[:, rows8, :]
        blocks.append(hb)
        h_in = hb[:, SUBLANES - 1:SUBLANES, :]
    hh = blocks[0] if len(blocks) == 1 else jnp.concatenate(blocks, axis=1)
    h = h_in.reshape(nb, BRANCH_W)
    h_scr[...] = h
    hout_ref[...] = h

    ga = ga_ref[...].astype(F32)
    gelu = 0.5 * ga * (1.0 + jnp.tanh(math.sqrt(2.0 / math.pi) * (ga + 0.044715 * ga * ga * ga)))
    y_ref[...] = (hh * gelu).astype(y_ref.dtype)


def _lru(proj3, prev8, h0, cw, cb, w, bab, lam, *, tt_len, s_valid, pos0_is_zero):
    nb, s_len, _ = proj3.shape
    blk = OFF_LRU // BRANCH_W
    kern = functools.partial(_lru_kernel, nb=nb, tt_len=tt_len, s_valid=s_valid, pos0_is_zero=pos0_is_zero)
    full2 = lambda t: (0, 0)
    return pl.pallas_call(
        kern,
        grid=(s_len // tt_len,),
        in_specs=[
            pl.BlockSpec((nb, tt_len, BRANCH_W), lambda t: (0, t, blk)),
            pl.BlockSpec((nb, tt_len, BRANCH_W), lambda t: (0, t, blk + 1)),
            pl.BlockSpec((nb, SUBLANES, BRANCH_W), lambda t: (0, 0, 0)),
            pl.BlockSpec((nb, BRANCH_W), full2),
            pl.BlockSpec((CONV_W, BRANCH_W), full2),
            pl.BlockSpec((1, BRANCH_W), full2),
            pl.BlockSpec((BRANCH_W, 2 * BRANCH_W), full2),
            pl.BlockSpec((2, BRANCH_W), full2),
            pl.BlockSpec((1, BRANCH_W), full2),
        ],
        out_specs=[
            pl.BlockSpec((nb, tt_len, BRANCH_W), lambda t: (0, t, 0)),
            pl.BlockSpec((nb, BRANCH_W), full2),
        ],
        out_shape=[
            jax.ShapeDtypeStruct((nb, s_len, BRANCH_W), proj3.dtype),
            jax.ShapeDtypeStruct((nb, BRANCH_W), F32),
        ],
        scratch_shapes=[
            pltpu.VMEM((nb, SUBLANES, BRANCH_W), F32),
            pltpu.VMEM((nb, BRANCH_W), F32),
        ],
        compiler_params=_params(("arbitrary",)),
        name="rglru",
    )(proj3, proj3, prev8, h0, cw, cb, w, bab, lam)


def _hg_diag_blocks(qh, kh, vh, bch):
    c = qh.shape[0]
    nblk = c // SUBLANES
    q3 = qh.reshape(nblk, SUBLANES, HG_D)
    k3 = kh.reshape(nblk, SUBLANES, HG_D)
    v3 = vh.reshape(nblk, SUBLANES, HG_D)
    b3 = bch.reshape(nblk, SUBLANES, HG_D)
    tin = lax.broadcasted_iota(jnp.int32, (1, SUBLANES, 1), 1)
    o3 = jnp.zeros((nblk, SUBLANES, HG_D), F32)
    for s in range(SUBLANES):
        dec = jnp.exp(jnp.minimum(b3 - b3[:, s:s + 1, :], 0.0))
        w = jnp.sum(q3 * k3[:, s:s + 1, :] * dec, axis=-1, keepdims=True)
        w = jnp.where(tin >= s, w, 0.0)
        o3 = o3 + w * v3[:, s:s + 1, :]
    return o3.reshape(c, HG_D)


def _hg_level_refs(bch, h, c):
    gq, gk = [], []
    zero = jnp.zeros((h, HG_D), F32)
    for j in range(c // h):
        if j % 2 == 1:
            gq.append(jnp.broadcast_to(bch[j * h - 1:j * h, :], (h, HG_D)))
            gk.append(zero)
        else:
            gq.append(zero)
            gk.append(jnp.broadcast_to(bch[(j + 1) * h - 1:(j + 1) * h, :], (h, HG_D)))
    return jnp.concatenate(gq, axis=0), jnp.concatenate(gk, axis=0)


def _hgrn_kernel(q_ref, f_ref, v_ref, g_ref, s0_ref, lbraw_ref, ng_ref, cum_ref, sbuf_ref, y_ref, sout_ref,
                 st_scr, qs_scr, k_scr, v_scr, bc_scr, qe_scr, kh_scr, et_scr, o_scr,
                 *, nbb, tt_len, chunk, s_valid, s_len, layer, ub):
    tt = pl.program_id(1)
    n_t = pl.num_programs(1)
    c = chunk
    rows = nbb * tt_len
    n_cb = tt_len // c

    raw = lbraw_ref[...]
    ex = jnp.exp(raw - jnp.max(raw, axis=0, keepdims=True))
    sm = ex / jnp.sum(ex, axis=0, keepdims=True)
    lb = jnp.zeros((1, BRANCH_W), F32)
    for l in range(1, layer + 1):
        lb = lb + sm[l:l + 1]

    @pl.when(tt == 0)
    def _():
        def init(bb, carry):
            for hd in range(HG_HEADS):
                st_scr[bb, hd] = s0_ref[bb, hd].T
            return carry

        lax.fori_loop(0, nbb, init, 0)

    q = q_ref[...].astype(F32).reshape(rows, BRANCH_W)
    fp = f_ref[...].astype(F32).reshape(rows, BRANCH_W)
    sg = _sigmoid(fp)
    f = lb + (1.0 - lb) * sg
    k = (1.0 - lb) * (1.0 - sg)
    logf = jnp.log(jnp.maximum(f, HG_F_MIN))
    if s_valid < s_len:
        t_in = lax.broadcasted_iota(jnp.int32, (rows, 1), 0) % tt_len
        valid = (tt * tt_len + t_in) < s_valid
        k = jnp.where(valid, k, 0.0)
        logf = jnp.where(valid, logf, 0.0)
    bc, btot = _chunk_sums(cum_ref[...], logf, nbb, tt_len)
    qs = q * _sigmoid_t(q)
    qs_scr[...] = qs
    k_scr[...] = k
    v_scr[...] = v_ref[...].astype(F32).reshape(rows, BRANCH_W)
    bc_scr[...] = bc
    qe_scr[...] = qs * jnp.exp(bc)
    kh_scr[...] = k * jnp.exp(btot - bc)
    et_scr[...] = jnp.exp(btot)

    ti = lax.broadcasted_iota(jnp.int32, (c, 1), 0)
    ii = lax.broadcasted_iota(jnp.int32, (c, c), 0)
    jj = lax.broadcasted_iota(jnp.int32, (c, c), 1)
    levels = []
    h = c // 2
    while h >= SUBLANES:
        odd = (ti // h) % 2 == 1
        pair = jnp.logical_and(ii // (2 * h) == jj // (2 * h),
                               jnp.logical_and((ii // h) % 2 == 1, (jj // h) % 2 == 0))
        levels.append((h, odd, pair))
        h //= 2

    def step(it, carry):
        cb = it // (nbb // ub)
        b0 = (it % (nbb // ub)) * ub
        chains = []
        for u in range(ub):
            r0 = pl.multiple_of((b0 + u) * tt_len + cb * c, c)
            for hd in range(HG_HEADS):
                ln = slice(hd * HG_D, (hd + 1) * HG_D)
                chains.append(dict(bb=b0 + u, hd=hd, r0=r0, ln=ln, st=st_scr[b0 + u, hd],
                                   qh=qs_scr[pl.ds(r0, c), ln], kh=k_scr[pl.ds(r0, c), ln],
                                   vh=v_scr[pl.ds(r0, c), ln], bch=bc_scr[pl.ds(r0, c), ln]))
        outs = [_dot_nt(qe_scr[pl.ds(x["r0"], c), x["ln"]], x["st"]) for x in chains]
        if levels:
            amats = []
            for x in chains:
                amat = None
                for (h, odd, pair) in levels:
                    gq, gk = _hg_level_refs(x["bch"], h, c)
                    qt = jnp.where(odd, x["qh"] * jnp.exp(jnp.where(odd, x["bch"] - gq, 0.0)), 0.0)
                    kt = jnp.where(odd, 0.0, x["kh"] * jnp.exp(jnp.where(odd, 0.0, gk - x["bch"])))
                    term = jnp.where(pair, _dot_nt(qt, kt), 0.0)
                    amat = term if amat is None else amat + term
                amats.append(amat)
            outs = [o + _dot(a, x["vh"]) for o, a, x in zip(outs, amats, chains)]
        for o, x in zip(outs, chains):
            o_scr[pl.ds(x["r0"], c), x["ln"]] = o + _hg_diag_blocks(x["qh"], x["kh"], x["vh"], x["bch"])
        for x in chains:
            upd = _dot_tn(x["vh"], kh_scr[pl.ds(x["r0"], c), x["ln"]])
            st_scr[x["bb"], x["hd"]] = x["st"] * et_scr[pl.ds(x["r0"], 1), x["ln"]] + upd
        return carry

    n_it = n_cb * (nbb // ub)
    if n_it == 1:
        step(0, 0)
    else:
        lax.fori_loop(0, n_it, step, 0)

    g = g_ref[...].astype(F32).reshape(rows, BRANCH_W)
    ng = ng_ref[...]
    outs = []
    for hd in range(HG_HEADS):
        ln = slice(hd * HG_D, (hd + 1) * HG_D)
        o = o_scr[:, ln]
        outs.append(o * lax.rsqrt(jnp.mean(o * o, axis=-1, keepdims=True) + EPS) * ng[:, ln])
    y = jnp.concatenate(outs, axis=-1) * (g * _sigmoid_t(g))
    y_ref[...] = y.reshape(nbb, tt_len, BRANCH_W).astype(y_ref.dtype)

    @pl.when(tt == n_t - 1)
    def _():
        def fin(bb, carry):
            for hd in range(HG_HEADS):
                sout_ref[bb, hd] = st_scr[bb, hd].T
            return carry

        lax.fori_loop(0, nbb, fin, 0)


def _chunk_cum_matrix(rows, c):
    i = jnp.arange(rows)[:, None]
    j = jnp.arange(rows)[None, :]
    same = i // c == j // c
    return jnp.concatenate([same & (i >= j), same], axis=0).astype(BF16)


def _hgrn(proj3, s0, sbuf, lbraw, ng, *, nbb, tt_len, chunk, s_valid, layer, ub):
    nb, s_len, _ = proj3.shape
    blk = OFF_HG // BRANCH_W
    rows = nbb * tt_len
    kern = functools.partial(_hgrn_kernel, nbb=nbb, tt_len=tt_len, chunk=chunk, s_valid=s_valid, s_len=s_len,
                             layer=layer, ub=ub)
    seq = lambda k: pl.BlockSpec((nbb, tt_len, BRANCH_W), lambda b, t, k=k: (b, t, blk + k))
    st_spec = pl.BlockSpec((None, nbb, HG_HEADS, HG_D, HG_D), lambda b, t: (layer, b, 0, 0, 0))
    return pl.pallas_call(
        kern,
        grid=(nb // nbb, s_len // tt_len),
        in_specs=[seq(0), seq(1), seq(2), seq(3), st_spec,
                  pl.BlockSpec(lbraw.shape, lambda b, t: (0, 0)),
                  pl.BlockSpec((1, BRANCH_W), lambda b, t: (0, 0)),
                  pl.BlockSpec((2 * tt_len, tt_len), lambda b, t: (0, 0)),
                  pl.BlockSpec(memory_space=pl.ANY)],
        out_specs=[pl.BlockSpec((nbb, tt_len, BRANCH_W), lambda b, t: (b, t, 0)), st_spec],
        out_shape=[jax.ShapeDtypeStruct((nb, s_len, BRANCH_W), proj3.dtype),
                   jax.ShapeDtypeStruct(sbuf.shape, F32)],
        input_output_aliases={8: 1},
        scratch_shapes=[pltpu.VMEM((nbb, HG_HEADS, HG_D, HG_D), F32)]
        + [pltpu.VMEM((rows, BRANCH_W), F32)] * 8,
        compiler_params=_params(("parallel", "arbitrary")),
        name="hgrn2",
    )(proj3, proj3, proj3, proj3, s0, lbraw, ng, _chunk_cum_matrix(tt_len, chunk), sbuf)


def _rwkv_kernel(c_ref, sh0_ref, s0_ref, mu_ref, wl_ref, vec_ref, cum_ref, sbuf_ref, y_ref, sout_ref,
                 sbd_scr, carry_scr, at_scr, rt_scr, bt_scr, kt_scr, bp_scr, kp_scr, v_scr, pc_scr,
                 bonus_scr, gate_scr, o_scr, tcat_scr, aak_scr, arb_scr, ark_scr,
                 *, nbb, tt_len, chunk, s_valid, s_len, ua, ub):
    tt = pl.program_id(1)
    n_t = pl.num_programs(1)
    c = chunk
    c4 = RW_GROUP * c
    n_groups = RW_HEADS // RW_GROUP
    rows = nbb * tt_len
    n_cb = tt_len // c
    n_ch = rows // c

    vec = vec_ref[...]
    w0, a0, k_k, k_a, r_k, ln_g, ln_b = [vec[i:i + 1] for i in range(7)]
    gi_ = lax.broadcasted_iota(jnp.int32, (RW_GW, RW_GW), 0)
    gj_ = lax.broadcasted_iota(jnp.int32, (RW_GW, RW_GW), 1)
    bd_state = gi_ // RW_HD == gj_ // RW_HD

    @pl.when(tt == 0)
    def _():
        carry_scr[...] = sh0_ref[...]

        def init(bb, carry):
            for g in range(n_groups):
                blk = jnp.concatenate([s0_ref[bb, g * RW_GROUP + h] for h in range(RW_GROUP)], axis=0)
                sbd_scr[bb, g] = jnp.where(bd_state, jnp.concatenate([blk] * RW_GROUP, axis=1), 0.0)
            return carry

        lax.fori_loop(0, nbb, init, 0)

    cc3 = c_ref[:, :, :RW_COLS].astype(F32)
    t3 = lax.broadcasted_iota(jnp.int32, (1, tt_len, 1), 1)
    prev3 = jnp.where(t3 == 0, carry_scr[...], pltpu.roll(cc3, 1, 1))
    carry_scr[...] = cc3[:, tt_len - 1:tt_len, :]
    xm = (cc3 + (prev3 - cc3) * mu_ref[...][None]).reshape(rows, RW_COLS)
    r = xm[:, 0:BRANCH_W]
    k = xm[:, BRANCH_W:2 * BRANCH_W]
    v = xm[:, 2 * BRANCH_W:3 * BRANCH_W]
    lo = xm[:, 3 * BRANCH_W:]
    lane_l = lax.broadcasted_iota(jnp.int32, (1, RW_LORA), 1)
    act = jnp.where(lane_l < RW_LORA_W, jnp.tanh(lo),
                    jnp.where(lane_l < RW_LORA_W + RW_LORA_A, lo, _sigmoid_t(lo)))
    z = _dot(act, wl_ref[...])
    ld = (-math.exp(-0.5)) * _sigmoid_t(w0 + z[:, 0:BRANCH_W])
    a = _sigmoid_t(a0 + z[:, BRANCH_W:2 * BRANCH_W])
    kk = k * k_k
    kbar = k * (1.0 + (a - 1.0) * k_a)
    sums = _seg_sum(jnp.concatenate([kk * kk, r * kbar * r_k], axis=0), RW_HD)
    kap = kk * lax.rsqrt(jnp.maximum(sums[:rows], 1e-24))
    if s_valid < s_len:
        t_in = lax.broadcasted_iota(jnp.int32, (rows, 1), 0) % tt_len
        valid = (tt * tt_len + t_in) < s_valid
        ld = jnp.where(valid, ld, 0.0)
        kap = jnp.where(valid, kap, 0.0)
        kbar = jnp.where(valid, kbar, 0.0)
    lw, ltot = _chunk_sums(cum_ref[...], ld, nbb, tt_len)
    back = ltot - lw
    e_in = jnp.exp(lw)
    e_neg = jnp.exp(-lw)
    e_back = jnp.exp(back)
    at_scr[...] = -kap * jnp.exp(lw - ld)
    rt_scr[...] = r * e_in
    bt_scr[...] = kap * a * e_neg
    kt_scr[...] = kbar * e_neg
    bp_scr[...] = kap * a * e_back
    kp_scr[...] = kbar * e_back
    v_scr[...] = v
    pc_scr[...] = jnp.exp(ltot)
    bonus_scr[...] = sums[rows:]
    gate_scr[...] = z[:, 2 * BRANCH_W:]

    si = lax.broadcasted_iota(jnp.int32, (c4, RW_GW), 0)
    sj = lax.broadcasted_iota(jnp.int32, (c4, RW_GW), 1)
    head_rows = si // c == sj // RW_HD
    qi = lax.broadcasted_iota(jnp.int32, (c4, c4), 0)
    qj = lax.broadcasted_iota(jnp.int32, (c4, c4), 1)
    same = qi // c == qj // c
    strict = jnp.logical_and(same, qi % c > qj % c)
    incl = jnp.logical_and(same, qi % c >= qj % c)
    eye = (qi == qj).astype(F32)
    fuse_sq = c4 % LANES == 0

    def stack(x):
        return jnp.where(head_rows, jnp.concatenate([x] * RW_GROUP, axis=0), 0.0)

    def unstack(x):
        out = x[0:c]
        for h in range(1, RW_GROUP):
            out = out + x[h * c:(h + 1) * c]
        return out

    def phase_a(it, carry):
        chains = [(it * ua + u, g) for u in range(ua) for g in range(n_groups)]
        nmats = []
        for ch, g in chains:
            r0 = pl.multiple_of(ch * c, c)
            ln = slice(g * RW_GW, (g + 1) * RW_GW)
            lhs = jnp.concatenate([stack(at_scr[pl.ds(r0, c), ln]), stack(rt_scr[pl.ds(r0, c), ln])], axis=0)
            rhs = jnp.concatenate([bt_scr[pl.ds(r0, c), ln]] * RW_GROUP + [kt_scr[pl.ds(r0, c), ln]] * RW_GROUP,
                                  axis=0)
            quad = _dot_nt(lhs, rhs)
            nmats.append(jnp.where(strict, quad[:c4, :c4], 0.0))
            aak_scr[ch, g] = unstack(jnp.where(strict, quad[:c4, c4:], 0.0))
            arb_scr[ch, g] = unstack(jnp.where(incl, quad[c4:, :c4], 0.0))
            ark_scr[ch, g] = unstack(jnp.where(incl, quad[c4:, c4:], 0.0))
        tinvs = [eye + n for n in nmats]
        npows = [_dot(n, n) for n in nmats]
        span = 2
        while 2 * span < c:
            if fuse_sq:
                boths = [_dot(p, jnp.concatenate([p, t], axis=1)) for p, t in zip(npows, tinvs)]
                tinvs = [t + bo[:, c4:] for t, bo in zip(tinvs, boths)]
                npows = [bo[:, :c4] for bo in boths]
            else:
                tinvs = [t + _dot(p, t) for p, t in zip(npows, tinvs)]
                npows = [_dot(p, p) for p in npows]
            span *= 2
        tinvs = [t + _dot(p, t) for p, t in zip(npows, tinvs)]
        for (ch, g), t in zip(chains, tinvs):
            tcat_scr[ch, g] = unstack(t)
        return carry

    if n_ch // ua == 1:
        phase_a(0, 0)
    else:
        lax.fori_loop(0, n_ch // ua, phase_a, 0)

    def phase_b(it, carry):
        cb = it // (nbb // ub)
        b0 = (it % (nbb // ub)) * ub
        chains = [(b0 + u, g) for u in range(ub) for g in range(n_groups)]
        ops = []
        for bb, g in chains:
            ch = bb * n_cb + cb
            r0 = pl.multiple_of(ch * c, c)
            ln = slice(g * RW_GW, (g + 1) * RW_GW)
            ops.append(dict(bb=bb, g=g, ch=ch, r0=r0, ln=ln, sbd=sbd_scr[bb, g],
                            v_bd=stack(v_scr[pl.ds(r0, c), ln])))
        wmats = [_dot_nt(at_scr[pl.ds(q["r0"], c), q["ln"]], q["sbd"]) + _dot(aak_scr[q["ch"], q["g"]], q["v_bd"])
                 for q in ops]
        us = [_dot(tcat_scr[q["ch"], q["g"]], stack(w)) for q, w in zip(ops, wmats)]
        for q, u in zip(ops, us):
            r0, ln = q["r0"], q["ln"]
            o_scr[pl.ds(r0, c), ln] = (_dot_nt(rt_scr[pl.ds(r0, c), ln], q["sbd"])
                                       + _dot(arb_scr[q["ch"], q["g"]], stack(u))
                                       + _dot(ark_scr[q["ch"], q["g"]], q["v_bd"]))
        for q, u in zip(ops, us):
            r0, ln = q["r0"], q["ln"]
            upd = _dot_tn(u, bp_scr[pl.ds(r0, c), ln]) + _dot_tn(v_scr[pl.ds(r0, c), ln], kp_scr[pl.ds(r0, c), ln])
            sbd_scr[q["bb"], q["g"]] = q["sbd"] * pc_scr[pl.ds(r0, 1), ln] + jnp.where(bd_state, upd, 0.0)
        return carry

    n_it = n_cb * (nbb // ub)
    if n_it == 1:
        phase_b(0, 0)
    else:
        lax.fori_loop(0, n_it, phase_b, 0)

    o = o_scr[...]
    inv_n = 1.0 / RW_HD
    mean = _seg_sum(o, RW_HD) * inv_n
    cen = o - mean
    var = _seg_sum(cen * cen, RW_HD) * inv_n
    on = cen * lax.rsqrt(var + RW_GN_EPS) * ln_g + ln_b
    y = (on + bonus_scr[...] * v_scr[...]) * gate_scr[...]
    y_ref[...] = y.reshape(nbb, tt_len, BRANCH_W).astype(y_ref.dtype)

    @pl.when(tt == n_t - 1)
    def _():
        def fin(bb, carry):
            for g in range(n_groups):
                sbd = sbd_scr[bb, g]
                for h in range(RW_GROUP):
                    sout_ref[bb, g * RW_GROUP + h] = sbd[h * RW_HD:(h + 1) * RW_HD, h * RW_HD:(h + 1) * RW_HD]
            return carry

        lax.fori_loop(0, nbb, fin, 0)


def _rwkv(proj3, sh0, s0, sbuf, mu, wl, vec, *, nbb, tt_len, chunk, s_valid, layer, ua, ub):
    nb, s_len, _ = proj3.shape
    kern = functools.partial(_rwkv_kernel, nbb=nbb, tt_len=tt_len, chunk=chunk, s_valid=s_valid, s_len=s_len,
                             ua=ua, ub=ub)
    st_spec = pl.BlockSpec((None, nbb, RW_HEADS, RW_HD, RW_HD), lambda b, t: (layer, b, 0, 0, 0))
    full2 = lambda b, t: (0, 0)
    rows = nbb * tt_len
    n_groups = RW_HEADS // RW_GROUP
    mats = pltpu.VMEM((rows // chunk, n_groups, chunk, RW_GROUP * chunk), F32)
    return pl.pallas_call(
        kern,
        grid=(nb // nbb, s_len // tt_len),
        in_specs=[
            pl.BlockSpec((nbb, tt_len, RW_BLOCK), lambda b, t: (b, t, OFF_RW // RW_BLOCK)),
            pl.BlockSpec((nbb, 1, RW_COLS), lambda b, t: (b, 0, 0)),
            st_spec,
            pl.BlockSpec((1, RW_COLS), full2),
            pl.BlockSpec((RW_LORA, 3 * BRANCH_W), full2),
            pl.BlockSpec((SUBLANES, BRANCH_W), full2),
            pl.BlockSpec((2 * tt_len, tt_len), full2),
            pl.BlockSpec(memory_space=pl.ANY),
        ],
        out_specs=[pl.BlockSpec((nbb, tt_len, BRANCH_W), lambda b, t: (b, t, 0)), st_spec],
        out_shape=[jax.ShapeDtypeStruct((nb, s_len, BRANCH_W), proj3.dtype),
                   jax.ShapeDtypeStruct(sbuf.shape, F32)],
        input_output_aliases={7: 1},
        scratch_shapes=[
            pltpu.VMEM((nbb, n_groups, RW_GW, RW_GW), F32),
            pltpu.VMEM((nbb, 1, RW_COLS), F32),
        ] + [pltpu.VMEM((rows, BRANCH_W), F32)] * 11 + [mats] * 4,
        compiler_params=_params(("parallel", "arbitrary")),
        name="rwkv7",
    )(proj3, sh0, s0, mu, wl, vec, _chunk_cum_matrix(tt_len, chunk), sbuf)


def _mix_kernel(x_ref, gts_ref, ya_ref, yb_ref, yc_ref, wb_ref, wo_ref, g_ref, o_ref):
    acc = None
    for n, y_ref in enumerate((ya_ref, yb_ref, yc_ref)):
        up = _dot(y_ref[...], wb_ref[n])
        term = _sigmoid_t(gts_ref[:, n * D_MODEL:(n + 1) * D_MODEL].astype(F32)) * up
        acc = term if acc is None else acc + term
    mix = _dot(acc, wo_ref[...])
    o_ref[...] = x_ref[...] + _rms(mix, g_ref[...])


def _mix(x, proj, ya, yb, yc, wb, wo, g, tm):
    t = x.shape[0]
    row = lambda w: pl.BlockSpec((tm, w), lambda i: (i, 0))
    return pl.pallas_call(
        _mix_kernel,
        grid=(t // tm,),
        in_specs=[row(D_MODEL), row(N_BRANCH * D_MODEL), row(BRANCH_W), row(BRANCH_W), row(BRANCH_W),
                  pl.BlockSpec((N_BRANCH, BRANCH_W, D_MODEL), lambda i: (0, 0, 0)),
                  pl.BlockSpec((D_MODEL, D_MODEL), lambda i: (0, 0)),
                  pl.BlockSpec((1, D_MODEL), lambda i: (0, 0))],
        out_specs=row(D_MODEL),
        out_shape=jax.ShapeDtypeStruct((t, D_MODEL), F32),
        compiler_params=_params(("parallel",)),
        name="branch_mix",
    )(x, proj, ya, yb, yc, wb, wo, g)


def _ffn_kernel(x_ref, p_ref, gpre_ref, wg_ref, wu_ref, wd_ref, gpost_ref, wple_ref, wpg_ref, gple_ref,
                o_ref, h_scr, acc_scr):
    j = pl.program_id(1)

    @pl.when(j == 0)
    def _():
        h_scr[...] = _rms(x_ref[...], gpre_ref[...]).astype(BF16)
        acc_scr[...] = jnp.zeros_like(acc_scr)

    h = h_scr[...]
    gt = jnp.dot(h, wg_ref[...], preferred_element_type=F32)
    up = jnp.dot(h, wu_ref[...], preferred_element_type=F32)
    acc_scr[...] += _dot(gt * _sigmoid_t(gt) * up, wd_ref[...])

    @pl.when(j == pl.num_programs(1) - 1)
    def _():
        x2 = x_ref[...] + _rms(acc_scr[...], gpost_ref[...])
        ple = _dot(p_ref[...], wple_ref[...]) * _sigmoid_t(_dot(x2, wpg_ref[...]))
        o_ref[...] = x2 + _rms(ple, gple_ref[...])


def _ffn(x, p, gpre, wg, wu, wd, gpost, wple, wpg, gple, tm, n_split, layer):
    t = x.shape[0]
    d_ff = wg.shape[1]
    tf = d_ff // n_split
    vecspec = pl.BlockSpec((1, D_MODEL), lambda i, j: (0, 0))
    return pl.pallas_call(
        _ffn_kernel,
        grid=(t // tm, n_split),
        in_specs=[
            pl.BlockSpec((tm, D_MODEL), lambda i, j: (i, 0)),
            pl.BlockSpec((None, tm, PLE_DIM), lambda i, j: (layer, i, 0)),
            vecspec,
            pl.BlockSpec((D_MODEL, tf), lambda i, j: (0, j)),
            pl.BlockSpec((D_MODEL, tf), lambda i, j: (0, j)),
            pl.BlockSpec((tf, D_MODEL), lambda i, j: (j, 0)),
            vecspec,
            pl.BlockSpec((PLE_DIM, D_MODEL), lambda i, j: (0, 0)),
            pl.BlockSpec((D_MODEL, D_MODEL), lambda i, j: (0, 0)),
            vecspec,
        ],
        out_specs=pl.BlockSpec((tm, D_MODEL), lambda i, j: (i, 0)),
        out_shape=jax.ShapeDtypeStruct((t, D_MODEL), F32),
        scratch_shapes=[pltpu.VMEM((tm, D_MODEL), BF16), pltpu.VMEM((tm, D_MODEL), F32)],
        compiler_params=_params(("parallel", "arbitrary")),
        name="ffn_ple",
    )(x, p, gpre, wg, wu, wd, gpost, wple, wpg, gple)


def _zeros_kernel(o_ref):
    o_ref[...] = jnp.zeros_like(o_ref)


def _zero_states(shape):
    depth, nb = shape[:2]
    bt = min(nb, 16)
    return pl.pallas_call(
        _zeros_kernel, grid=(depth, nb // bt),
        out_specs=pl.BlockSpec((None, bt) + tuple(shape[2:]), lambda l, b: (l, b, 0, 0, 0)),
        out_shape=jax.ShapeDtypeStruct(shape, F32),
        compiler_params=_params(("parallel", "parallel")), name="zero_states",
    )()


def _cast_kernel(x_ref, o_ref):
    o_ref[...] = x_ref[...].astype(o_ref.dtype)


def _to_bf16(w, tr):
    d, r, c = w.shape
    spec = pl.BlockSpec((None, tr, c), lambda l, i: (l, i, 0))
    return pl.pallas_call(
        _cast_kernel, grid=(d, r // tr), in_specs=[spec], out_specs=spec,
        out_shape=jax.ShapeDtypeStruct(w.shape, BF16),
        compiler_params=_params(("parallel", "parallel")), name="to_bf16",
    )(w)


W_IN_TILE = 256


def _w_in_kernel(x_ref, o_ref):
    is_pad = pl.program_id(1) >= (OFF_RW + RW_COLS) // W_IN_TILE
    o_ref[...] = jnp.where(is_pad, 0.0, x_ref[...]).astype(o_ref.dtype)


def _w_in_layout(w_in):
    d = w_in.shape[0]
    n_gate = N_BRANCH * D_MODEL // W_IN_TILE
    n_rest = (OFF_RW + RW_COLS) // W_IN_TILE - n_gate

    def src(l, j):
        return l, 0, jnp.where(j < n_gate, j + n_rest, jnp.where(j < n_gate + n_rest, j - n_gate, 0))

    return pl.pallas_call(
        _w_in_kernel, grid=(d, PROJ_COLS // W_IN_TILE),
        in_specs=[pl.BlockSpec((None, D_MODEL, W_IN_TILE), src)],
        out_specs=pl.BlockSpec((None, D_MODEL, W_IN_TILE), lambda l, j: (l, 0, j)),
        out_shape=jax.ShapeDtypeStruct((d, D_MODEL, PROJ_COLS), BF16),
        compiler_params=_params(("parallel", "parallel")), name="w_in_layout",
    )(w_in)


def _block_diag(w):
    n, r, c = w.shape
    eye = jnp.eye(n, dtype=w.dtype)
    return (eye[:, None, :, None] * w[:, :, None, :]).reshape(n * r, n * c)


def _prep_layer(i, W, big):
    lora = jnp.zeros((RW_LORA, 3 * BRANCH_W), F32)
    lora = lora.at[0:RW_LORA_W, 0:BRANCH_W].set(W["rw_w_up"][i])
    lora = lora.at[RW_LORA_W:RW_LORA_W + RW_LORA_A, BRANCH_W:2 * BRANCH_W].set(W["rw_a_up"][i])
    lora = lora.at[RW_LORA_W + RW_LORA_A:, 2 * BRANCH_W:].set(W["rw_g_up"][i])
    vec = jnp.stack([W["rw_w0"][i], W["rw_a0"][i], W["rw_k_k"][i], W["rw_k_a"][i],
                     W["rw_r_k"][i].reshape(BRANCH_W), W["rw_ln_g"][i], W["rw_ln_b"][i],
                     jnp.zeros((BRANCH_W,), F32)])
    row = lambda name: W[name][i].reshape(1, -1)
    return dict(
        w_in=big["w_in"][i], norm_pre_mix=row("norm_pre_mix"),
        conv_w=W["conv_w"][i], conv_b=row("conv_b"),
        lru_w=jnp.concatenate([_block_diag(W["lru_wa"][i]), _block_diag(W["lru_wx"][i])], axis=1).astype(BF16),
        lru_b=jnp.stack([W["lru_ba"][i], W["lru_bx"][i]]), lru_lambda=row("lru_lambda"),
        hg_norm_g=row("hg_norm_g"),
        rw_mu=row("rw_mu"), rw_lora=lora.astype(BF16), rw_vec=vec,
        w_branch=big["w_branch"][i].reshape(N_BRANCH, BRANCH_W, D_MODEL), w_out=big["w_out"][i],
        norm_post_mix=row("norm_post_mix"), norm_pre_ffn=row("norm_pre_ffn"),
        w_ffn_gate=big["w_ffn_gate"][i], w_ffn_up=big["w_ffn_up"][i],
        w_ffn_down=big["w_ffn_down"][i], norm_post_ffn=row("norm_post_ffn"),
        w_ple=big["w_ple"][i], w_ple_gate=big["w_ple_gate"][i], norm_ple=row("norm_ple"),
    )


def _tiles(nb, s_len):
    t = nb * s_len
    tm_in = min(t, 1024)
    tm_tok = min(t, 512)
    if s_len >= 512:
        return dict(tm_in=tm_in, tn_in=2048, proj_dtype=BF16, tm_mix=tm_tok, tm_ffn=tm_tok, ffn_split=2,
                    lru_tt=128,
                    hg=dict(nbb=2, tt_len=256, chunk=64, ub=2),
                    rw=dict(nbb=4, tt_len=128, chunk=64, ua=4, ub=4))
    return dict(tm_in=tm_in, tn_in=2048, proj_dtype=F32, tm_mix=tm_tok, tm_ffn=tm_tok, ffn_split=2,
                lru_tt=s_len,
                hg=dict(nbb=min(nb, 16), tt_len=s_len, chunk=s_len, ub=4),
                rw=dict(nbb=min(nb, 16), tt_len=s_len, chunk=s_len, ua=8, ub=8))


def _run_trunk(x3, p4, states, layers, lbraw, *, s_valid, pos0_is_zero):
    nb, s_len, _ = x3.shape
    t = nb * s_len
    plan = _tiles(nb, s_len)
    conv0, lru0, hg0, rw0, sh0 = states
    x = x3.reshape(t, D_MODEL)
    new = ([], [], [])
    nhg = _zero_states(hg0.shape)
    nrw = _zero_states(rw0.shape)
    for i, L in enumerate(layers):
        proj = _in_proj(x, L["norm_pre_mix"], L["w_in"], plan["tm_in"], plan["tn_in"], plan["proj_dtype"])
        proj3 = proj.reshape(nb, s_len, PROJ_COLS)
        prev8 = jnp.pad(conv0[i], ((0, 0), (SUBLANES - (CONV_W - 1), 0), (0, 0)))
        ya, nlru = _lru(proj3, prev8, lru0[i], L["conv_w"], L["conv_b"], L["lru_w"], L["lru_b"], L["lru_lambda"],
                        tt_len=plan["lru_tt"], s_valid=s_valid, pos0_is_zero=pos0_is_zero)
        yb, nhg = _hgrn(proj3, hg0, nhg, lbraw, L["hg_norm_g"], s_valid=s_valid, layer=i, **plan["hg"])
        yc, nrw = _rwkv(proj3, sh0[i].reshape(nb, 1, RW_COLS), rw0, nrw, L["rw_mu"], L["rw_lora"], L["rw_vec"],
                        s_valid=s_valid, layer=i, **plan["rw"])
        x = _mix(x, proj, ya.reshape(t, BRANCH_W), yb.reshape(t, BRANCH_W), yc.reshape(t, BRANCH_W),
                 L["w_branch"], L["w_out"], L["norm_post_mix"], plan["tm_mix"])
        x = _ffn(x, p4.reshape(-1, t, PLE_DIM), L["norm_pre_ffn"], L["w_ffn_gate"], L["w_ffn_up"], L["w_ffn_down"],
                 L["norm_post_ffn"], L["w_ple"], L["w_ple_gate"], L["norm_ple"], plan["tm_ffn"], plan["ffn_split"], i)
        nconv = proj3[:, s_valid - (CONV_W - 1):s_valid, OFF_LRU:OFF_LRU + BRANCH_W].astype(F32)
        nsh = proj3[:, s_valid - 1, OFF_RW:OFF_RW + RW_COLS].astype(F32)
        for lst, val in zip(new, (nconv, nlru, nsh)):
            lst.append(val)
    nconv, nlru, nsh = (jnp.stack(l) for l in new)
    return x.reshape(nb, s_len, D_MODEL), (nconv, nlru, nhg, nrw, nsh)


def kernel(x_prompt, x_sample, p_prompt, p_sample, state_conv_a, state_lru_a, state_hgrn, state_rwkv, state_shift_c, norm_pre_mix, w_in, conv_w, conv_b, lru_wa, lru_ba, lru_wx, lru_bx, lru_lambda, hg_lower_bounds, hg_norm_g, rw_mu, rw_w0, rw_w_up, rw_a0, rw_a_up, rw_g_up, rw_k_k, rw_k_a, rw_r_k, rw_ln_g, rw_ln_b, w_branch, w_out, norm_post_mix, norm_pre_ffn, w_ffn_gate, w_ffn_up, w_ffn_down, norm_post_ffn, w_ple, w_ple_gate, norm_ple):
    W = dict(norm_pre_mix=norm_pre_mix, w_in=w_in, conv_w=conv_w, conv_b=conv_b, lru_wa=lru_wa, lru_ba=lru_ba,
             lru_wx=lru_wx, lru_bx=lru_bx, lru_lambda=lru_lambda, hg_norm_g=hg_norm_g, rw_mu=rw_mu, rw_w0=rw_w0,
             rw_w_up=rw_w_up, rw_a0=rw_a0, rw_a_up=rw_a_up, rw_g_up=rw_g_up, rw_k_k=rw_k_k, rw_k_a=rw_k_a,
             rw_r_k=rw_r_k, rw_ln_g=rw_ln_g, rw_ln_b=rw_ln_b, w_branch=w_branch, w_out=w_out,
             norm_post_mix=norm_post_mix, norm_pre_ffn=norm_pre_ffn, w_ffn_gate=w_ffn_gate, w_ffn_up=w_ffn_up,
             w_ffn_down=w_ffn_down, norm_post_ffn=norm_post_ffn, w_ple=w_ple, w_ple_gate=w_ple_gate, norm_ple=norm_ple)
    depth = w_in.shape[0]
    d_ff = w_ffn_gate.shape[2]
    big = dict(
        w_in=_w_in_layout(w_in),
        w_branch=_to_bf16(w_branch.reshape(depth, N_BRANCH * BRANCH_W, D_MODEL), N_BRANCH * BRANCH_W // 2),
        w_out=_to_bf16(w_out, D_MODEL),
        w_ffn_gate=_to_bf16(w_ffn_gate, D_MODEL // 2), w_ffn_up=_to_bf16(w_ffn_up, D_MODEL // 2),
        w_ffn_down=_to_bf16(w_ffn_down, d_ff // 2),
        w_ple=_to_bf16(w_ple, PLE_DIM), w_ple_gate=_to_bf16(w_ple_gate, D_MODEL),
    )
    layers = [_prep_layer(i, W, big) for i in range(depth)]
    lbraw = hg_lower_bounds.astype(F32)

    bp, sp, _ = x_prompt.shape
    zeros = lambda *shape: jnp.zeros((depth, bp) + shape, F32)
    zero_states = (zeros(CONV_W - 1, BRANCH_W), zeros(BRANCH_W), zeros(HG_HEADS, HG_D, HG_D),
                   zeros(RW_HEADS, RW_HD, RW_HD), zeros(RW_COLS))
    y_prompt, st_p = _run_trunk(x_prompt, p_prompt, zero_states, layers, lbraw, s_valid=sp, pos0_is_zero=True)

    bs, ss, _ = x_sample.shape
    ss_pad = -(-ss // SUBLANES) * SUBLANES
    xs = jnp.pad(x_sample, ((0, 0), (0, ss_pad - ss), (0, 0)))
    ps = jnp.pad(p_sample, ((0, 0), (0, 0), (0, ss_pad - ss), (0, 0)))
    y_sample, st_s = _run_trunk(xs, ps, (state_conv_a, state_lru_a, state_hgrn, state_rwkv, state_shift_c),
                                layers, lbraw, s_valid=ss, pos0_is_zero=False)
    return (y_prompt, y_sample[:, :ss]) + st_p + st_s
```

```python
import functools
import math

import jax
import jax.numpy as jnp
from jax import lax
from jax.experimental import pallas as pl
from jax.experimental.pallas import tpu as pltpu

F32 = jnp.float32
BF16 = jnp.bfloat16

D_MODEL = 1024
BRANCH_W = 512
N_BRANCH = 3
LRU_BLOCKS = 8
LRU_BW = BRANCH_W // LRU_BLOCKS
CONV_W = 4
LRU_C = 8.0
HG_HEADS = 4
HG_D = BRANCH_W // HG_HEADS
HG_F_MIN = 1e-20
RW_HD = 64
RW_HEADS = BRANCH_W // RW_HD
RW_LORA_W = 64
RW_LORA_A = 64
RW_LORA_G = 128
RW_LORA = RW_LORA_W + RW_LORA_A + RW_LORA_G
RW_GN_EPS = 64e-5
RW_COLS = 3 * BRANCH_W + RW_LORA
PLE_DIM = 256
EPS = 1e-6

SUBLANES = 8
LANES = 128
MXU_DIM = 256
VMEM_LIMIT = 56 * 1024 * 1024

PROJ_COLS = 8192
OFF_GATES = 0
OFF_LRU = N_BRANCH * D_MODEL
OFF_HG = OFF_LRU + 2 * BRANCH_W
OFF_RW = OFF_HG + 4 * BRANCH_W
RW_BLOCK = PROJ_COLS - OFF_RW

RW_GROUP = LANES // RW_HD
RW_GW = RW_GROUP * RW_HD


def _params(sem):
    return pltpu.CompilerParams(dimension_semantics=sem, vmem_limit_bytes=VMEM_LIMIT)


def _rms(x, g):
    return x * lax.rsqrt(jnp.mean(x * x, axis=-1, keepdims=True) + EPS) * g


def _sigmoid(x):
    return 1.0 / (1.0 + jnp.exp(-x))


def _sigmoid_t(x):
    return 0.5 * jnp.tanh(0.5 * x) + 0.5


def _softplus(x):
    return jnp.maximum(x, 0.0) + jnp.log1p(jnp.exp(-jnp.abs(x)))


def _dot(a, b):
    return jnp.dot(a.astype(BF16), b.astype(BF16), preferred_element_type=F32)


def _dot_nt(a, b):
    return lax.dot_general(a.astype(BF16), b.astype(BF16), (((1,), (1,)), ((), ())),
                           preferred_element_type=F32)


def _dot_tn(a, b):
    return lax.dot_general(a.astype(BF16), b.astype(BF16), (((0,), (0,)), ((), ())),
                           preferred_element_type=F32)


def _split2(x):
    hi = x.astype(BF16)
    return hi, (x - hi.astype(F32)).astype(BF16)


def _seg_sum(x, seg):
    i = lax.broadcasted_iota(jnp.int32, (MXU_DIM, MXU_DIM), 0)
    j = lax.broadcasted_iota(jnp.int32, (MXU_DIM, MXU_DIM), 1)
    ones = (i // seg == j // seg).astype(BF16)
    xb = x.astype(BF16)
    tiles = [jnp.dot(xb[:, l:l + MXU_DIM], ones, preferred_element_type=F32)
             for l in range(0, x.shape[1], MXU_DIM)]
    return jnp.concatenate(tiles, axis=1)


def _chunk_sums(cum, x, nbb, tt_len):
    hi, lo = _split2(x)
    m = cum.astype(BF16)
    pre, tot = [], []
    for b in range(nbb):
        rs = slice(b * tt_len, (b + 1) * tt_len)
        both = jnp.dot(m, lo[rs], preferred_element_type=F32) + jnp.dot(m, hi[rs], preferred_element_type=F32)
        pre.append(both[:tt_len])
        tot.append(both[tt_len:])
    if nbb == 1:
        return pre[0], tot[0]
    return jnp.concatenate(pre, axis=0), jnp.concatenate(tot, axis=0)


def _in_proj_kernel(x_ref, g_ref, w_ref, o_ref, h_scr):
    @pl.when(pl.program_id(1) == 0)
    def _():
        h_scr[...] = _rms(x_ref[...], g_ref[...]).astype(BF16)

    o_ref[...] = jnp.dot(h_scr[...], w_ref[...], preferred_element_type=F32).astype(o_ref.dtype)


def _in_proj(x, g, w, tm, tn, out_dtype, layer):
    t = x.shape[0]
    return pl.pallas_call(
        _in_proj_kernel,
        grid=(t // tm, PROJ_COLS // tn),
        in_specs=[
            pl.BlockSpec((tm, D_MODEL), lambda i, j: (i, 0)),
            pl.BlockSpec((1, D_MODEL), lambda i, j: (0, 0)),
            pl.BlockSpec((None, D_MODEL, tn), lambda i, j: (layer, 0, j)),
        ],
        out_specs=pl.BlockSpec((tm, tn), lambda i, j: (i, j)),
        out_shape=jax.ShapeDtypeStruct((t, PROJ_COLS), out_dtype),
        scratch_shapes=[pltpu.VMEM((tm, D_MODEL), BF16)],
        compiler_params=_params(("parallel", "arbitrary")),
        name="in_proj",
    )(x, g, w)


def _lru_kernel(xa_ref, ga_ref, prev8_ref, h0_ref, cw_ref, cb_ref, w_ref, bab_ref, lam_ref,
                y_ref, hout_ref, prev_scr, h_scr, *, nb, tt_len, s_valid, pos0_is_zero):
    tt = pl.program_id(0)
    rows = nb * tt_len

    @pl.when(tt == 0)
    def _():
        prev_scr[...] = prev8_ref[...]
        h_scr[...] = h0_ref[...]

    xa = xa_ref[...].astype(F32)
    n8 = tt_len // SUBLANES
    xa4 = xa.reshape(nb, n8, SUBLANES, BRANCH_W)
    prev4 = prev_scr[...][:, None]
    t8 = lax.broadcasted_iota(jnp.int32, (1, 1, SUBLANES, 1), 2)
    cw = cw_ref[...]
    xc4 = cb_ref[...][None, None] + cw[CONV_W - 1][None, None, None] * xa4
    for j in range(1, CONV_W):
        rot = pltpu.roll(xa4, j, 2)
        rot_before = pltpu.roll(prev4, j, 2)
        if n8 > 1:
            rot_before = jnp.concatenate([rot_before, rot[:, :n8 - 1]], axis=1)
        xc4 = xc4 + cw[CONV_W - 1 - j][None, None, None] * jnp.where(t8 < j, rot_before, rot)
    prev_scr[...] = xa[:, tt_len - SUBLANES:, :]

    xc2 = xc4.reshape(rows, BRANCH_W)
    z = _dot(xc2, w_ref[...])
    bab = bab_ref[...]
    r = _sigmoid_t(z[:, :BRANCH_W] + bab[0:1])
    i = _sigmoid_t(z[:, BRANCH_W:] + bab[1:2])
    log_a = (-LRU_C) * r * _softplus(-lam_ref[...])
    a = jnp.exp(log_a)
    m2 = jnp.maximum(1.0 - a * a, 0.0)
    mult = m2 * lax.rsqrt(jnp.maximum(m2, 1e-30))
    t_in = lax.broadcasted_iota(jnp.int32, (rows, 1), 0) % tt_len
    if pos0_is_zero:
        mult = jnp.where(jnp.logical_and(tt == 0, t_in == 0), 1.0, mult)
    b = xc2 * i * mult
    valid = (tt * tt_len + t_in) < s_valid
    a4 = jnp.where(valid, a, 1.0).reshape(nb, n8, SUBLANES, BRANCH_W)
    b4 = jnp.where(valid, b, 0.0).reshape(nb, n8, SUBLANES, BRANCH_W)

    d = 1
    while d < SUBLANES:
        keep = t8 >= d
        b4 = a4 * jnp.where(keep, pltpu.roll(b4, d, 2), 0.0) + b4
        a4 = a4 * jnp.where(keep, pltpu.roll(a4, d, 2), 1.0)
        d *= 2
    h_in = h_scr[...][:, None, :]
    blocks = []
    for blk in range(n8):
        hb = a4[:, blk] * h_in + b4[:, blk]
        blocks.append(hb)
        h_in = hb[:, SUBLANES - 1:SUBLANES, :]
    hh = blocks[0] if n8 == 1 else jnp.concatenate(blocks, axis=1)
    h = h_in.reshape(nb, BRANCH_W)
    h_scr[...] = h
    hout_ref[...] = h

    ga = ga_ref[...].astype(F32)
    gelu = 0.5 * ga * (1.0 + jnp.tanh(math.sqrt(2.0 / math.pi) * (ga + 0.044715 * ga * ga * ga)))
    y_ref[...] = (hh * gelu).astype(y_ref.dtype)


def _lru(proj3, prev8, h0, cw, cb, w, bab, lam, *, tt_len, s_valid, pos0_is_zero):
    nb, s_len, _ = proj3.shape
    blk = OFF_LRU // BRANCH_W
    kern = functools.partial(_lru_kernel, nb=nb, tt_len=tt_len, s_valid=s_valid, pos0_is_zero=pos0_is_zero)
    full2 = lambda t: (0, 0)
    return pl.pallas_call(
        kern,
        grid=(s_len // tt_len,),
        in_specs=[
            pl.BlockSpec((nb, tt_len, BRANCH_W), lambda t: (0, t, blk)),
            pl.BlockSpec((nb, tt_len, BRANCH_W), lambda t: (0, t, blk + 1)),
            pl.BlockSpec((nb, SUBLANES, BRANCH_W), lambda t: (0, 0, 0)),
            pl.BlockSpec((nb, BRANCH_W), full2),
            pl.BlockSpec((CONV_W, BRANCH_W), full2),
            pl.BlockSpec((1, BRANCH_W), full2),
            pl.BlockSpec((BRANCH_W, 2 * BRANCH_W), full2),
            pl.BlockSpec((2, BRANCH_W), full2),
            pl.BlockSpec((1, BRANCH_W), full2),
        ],
        out_specs=[
            pl.BlockSpec((nb, tt_len, BRANCH_W), lambda t: (0, t, 0)),
            pl.BlockSpec((nb, BRANCH_W), full2),
        ],
        out_shape=[
            jax.ShapeDtypeStruct((nb, s_len, BRANCH_W), proj3.dtype),
            jax.ShapeDtypeStruct((nb, BRANCH_W), F32),
        ],
        scratch_shapes=[
            pltpu.VMEM((nb, SUBLANES, BRANCH_W), F32),
            pltpu.VMEM((nb, BRANCH_W), F32),
        ],
        compiler_params=_params(("arbitrary",)),
        name="rglru",
    )(proj3, proj3, prev8, h0, cw, cb, w, bab, lam)


def _hg_diag_blocks(qh, kh, vh, bch):
    c = qh.shape[0]
    nblk = c // SUBLANES
    q3 = qh.reshape(nblk, SUBLANES, HG_D)
    k3 = kh.reshape(nblk, SUBLANES, HG_D)
    v3 = vh.reshape(nblk, SUBLANES, HG_D)
    b3 = bch.reshape(nblk, SUBLANES, HG_D)
    tin = lax.broadcasted_iota(jnp.int32, (1, SUBLANES, 1), 1)
    o3 = jnp.zeros((nblk, SUBLANES, HG_D), F32)
    for s in range(SUBLANES):
        dec = jnp.exp(jnp.minimum(b3 - b3[:, s:s + 1, :], 0.0))
        w = jnp.sum(q3 * k3[:, s:s + 1, :] * dec, axis=-1, keepdims=True)
        w = jnp.where(tin >= s, w, 0.0)
        o3 = o3 + w * v3[:, s:s + 1, :]
    return o3.reshape(c, HG_D)


def _hg_level_refs(bch, h, c):
    gq, gk = [], []
    zero = jnp.zeros((h, HG_D), F32)
    for j in range(c // h):
        if j % 2 == 1:
            gq.append(jnp.broadcast_to(bch[j * h - 1:j * h, :], (h, HG_D)))
            gk.append(zero)
        else:
            gq.append(zero)
            gk.append(jnp.broadcast_to(bch[(j + 1) * h - 1:(j + 1) * h, :], (h, HG_D)))
    return jnp.concatenate(gq, axis=0), jnp.concatenate(gk, axis=0)


def _hgrn_kernel(q_ref, f_ref, v_ref, g_ref, s0_ref, lbraw_ref, ng_ref, cum_ref, sbuf_ref, y_ref, sout_ref,
                 st_scr, qs_scr, k_scr, v_scr, bc_scr, qe_scr, kh_scr, et_scr, o_scr,
                 *, nbb, tt_len, chunk, s_valid, s_len, layer, ub):
    tt = pl.program_id(1)
    n_t = pl.num_programs(1)
    c = chunk
    rows = nbb * tt_len
    n_cb = tt_len // c

    raw = lbraw_ref[...]
    ex = jnp.exp(raw - jnp.max(raw, axis=0, keepdims=True))
    sm = ex / jnp.sum(ex, axis=0, keepdims=True)
    lb = jnp.zeros((1, BRANCH_W), F32)
    for l in range(1, layer + 1):
        lb = lb + sm[l:l + 1]

    @pl.when(tt == 0)
    def _():
        def init(bb, carry):
            for hd in range(HG_HEADS):
                st_scr[bb, hd] = s0_ref[bb, hd].T
            return carry

        lax.fori_loop(0, nbb, init, 0)

    q = q_ref[...].astype(F32).reshape(rows, BRANCH_W)
    fp = f_ref[...].astype(F32).reshape(rows, BRANCH_W)
    sg = _sigmoid(fp)
    f = lb + (1.0 - lb) * sg
    k = (1.0 - lb) * (1.0 - sg)
    logf = jnp.log(jnp.maximum(f, HG_F_MIN))
    if s_valid < s_len:
        t_in = lax.broadcasted_iota(jnp.int32, (rows, 1), 0) % tt_len
        valid = (tt * tt_len + t_in) < s_valid
        k = jnp.where(valid, k, 0.0)
        logf = jnp.where(valid, logf, 0.0)
    bc, btot = _chunk_sums(cum_ref[...], logf, nbb, tt_len)
    qs = q * _sigmoid_t(q)
    qs_scr[...] = qs
    k_scr[...] = k
    v_scr[...] = v_ref[...].astype(F32).reshape(rows, BRANCH_W)
    bc_scr[...] = bc
    qe_scr[...] = qs * jnp.exp(bc)
    kh_scr[...] = k * jnp.exp(btot - bc)
    et_scr[...] = jnp.exp(btot)

    ti = lax.broadcasted_iota(jnp.int32, (c, 1), 0)
    ii = lax.broadcasted_iota(jnp.int32, (c, c), 0)
    jj = lax.broadcasted_iota(jnp.int32, (c, c), 1)
    levels = []
    h = c // 2
    while h >= SUBLANES:
        odd = (ti // h) % 2 == 1
        pair = jnp.logical_and(ii // (2 * h) == jj // (2 * h),
                               jnp.logical_and((ii // h) % 2 == 1, (jj // h) % 2 == 0))
        levels.append((h, odd, pair))
        h //= 2

    def step(it, carry):
        cb = it // (nbb // ub)
        b0 = (it % (nbb // ub)) * ub
        chains = []
        for u in range(ub):
            r0 = pl.multiple_of((b0 + u) * tt_len + cb * c, c)
            for hd in range(HG_HEADS):
                ln = slice(hd * HG_D, (hd + 1) * HG_D)
                chains.append(dict(bb=b0 + u, hd=hd, r0=r0, ln=ln, st=st_scr[b0 + u, hd],
                                   qh=qs_scr[pl.ds(r0, c), ln], kh=k_scr[pl.ds(r0, c), ln],
                                   vh=v_scr[pl.ds(r0, c), ln], bch=bc_scr[pl.ds(r0, c), ln]))
        outs = [_dot_nt(qe_scr[pl.ds(x["r0"], c), x["ln"]], x["st"]) for x in chains]
        if levels:
            amats = []
            for x in chains:
                amat = None
                for (h, odd, pair) in levels:
                    gq, gk = _hg_level_refs(x["bch"], h, c)
                    qt = jnp.where(odd, x["qh"] * jnp.exp(jnp.where(odd, x["bch"] - gq, 0.0)), 0.0)
                    kt = jnp.where(odd, 0.0, x["kh"] * jnp.exp(jnp.where(odd, 0.0, gk - x["bch"])))
                    term = jnp.where(pair, _dot_nt(qt, kt), 0.0)
                    amat = term if amat is None else amat + term
                amats.append(amat)
            outs = [o + _dot(a, x["vh"]) for o, a, x in zip(outs, amats, chains)]
        for o, x in zip(outs, chains):
            o_scr[pl.ds(x["r0"], c), x["ln"]] = o + _hg_diag_blocks(x["qh"], x["kh"], x["vh"], x["bch"])
        for x in chains:
            upd = _dot_tn(x["vh"], kh_scr[pl.ds(x["r0"], c), x["ln"]])
            st_scr[x["bb"], x["hd"]] = x["st"] * et_scr[pl.ds(x["r0"], 1), x["ln"]] + upd
        return carry

    n_it = n_cb * (nbb // ub)
    if n_it == 1:
        step(0, 0)
    else:
        lax.fori_loop(0, n_it, step, 0)

    g = g_ref[...].astype(F32).reshape(rows, BRANCH_W)
    ng = ng_ref[...]
    outs = []
    for hd in range(HG_HEADS):
        ln = slice(hd * HG_D, (hd + 1) * HG_D)
        o = o_scr[:, ln]
        outs.append(o * lax.rsqrt(jnp.mean(o * o, axis=-1, keepdims=True) + EPS) * ng[:, ln])
    y = jnp.concatenate(outs, axis=-1) * (g * _sigmoid_t(g))
    y_ref[...] = y.reshape(nbb, tt_len, BRANCH_W).astype(y_ref.dtype)

    @pl.when(tt == n_t - 1)
    def _():
        def fin(bb, carry):
            for hd in range(HG_HEADS):
                sout_ref[bb, hd] = st_scr[bb, hd].T
            return carry

        lax.fori_loop(0, nbb, fin, 0)


def _chunk_cum_matrix(rows, c):
    i = jnp.arange(rows)[:, None]
    j = jnp.arange(rows)[None, :]
    same = i // c == j // c
    return jnp.concatenate([same & (i >= j), same], axis=0).astype(BF16)


def _hgrn(proj3, s0, sbuf, lbraw, ng, *, nbb, tt_len, chunk, s_valid, layer, ub):
    nb, s_len, _ = proj3.shape
    blk = OFF_HG // BRANCH_W
    rows = nbb * tt_len
    kern = functools.partial(_hgrn_kernel, nbb=nbb, tt_len=tt_len, chunk=chunk, s_valid=s_valid, s_len=s_len,
                             layer=layer, ub=ub)
    seq = lambda k: pl.BlockSpec((nbb, tt_len, BRANCH_W), lambda b, t, k=k: (b, t, blk + k))
    st_spec = pl.BlockSpec((None, nbb, HG_HEADS, HG_D, HG_D), lambda b, t: (layer, b, 0, 0, 0))
    return pl.pallas_call(
        kern,
        grid=(nb // nbb, s_len // tt_len),
        in_specs=[seq(0), seq(1), seq(2), seq(3), st_spec,
                  pl.BlockSpec(lbraw.shape, lambda b, t: (0, 0)),
                  pl.BlockSpec((1, BRANCH_W), lambda b, t: (0, 0)),
                  pl.BlockSpec((2 * tt_len, tt_len), lambda b, t: (0, 0)),
                  pl.BlockSpec(memory_space=pl.ANY)],
        out_specs=[pl.BlockSpec((nbb, tt_len, BRANCH_W), lambda b, t: (b, t, 0)), st_spec],
        out_shape=[jax.ShapeDtypeStruct((nb, s_len, BRANCH_W), proj3.dtype),
                   jax.ShapeDtypeStruct(sbuf.shape, F32)],
        input_output_aliases={8: 1},
        scratch_shapes=[pltpu.VMEM((nbb, HG_HEADS, HG_D, HG_D), F32)]
        + [pltpu.VMEM((rows, BRANCH_W), F32)] * 8,
        compiler_params=_params(("parallel", "arbitrary")),
        name="hgrn2",
    )(proj3, proj3, proj3, proj3, s0, lbraw, ng, _chunk_cum_matrix(tt_len, chunk), sbuf)


def _rwkv_kernel(c_ref, sh0_ref, s0_ref, mu_ref, wl_ref, vec_ref, cum_ref, sbuf_ref, y_ref, sout_ref,
                 sbd_scr, carry_scr, at_scr, rt_scr, bt_scr, kt_scr, bp_scr, kp_scr, v_scr, pc_scr,
                 bonus_scr, gate_scr, o_scr, tcat_scr, aak_scr, arb_scr, ark_scr,
                 *, nbb, tt_len, chunk, s_valid, s_len, ua, ub):
    tt = pl.program_id(1)
    n_t = pl.num_programs(1)
    c = chunk
    c4 = RW_GROUP * c
    n_groups = RW_HEADS // RW_GROUP
    rows = nbb * tt_len
    n_cb = tt_len // c
    n_ch = rows // c

    vec = vec_ref[...]
    w0, a0, k_k, k_a, r_k, ln_g, ln_b = [vec[i:i + 1] for i in range(7)]
    gi_ = lax.broadcasted_iota(jnp.int32, (RW_GW, RW_GW), 0)
    gj_ = lax.broadcasted_iota(jnp.int32, (RW_GW, RW_GW), 1)
    bd_state = gi_ // RW_HD == gj_ // RW_HD

    @pl.when(tt == 0)
    def _():
        carry_scr[...] = sh0_ref[...]

        def init(bb, carry):
            for g in range(n_groups):
                blk = jnp.concatenate([s0_ref[bb, g * RW_GROUP + h] for h in range(RW_GROUP)], axis=0)
                sbd_scr[bb, g] = jnp.where(bd_state, jnp.concatenate([blk] * RW_GROUP, axis=1), 0.0)
            return carry

        lax.fori_loop(0, nbb, init, 0)

    cc3 = c_ref[:, :, :RW_COLS].astype(F32)
    t3 = lax.broadcasted_iota(jnp.int32, (1, tt_len, 1), 1)
    prev3 = jnp.where(t3 == 0, carry_scr[...], pltpu.roll(cc3, 1, 1))
    carry_scr[...] = cc3[:, tt_len - 1:tt_len, :]
    xm = (cc3 + (prev3 - cc3) * mu_ref[...][None]).reshape(rows, RW_COLS)
    r = xm[:, 0:BRANCH_W]
    k = xm[:, BRANCH_W:2 * BRANCH_W]
    v = xm[:, 2 * BRANCH_W:3 * BRANCH_W]
    lo = xm[:, 3 * BRANCH_W:]
    lane_l = lax.broadcasted_iota(jnp.int32, (1, RW_LORA), 1)
    act = jnp.where(lane_l < RW_LORA_W, jnp.tanh(lo),
                    jnp.where(lane_l < RW_LORA_W + RW_LORA_A, lo, _sigmoid_t(lo)))
    z = _dot(act, wl_ref[...])
    ld = (-math.exp(-0.5)) * _sigmoid_t(w0 + z[:, 0:BRANCH_W])
    a = _sigmoid_t(a0 + z[:, BRANCH_W:2 * BRANCH_W])
    kk = k * k_k
    kbar = k * (1.0 + (a - 1.0) * k_a)
    sums = _seg_sum(jnp.concatenate([kk * kk, r * kbar * r_k], axis=0), RW_HD)
    kap = kk * lax.rsqrt(jnp.maximum(sums[:rows], 1e-24))
    if s_valid < s_len:
        t_in = lax.broadcasted_iota(jnp.int32, (rows, 1), 0) % tt_len
        valid = (tt * tt_len + t_in) < s_valid
        ld = jnp.where(valid, ld, 0.0)
        kap = jnp.where(valid, kap, 0.0)
        kbar = jnp.where(valid, kbar, 0.0)
    lw, ltot = _chunk_sums(cum_ref[...], ld, nbb, tt_len)
    back = ltot - lw
    e_in = jnp.exp(lw)
    e_neg = jnp.exp(-lw)
    e_back = jnp.exp(back)
    at_scr[...] = -kap * jnp.exp(lw - ld)
    rt_scr[...] = r * e_in
    bt_scr[...] = kap * a * e_neg
    kt_scr[...] = kbar * e_neg
    bp_scr[...] = kap * a * e_back
    kp_scr[...] = kbar * e_back
    v_scr[...] = v
    pc_scr[...] = jnp.exp(ltot)
    bonus_scr[...] = sums[rows:]
    gate_scr[...] = z[:, 2 * BRANCH_W:]

    si = lax.broadcasted_iota(jnp.int32, (c4, RW_GW), 0)
    sj = lax.broadcasted_iota(jnp.int32, (c4, RW_GW), 1)
    head_rows = si // c == sj // RW_HD
    qi = lax.broadcasted_iota(jnp.int32, (c4, c4), 0)
    qj = lax.broadcasted_iota(jnp.int32, (c4, c4), 1)
    same = qi // c == qj // c
    strict = jnp.logical_and(same, qi % c > qj % c)
    incl = jnp.logical_and(same, qi % c >= qj % c)
    eye = (qi == qj).astype(F32)
    fuse_sq = c4 % LANES == 0

    def stack(x):
        return jnp.where(head_rows, jnp.concatenate([x] * RW_GROUP, axis=0), 0.0)

    def unstack(x):
        out = x[0:c]
        for h in range(1, RW_GROUP):
            out = out + x[h * c:(h + 1) * c]
        return out

    def phase_a(it, carry):
        chains = [(it * ua + u, g) for u in range(ua) for g in range(n_groups)]
        nmats = []
        for ch, g in chains:
            r0 = pl.multiple_of(ch * c, c)
            ln = slice(g * RW_GW, (g + 1) * RW_GW)
            lhs = jnp.concatenate([stack(at_scr[pl.ds(r0, c), ln]), stack(rt_scr[pl.ds(r0, c), ln])], axis=0)
            rhs = jnp.concatenate([bt_scr[pl.ds(r0, c), ln]] * RW_GROUP + [kt_scr[pl.ds(r0, c), ln]] * RW_GROUP,
                                  axis=0)
            quad = _dot_nt(lhs, rhs)
            nmats.append(jnp.where(strict, quad[:c4, :c4], 0.0))
            aak_scr[ch, g] = unstack(jnp.where(strict, quad[:c4, c4:], 0.0))
            arb_scr[ch, g] = unstack(jnp.where(incl, quad[c4:, :c4], 0.0))
            ark_scr[ch, g] = unstack(jnp.where(incl, quad[c4:, c4:], 0.0))
        tinvs = [eye + n for n in nmats]
        npows = [_dot(n, n) for n in nmats]
        span = 2
        while 2 * span < c:
            if fuse_sq:
                boths = [_dot(p, jnp.concatenate([p, t], axis=1)) for p, t in zip(npows, tinvs)]
                tinvs = [t + bo[:, c4:] for t, bo in zip(tinvs, boths)]
                npows = [bo[:, :c4] for bo in boths]
            else:
                tinvs = [t + _dot(p, t) for p, t in zip(npows, tinvs)]
                npows = [_dot(p, p) for p in npows]
            span *= 2
        tinvs = [t + _dot(p, t) for p, t in zip(npows, tinvs)]
        for (ch, g), t in zip(chains, tinvs):
            tcat_scr[ch, g] = unstack(t)
        return carry

    if n_ch // ua == 1:
        phase_a(0, 0)
    else:
        lax.fori_loop(0, n_ch // ua, phase_a, 0)

    def phase_b(it, carry):
        cb = it // (nbb // ub)
        b0 = (it % (nbb // ub)) * ub
        chains = [(b0 + u, g) for u in range(ub) for g in range(n_groups)]
        ops = []
        for bb, g in chains:
            ch = bb * n_cb + cb
            r0 = pl.multiple_of(ch * c, c)
            ln = slice(g * RW_GW, (g + 1) * RW_GW)
            ops.append(dict(bb=bb, g=g, ch=ch, r0=r0, ln=ln, sbd=sbd_scr[bb, g],
                            v_bd=stack(v_scr[pl.ds(r0, c), ln])))
        wmats = [_dot_nt(at_scr[pl.ds(q["r0"], c), q["ln"]], q["sbd"]) + _dot(aak_scr[q["ch"], q["g"]], q["v_bd"])
                 for q in ops]
        us = [_dot(tcat_scr[q["ch"], q["g"]], stack(w)) for q, w in zip(ops, wmats)]
        for q, u in zip(ops, us):
            r0, ln = q["r0"], q["ln"]
            o_scr[pl.ds(r0, c), ln] = (_dot_nt(rt_scr[pl.ds(r0, c), ln], q["sbd"])
                                       + _dot(arb_scr[q["ch"], q["g"]], stack(u))
                                       + _dot(ark_scr[q["ch"], q["g"]], q["v_bd"]))
        for q, u in zip(ops, us):
            r0, ln = q["r0"], q["ln"]
            upd = _dot_tn(u, bp_scr[pl.ds(r0, c), ln]) + _dot_tn(v_scr[pl.ds(r0, c), ln], kp_scr[pl.ds(r0, c), ln])
            sbd_scr[q["bb"], q["g"]] = q["sbd"] * pc_scr[pl.ds(r0, 1), ln] + jnp.where(bd_state, upd, 0.0)
        return carry

    n_it = n_cb * (nbb // ub)
    if n_it == 1:
        phase_b(0, 0)
    else:
        lax.fori_loop(0, n_it, phase_b, 0)

    o = o_scr[...]
    inv_n = 1.0 / RW_HD
    mean = _seg_sum(o, RW_HD) * inv_n
    cen = o - mean
    var = _seg_sum(cen * cen, RW_HD) * inv_n
    on = cen * lax.rsqrt(var + RW_GN_EPS) * ln_g + ln_b
    y = (on + bonus_scr[...] * v_scr[...]) * gate_scr[...]
    y_ref[...] = y.reshape(nbb, tt_len, BRANCH_W).astype(y_ref.dtype)

    @pl.when(tt == n_t - 1)
    def _():
        def fin(bb, carry):
            for g in range(n_groups):
                sbd = sbd_scr[bb, g]
                for h in range(RW_GROUP):
                    sout_ref[bb, g * RW_GROUP + h] = sbd[h * RW_HD:(h + 1) * RW_HD, h * RW_HD:(h + 1) * RW_HD]
            return carry

        lax.fori_loop(0, nbb, fin, 0)


def _rwkv(proj3, sh0, s0, sbuf, mu, wl, vec, *, nbb, tt_len, chunk, s_valid, layer, ua, ub):
    nb, s_len, _ = proj3.shape
    kern = functools.partial(_rwkv_kernel, nbb=nbb, tt_len=tt_len, chunk=chunk, s_valid=s_valid, s_len=s_len,
                             ua=ua, ub=ub)
    st_spec = pl.BlockSpec((None, nbb, RW_HEADS, RW_HD, RW_HD), lambda b, t: (layer, b, 0, 0, 0))
    full2 = lambda b, t: (0, 0)
    rows = nbb * tt_len
    n_groups = RW_HEADS // RW_GROUP
    mats = pltpu.VMEM((rows // chunk, n_groups, chunk, RW_GROUP * chunk), F32)
    return pl.pallas_call(
        kern,
        grid=(nb // nbb, s_len // tt_len),
        in_specs=[
            pl.BlockSpec((nbb, tt_len, RW_BLOCK), lambda b, t: (b, t, OFF_RW // RW_BLOCK)),
            pl.BlockSpec((nbb, 1, RW_COLS), lambda b, t: (b, 0, 0)),
            st_spec,
            pl.BlockSpec((1, RW_COLS), full2),
            pl.BlockSpec((RW_LORA, 3 * BRANCH_W), full2),
            pl.BlockSpec((SUBLANES, BRANCH_W), full2),
            pl.BlockSpec((2 * tt_len, tt_len), full2),
            pl.BlockSpec(memory_space=pl.ANY),
        ],
        out_specs=[pl.BlockSpec((nbb, tt_len, BRANCH_W), lambda b, t: (b, t, 0)), st_spec],
        out_shape=[jax.ShapeDtypeStruct((nb, s_len, BRANCH_W), proj3.dtype),
                   jax.ShapeDtypeStruct(sbuf.shape, F32)],
        input_output_aliases={7: 1},
        scratch_shapes=[
            pltpu.VMEM((nbb, n_groups, RW_GW, RW_GW), F32),
            pltpu.VMEM((nbb, 1, RW_COLS), F32),
        ] + [pltpu.VMEM((rows, BRANCH_W), F32)] * 11 + [mats] * 4,
        compiler_params=_params(("parallel", "arbitrary")),
        name="rwkv7",
    )(proj3, sh0, s0, mu, wl, vec, _chunk_cum_matrix(tt_len, chunk), sbuf)


def _mix_kernel(x_ref, gts_ref, ya_ref, yb_ref, yc_ref, wb_ref, wo_ref, g_ref, o_ref):
    acc = None
    for n, y_ref in enumerate((ya_ref, yb_ref, yc_ref)):
        up = _dot(y_ref[...], wb_ref[n * BRANCH_W:(n + 1) * BRANCH_W, :])
        term = _sigmoid_t(gts_ref[:, n * D_MODEL:(n + 1) * D_MODEL].astype(F32)) * up
        acc = term if acc is None else acc + term
    mix = _dot(acc, wo_ref[...])
    o_ref[...] = x_ref[...] + _rms(mix, g_ref[...])


def _mix(x, proj, ya, yb, yc, wb, wo, g, tm, layer):
    t = x.shape[0]
    row = lambda w: pl.BlockSpec((tm, w), lambda i: (i, 0))
    return pl.pallas_call(
        _mix_kernel,
        grid=(t // tm,),
        in_specs=[row(D_MODEL), row(N_BRANCH * D_MODEL), row(BRANCH_W), row(BRANCH_W), row(BRANCH_W),
                  pl.BlockSpec((None, N_BRANCH * BRANCH_W, D_MODEL), lambda i: (layer, 0, 0)),
                  pl.BlockSpec((None, D_MODEL, D_MODEL), lambda i: (layer, 0, 0)),
                  pl.BlockSpec((1, D_MODEL), lambda i: (0, 0))],
        out_specs=row(D_MODEL),
        out_shape=jax.ShapeDtypeStruct((t, D_MODEL), F32),
        compiler_params=_params(("parallel",)),
        name="branch_mix",
    )(x, proj, ya, yb, yc, wb, wo, g)


def _ffn_kernel(x_ref, p_ref, gpre_ref, wg_ref, wu_ref, wd_ref, gpost_ref, wple_ref, wpg_ref, gple_ref,
                o_ref, h_scr, acc_scr):
    j = pl.program_id(1)

    @pl.when(j == 0)
    def _():
        h_scr[...] = _rms(x_ref[...], gpre_ref[...]).astype(BF16)
        acc_scr[...] = jnp.zeros_like(acc_scr)

    h = h_scr[...]
    gt = jnp.dot(h, wg_ref[...], preferred_element_type=F32)
    up = jnp.dot(h, wu_ref[...], preferred_element_type=F32)
    acc_scr[...] += _dot(gt * _sigmoid_t(gt) * up, wd_ref[...])

    @pl.when(j == pl.num_programs(1) - 1)
    def _():
        x2 = x_ref[...] + _rms(acc_scr[...], gpost_ref[...])
        ple = _dot(p_ref[...], wple_ref[...]) * _sigmoid_t(_dot(x2, wpg_ref[...]))
        o_ref[...] = x2 + _rms(ple, gple_ref[...])


def _ffn(x, p, gpre, wg, wu, wd, gpost, wple, wpg, gple, tm, n_split, layer):
    t = x.shape[0]
    d_ff = wg.shape[2]
    tf = d_ff // n_split
    vecspec = pl.BlockSpec((1, D_MODEL), lambda i, j: (0, 0))
    return pl.pallas_call(
        _ffn_kernel,
        grid=(t // tm, n_split),
        in_specs=[
            pl.BlockSpec((tm, D_MODEL), lambda i, j: (i, 0)),
            pl.BlockSpec((None, tm, PLE_DIM), lambda i, j: (layer, i, 0)),
            vecspec,
            pl.BlockSpec((None, D_MODEL, tf), lambda i, j: (layer, 0, j)),
            pl.BlockSpec((None, D_MODEL, tf), lambda i, j: (layer, 0, j)),
            pl.BlockSpec((None, tf, D_MODEL), lambda i, j: (layer, j, 0)),
            vecspec,
            pl.BlockSpec((None, PLE_DIM, D_MODEL), lambda i, j: (layer, 0, 0)),
            pl.BlockSpec((None, D_MODEL, D_MODEL), lambda i, j: (layer, 0, 0)),
            vecspec,
        ],
        out_specs=pl.BlockSpec((tm, D_MODEL), lambda i, j: (i, 0)),
        out_shape=jax.ShapeDtypeStruct((t, D_MODEL), F32),
        scratch_shapes=[pltpu.VMEM((tm, D_MODEL), BF16), pltpu.VMEM((tm, D_MODEL), F32)],
        compiler_params=_params(("parallel", "arbitrary")),
        name="ffn_ple",
    )(x, p, gpre, wg, wu, wd, gpost, wple, wpg, gple)


def _zeros_kernel(o_ref):
    o_ref[...] = jnp.zeros_like(o_ref)


def _zero_states(shape):
    depth, nb = shape[:2]
    bt = min(nb, 16)
    return pl.pallas_call(
        _zeros_kernel, grid=(depth, nb // bt),
        out_specs=pl.BlockSpec((None, bt) + tuple(shape[2:]), lambda l, b: (l, b, 0, 0, 0)),
        out_shape=jax.ShapeDtypeStruct(shape, F32),
        compiler_params=_params(("parallel", "parallel")), name="zero_states",
    )()


def _cast_kernel(x_ref, o_ref):
    o_ref[...] = x_ref[...].astype(o_ref.dtype)


def _to_bf16(w, tr):
    d, r, c = w.shape
    spec = pl.BlockSpec((None, tr, c), lambda l, i: (l, i, 0))
    return pl.pallas_call(
        _cast_kernel, grid=(d, r // tr), in_specs=[spec], out_specs=spec,
        out_shape=jax.ShapeDtypeStruct(w.shape, BF16),
        compiler_params=_params(("parallel", "parallel")), name="to_bf16",
    )(w)


W_IN_TILE = 256


def _w_in_kernel(x_ref, o_ref):
    is_pad = pl.program_id(1) >= (OFF_RW + RW_COLS) // W_IN_TILE
    o_ref[...] = jnp.where(is_pad, 0.0, x_ref[...]).astype(o_ref.dtype)


def _w_in_layout(w_in):
    d = w_in.shape[0]
    n_gate = N_BRANCH * D_MODEL // W_IN_TILE
    n_rest = (OFF_RW + RW_COLS) // W_IN_TILE - n_gate

    def src(l, j):
        return l, 0, jnp.where(j < n_gate, j + n_rest, jnp.where(j < n_gate + n_rest, j - n_gate, 0))

    return pl.pallas_call(
        _w_in_kernel, grid=(d, PROJ_COLS // W_IN_TILE),
        in_specs=[pl.BlockSpec((None, D_MODEL, W_IN_TILE), src)],
        out_specs=pl.BlockSpec((None, D_MODEL, W_IN_TILE), lambda l, j: (l, 0, j)),
        out_shape=jax.ShapeDtypeStruct((d, D_MODEL, PROJ_COLS), BF16),
        compiler_params=_params(("parallel", "parallel")), name="w_in_layout",
    )(w_in)


def _block_diag(w):
    n, r, c = w.shape
    eye = jnp.eye(n, dtype=w.dtype)
    return (eye[:, None, :, None] * w[:, :, None, :]).reshape(n * r, n * c)


def _prep_layer(i, W):
    lora = jnp.zeros((RW_LORA, 3 * BRANCH_W), F32)
    lora = lora.at[0:RW_LORA_W, 0:BRANCH_W].set(W["rw_w_up"][i])
    lora = lora.at[RW_LORA_W:RW_LORA_W + RW_LORA_A, BRANCH_W:2 * BRANCH_W].set(W["rw_a_up"][i])
    lora = lora.at[RW_LORA_W + RW_LORA_A:, 2 * BRANCH_W:].set(W["rw_g_up"][i])
    vec = jnp.stack([W["rw_w0"][i], W["rw_a0"][i], W["rw_k_k"][i], W["rw_k_a"][i],
                     W["rw_r_k"][i].reshape(BRANCH_W), W["rw_ln_g"][i], W["rw_ln_b"][i],
                     jnp.zeros((BRANCH_W,), F32)])
    row = lambda name: W[name][i].reshape(1, -1)
    return dict(
        norm_pre_mix=row("norm_pre_mix"),
        conv_w=W["conv_w"][i], conv_b=row("conv_b"),
        lru_w=jnp.concatenate([_block_diag(W["lru_wa"][i]), _block_diag(W["lru_wx"][i])], axis=1).astype(BF16),
        lru_b=jnp.stack([W["lru_ba"][i], W["lru_bx"][i]]), lru_lambda=row("lru_lambda"),
        hg_norm_g=row("hg_norm_g"),
        rw_mu=row("rw_mu"), rw_lora=lora.astype(BF16), rw_vec=vec,
        norm_post_mix=row("norm_post_mix"), norm_pre_ffn=row("norm_pre_ffn"),
        norm_post_ffn=row("norm_post_ffn"), norm_ple=row("norm_ple"),
    )


def _tiles(nb, s_len):
    t = nb * s_len
    tm_in = min(t, 1024)
    tm_tok = min(t, 512)
    if s_len >= 512:
        return dict(tm_in=tm_in, tn_in=2048, proj_dtype=BF16, tm_mix=tm_tok, tm_ffn=tm_tok, ffn_split=2,
                    lru_tt=128,
                    hg=dict(nbb=2, tt_len=256, chunk=64, ub=2),
                    rw=dict(nbb=4, tt_len=128, chunk=64, ua=4, ub=4))
    return dict(tm_in=tm_in, tn_in=2048, proj_dtype=F32, tm_mix=tm_tok, tm_ffn=tm_tok, ffn_split=2,
                lru_tt=s_len,
                hg=dict(nbb=min(nb, 16), tt_len=s_len, chunk=s_len, ub=4),
                rw=dict(nbb=min(nb, 16), tt_len=s_len, chunk=s_len, ua=8, ub=8))


def _run_trunk(x3, p4, states, layers, big, lbraw, *, s_valid, pos0_is_zero):
    nb, s_len, _ = x3.shape
    t = nb * s_len
    plan = _tiles(nb, s_len)
    conv0, lru0, hg0, rw0, sh0 = states
    x = x3.reshape(t, D_MODEL)
    new = ([], [], [])
    nhg = _zero_states(hg0.shape)
    nrw = _zero_states(rw0.shape)
    for i, L in enumerate(layers):
        proj = _in_proj(x, L["norm_pre_mix"], big["w_in"], plan["tm_in"], plan["tn_in"], plan["proj_dtype"], i)
        proj3 = proj.reshape(nb, s_len, PROJ_COLS)
        prev8 = jnp.pad(conv0[i], ((0, 0), (SUBLANES - (CONV_W - 1), 0), (0, 0)))
        ya, nlru = _lru(proj3, prev8, lru0[i], L["conv_w"], L["conv_b"], L["lru_w"], L["lru_b"], L["lru_lambda"],
                        tt_len=plan["lru_tt"], s_valid=s_valid, pos0_is_zero=pos0_is_zero)
        yb, nhg = _hgrn(proj3, hg0, nhg, lbraw, L["hg_norm_g"], s_valid=s_valid, layer=i, **plan["hg"])
        yc, nrw = _rwkv(proj3, sh0[i].reshape(nb, 1, RW_COLS), rw0, nrw, L["rw_mu"], L["rw_lora"], L["rw_vec"],
                        s_valid=s_valid, layer=i, **plan["rw"])
        x = _mix(x, proj, ya.reshape(t, BRANCH_W), yb.reshape(t, BRANCH_W), yc.reshape(t, BRANCH_W),
                 big["w_branch"], big["w_out"], L["norm_post_mix"], plan["tm_mix"], i)
        x = _ffn(x, p4.reshape(-1, t, PLE_DIM), L["norm_pre_ffn"], big["w_ffn_gate"], big["w_ffn_up"],
                 big["w_ffn_down"], L["norm_post_ffn"], big["w_ple"], big["w_ple_gate"], L["norm_ple"], plan["tm_ffn"], plan["ffn_split"], i)
        nconv = proj3[:, s_valid - (CONV_W - 1):s_valid, OFF_LRU:OFF_LRU + BRANCH_W].astype(F32)
        nsh = proj3[:, s_valid - 1, OFF_RW:OFF_RW + RW_COLS].astype(F32)
        for lst, val in zip(new, (nconv, nlru, nsh)):
            lst.append(val)
    nconv, nlru, nsh = (jnp.stack(l) for l in new)
    return x.reshape(nb, s_len, D_MODEL), (nconv, nlru, nhg, nrw, nsh)


def kernel(x_prompt, x_sample, p_prompt, p_sample, state_conv_a, state_lru_a, state_hgrn, state_rwkv, state_shift_c, norm_pre_mix, w_in, conv_w, conv_b, lru_wa, lru_ba, lru_wx, lru_bx, lru_lambda, hg_lower_bounds, hg_norm_g, rw_mu, rw_w0, rw_w_up, rw_a0, rw_a_up, rw_g_up, rw_k_k, rw_k_a, rw_r_k, rw_ln_g, rw_ln_b, w_branch, w_out, norm_post_mix, norm_pre_ffn, w_ffn_gate, w_ffn_up, w_ffn_down, norm_post_ffn, w_ple, w_ple_gate, norm_ple):
    W = dict(norm_pre_mix=norm_pre_mix, w_in=w_in, conv_w=conv_w, conv_b=conv_b, lru_wa=lru_wa, lru_ba=lru_ba,
             lru_wx=lru_wx, lru_bx=lru_bx, lru_lambda=lru_lambda, hg_norm_g=hg_norm_g, rw_mu=rw_mu, rw_w0=rw_w0,
             rw_w_up=rw_w_up, rw_a0=rw_a0, rw_a_up=rw_a_up, rw_g_up=rw_g_up, rw_k_k=rw_k_k, rw_k_a=rw_k_a,
             rw_r_k=rw_r_k, rw_ln_g=rw_ln_g, rw_ln_b=rw_ln_b, w_branch=w_branch, w_out=w_out,
             norm_post_mix=norm_post_mix, norm_pre_ffn=norm_pre_ffn, w_ffn_gate=w_ffn_gate, w_ffn_up=w_ffn_up,
             w_ffn_down=w_ffn_down, norm_post_ffn=norm_post_ffn, w_ple=w_ple, w_ple_gate=w_ple_gate, norm_ple=norm_ple)
    depth = w_in.shape[0]
    d_ff = w_ffn_gate.shape[2]
    big = dict(
        w_in=_w_in_layout(w_in),
        w_branch=_to_bf16(w_branch.reshape(depth, N_BRANCH * BRANCH_W, D_MODEL), N_BRANCH * BRANCH_W // 2),
        w_out=_to_bf16(w_out, D_MODEL),
        w_ffn_gate=_to_bf16(w_ffn_gate, D_MODEL // 2), w_ffn_up=_to_bf16(w_ffn_up, D_MODEL // 2),
        w_ffn_down=_to_bf16(w_ffn_down, d_ff // 2),
        w_ple=_to_bf16(w_ple, PLE_DIM), w_ple_gate=_to_bf16(w_ple_gate, D_MODEL),
    )
    layers = [_prep_layer(i, W) for i in range(depth)]
    lbraw = hg_lower_bounds.astype(F32)

    bp, sp, _ = x_prompt.shape
    zeros = lambda *shape: jnp.zeros((depth, bp) + shape, F32)
    zero_states = (zeros(CONV_W - 1, BRANCH_W), zeros(BRANCH_W), zeros(HG_HEADS, HG_D, HG_D),
                   zeros(RW_HEADS, RW_HD, RW_HD), zeros(RW_COLS))
    y_prompt, st_p = _run_trunk(x_prompt, p_prompt, zero_states, layers, big, lbraw, s_valid=sp, pos0_is_zero=True)

    bs, ss, _ = x_sample.shape
    ss_pad = -(-ss // SUBLANES) * SUBLANES
    xs = jnp.pad(x_sample, ((0, 0), (0, ss_pad - ss), (0, 0)))
    ps = jnp.pad(p_sample, ((0, 0), (0, 0), (0, ss_pad - ss), (0, 0)))
    y_sample, st_s = _run_trunk(xs, ps, (state_conv_a, state_lru_a, state_hgrn, state_rwkv, state_shift_c),
                                layers, big, lbraw, s_valid=ss, pos0_is_zero=False)
    return (y_prompt, y_sample[:, :ss]) + st_p + st_s
```

```python
import functools
import math

import jax
import jax.numpy as jnp
from jax import lax
from jax.experimental import pallas as pl
from jax.experimental.pallas import tpu as pltpu

F32 = jnp.float32
BF16 = jnp.bfloat16

D_MODEL = 1024
BRANCH_W = 512
N_BRANCH = 3
LRU_BLOCKS = 8
LRU_BW = BRANCH_W // LRU_BLOCKS
CONV_W = 4
LRU_C = 8.0
HG_HEADS = 4
HG_D = BRANCH_W // HG_HEADS
HG_F_MIN = 1e-20
RW_HD = 64
RW_HEADS = BRANCH_W // RW_HD
RW_LORA_W = 64
RW_LORA_A = 64
RW_LORA_G = 128
RW_LORA = RW_LORA_W + RW_LORA_A + RW_LORA_G
RW_GN_EPS = 64e-5
RW_COLS = 3 * BRANCH_W + RW_LORA
PLE_DIM = 256
EPS = 1e-6

SUBLANES = 8
LANES = 128
MXU_DIM = 256
VMEM_LIMIT = 56 * 1024 * 1024

PROJ_COLS = 8192
OFF_GATES = 0
OFF_LRU = N_BRANCH * D_MODEL
OFF_HG = OFF_LRU + 2 * BRANCH_W
OFF_RW = OFF_HG + 4 * BRANCH_W
RW_BLOCK = PROJ_COLS - OFF_RW

RW_GROUP = LANES // RW_HD
RW_GW = RW_GROUP * RW_HD


def _params(sem):
    return pltpu.CompilerParams(dimension_semantics=sem, vmem_limit_bytes=VMEM_LIMIT)


def _rms(x, g):
    return x * lax.rsqrt(jnp.mean(x * x, axis=-1, keepdims=True) + EPS) * g


def _sigmoid(x):
    return 1.0 / (1.0 + jnp.exp(-x))


def _sigmoid_t(x):
    return 0.5 * jnp.tanh(0.5 * x) + 0.5


def _softplus(x):
    return jnp.maximum(x, 0.0) + jnp.log1p(jnp.exp(-jnp.abs(x)))


def _dot(a, b):
    return jnp.dot(a.astype(BF16), b.astype(BF16), preferred_element_type=F32)


def _dot_nt(a, b):
    return lax.dot_general(a.astype(BF16), b.astype(BF16), (((1,), (1,)), ((), ())),
                           preferred_element_type=F32)


def _dot_tn(a, b):
    return lax.dot_general(a.astype(BF16), b.astype(BF16), (((0,), (0,)), ((), ())),
                           preferred_element_type=F32)


def _split2(x):
    hi = x.astype(BF16)
    return hi, (x - hi.astype(F32)).astype(BF16)


def _seg_sum(x, seg):
    i = lax.broadcasted_iota(jnp.int32, (MXU_DIM, MXU_DIM), 0)
    j = lax.broadcasted_iota(jnp.int32, (MXU_DIM, MXU_DIM), 1)
    ones = (i // seg == j // seg).astype(BF16)
    xb = x.astype(BF16)
    tiles = [jnp.dot(xb[:, l:l + MXU_DIM], ones, preferred_element_type=F32)
             for l in range(0, x.shape[1], MXU_DIM)]
    return jnp.concatenate(tiles, axis=1)


def _chunk_sums(cum, x, nbb, tt_len):
    hi, lo = _split2(x)
    m = cum.astype(BF16)
    pre, tot = [], []
    for b in range(nbb):
        rs = slice(b * tt_len, (b + 1) * tt_len)
        both = jnp.dot(m, lo[rs], preferred_element_type=F32) + jnp.dot(m, hi[rs], preferred_element_type=F32)
        pre.append(both[:tt_len])
        tot.append(both[tt_len:])
    if nbb == 1:
        return pre[0], tot[0]
    return jnp.concatenate(pre, axis=0), jnp.concatenate(tot, axis=0)


def _in_proj_kernel(x_ref, g_ref, w_ref, o_ref, h_scr):
    @pl.when(pl.program_id(1) == 0)
    def _():
        h_scr[...] = _rms(x_ref[...], g_ref[...]).astype(BF16)

    o_ref[...] = jnp.dot(h_scr[...], w_ref[...], preferred_element_type=F32).astype(o_ref.dtype)


def _in_proj(x, g, w, tm, tn, out_dtype, layer):
    t = x.shape[0]
    return pl.pallas_call(
        _in_proj_kernel,
        grid=(t // tm, PROJ_COLS // tn),
        in_specs=[
            pl.BlockSpec((tm, D_MODEL), lambda i, j: (i, 0)),
            pl.BlockSpec((1, D_MODEL), lambda i, j: (0, 0)),
            pl.BlockSpec((None, D_MODEL, tn), lambda i, j: (layer, 0, j)),
        ],
        out_specs=pl.BlockSpec((tm, tn), lambda i, j: (i, j)),
        out_shape=jax.ShapeDtypeStruct((t, PROJ_COLS), out_dtype),
        scratch_shapes=[pltpu.VMEM((tm, D_MODEL), BF16)],
        compiler_params=_params(("parallel", "arbitrary")),
        name="in_proj",
    )(x, g, w)


def _lru_kernel(xa_ref, ga_ref, prev8_ref, h0_ref, cw_ref, cb_ref, w_ref, bab_ref, lam_ref,
                y_ref, hout_ref, prev_scr, h_scr, *, nb, tt_len, s_valid, pos0_is_zero):
    tt = pl.program_id(0)
    rows = nb * tt_len

    @pl.when(tt == 0)
    def _():
        prev_scr[...] = prev8_ref[...]
        h_scr[...] = h0_ref[...]

    xa = xa_ref[...].astype(F32)
    n8 = tt_len // SUBLANES
    xa4 = xa.reshape(nb, n8, SUBLANES, BRANCH_W)
    prev4 = prev_scr[...][:, None]
    t8 = lax.broadcasted_iota(jnp.int32, (1, 1, SUBLANES, 1), 2)
    cw = cw_ref[...]
    xc4 = cb_ref[...][None, None] + cw[CONV_W - 1][None, None, None] * xa4
    for j in range(1, CONV_W):
        rot = pltpu.roll(xa4, j, 2)
        rot_before = pltpu.roll(prev4, j, 2)
        if n8 > 1:
            rot_before = jnp.concatenate([rot_before, rot[:, :n8 - 1]], axis=1)
        xc4 = xc4 + cw[CONV_W - 1 - j][None, None, None] * jnp.where(t8 < j, rot_before, rot)
    prev_scr[...] = xa[:, tt_len - SUBLANES:, :]

    xc2 = xc4.reshape(rows, BRANCH_W)
    z = _dot(xc2, w_ref[...])
    bab = bab_ref[...]
    r = _sigmoid_t(z[:, :BRANCH_W] + bab[0:1])
    i = _sigmoid_t(z[:, BRANCH_W:] + bab[1:2])
    log_a = (-LRU_C) * r * _softplus(-lam_ref[...])
    a = jnp.exp(log_a)
    m2 = jnp.maximum(1.0 - a * a, 0.0)
    mult = m2 * lax.rsqrt(jnp.maximum(m2, 1e-30))
    t_in = lax.broadcasted_iota(jnp.int32, (rows, 1), 0) % tt_len
    if pos0_is_zero:
        mult = jnp.where(jnp.logical_and(tt == 0, t_in == 0), 1.0, mult)
    b = xc2 * i * mult
    valid = (tt * tt_len + t_in) < s_valid
    a4 = jnp.where(valid, a, 1.0).reshape(nb, n8, SUBLANES, BRANCH_W)
    b4 = jnp.where(valid, b, 0.0).reshape(nb, n8, SUBLANES, BRANCH_W)

    d = 1
    while d < SUBLANES:
        keep = t8 >= d
        b4 = a4 * jnp.where(keep, pltpu.roll(b4, d, 2), 0.0) + b4
        a4 = a4 * jnp.where(keep, pltpu.roll(a4, d, 2), 1.0)
        d *= 2
    h_in = h_scr[...][:, None, :]
    blocks = []
    for blk in range(n8):
        hb = a4[:, blk] * h_in + b4[:, blk]
        blocks.append(hb)
        h_in = hb[:, SUBLANES - 1:SUBLANES, :]
    hh = blocks[0] if n8 == 1 else jnp.concatenate(blocks, axis=1)
    h = h_in.reshape(nb, BRANCH_W)
    h_scr[...] = h
    hout_ref[...] = h

    ga = ga_ref[...].astype(F32)
    gelu = 0.5 * ga * (1.0 + jnp.tanh(math.sqrt(2.0 / math.pi) * (ga + 0.044715 * ga * ga * ga)))
    y_ref[...] = (hh * gelu).astype(y_ref.dtype)


def _lru(proj3, prev8, h0, cw, cb, w, bab, lam, *, tt_len, s_valid, pos0_is_zero):
    nb, s_len, _ = proj3.shape
    blk = OFF_LRU // BRANCH_W
    kern = functools.partial(_lru_kernel, nb=nb, tt_len=tt_len, s_valid=s_valid, pos0_is_zero=pos0_is_zero)
    full2 = lambda t: (0, 0)
    return pl.pallas_call(
        kern,
        grid=(s_len // tt_len,),
        in_specs=[
            pl.BlockSpec((nb, tt_len, BRANCH_W), lambda t: (0, t, blk)),
            pl.BlockSpec((nb, tt_len, BRANCH_W), lambda t: (0, t, blk + 1)),
            pl.BlockSpec((nb, SUBLANES, BRANCH_W), lambda t: (0, 0, 0)),
            pl.BlockSpec((nb, BRANCH_W), full2),
            pl.BlockSpec((CONV_W, BRANCH_W), full2),
            pl.BlockSpec((1, BRANCH_W), full2),
            pl.BlockSpec((BRANCH_W, 2 * BRANCH_W), full2),
            pl.BlockSpec((2, BRANCH_W), full2),
            pl.BlockSpec((1, BRANCH_W), full2),
        ],
        out_specs=[
            pl.BlockSpec((nb, tt_len, BRANCH_W), lambda t: (0, t, 0)),
            pl.BlockSpec((nb, BRANCH_W), full2),
        ],
        out_shape=[
            jax.ShapeDtypeStruct((nb, s_len, BRANCH_W), proj3.dtype),
            jax.ShapeDtypeStruct((nb, BRANCH_W), F32),
        ],
        scratch_shapes=[
            pltpu.VMEM((nb, SUBLANES, BRANCH_W), F32),
            pltpu.VMEM((nb, BRANCH_W), F32),
        ],
        compiler_params=_params(("arbitrary",)),
        name="rglru",
    )(proj3, proj3, prev8, h0, cw, cb, w, bab, lam)


def _own_layer_block(sout_ref, layer):
    if layer > 0:
        return sout_ref
    if sout_ref.shape[0] > 1:
        sout_ref[1:] = jnp.zeros((sout_ref.shape[0] - 1,) + tuple(sout_ref.shape[1:]), sout_ref.dtype)
    return sout_ref.at[0]


def _state_out(s0, sbuf, layer, nbb, tail):
    depth = s0.shape[0]
    zeros = (0,) * len(tail)
    if layer == 0:
        spec = pl.BlockSpec((depth, nbb) + tail, lambda b, t: (0, b) + zeros)
        return spec, [], []
    spec = pl.BlockSpec((None, nbb) + tail, lambda b, t: (layer, b) + zeros)
    return spec, [pl.BlockSpec(memory_space=pl.ANY)], [sbuf]


def _hg_diag_blocks(qh, kh, vh, bch):
    c = qh.shape[0]
    nblk = c // SUBLANES
    q3 = qh.reshape(nblk, SUBLANES, HG_D)
    k3 = kh.reshape(nblk, SUBLANES, HG_D)
    v3 = vh.reshape(nblk, SUBLANES, HG_D)
    b3 = bch.reshape(nblk, SUBLANES, HG_D)
    tin = lax.broadcasted_iota(jnp.int32, (1, SUBLANES, 1), 1)
    o3 = jnp.zeros((nblk, SUBLANES, HG_D), F32)
    for s in range(SUBLANES):
        dec = jnp.exp(jnp.minimum(b3 - b3[:, s:s + 1, :], 0.0))
        w = jnp.sum(q3 * k3[:, s:s + 1, :] * dec, axis=-1, keepdims=True)
        w = jnp.where(tin >= s, w, 0.0)
        o3 = o3 + w * v3[:, s:s + 1, :]
    return o3.reshape(c, HG_D)


def _hg_level_refs(bch, h, c):
    gq, gk = [], []
    zero = jnp.zeros((h, HG_D), F32)
    for j in range(c // h):
        if j % 2 == 1:
            gq.append(jnp.broadcast_to(bch[j * h - 1:j * h, :], (h, HG_D)))
            gk.append(zero)
        else:
            gq.append(zero)
            gk.append(jnp.broadcast_to(bch[(j + 1) * h - 1:(j + 1) * h, :], (h, HG_D)))
    return jnp.concatenate(gq, axis=0), jnp.concatenate(gk, axis=0)


def _hgrn_kernel(q_ref, f_ref, v_ref, g_ref, s0_ref, lbraw_ref, ng_ref, cum_ref, y_ref, sout_ref,
                 st_scr, qs_scr, k_scr, v_scr, bc_scr, qe_scr, kh_scr, et_scr, o_scr,
                 *, nbb, tt_len, chunk, s_valid, s_len, layer, ub):
    tt = pl.program_id(1)
    n_t = pl.num_programs(1)
    c = chunk
    rows = nbb * tt_len
    n_cb = tt_len // c

    raw = lbraw_ref[...]
    ex = jnp.exp(raw - jnp.max(raw, axis=0, keepdims=True))
    sm = ex / jnp.sum(ex, axis=0, keepdims=True)
    lb = jnp.zeros((1, BRANCH_W), F32)
    for l in range(1, layer + 1):
        lb = lb + sm[l:l + 1]

    @pl.when(tt == 0)
    def _():
        def init(bb, carry):
            for hd in range(HG_HEADS):
                st_scr[bb, hd] = s0_ref[bb, hd].T
            return carry

        lax.fori_loop(0, nbb, init, 0)

    q = q_ref[...].astype(F32).reshape(rows, BRANCH_W)
    fp = f_ref[...].astype(F32).reshape(rows, BRANCH_W)
    sg = _sigmoid(fp)
    f = lb + (1.0 - lb) * sg
    k = (1.0 - lb) * (1.0 - sg)
    logf = jnp.log(jnp.maximum(f, HG_F_MIN))
    if s_valid < s_len:
        t_in = lax.broadcasted_iota(jnp.int32, (rows, 1), 0) % tt_len
        valid = (tt * tt_len + t_in) < s_valid
        k = jnp.where(valid, k, 0.0)
        logf = jnp.where(valid, logf, 0.0)
    bc, btot = _chunk_sums(cum_ref[...], logf, nbb, tt_len)
    qs = q * _sigmoid_t(q)
    qs_scr[...] = qs
    k_scr[...] = k
    v_scr[...] = v_ref[...].astype(F32).reshape(rows, BRANCH_W)
    bc_scr[...] = bc
    qe_scr[...] = qs * jnp.exp(bc)
    kh_scr[...] = k * jnp.exp(btot - bc)
    et_scr[...] = jnp.exp(btot)

    ti = lax.broadcasted_iota(jnp.int32, (c, 1), 0)
    ii = lax.broadcasted_iota(jnp.int32, (c, c), 0)
    jj = lax.broadcasted_iota(jnp.int32, (c, c), 1)
    levels = []
    h = c // 2
    while h >= SUBLANES:
        odd = (ti // h) % 2 == 1
        pair = jnp.logical_and(ii // (2 * h) == jj // (2 * h),
                               jnp.logical_and((ii // h) % 2 == 1, (jj // h) % 2 == 0))
        levels.append((h, odd, pair))
        h //= 2

    def step(it, carry):
        cb = it // (nbb // ub)
        b0 = (it % (nbb // ub)) * ub
        chains = []
        for u in range(ub):
            r0 = pl.multiple_of((b0 + u) * tt_len + cb * c, c)
            for hd in range(HG_HEADS):
                ln = slice(hd * HG_D, (hd + 1) * HG_D)
                chains.append(dict(bb=b0 + u, hd=hd, r0=r0, ln=ln, st=st_scr[b0 + u, hd],
                                   qh=qs_scr[pl.ds(r0, c), ln], kh=k_scr[pl.ds(r0, c), ln],
                                   vh=v_scr[pl.ds(r0, c), ln], bch=bc_scr[pl.ds(r0, c), ln]))
        outs = [_dot_nt(qe_scr[pl.ds(x["r0"], c), x["ln"]], x["st"]) for x in chains]
        if levels:
            amats = []
            for x in chains:
                amat = None
                for (h, odd, pair) in levels:
                    gq, gk = _hg_level_refs(x["bch"], h, c)
                    qt = jnp.where(odd, x["qh"] * jnp.exp(jnp.where(odd, x["bch"] - gq, 0.0)), 0.0)
                    kt = jnp.where(odd, 0.0, x["kh"] * jnp.exp(jnp.where(odd, 0.0, gk - x["bch"])))
                    term = jnp.where(pair, _dot_nt(qt, kt), 0.0)
                    amat = term if amat is None else amat + term
                amats.append(amat)
            outs = [o + _dot(a, x["vh"]) for o, a, x in zip(outs, amats, chains)]
        for o, x in zip(outs, chains):
            o_scr[pl.ds(x["r0"], c), x["ln"]] = o + _hg_diag_blocks(x["qh"], x["kh"], x["vh"], x["bch"])
        for x in chains:
            upd = _dot_tn(x["vh"], kh_scr[pl.ds(x["r0"], c), x["ln"]])
            st_scr[x["bb"], x["hd"]] = x["st"] * et_scr[pl.ds(x["r0"], 1), x["ln"]] + upd
        return carry

    n_it = n_cb * (nbb // ub)
    if n_it == 1:
        step(0, 0)
    else:
        lax.fori_loop(0, n_it, step, 0)

    g = g_ref[...].astype(F32).reshape(rows, BRANCH_W)
    ng = ng_ref[...]
    outs = []
    for hd in range(HG_HEADS):
        ln = slice(hd * HG_D, (hd + 1) * HG_D)
        o = o_scr[:, ln]
        outs.append(o * lax.rsqrt(jnp.mean(o * o, axis=-1, keepdims=True) + EPS) * ng[:, ln])
    y = jnp.concatenate(outs, axis=-1) * (g * _sigmoid_t(g))
    y_ref[...] = y.reshape(nbb, tt_len, BRANCH_W).astype(y_ref.dtype)

    @pl.when(tt == n_t - 1)
    def _():
        out = _own_layer_block(sout_ref, layer)

        def fin(bb, carry):
            for hd in range(HG_HEADS):
                out[bb, hd] = st_scr[bb, hd].T
            return carry

        lax.fori_loop(0, nbb, fin, 0)


def _hgrn_kernel_inplace(q_ref, f_ref, v_ref, g_ref, s0_ref, lbraw_ref, ng_ref, cum_ref, sbuf_ref, *rest, **kw):
    _hgrn_kernel(q_ref, f_ref, v_ref, g_ref, s0_ref, lbraw_ref, ng_ref, cum_ref, *rest, **kw)


def _chunk_cum_matrix(rows, c):
    i = jnp.arange(rows)[:, None]
    j = jnp.arange(rows)[None, :]
    same = i // c == j // c
    return jnp.concatenate([same & (i >= j), same], axis=0).astype(BF16)


def _hgrn(proj3, s0, sbuf, lbraw, ng, *, nbb, tt_len, chunk, s_valid, layer, ub):
    nb, s_len, _ = proj3.shape
    blk = OFF_HG // BRANCH_W
    rows = nbb * tt_len
    kern = functools.partial(_hgrn_kernel if layer == 0 else _hgrn_kernel_inplace, nbb=nbb, tt_len=tt_len,
                             chunk=chunk, s_valid=s_valid, s_len=s_len, layer=layer, ub=ub)
    seq = lambda k: pl.BlockSpec((nbb, tt_len, BRANCH_W), lambda b, t, k=k: (b, t, blk + k))
    st_in = pl.BlockSpec((None, nbb, HG_HEADS, HG_D, HG_D), lambda b, t: (layer, b, 0, 0, 0))
    st_out, extra_specs, extra_args = _state_out(s0, sbuf, layer, nbb, (HG_HEADS, HG_D, HG_D))
    return pl.pallas_call(
        kern,
        grid=(nb // nbb, s_len // tt_len),
        in_specs=[seq(0), seq(1), seq(2), seq(3), st_in,
                  pl.BlockSpec(lbraw.shape, lambda b, t: (0, 0)),
                  pl.BlockSpec((1, BRANCH_W), lambda b, t: (0, 0)),
                  pl.BlockSpec((2 * tt_len, tt_len), lambda b, t: (0, 0))] + extra_specs,
        out_specs=[pl.BlockSpec((nbb, tt_len, BRANCH_W), lambda b, t: (b, t, 0)), st_out],
        out_shape=[jax.ShapeDtypeStruct((nb, s_len, BRANCH_W), proj3.dtype),
                   jax.ShapeDtypeStruct(s0.shape, F32)],
        input_output_aliases={8: 1} if extra_args else {},
        scratch_shapes=[pltpu.VMEM((nbb, HG_HEADS, HG_D, HG_D), F32)]
        + [pltpu.VMEM((rows, BRANCH_W), F32)] * 8,
        compiler_params=_params(("parallel", "arbitrary")),
        name="hgrn2",
    )(proj3, proj3, proj3, proj3, s0, lbraw, ng, _chunk_cum_matrix(tt_len, chunk), *extra_args)


def _rwkv_kernel(c_ref, sh0_ref, s0_ref, mu_ref, wl_ref, vec_ref, cum_ref, y_ref, sout_ref,
                 sbd_scr, carry_scr, at_scr, rt_scr, bt_scr, kt_scr, bp_scr, kp_scr, v_scr, pc_scr,
                 bonus_scr, gate_scr, o_scr, tcat_scr, aak_scr, arb_scr, ark_scr,
                 *, nbb, tt_len, chunk, s_valid, s_len, layer, ua, ub):
    tt = pl.program_id(1)
    n_t = pl.num_programs(1)
    c = chunk
    c4 = RW_GROUP * c
    n_groups = RW_HEADS // RW_GROUP
    rows = nbb * tt_len
    n_cb = tt_len // c
    n_ch = rows // c

    vec = vec_ref[...]
    w0, a0, k_k, k_a, r_k, ln_g, ln_b = [vec[i:i + 1] for i in range(7)]
    gi_ = lax.broadcasted_iota(jnp.int32, (RW_GW, RW_GW), 0)
    gj_ = lax.broadcasted_iota(jnp.int32, (RW_GW, RW_GW), 1)
    bd_state = gi_ // RW_HD == gj_ // RW_HD

    @pl.when(tt == 0)
    def _():
        carry_scr[...] = sh0_ref[...]

        def init(bb, carry):
            for g in range(n_groups):
                blk = jnp.concatenate([s0_ref[bb, g * RW_GROUP + h] for h in range(RW_GROUP)], axis=0)
                sbd_scr[bb, g] = jnp.where(bd_state, jnp.concatenate([blk] * RW_GROUP, axis=1), 0.0)
            return carry

        lax.fori_loop(0, nbb, init, 0)

    cc3 = c_ref[:, :, :RW_COLS].astype(F32)
    t3 = lax.broadcasted_iota(jnp.int32, (1, tt_len, 1), 1)
    prev3 = jnp.where(t3 == 0, carry_scr[...], pltpu.roll(cc3, 1, 1))
    carry_scr[...] = cc3[:, tt_len - 1:tt_len, :]
    xm = (cc3 + (prev3 - cc3) * mu_ref[...][None]).reshape(rows, RW_COLS)
    r = xm[:, 0:BRANCH_W]
    k = xm[:, BRANCH_W:2 * BRANCH_W]
    v = xm[:, 2 * BRANCH_W:3 * BRANCH_W]
    lo = xm[:, 3 * BRANCH_W:]
    lane_l = lax.broadcasted_iota(jnp.int32, (1, RW_LORA), 1)
    act = jnp.where(lane_l < RW_LORA_W, jnp.tanh(lo),
                    jnp.where(lane_l < RW_LORA_W + RW_LORA_A, lo, _sigmoid_t(lo)))
    z = _dot(act, wl_ref[...])
    ld = (-math.exp(-0.5)) * _sigmoid_t(w0 + z[:, 0:BRANCH_W])
    a = _sigmoid_t(a0 + z[:, BRANCH_W:2 * BRANCH_W])
    kk = k * k_k
    kbar = k * (1.0 + (a - 1.0) * k_a)
    sums = _seg_sum(jnp.concatenate([kk * kk, r * kbar * r_k], axis=0), RW_HD)
    kap = kk * lax.rsqrt(jnp.maximum(sums[:rows], 1e-24))
    if s_valid < s_len:
        t_in = lax.broadcasted_iota(jnp.int32, (rows, 1), 0) % tt_len
        valid = (tt * tt_len + t_in) < s_valid
        ld = jnp.where(valid, ld, 0.0)
        kap = jnp.where(valid, kap, 0.0)
        kbar = jnp.where(valid, kbar, 0.0)
    lw, ltot = _chunk_sums(cum_ref[...], ld, nbb, tt_len)
    back = ltot - lw
    e_in = jnp.exp(lw)
    e_neg = jnp.exp(-lw)
    e_back = jnp.exp(back)
    at_scr[...] = -kap * jnp.exp(lw - ld)
    rt_scr[...] = r * e_in
    bt_scr[...] = kap * a * e_neg
    kt_scr[...] = kbar * e_neg
    bp_scr[...] = kap * a * e_back
    kp_scr[...] = kbar * e_back
    v_scr[...] = v
    pc_scr[...] = jnp.exp(ltot)
    bonus_scr[...] = sums[rows:]
    gate_scr[...] = z[:, 2 * BRANCH_W:]

    si = lax.broadcasted_iota(jnp.int32, (c4, RW_GW), 0)
    sj = lax.broadcasted_iota(jnp.int32, (c4, RW_GW), 1)
    head_rows = si // c == sj // RW_HD
    qi = lax.broadcasted_iota(jnp.int32, (c4, c4), 0)
    qj = lax.broadcasted_iota(jnp.int32, (c4, c4), 1)
    same = qi // c == qj // c
    strict = jnp.logical_and(same, qi % c > qj % c)
    incl = jnp.logical_and(same, qi % c >= qj % c)
    eye = (qi == qj).astype(F32)
    fuse_sq = c4 % LANES == 0

    def stack(x):
        return jnp.where(head_rows, jnp.concatenate([x] * RW_GROUP, axis=0), 0.0)

    def unstack(x):
        out = x[0:c]
        for h in range(1, RW_GROUP):
            out = out + x[h * c:(h + 1) * c]
        return out

    def phase_a(it, carry):
        chains = [(it * ua + u, g) for u in range(ua) for g in range(n_groups)]
        nmats = []
        for ch, g in chains:
            r0 = pl.multiple_of(ch * c, c)
            ln = slice(g * RW_GW, (g + 1) * RW_GW)
            lhs = jnp.concatenate([stack(at_scr[pl.ds(r0, c), ln]), stack(rt_scr[pl.ds(r0, c), ln])], axis=0)
            rhs = jnp.concatenate([bt_scr[pl.ds(r0, c), ln]] * RW_GROUP + [kt_scr[pl.ds(r0, c), ln]] * RW_GROUP,
                                  axis=0)
            quad = _dot_nt(lhs, rhs)
            nmats.append(jnp.where(strict, quad[:c4, :c4], 0.0))
            aak_scr[ch, g] = unstack(jnp.where(strict, quad[:c4, c4:], 0.0))
            arb_scr[ch, g] = unstack(jnp.where(incl, quad[c4:, :c4], 0.0))
            ark_scr[ch, g] = unstack(jnp.where(incl, quad[c4:, c4:], 0.0))
        tinvs = [eye + n for n in nmats]
        npows = [_dot(n, n) for n in nmats]
        span = 2
        while 2 * span < c:
            if fuse_sq:
                boths = [_dot(p, jnp.concatenate([p, t], axis=1)) for p, t in zip(npows, tinvs)]
                tinvs = [t + bo[:, c4:] for t, bo in zip(tinvs, boths)]
                npows = [bo[:, :c4] for bo in boths]
            else:
                tinvs = [t + _dot(p, t) for p, t in zip(npows, tinvs)]
                npows = [_dot(p, p) for p in npows]
            span *= 2
        tinvs = [t + _dot(p, t) for p, t in zip(npows, tinvs)]
        for (ch, g), t in zip(chains, tinvs):
            tcat_scr[ch, g] = unstack(t)
        return carry

    if n_ch // ua == 1:
        phase_a(0, 0)
    else:
        lax.fori_loop(0, n_ch // ua, phase_a, 0)

    def phase_b(it, carry):
        cb = it // (nbb // ub)
        b0 = (it % (nbb // ub)) * ub
        chains = [(b0 + u, g) for u in range(ub) for g in range(n_groups)]
        ops = []
        for bb, g in chains:
            ch = bb * n_cb + cb
            r0 = pl.multiple_of(ch * c, c)
            ln = slice(g * RW_GW, (g + 1) * RW_GW)
            ops.append(dict(bb=bb, g=g, ch=ch, r0=r0, ln=ln, sbd=sbd_scr[bb, g],
                            v_bd=stack(v_scr[pl.ds(r0, c), ln])))
        wmats = [_dot_nt(at_scr[pl.ds(q["r0"], c), q["ln"]], q["sbd"]) + _dot(aak_scr[q["ch"], q["g"]], q["v_bd"])
                 for q in ops]
        us = [_dot(tcat_scr[q["ch"], q["g"]], stack(w)) for q, w in zip(ops, wmats)]
        for q, u in zip(ops, us):
            r0, ln = q["r0"], q["ln"]
            o_scr[pl.ds(r0, c), ln] = (_dot_nt(rt_scr[pl.ds(r0, c), ln], q["sbd"])
                                       + _dot(arb_scr[q["ch"], q["g"]], stack(u))
                                       + _dot(ark_scr[q["ch"], q["g"]], q["v_bd"]))
        for q, u in zip(ops, us):
            r0, ln = q["r0"], q["ln"]
            upd = _dot_tn(u, bp_scr[pl.ds(r0, c), ln]) + _dot_tn(v_scr[pl.ds(r0, c), ln], kp_scr[pl.ds(r0, c), ln])
            sbd_scr[q["bb"], q["g"]] = q["sbd"] * pc_scr[pl.ds(r0, 1), ln] + jnp.where(bd_state, upd, 0.0)
        return carry

    n_it = n_cb * (nbb // ub)
    if n_it == 1:
        phase_b(0, 0)
    else:
        lax.fori_loop(0, n_it, phase_b, 0)

    o = o_scr[...]
    inv_n = 1.0 / RW_HD
    mean = _seg_sum(o, RW_HD) * inv_n
    cen = o - mean
    var = _seg_sum(cen * cen, RW_HD) * inv_n
    on = cen * lax.rsqrt(var + RW_GN_EPS) * ln_g + ln_b
    y = (on + bonus_scr[...] * v_scr[...]) * gate_scr[...]
    y_ref[...] = y.reshape(nbb, tt_len, BRANCH_W).astype(y_ref.dtype)

    @pl.when(tt == n_t - 1)
    def _():
        out = _own_layer_block(sout_ref, layer)

        def fin(bb, carry):
            for g in range(n_groups):
                sbd = sbd_scr[bb, g]
                for h in range(RW_GROUP):
                    out[bb, g * RW_GROUP + h] = sbd[h * RW_HD:(h + 1) * RW_HD, h * RW_HD:(h + 1) * RW_HD]
            return carry

        lax.fori_loop(0, nbb, fin, 0)


def _rwkv_kernel_inplace(c_ref, sh0_ref, s0_ref, mu_ref, wl_ref, vec_ref, cum_ref, sbuf_ref, *rest, **kw):
    _rwkv_kernel(c_ref, sh0_ref, s0_ref, mu_ref, wl_ref, vec_ref, cum_ref, *rest, **kw)


def _rwkv(proj3, sh0, s0, sbuf, mu, wl, vec, *, nbb, tt_len, chunk, s_valid, layer, ua, ub):
    nb, s_len, _ = proj3.shape
    kern = functools.partial(_rwkv_kernel if layer == 0 else _rwkv_kernel_inplace, nbb=nbb, tt_len=tt_len,
                             chunk=chunk, s_valid=s_valid, s_len=s_len, layer=layer, ua=ua, ub=ub)
    st_in = pl.BlockSpec((None, nbb, RW_HEADS, RW_HD, RW_HD), lambda b, t: (layer, b, 0, 0, 0))
    st_out, extra_specs, extra_args = _state_out(s0, sbuf, layer, nbb, (RW_HEADS, RW_HD, RW_HD))
    full2 = lambda b, t: (0, 0)
    rows = nbb * tt_len
    n_groups = RW_HEADS // RW_GROUP
    mats = pltpu.VMEM((rows // chunk, n_groups, chunk, RW_GROUP * chunk), F32)
    return pl.pallas_call(
        kern,
        grid=(nb // nbb, s_len // tt_len),
        in_specs=[
            pl.BlockSpec((nbb, tt_len, RW_BLOCK), lambda b, t: (b, t, OFF_RW // RW_BLOCK)),
            pl.BlockSpec((nbb, 1, RW_COLS), lambda b, t: (b, 0, 0)),
            st_in,
            pl.BlockSpec((1, RW_COLS), full2),
            pl.BlockSpec((RW_LORA, 3 * BRANCH_W), full2),
            pl.BlockSpec((SUBLANES, BRANCH_W), full2),
            pl.BlockSpec((2 * tt_len, tt_len), full2),
        ] + extra_specs,
        out_specs=[pl.BlockSpec((nbb, tt_len, BRANCH_W), lambda b, t: (b, t, 0)), st_out],
        out_shape=[jax.ShapeDtypeStruct((nb, s_len, BRANCH_W), proj3.dtype),
                   jax.ShapeDtypeStruct(s0.shape, F32)],
        input_output_aliases={7: 1} if extra_args else {},
        scratch_shapes=[
            pltpu.VMEM((nbb, n_groups, RW_GW, RW_GW), F32),
            pltpu.VMEM((nbb, 1, RW_COLS), F32),
        ] + [pltpu.VMEM((rows, BRANCH_W), F32)] * 11 + [mats] * 4,
        compiler_params=_params(("parallel", "arbitrary")),
        name="rwkv7",
    )(proj3, sh0, s0, mu, wl, vec, _chunk_cum_matrix(tt_len, chunk), *extra_args)


def _mix_kernel(x_ref, gts_ref, ya_ref, yb_ref, yc_ref, wb_ref, wo_ref, g_ref, o_ref):
    acc = None
    for n, y_ref in enumerate((ya_ref, yb_ref, yc_ref)):
        up = _dot(y_ref[...], wb_ref[n * BRANCH_W:(n + 1) * BRANCH_W, :])
        term = _sigmoid_t(gts_ref[:, n * D_MODEL:(n + 1) * D_MODEL].astype(F32)) * up
        acc = term if acc is None else acc + term
    mix = _dot(acc, wo_ref[...])
    o_ref[...] = x_ref[...] + _rms(mix, g_ref[...])


def _mix(x, proj, ya, yb, yc, wb, wo, g, tm, layer):
    t = x.shape[0]
    row = lambda w: pl.BlockSpec((tm, w), lambda i: (i, 0))
    return pl.pallas_call(
        _mix_kernel,
        grid=(t // tm,),
        in_specs=[row(D_MODEL), row(N_BRANCH * D_MODEL), row(BRANCH_W), row(BRANCH_W), row(BRANCH_W),
                  pl.BlockSpec((None, N_BRANCH * BRANCH_W, D_MODEL), lambda i: (layer, 0, 0)),
                  pl.BlockSpec((None, D_MODEL, D_MODEL), lambda i: (layer, 0, 0)),
                  pl.BlockSpec((1, D_MODEL), lambda i: (0, 0))],
        out_specs=row(D_MODEL),
        out_shape=jax.ShapeDtypeStruct((t, D_MODEL), F32),
        compiler_params=_params(("parallel",)),
        name="branch_mix",
    )(x, proj, ya, yb, yc, wb, wo, g)


def _ffn_kernel(x_ref, p_ref, gpre_ref, wg_ref, wu_ref, wd_ref, gpost_ref, wple_ref, wpg_ref, gple_ref,
                o_ref, h_scr, acc_scr):
    j = pl.program_id(1)

    @pl.when(j == 0)
    def _():
        h_scr[...] = _rms(x_ref[...], gpre_ref[...]).astype(BF16)
        acc_scr[...] = jnp.zeros_like(acc_scr)

    h = h_scr[...]
    gt = jnp.dot(h, wg_ref[...], preferred_element_type=F32)
    up = jnp.dot(h, wu_ref[...], preferred_element_type=F32)
    acc_scr[...] += _dot(gt * _sigmoid_t(gt) * up, wd_ref[...])

    @pl.when(j == pl.num_programs(1) - 1)
    def _():
        x2 = x_ref[...] + _rms(acc_scr[...], gpost_ref[...])
        ple = _dot(p_ref[...], wple_ref[...]) * _sigmoid_t(_dot(x2, wpg_ref[...]))
        o_ref[...] = x2 + _rms(ple, gple_ref[...])


def _ffn(x, p, gpre, wg, wu, wd, gpost, wple, wpg, gple, tm, n_split, layer):
    t = x.shape[0]
    d_ff = wg.shape[2]
    tf = d_ff // n_split
    vecspec = pl.BlockSpec((1, D_MODEL), lambda i, j: (0, 0))
    return pl.pallas_call(
        _ffn_kernel,
        grid=(t // tm, n_split),
        in_specs=[
            pl.BlockSpec((tm, D_MODEL), lambda i, j: (i, 0)),
            pl.BlockSpec((None, tm, PLE_DIM), lambda i, j: (layer, i, 0)),
            vecspec,
            pl.BlockSpec((None, D_MODEL, tf), lambda i, j: (layer, 0, j)),
            pl.BlockSpec((None, D_MODEL, tf), lambda i, j: (layer, 0, j)),
            pl.BlockSpec((None, tf, D_MODEL), lambda i, j: (layer, j, 0)),
            vecspec,
            pl.BlockSpec((None, PLE_DIM, D_MODEL), lambda i, j: (layer, 0, 0)),
            pl.BlockSpec((None, D_MODEL, D_MODEL), lambda i, j: (layer, 0, 0)),
            vecspec,
        ],
        out_specs=pl.BlockSpec((tm, D_MODEL), lambda i, j: (i, 0)),
        out_shape=jax.ShapeDtypeStruct((t, D_MODEL), F32),
        scratch_shapes=[pltpu.VMEM((tm, D_MODEL), BF16), pltpu.VMEM((tm, D_MODEL), F32)],
        compiler_params=_params(("parallel", "arbitrary")),
        name="ffn_ple",
    )(x, p, gpre, wg, wu, wd, gpost, wple, wpg, gple)


def _cast_kernel(x_ref, o_ref):
    o_ref[...] = x_ref[...].astype(o_ref.dtype)


def _to_bf16(w, tr):
    d, r, c = w.shape
    spec = pl.BlockSpec((None, tr, c), lambda l, i: (l, i, 0))
    return pl.pallas_call(
        _cast_kernel, grid=(d, r // tr), in_specs=[spec], out_specs=spec,
        out_shape=jax.ShapeDtypeStruct(w.shape, BF16),
        compiler_params=_params(("parallel", "parallel")), name="to_bf16",
    )(w)


W_IN_TILE = 256


def _w_in_kernel(x_ref, o_ref):
    is_pad = pl.program_id(1) >= (OFF_RW + RW_COLS) // W_IN_TILE
    o_ref[...] = jnp.where(is_pad, 0.0, x_ref[...]).astype(o_ref.dtype)


def _w_in_layout(w_in):
    d = w_in.shape[0]
    n_gate = N_BRANCH * D_MODEL // W_IN_TILE
    n_rest = (OFF_RW + RW_COLS) // W_IN_TILE - n_gate

    def src(l, j):
        return l, 0, jnp.where(j < n_gate, j + n_rest, jnp.where(j < n_gate + n_rest, j - n_gate, 0))

    return pl.pallas_call(
        _w_in_kernel, grid=(d, PROJ_COLS // W_IN_TILE),
        in_specs=[pl.BlockSpec((None, D_MODEL, W_IN_TILE), src)],
        out_specs=pl.BlockSpec((None, D_MODEL, W_IN_TILE), lambda l, j: (l, 0, j)),
        out_shape=jax.ShapeDtypeStruct((d, D_MODEL, PROJ_COLS), BF16),
        compiler_params=_params(("parallel", "parallel")), name="w_in_layout",
    )(w_in)


def _block_diag(w):
    n, r, c = w.shape
    eye = jnp.eye(n, dtype=w.dtype)
    return (eye[:, None, :, None] * w[:, :, None, :]).reshape(n * r, n * c)


def _prep_layer(i, W):
    lora = jnp.zeros((RW_LORA, 3 * BRANCH_W), F32)
    lora = lora.at[0:RW_LORA_W, 0:BRANCH_W].set(W["rw_w_up"][i])
    lora = lora.at[RW_LORA_W:RW_LORA_W + RW_LORA_A, BRANCH_W:2 * BRANCH_W].set(W["rw_a_up"][i])
    lora = lora.at[RW_LORA_W + RW_LORA_A:, 2 * BRANCH_W:].set(W["rw_g_up"][i])
    vec = jnp.stack([W["rw_w0"][i], W["rw_a0"][i], W["rw_k_k"][i], W["rw_k_a"][i],
                     W["rw_r_k"][i].reshape(BRANCH_W), W["rw_ln_g"][i], W["rw_ln_b"][i],
                     jnp.zeros((BRANCH_W,), F32)])
    row = lambda name: W[name][i].reshape(1, -1)
    return dict(
        norm_pre_mix=row("norm_pre_mix"),
        conv_w=W["conv_w"][i], conv_b=row("conv_b"),
        lru_w=jnp.concatenate([_block_diag(W["lru_wa"][i]), _block_diag(W["lru_wx"][i])], axis=1).astype(BF16),
        lru_b=jnp.stack([W["lru_ba"][i], W["lru_bx"][i]]), lru_lambda=row("lru_lambda"),
        hg_norm_g=row("hg_norm_g"),
        rw_mu=row("rw_mu"), rw_lora=lora.astype(BF16), rw_vec=vec,
        norm_post_mix=row("norm_post_mix"), norm_pre_ffn=row("norm_pre_ffn"),
        norm_post_ffn=row("norm_post_ffn"), norm_ple=row("norm_ple"),
    )


def _tiles(nb, s_len):
    t = nb * s_len
    tm_in = min(t, 1024)
    tm_tok = min(t, 512)
    if s_len >= 512:
        return dict(tm_in=tm_in, tn_in=2048, proj_dtype=BF16, tm_mix=tm_tok, tm_ffn=tm_tok, ffn_split=2,
                    lru_tt=128,
                    hg=dict(nbb=2, tt_len=256, chunk=64, ub=2),
                    rw=dict(nbb=8, tt_len=64, chunk=64, ua=4, ub=8))
    return dict(tm_in=tm_in, tn_in=2048, proj_dtype=F32, tm_mix=tm_tok, tm_ffn=tm_tok, ffn_split=2,
                lru_tt=s_len,
                hg=dict(nbb=min(nb, 16), tt_len=s_len, chunk=s_len, ub=4),
                rw=dict(nbb=min(nb, 16), tt_len=s_len, chunk=s_len, ua=8, ub=8))


def _run_trunk(x3, p4, states, layers, big, lbraw, *, s_valid, pos0_is_zero):
    nb, s_len, _ = x3.shape
    t = nb * s_len
    plan = _tiles(nb, s_len)
    conv0, lru0, hg0, rw0, sh0 = states
    x = x3.reshape(t, D_MODEL)
    new = ([], [], [])
    nhg = nrw = None
    for i, L in enumerate(layers):
        proj = _in_proj(x, L["norm_pre_mix"], big["w_in"], plan["tm_in"], plan["tn_in"], plan["proj_dtype"], i)
        proj3 = proj.reshape(nb, s_len, PROJ_COLS)
        prev8 = jnp.pad(conv0[i], ((0, 0), (SUBLANES - (CONV_W - 1), 0), (0, 0)))
        ya, nlru = _lru(proj3, prev8, lru0[i], L["conv_w"], L["conv_b"], L["lru_w"], L["lru_b"], L["lru_lambda"],
                        tt_len=plan["lru_tt"], s_valid=s_valid, pos0_is_zero=pos0_is_zero)
        yb, nhg = _hgrn(proj3, hg0, nhg, lbraw, L["hg_norm_g"], s_valid=s_valid, layer=i, **plan["hg"])
        yc, nrw = _rwkv(proj3, sh0[i].reshape(nb, 1, RW_COLS), rw0, nrw, L["rw_mu"], L["rw_lora"], L["rw_vec"],
                        s_valid=s_valid, layer=i, **plan["rw"])
        x = _mix(x, proj, ya.reshape(t, BRANCH_W), yb.reshape(t, BRANCH_W), yc.reshape(t, BRANCH_W),
                 big["w_branch"], big["w_out"], L["norm_post_mix"], plan["tm_mix"], i)
        x = _ffn(x, p4.reshape(-1, t, PLE_DIM), L["norm_pre_ffn"], big["w_ffn_gate"], big["w_ffn_up"],
                 big["w_ffn_down"], L["norm_post_ffn"], big["w_ple"], big["w_ple_gate"], L["norm_ple"], plan["tm_ffn"], plan["ffn_split"], i)
        nconv = proj3[:, s_valid - (CONV_W - 1):s_valid, OFF_LRU:OFF_LRU + BRANCH_W].astype(F32)
        nsh = proj3[:, s_valid - 1, OFF_RW:OFF_RW + RW_COLS].astype(F32)
        for lst, val in zip(new, (nconv, nlru, nsh)):
            lst.append(val)
    nconv, nlru, nsh = (jnp.stack(l) for l in new)
    return x.reshape(nb, s_len, D_MODEL), (nconv, nlru, nhg, nrw, nsh)


def kernel(x_prompt, x_sample, p_prompt, p_sample, state_conv_a, state_lru_a, state_hgrn, state_rwkv, state_shift_c, norm_pre_mix, w_in, conv_w, conv_b, lru_wa, lru_ba, lru_wx, lru_bx, lru_lambda, hg_lower_bounds, hg_norm_g, rw_mu, rw_w0, rw_w_up, rw_a0, rw_a_up, rw_g_up, rw_k_k, rw_k_a, rw_r_k, rw_ln_g, rw_ln_b, w_branch, w_out, norm_post_mix, norm_pre_ffn, w_ffn_gate, w_ffn_up, w_ffn_down, norm_post_ffn, w_ple, w_ple_gate, norm_ple):
    W = dict(norm_pre_mix=norm_pre_mix, w_in=w_in, conv_w=conv_w, conv_b=conv_b, lru_wa=lru_wa, lru_ba=lru_ba,
             lru_wx=lru_wx, lru_bx=lru_bx, lru_lambda=lru_lambda, hg_norm_g=hg_norm_g, rw_mu=rw_mu, rw_w0=rw_w0,
             rw_w_up=rw_w_up, rw_a0=rw_a0, rw_a_up=rw_a_up, rw_g_up=rw_g_up, rw_k_k=rw_k_k, rw_k_a=rw_k_a,
             rw_r_k=rw_r_k, rw_ln_g=rw_ln_g, rw_ln_b=rw_ln_b, w_branch=w_branch, w_out=w_out,
             norm_post_mix=norm_post_mix, norm_pre_ffn=norm_pre_ffn, w_ffn_gate=w_ffn_gate, w_ffn_up=w_ffn_up,
             w_ffn_down=w_ffn_down, norm_post_ffn=norm_post_ffn, w_ple=w_ple, w_ple_gate=w_ple_gate, norm_ple=norm_ple)
    depth = w_in.shape[0]
    d_ff = w_ffn_gate.shape[2]
    big = dict(
        w_in=_w_in_layout(w_in),
        w_branch=_to_bf16(w_branch.reshape(depth, N_BRANCH * BRANCH_W, D_MODEL), N_BRANCH * BRANCH_W // 2),
        w_out=_to_bf16(w_out, D_MODEL),
        w_ffn_gate=_to_bf16(w_ffn_gate, D_MODEL // 2), w_ffn_up=_to_bf16(w_ffn_up, D_MODEL // 2),
        w_ffn_down=_to_bf16(w_ffn_down, d_ff // 2),
        w_ple=_to_bf16(w_ple, PLE_DIM), w_ple_gate=_to_bf16(w_ple_gate, D_MODEL),
    )
    layers = [_prep_layer(i, W) for i in range(depth)]
    lbraw = hg_lower_bounds.astype(F32)

    bp, sp, _ = x_prompt.shape
    zeros = lambda *shape: jnp.zeros((depth, bp) + shape, F32)
    zero_states = (zeros(CONV_W - 1, BRANCH_W), zeros(BRANCH_W), zeros(HG_HEADS, HG_D, HG_D),
                   zeros(RW_HEADS, RW_HD, RW_HD), zeros(RW_COLS))
    y_prompt, st_p = _run_trunk(x_prompt, p_prompt, zero_states, layers, big, lbraw, s_valid=sp, pos0_is_zero=True)

    bs, ss, _ = x_sample.shape
    ss_pad = -(-ss // SUBLANES) * SUBLANES
    xs = jnp.pad(x_sample, ((0, 0), (0, ss_pad - ss), (0, 0)))
    ps = jnp.pad(p_sample, ((0, 0), (0, 0), (0, ss_pad - ss), (0, 0)))
    y_sample, st_s = _run_trunk(xs, ps, (state_conv_a, state_lru_a, state_hgrn, state_rwkv, state_shift_c),
                                layers, big, lbraw, s_valid=ss, pos0_is_zero=False)
    return (y_prompt, y_sample[:, :ss]) + st_p + st_s
```

```python
import functools
import math

import jax
import jax.numpy as jnp
from jax import lax
from jax.experimental import pallas as pl
from jax.experimental.pallas import tpu as pltpu

F32 = jnp.float32
BF16 = jnp.bfloat16

D_MODEL = 1024
BRANCH_W = 512
N_BRANCH = 3
LRU_BLOCKS = 8
LRU_BW = BRANCH_W // LRU_BLOCKS
CONV_W = 4
LRU_C = 8.0
HG_HEADS = 4
HG_D = BRANCH_W // HG_HEADS
HG_F_MIN = 1e-20
RW_HD = 64
RW_HEADS = BRANCH_W // RW_HD
RW_LORA_W = 64
RW_LORA_A = 64
RW_LORA_G = 128
RW_LORA = RW_LORA_W + RW_LORA_A + RW_LORA_G
RW_GN_EPS = 64e-5
RW_COLS = 3 * BRANCH_W + RW_LORA
PLE_DIM = 256
EPS = 1e-6

SUBLANES = 8
LANES = 128
MXU_DIM = 256
VMEM_LIMIT = 56 * 1024 * 1024

PROJ_COLS = 8192
OFF_GATES = 0
OFF_LRU = N_BRANCH * D_MODEL
OFF_HG = OFF_LRU + 2 * BRANCH_W
OFF_RW = OFF_HG + 4 * BRANCH_W
RW_BLOCK = PROJ_COLS - OFF_RW

RW_GROUP = LANES // RW_HD
RW_GW = RW_GROUP * RW_HD


def _params(sem):
    return pltpu.CompilerParams(dimension_semantics=sem, vmem_limit_bytes=VMEM_LIMIT)


def _rms(x, g):
    return x * lax.rsqrt(jnp.mean(x * x, axis=-1, keepdims=True) + EPS) * g


def _sigmoid(x):
    return 1.0 / (1.0 + jnp.exp(-x))


def _sigmoid_t(x):
    return 0.5 * jnp.tanh(0.5 * x) + 0.5


def _softplus(x):
    return jnp.maximum(x, 0.0) + jnp.log1p(jnp.exp(-jnp.abs(x)))


def _dot(a, b):
    return jnp.dot(a.astype(BF16), b.astype(BF16), preferred_element_type=F32)


def _dot_nt(a, b):
    return lax.dot_general(a.astype(BF16), b.astype(BF16), (((1,), (1,)), ((), ())),
                           preferred_element_type=F32)


def _dot_tn(a, b):
    return lax.dot_general(a.astype(BF16), b.astype(BF16), (((0,), (0,)), ((), ())),
                           preferred_element_type=F32)


def _split2(x):
    hi = x.astype(BF16)
    return hi, (x - hi.astype(F32)).astype(BF16)


def _seg_sum(x, seg):
    i = lax.broadcasted_iota(jnp.int32, (MXU_DIM, MXU_DIM), 0)
    j = lax.broadcasted_iota(jnp.int32, (MXU_DIM, MXU_DIM), 1)
    ones = (i // seg == j // seg).astype(BF16)
    xb = x.astype(BF16)
    tiles = [jnp.dot(xb[:, l:l + MXU_DIM], ones, preferred_element_type=F32)
             for l in range(0, x.shape[1], MXU_DIM)]
    return jnp.concatenate(tiles, axis=1)


def _chunk_sums(cum, x, nbb, tt_len):
    hi, lo = _split2(x)
    m = cum.astype(BF16)
    pre, tot = [], []
    for b in range(nbb):
        rs = slice(b * tt_len, (b + 1) * tt_len)
        both = jnp.dot(m, lo[rs], preferred_element_type=F32) + jnp.dot(m, hi[rs], preferred_element_type=F32)
        pre.append(both[:tt_len])
        tot.append(both[tt_len:])
    if nbb == 1:
        return pre[0], tot[0]
    return jnp.concatenate(pre, axis=0), jnp.concatenate(tot, axis=0)


def _in_proj_kernel(x_ref, g_ref, w_ref, o_ref, h_scr):
    @pl.when(pl.program_id(1) == 0)
    def _():
        h_scr[...] = _rms(x_ref[...], g_ref[...]).astype(BF16)

    o_ref[...] = jnp.dot(h_scr[...], w_ref[...], preferred_element_type=F32).astype(o_ref.dtype)


def _in_proj(x, g, w, tm, tn, out_dtype, layer):
    t = x.shape[0]
    return pl.pallas_call(
        _in_proj_kernel,
        grid=(t // tm, PROJ_COLS // tn),
        in_specs=[
            pl.BlockSpec((tm, D_MODEL), lambda i, j: (i, 0)),
            pl.BlockSpec((1, D_MODEL), lambda i, j: (0, 0)),
            pl.BlockSpec((None, D_MODEL, tn), lambda i, j: (layer, 0, j)),
        ],
        out_specs=pl.BlockSpec((tm, tn), lambda i, j: (i, j)),
        out_shape=jax.ShapeDtypeStruct((t, PROJ_COLS), out_dtype),
        scratch_shapes=[pltpu.VMEM((tm, D_MODEL), BF16)],
        compiler_params=_params(("parallel", "arbitrary")),
        name="in_proj",
    )(x, g, w)


def _lru_kernel(xa_ref, ga_ref, prev8_ref, h0_ref, cw_ref, cb_ref, w_ref, bab_ref, lam_ref,
                y_ref, hout_ref, prev_scr, h_scr, *, nb, tt_len, s_valid, pos0_is_zero):
    tt = pl.program_id(0)
    rows = nb * tt_len

    @pl.when(tt == 0)
    def _():
        prev_scr[...] = prev8_ref[...]
        h_scr[...] = h0_ref[...]

    xa = xa_ref[...].astype(F32)
    n8 = tt_len // SUBLANES
    xa4 = xa.reshape(nb, n8, SUBLANES, BRANCH_W)
    prev4 = prev_scr[...][:, None]
    t8 = lax.broadcasted_iota(jnp.int32, (1, 1, SUBLANES, 1), 2)
    cw = cw_ref[...]
    xc4 = cb_ref[...][None, None] + cw[CONV_W - 1][None, None, None] * xa4
    for j in range(1, CONV_W):
        rot = pltpu.roll(xa4, j, 2)
        rot_before = pltpu.roll(prev4, j, 2)
        if n8 > 1:
            rot_before = jnp.concatenate([rot_before, rot[:, :n8 - 1]], axis=1)
        xc4 = xc4 + cw[CONV_W - 1 - j][None, None, None] * jnp.where(t8 < j, rot_before, rot)
    prev_scr[...] = xa[:, tt_len - SUBLANES:, :]

    xc2 = xc4.reshape(rows, BRANCH_W)
    z = _dot(xc2, w_ref[...])
    bab = bab_ref[...]
    r = _sigmoid_t(z[:, :BRANCH_W] + bab[0:1])
    i = _sigmoid_t(z[:, BRANCH_W:] + bab[1:2])
    log_a = (-LRU_C) * r * _softplus(-lam_ref[...])
    a = jnp.exp(log_a)
    m2 = jnp.maximum(1.0 - a * a, 0.0)
    mult = m2 * lax.rsqrt(jnp.maximum(m2, 1e-30))
    t_in = lax.broadcasted_iota(jnp.int32, (rows, 1), 0) % tt_len
    if pos0_is_zero:
        mult = jnp.where(jnp.logical_and(tt == 0, t_in == 0), 1.0, mult)
    b = xc2 * i * mult
    valid = (tt * tt_len + t_in) < s_valid
    a4 = jnp.where(valid, a, 1.0).reshape(nb, n8, SUBLANES, BRANCH_W)
    b4 = jnp.where(valid, b, 0.0).reshape(nb, n8, SUBLANES, BRANCH_W)

    d = 1
    while d < SUBLANES:
        keep = t8 >= d
        b4 = a4 * jnp.where(keep, pltpu.roll(b4, d, 2), 0.0) + b4
        a4 = a4 * jnp.where(keep, pltpu.roll(a4, d, 2), 1.0)
        d *= 2
    h_in = h_scr[...][:, None, :]
    blocks = []
    for blk in range(n8):
        hb = a4[:, blk] * h_in + b4[:, blk]
        blocks.append(hb)
        h_in = hb[:, SUBLANES - 1:SUBLANES, :]
    hh = blocks[0] if n8 == 1 else jnp.concatenate(blocks, axis=1)
    h = h_in.reshape(nb, BRANCH_W)
    h_scr[...] = h
    hout_ref[...] = h

    ga = ga_ref[...].astype(F32)
    gelu = 0.5 * ga * (1.0 + jnp.tanh(math.sqrt(2.0 / math.pi) * (ga + 0.044715 * ga * ga * ga)))
    y_ref[...] = (hh * gelu).astype(y_ref.dtype)


def _lru(proj3, prev8, h0, cw, cb, w, bab, lam, *, tt_len, s_valid, pos0_is_zero):
    nb, s_len, _ = proj3.shape
    blk = OFF_LRU // BRANCH_W
    kern = functools.partial(_lru_kernel, nb=nb, tt_len=tt_len, s_valid=s_valid, pos0_is_zero=pos0_is_zero)
    full2 = lambda t: (0, 0)
    return pl.pallas_call(
        kern,
        grid=(s_len // tt_len,),
        in_specs=[
            pl.BlockSpec((nb, tt_len, BRANCH_W), lambda t: (0, t, blk)),
            pl.BlockSpec((nb, tt_len, BRANCH_W), lambda t: (0, t, blk + 1)),
            pl.BlockSpec((nb, SUBLANES, BRANCH_W), lambda t: (0, 0, 0)),
            pl.BlockSpec((nb, BRANCH_W), full2),
            pl.BlockSpec((CONV_W, BRANCH_W), full2),
            pl.BlockSpec((1, BRANCH_W), full2),
            pl.BlockSpec((BRANCH_W, 2 * BRANCH_W), full2),
            pl.BlockSpec((2, BRANCH_W), full2),
            pl.BlockSpec((1, BRANCH_W), full2),
        ],
        out_specs=[
            pl.BlockSpec((nb, tt_len, BRANCH_W), lambda t: (0, t, 0)),
            pl.BlockSpec((nb, BRANCH_W), full2),
        ],
        out_shape=[
            jax.ShapeDtypeStruct((nb, s_len, BRANCH_W), proj3.dtype),
            jax.ShapeDtypeStruct((nb, BRANCH_W), F32),
        ],
        scratch_shapes=[
            pltpu.VMEM((nb, SUBLANES, BRANCH_W), F32),
            pltpu.VMEM((nb, BRANCH_W), F32),
        ],
        compiler_params=_params(("arbitrary",)),
        name="rglru",
    )(proj3, proj3, prev8, h0, cw, cb, w, bab, lam)


def _own_layer_block(sout_ref, layer):
    if layer > 0:
        return sout_ref
    if sout_ref.shape[0] > 1:
        sout_ref[1:] = jnp.zeros((sout_ref.shape[0] - 1,) + tuple(sout_ref.shape[1:]), sout_ref.dtype)
    return sout_ref.at[0]


def _state_out(s0, sbuf, layer, nbb, tail):
    depth = s0.shape[0]
    zeros = (0,) * len(tail)
    if layer == 0:
        spec = pl.BlockSpec((depth, nbb) + tail, lambda b, t: (0, b) + zeros)
        return spec, [], []
    spec = pl.BlockSpec((None, nbb) + tail, lambda b, t: (layer, b) + zeros)
    return spec, [pl.BlockSpec(memory_space=pl.ANY)], [sbuf]


def _hg_diag_blocks(qh, kh, vh, bch):
    c = qh.shape[0]
    nblk = c // SUBLANES
    q3 = qh.reshape(nblk, SUBLANES, HG_D)
    k3 = kh.reshape(nblk, SUBLANES, HG_D)
    v3 = vh.reshape(nblk, SUBLANES, HG_D)
    b3 = bch.reshape(nblk, SUBLANES, HG_D)
    tin = lax.broadcasted_iota(jnp.int32, (1, SUBLANES, 1), 1)
    o3 = jnp.zeros((nblk, SUBLANES, HG_D), F32)
    for s in range(SUBLANES):
        dec = jnp.exp(jnp.minimum(b3 - b3[:, s:s + 1, :], 0.0))
        w = jnp.sum(q3 * k3[:, s:s + 1, :] * dec, axis=-1, keepdims=True)
        w = jnp.where(tin >= s, w, 0.0)
        o3 = o3 + w * v3[:, s:s + 1, :]
    return o3.reshape(c, HG_D)


def _hg_level_refs(bch, h, c):
    gq, gk = [], []
    zero = jnp.zeros((h, HG_D), F32)
    for j in range(c // h):
        if j % 2 == 1:
            gq.append(jnp.broadcast_to(bch[j * h - 1:j * h, :], (h, HG_D)))
            gk.append(zero)
        else:
            gq.append(zero)
            gk.append(jnp.broadcast_to(bch[(j + 1) * h - 1:(j + 1) * h, :], (h, HG_D)))
    return jnp.concatenate(gq, axis=0), jnp.concatenate(gk, axis=0)


def _hgrn_kernel(q_ref, f_ref, v_ref, g_ref, s0_ref, lbraw_ref, ng_ref, cum_ref, y_ref, sout_ref,
                 st_scr, qs_scr, k_scr, v_scr, bc_scr, qe_scr, kh_scr, et_scr, o_scr,
                 *, nbb, tt_len, chunk, s_valid, s_len, layer, ub):
    tt = pl.program_id(1)
    n_t = pl.num_programs(1)
    c = chunk
    rows = nbb * tt_len
    n_cb = tt_len // c

    raw = lbraw_ref[...]
    ex = jnp.exp(raw - jnp.max(raw, axis=0, keepdims=True))
    sm = ex / jnp.sum(ex, axis=0, keepdims=True)
    lb = jnp.zeros((1, BRANCH_W), F32)
    for l in range(1, layer + 1):
        lb = lb + sm[l:l + 1]

    @pl.when(tt == 0)
    def _():
        def init(bb, carry):
            for hd in range(HG_HEADS):
                st_scr[bb, hd] = s0_ref[bb, hd].T
            return carry

        lax.fori_loop(0, nbb, init, 0)

    q = q_ref[...].astype(F32).reshape(rows, BRANCH_W)
    fp = f_ref[...].astype(F32).reshape(rows, BRANCH_W)
    sg = _sigmoid(fp)
    f = lb + (1.0 - lb) * sg
    k = (1.0 - lb) * (1.0 - sg)
    logf = jnp.log(jnp.maximum(f, HG_F_MIN))
    if s_valid < s_len:
        t_in = lax.broadcasted_iota(jnp.int32, (rows, 1), 0) % tt_len
        valid = (tt * tt_len + t_in) < s_valid
        k = jnp.where(valid, k, 0.0)
        logf = jnp.where(valid, logf, 0.0)
    bc, btot = _chunk_sums(cum_ref[...], logf, nbb, tt_len)
    qs = q * _sigmoid_t(q)
    qs_scr[...] = qs
    k_scr[...] = k
    v_scr[...] = v_ref[...].astype(F32).reshape(rows, BRANCH_W)
    bc_scr[...] = bc
    qe_scr[...] = qs * jnp.exp(bc)
    kh_scr[...] = k * jnp.exp(btot - bc)
    et_scr[...] = jnp.exp(btot)

    ti = lax.broadcasted_iota(jnp.int32, (c, 1), 0)
    ii = lax.broadcasted_iota(jnp.int32, (c, c), 0)
    jj = lax.broadcasted_iota(jnp.int32, (c, c), 1)
    levels = []
    h = c // 2
    while h >= SUBLANES:
        odd = (ti // h) % 2 == 1
        pair = jnp.logical_and(ii // (2 * h) == jj // (2 * h),
                               jnp.logical_and((ii // h) % 2 == 1, (jj // h) % 2 == 0))
        levels.append((h, odd, pair))
        h //= 2

    def step(it, carry):
        cb = it // (nbb // ub)
        b0 = (it % (nbb // ub)) * ub
        chains = []
        for u in range(ub):
            r0 = pl.multiple_of((b0 + u) * tt_len + cb * c, c)
            for hd in range(HG_HEADS):
                ln = slice(hd * HG_D, (hd + 1) * HG_D)
                chains.append(dict(bb=b0 + u, hd=hd, r0=r0, ln=ln, st=st_scr[b0 + u, hd],
                                   qh=qs_scr[pl.ds(r0, c), ln], kh=k_scr[pl.ds(r0, c), ln],
                                   vh=v_scr[pl.ds(r0, c), ln], bch=bc_scr[pl.ds(r0, c), ln]))
        outs = [_dot_nt(qe_scr[pl.ds(x["r0"], c), x["ln"]], x["st"]) for x in chains]
        if levels:
            amats = []
            for x in chains:
                amat = None
                for (h, odd, pair) in levels:
                    gq, gk = _hg_level_refs(x["bch"], h, c)
                    qt = jnp.where(odd, x["qh"] * jnp.exp(jnp.where(odd, x["bch"] - gq, 0.0)), 0.0)
                    kt = jnp.where(odd, 0.0, x["kh"] * jnp.exp(jnp.where(odd, 0.0, gk - x["bch"])))
                    term = jnp.where(pair, _dot_nt(qt, kt), 0.0)
                    amat = term if amat is None else amat + term
                amats.append(amat)
            outs = [o + _dot(a, x["vh"]) for o, a, x in zip(outs, amats, chains)]
        for o, x in zip(outs, chains):
            o_scr[pl.ds(x["r0"], c), x["ln"]] = o + _hg_diag_blocks(x["qh"], x["kh"], x["vh"], x["bch"])
        for x in chains:
            upd = _dot_tn(x["vh"], kh_scr[pl.ds(x["r0"], c), x["ln"]])
            st_scr[x["bb"], x["hd"]] = x["st"] * et_scr[pl.ds(x["r0"], 1), x["ln"]] + upd
        return carry

    n_it = n_cb * (nbb // ub)
    if n_it == 1:
        step(0, 0)
    else:
        lax.fori_loop(0, n_it, step, 0)

    g = g_ref[...].astype(F32).reshape(rows, BRANCH_W)
    ng = ng_ref[...]
    outs = []
    for hd in range(HG_HEADS):
        ln = slice(hd * HG_D, (hd + 1) * HG_D)
        o = o_scr[:, ln]
        outs.append(o * lax.rsqrt(jnp.mean(o * o, axis=-1, keepdims=True) + EPS) * ng[:, ln])
    y = jnp.concatenate(outs, axis=-1) * (g * _sigmoid_t(g))
    y_ref[...] = y.reshape(nbb, tt_len, BRANCH_W).astype(y_ref.dtype)

    @pl.when(tt == n_t - 1)
    def _():
        out = _own_layer_block(sout_ref, layer)

        def fin(bb, carry):
            for hd in range(HG_HEADS):
                out[bb, hd] = st_scr[bb, hd].T
            return carry

        lax.fori_loop(0, nbb, fin, 0)


def _hgrn_kernel_inplace(q_ref, f_ref, v_ref, g_ref, s0_ref, lbraw_ref, ng_ref, cum_ref, sbuf_ref, *rest, **kw):
    _hgrn_kernel(q_ref, f_ref, v_ref, g_ref, s0_ref, lbraw_ref, ng_ref, cum_ref, *rest, **kw)


def _chunk_cum_matrix(rows, c):
    i = jnp.arange(rows)[:, None]
    j = jnp.arange(rows)[None, :]
    same = i // c == j // c
    return jnp.concatenate([same & (i >= j), same], axis=0).astype(BF16)


def _hgrn(proj3, s0, sbuf, lbraw, ng, *, nbb, tt_len, chunk, s_valid, layer, ub):
    nb, s_len, _ = proj3.shape
    blk = OFF_HG // BRANCH_W
    rows = nbb * tt_len
    kern = functools.partial(_hgrn_kernel if layer == 0 else _hgrn_kernel_inplace, nbb=nbb, tt_len=tt_len,
                             chunk=chunk, s_valid=s_valid, s_len=s_len, layer=layer, ub=ub)
    seq = lambda k: pl.BlockSpec((nbb, tt_len, BRANCH_W), lambda b, t, k=k: (b, t, blk + k))
    st_in = pl.BlockSpec((None, nbb, HG_HEADS, HG_D, HG_D), lambda b, t: (layer, b, 0, 0, 0))
    st_out, extra_specs, extra_args = _state_out(s0, sbuf, layer, nbb, (HG_HEADS, HG_D, HG_D))
    return pl.pallas_call(
        kern,
        grid=(nb // nbb, s_len // tt_len),
        in_specs=[seq(0), seq(1), seq(2), seq(3), st_in,
                  pl.BlockSpec(lbraw.shape, lambda b, t: (0, 0)),
                  pl.BlockSpec((1, BRANCH_W), lambda b, t: (0, 0)),
                  pl.BlockSpec((2 * tt_len, tt_len), lambda b, t: (0, 0))] + extra_specs,
        out_specs=[pl.BlockSpec((nbb, tt_len, BRANCH_W), lambda b, t: (b, t, 0)), st_out],
        out_shape=[jax.ShapeDtypeStruct((nb, s_len, BRANCH_W), proj3.dtype),
                   jax.ShapeDtypeStruct(s0.shape, F32)],
        input_output_aliases={8: 1} if extra_args else {},
        scratch_shapes=[pltpu.VMEM((nbb, HG_HEADS, HG_D, HG_D), F32)]
        + [pltpu.VMEM((rows, BRANCH_W), F32)] * 8,
        compiler_params=_params(("parallel", "arbitrary")),
        name="hgrn2",
    )(proj3, proj3, proj3, proj3, s0, lbraw, ng, _chunk_cum_matrix(tt_len, chunk), *extra_args)


def _rwkv_kernel(c_ref, sh0_ref, s0_ref, mu_ref, wl_ref, vec_ref, cum_ref, y_ref, sout_ref,
                 sbd_scr, carry_scr, at_scr, rt_scr, bt_scr, kt_scr, bp_scr, kp_scr, v_scr, pc_scr,
                 bonus_scr, gate_scr, o_scr, tcat_scr, aak_scr, arb_scr, ark_scr,
                 *, nbb, tt_len, chunk, s_valid, s_len, layer, ua, ub):
    tt = pl.program_id(1)
    n_t = pl.num_programs(1)
    c = chunk
    c4 = RW_GROUP * c
    n_groups = RW_HEADS // RW_GROUP
    rows = nbb * tt_len
    n_cb = tt_len // c
    n_ch = rows // c

    vec = vec_ref[...]
    w0, a0, k_k, k_a, r_k, ln_g, ln_b = [vec[i:i + 1] for i in range(7)]
    gi_ = lax.broadcasted_iota(jnp.int32, (RW_GW, RW_GW), 0)
    gj_ = lax.broadcasted_iota(jnp.int32, (RW_GW, RW_GW), 1)
    bd_state = gi_ // RW_HD == gj_ // RW_HD

    @pl.when(tt == 0)
    def _():
        carry_scr[...] = sh0_ref[...]

        def init(bb, carry):
            for g in range(n_groups):
                blk = jnp.concatenate([s0_ref[bb, g * RW_GROUP + h] for h in range(RW_GROUP)], axis=0)
                sbd_scr[bb, g] = jnp.where(bd_state, jnp.concatenate([blk] * RW_GROUP, axis=1), 0.0)
            return carry

        lax.fori_loop(0, nbb, init, 0)

    cc3 = c_ref[:, :, :RW_COLS].astype(F32)
    t3 = lax.broadcasted_iota(jnp.int32, (1, tt_len, 1), 1)
    prev3 = jnp.where(t3 == 0, carry_scr[...], pltpu.roll(cc3, 1, 1))
    carry_scr[...] = cc3[:, tt_len - 1:tt_len, :]
    xm = (cc3 + (prev3 - cc3) * mu_ref[...][None]).reshape(rows, RW_COLS)
    r = xm[:, 0:BRANCH_W]
    k = xm[:, BRANCH_W:2 * BRANCH_W]
    v = xm[:, 2 * BRANCH_W:3 * BRANCH_W]
    lo = xm[:, 3 * BRANCH_W:]
    lane_l = lax.broadcasted_iota(jnp.int32, (1, RW_LORA), 1)
    act = jnp.where(lane_l < RW_LORA_W, jnp.tanh(lo),
                    jnp.where(lane_l < RW_LORA_W + RW_LORA_A, lo, _sigmoid_t(lo)))
    z = _dot(act, wl_ref[...])
    ld = (-math.exp(-0.5)) * _sigmoid_t(w0 + z[:, 0:BRANCH_W])
    a = _sigmoid_t(a0 + z[:, BRANCH_W:2 * BRANCH_W])
    kk = k * k_k
    kbar = k * (1.0 + (a - 1.0) * k_a)
    sums = _seg_sum(jnp.concatenate([kk * kk, r * kbar * r_k], axis=0), RW_HD)
    kap = kk * lax.rsqrt(jnp.maximum(sums[:rows], 1e-24))
    if s_valid < s_len:
        t_in = lax.broadcasted_iota(jnp.int32, (rows, 1), 0) % tt_len
        valid = (tt * tt_len + t_in) < s_valid
        ld = jnp.where(valid, ld, 0.0)
        kap = jnp.where(valid, kap, 0.0)
        kbar = jnp.where(valid, kbar, 0.0)
    lw, ltot = _chunk_sums(cum_ref[...], ld, nbb, tt_len)
    back = ltot - lw
    e_in = jnp.exp(lw)
    e_neg = jnp.exp(-lw)
    e_back = jnp.exp(back)
    at_scr[...] = -kap * jnp.exp(lw - ld)
    rt_scr[...] = r * e_in
    bt_scr[...] = kap * a * e_neg
    kt_scr[...] = kbar * e_neg
    bp_scr[...] = kap * a * e_back
    kp_scr[...] = kbar * e_back
    v_scr[...] = v
    pc_scr[...] = jnp.exp(ltot)
    bonus_scr[...] = sums[rows:]
    gate_scr[...] = z[:, 2 * BRANCH_W:]

    si = lax.broadcasted_iota(jnp.int32, (c4, RW_GW), 0)
    sj = lax.broadcasted_iota(jnp.int32, (c4, RW_GW), 1)
    head_rows = si // c == sj // RW_HD
    qi = lax.broadcasted_iota(jnp.int32, (c4, c4), 0)
    qj = lax.broadcasted_iota(jnp.int32, (c4, c4), 1)
    same = qi // c == qj // c
    strict = jnp.logical_and(same, qi % c > qj % c)
    incl = jnp.logical_and(same, qi % c >= qj % c)
    eye = (qi == qj).astype(F32)
    fuse_sq = c4 % LANES == 0

    def stack(x):
        return jnp.where(head_rows, jnp.concatenate([x] * RW_GROUP, axis=0), 0.0)

    def unstack(x):
        out = x[0:c]
        for h in range(1, RW_GROUP):
            out = out + x[h * c:(h + 1) * c]
        return out

    def phase_a(it, carry):
        chains = [(it * ua + u, g) for u in range(ua) for g in range(n_groups)]
        nmats = []
        for ch, g in chains:
            r0 = pl.multiple_of(ch * c, c)
            ln = slice(g * RW_GW, (g + 1) * RW_GW)
            lhs = jnp.concatenate([stack(at_scr[pl.ds(r0, c), ln]), stack(rt_scr[pl.ds(r0, c), ln])], axis=0)
            rhs = jnp.concatenate([bt_scr[pl.ds(r0, c), ln]] * RW_GROUP + [kt_scr[pl.ds(r0, c), ln]] * RW_GROUP,
                                  axis=0)
            quad = _dot_nt(lhs, rhs)
            nmats.append(jnp.where(strict, quad[:c4, :c4], 0.0))
            aak_scr[ch, g] = unstack(jnp.where(strict, quad[:c4, c4:], 0.0))
            arb_scr[ch, g] = unstack(jnp.where(incl, quad[c4:, :c4], 0.0))
            ark_scr[ch, g] = unstack(jnp.where(incl, quad[c4:, c4:], 0.0))
        tinvs = [eye + n for n in nmats]
        npows = [_dot(n, n) for n in nmats]
        span = 2
        while 2 * span < c:
            if fuse_sq:
                boths = [_dot(p, jnp.concatenate([p, t], axis=1)) for p, t in zip(npows, tinvs)]
                tinvs = [t + bo[:, c4:] for t, bo in zip(tinvs, boths)]
                npows = [bo[:, :c4] for bo in boths]
            else:
                tinvs = [t + _dot(p, t) for p, t in zip(npows, tinvs)]
                npows = [_dot(p, p) for p in npows]
            span *= 2
        tinvs = [t + _dot(p, t) for p, t in zip(npows, tinvs)]
        for (ch, g), t in zip(chains, tinvs):
            tcat_scr[ch, g] = unstack(t)
        return carry

    if n_ch // ua == 1:
        phase_a(0, 0)
    else:
        lax.fori_loop(0, n_ch // ua, phase_a, 0)

    def phase_b(it, carry):
        cb = it // (nbb // ub)
        b0 = (it % (nbb // ub)) * ub
        chains = [(b0 + u, g) for u in range(ub) for g in range(n_groups)]
        ops = []
        for bb, g in chains:
            ch = bb * n_cb + cb
            r0 = pl.multiple_of(ch * c, c)
            ln = slice(g * RW_GW, (g + 1) * RW_GW)
            ops.append(dict(bb=bb, g=g, ch=ch, r0=r0, ln=ln, sbd=sbd_scr[bb, g],
                            v_bd=stack(v_scr[pl.ds(r0, c), ln])))
        sprods = [_dot_nt(jnp.concatenate([at_scr[pl.ds(q["r0"], c), q["ln"]], rt_scr[pl.ds(q["r0"], c), q["ln"]]],
                                          axis=0), q["sbd"]) for q in ops]
        wmats = [sp[:c] + _dot(aak_scr[q["ch"], q["g"]], q["v_bd"]) for q, sp in zip(ops, sprods)]
        us = [_dot(tcat_scr[q["ch"], q["g"]], stack(w)) for q, w in zip(ops, wmats)]
        for q, u, sp in zip(ops, us, sprods):
            r0, ln = q["r0"], q["ln"]
            o_scr[pl.ds(r0, c), ln] = (sp[c:] + _dot(arb_scr[q["ch"], q["g"]], stack(u))
                                       + _dot(ark_scr[q["ch"], q["g"]], q["v_bd"]))
        for q, u in zip(ops, us):
            r0, ln = q["r0"], q["ln"]
            upd = _dot_tn(jnp.concatenate([u, v_scr[pl.ds(r0, c), ln]], axis=0),
                          jnp.concatenate([bp_scr[pl.ds(r0, c), ln], kp_scr[pl.ds(r0, c), ln]], axis=0))
            sbd_scr[q["bb"], q["g"]] = q["sbd"] * pc_scr[pl.ds(r0, 1), ln] + jnp.where(bd_state, upd, 0.0)
        return carry

    n_it = n_cb * (nbb // ub)
    if n_it == 1:
        phase_b(0, 0)
    else:
        lax.fori_loop(0, n_it, phase_b, 0)

    o = o_scr[...]
    inv_n = 1.0 / RW_HD
    mean = _seg_sum(o, RW_HD) * inv_n
    cen = o - mean
    var = _seg_sum(cen * cen, RW_HD) * inv_n
    on = cen * lax.rsqrt(var + RW_GN_EPS) * ln_g + ln_b
    y = (on + bonus_scr[...] * v_scr[...]) * gate_scr[...]
    y_ref[...] = y.reshape(nbb, tt_len, BRANCH_W).astype(y_ref.dtype)

    @pl.when(tt == n_t - 1)
    def _():
        out = _own_layer_block(sout_ref, layer)

        def fin(bb, carry):
            for g in range(n_groups):
                sbd = sbd_scr[bb, g]
                for h in range(RW_GROUP):
                    out[bb, g * RW_GROUP + h] = sbd[h * RW_HD:(h + 1) * RW_HD, h * RW_HD:(h + 1) * RW_HD]
            return carry

        lax.fori_loop(0, nbb, fin, 0)


def _rwkv_kernel_inplace(c_ref, sh0_ref, s0_ref, mu_ref, wl_ref, vec_ref, cum_ref, sbuf_ref, *rest, **kw):
    _rwkv_kernel(c_ref, sh0_ref, s0_ref, mu_ref, wl_ref, vec_ref, cum_ref, *rest, **kw)


def _rwkv(proj3, sh0, s0, sbuf, mu, wl, vec, *, nbb, tt_len, chunk, s_valid, layer, ua, ub):
    nb, s_len, _ = proj3.shape
    kern = functools.partial(_rwkv_kernel if layer == 0 else _rwkv_kernel_inplace, nbb=nbb, tt_len=tt_len,
                             chunk=chunk, s_valid=s_valid, s_len=s_len, layer=layer, ua=ua, ub=ub)
    st_in = pl.BlockSpec((None, nbb, RW_HEADS, RW_HD, RW_HD), lambda b, t: (layer, b, 0, 0, 0))
    st_out, extra_specs, extra_args = _state_out(s0, sbuf, layer, nbb, (RW_HEADS, RW_HD, RW_HD))
    full2 = lambda b, t: (0, 0)
    rows = nbb * tt_len
    n_groups = RW_HEADS // RW_GROUP
    mats = pltpu.VMEM((rows // chunk, n_groups, chunk, RW_GROUP * chunk), F32)
    return pl.pallas_call(
        kern,
        grid=(nb // nbb, s_len // tt_len),
        in_specs=[
            pl.BlockSpec((nbb, tt_len, RW_BLOCK), lambda b, t: (b, t, OFF_RW // RW_BLOCK)),
            pl.BlockSpec((nbb, 1, RW_COLS), lambda b, t: (b, 0, 0)),
            st_in,
            pl.BlockSpec((1, RW_COLS), full2),
            pl.BlockSpec((RW_LORA, 3 * BRANCH_W), full2),
            pl.BlockSpec((SUBLANES, BRANCH_W), full2),
            pl.BlockSpec((2 * tt_len, tt_len), full2),
        ] + extra_specs,
        out_specs=[pl.BlockSpec((nbb, tt_len, BRANCH_W), lambda b, t: (b, t, 0)), st_out],
        out_shape=[jax.ShapeDtypeStruct((nb, s_len, BRANCH_W), proj3.dtype),
                   jax.ShapeDtypeStruct(s0.shape, F32)],
        input_output_aliases={7: 1} if extra_args else {},
        scratch_shapes=[
            pltpu.VMEM((nbb, n_groups, RW_GW, RW_GW), F32),
            pltpu.VMEM((nbb, 1, RW_COLS), F32),
        ] + [pltpu.VMEM((rows, BRANCH_W), F32)] * 11 + [mats] * 4,
        compiler_params=_params(("parallel", "arbitrary")),
        name="rwkv7",
    )(proj3, sh0, s0, mu, wl, vec, _chunk_cum_matrix(tt_len, chunk), *extra_args)


def _mix_kernel(x_ref, gts_ref, ya_ref, yb_ref, yc_ref, wb_ref, wo_ref, g_ref, o_ref):
    acc = None
    for n, y_ref in enumerate((ya_ref, yb_ref, yc_ref)):
        up = _dot(y_ref[...], wb_ref[n * BRANCH_W:(n + 1) * BRANCH_W, :])
        term = _sigmoid_t(gts_ref[:, n * D_MODEL:(n + 1) * D_MODEL].astype(F32)) * up
        acc = term if acc is None else acc + term
    mix = _dot(acc, wo_ref[...])
    o_ref[...] = x_ref[...] + _rms(mix, g_ref[...])


def _mix(x, proj, ya, yb, yc, wb, wo, g, tm, layer):
    t = x.shape[0]
    row = lambda w: pl.BlockSpec((tm, w), lambda i: (i, 0))
    return pl.pallas_call(
        _mix_kernel,
        grid=(t // tm,),
        in_specs=[row(D_MODEL), row(N_BRANCH * D_MODEL), row(BRANCH_W), row(BRANCH_W), row(BRANCH_W),
                  pl.BlockSpec((None, N_BRANCH * BRANCH_W, D_MODEL), lambda i: (layer, 0, 0)),
                  pl.BlockSpec((None, D_MODEL, D_MODEL), lambda i: (layer, 0, 0)),
                  pl.BlockSpec((1, D_MODEL), lambda i: (0, 0))],
        out_specs=row(D_MODEL),
        out_shape=jax.ShapeDtypeStruct((t, D_MODEL), F32),
        compiler_params=_params(("parallel",)),
        name="branch_mix",
    )(x, proj, ya, yb, yc, wb, wo, g)


FFN_SUB = 256


def _ffn_kernel(x_ref, p_ref, gpre_ref, wg_ref, wu_ref, wd_ref, gpost_ref, wple_ref, wpg_ref, gple_ref,
                o_ref, h_scr, acc_scr):
    j = pl.program_id(1)

    @pl.when(j == 0)
    def _():
        h_scr[...] = _rms(x_ref[...], gpre_ref[...]).astype(BF16)
        acc_scr[...] = jnp.zeros_like(acc_scr)

    h = h_scr[...]
    tf = wg_ref.shape[1]
    for c0 in range(0, tf, FFN_SUB):
        c1 = min(c0 + FFN_SUB, tf)
        gt = jnp.dot(h, wg_ref[:, c0:c1], preferred_element_type=F32)
        up = jnp.dot(h, wu_ref[:, c0:c1], preferred_element_type=F32)
        acc_scr[...] += _dot(gt * _sigmoid_t(gt) * up, wd_ref[c0:c1, :])

    @pl.when(j == pl.num_programs(1) - 1)
    def _():
        x2 = x_ref[...] + _rms(acc_scr[...], gpost_ref[...])
        ple = _dot(p_ref[...], wple_ref[...]) * _sigmoid_t(_dot(x2, wpg_ref[...]))
        o_ref[...] = x2 + _rms(ple, gple_ref[...])


def _ffn(x, p, gpre, wg, wu, wd, gpost, wple, wpg, gple, tm, n_split, layer):
    t = x.shape[0]
    d_ff = wg.shape[2]
    tf = d_ff // n_split
    vecspec = pl.BlockSpec((1, D_MODEL), lambda i, j: (0, 0))
    return pl.pallas_call(
        _ffn_kernel,
        grid=(t // tm, n_split),
        in_specs=[
            pl.BlockSpec((tm, D_MODEL), lambda i, j: (i, 0)),
            pl.BlockSpec((None, tm, PLE_DIM), lambda i, j: (layer, i, 0)),
            vecspec,
            pl.BlockSpec((None, D_MODEL, tf), lambda i, j: (layer, 0, j)),
            pl.BlockSpec((None, D_MODEL, tf), lambda i, j: (layer, 0, j)),
            pl.BlockSpec((None, tf, D_MODEL), lambda i, j: (layer, j, 0)),
            vecspec,
            pl.BlockSpec((None, PLE_DIM, D_MODEL), lambda i, j: (layer, 0, 0), pipeline_mode=pl.Buffered(1)),
            pl.BlockSpec((None, D_MODEL, D_MODEL), lambda i, j: (layer, 0, 0), pipeline_mode=pl.Buffered(1)),
            vecspec,
        ],
        out_specs=pl.BlockSpec((tm, D_MODEL), lambda i, j: (i, 0), pipeline_mode=pl.Buffered(1)),
        out_shape=jax.ShapeDtypeStruct((t, D_MODEL), F32),
        scratch_shapes=[pltpu.VMEM((tm, D_MODEL), BF16), pltpu.VMEM((tm, D_MODEL), F32)],
        compiler_params=_params(("parallel", "arbitrary")),
        name="ffn_ple",
    )(x, p, gpre, wg, wu, wd, gpost, wple, wpg, gple)


def _cast_kernel(x_ref, o_ref):
    o_ref[...] = x_ref[...].astype(o_ref.dtype)


def _to_bf16(w, tr):
    d, r, c = w.shape
    spec = pl.BlockSpec((None, tr, c), lambda l, i: (l, i, 0))
    return pl.pallas_call(
        _cast_kernel, grid=(d, r // tr), in_specs=[spec], out_specs=spec,
        out_shape=jax.ShapeDtypeStruct(w.shape, BF16),
        compiler_params=_params(("parallel", "parallel")), name="to_bf16",
    )(w)


W_IN_TILE = 256


def _w_in_kernel(x_ref, o_ref):
    is_pad = pl.program_id(1) >= (OFF_RW + RW_COLS) // W_IN_TILE
    o_ref[...] = jnp.where(is_pad, 0.0, x_ref[...]).astype(o_ref.dtype)


def _w_in_layout(w_in):
    d = w_in.shape[0]
    n_gate = N_BRANCH * D_MODEL // W_IN_TILE
    n_rest = (OFF_RW + RW_COLS) // W_IN_TILE - n_gate

    def src(l, j):
        return l, 0, jnp.where(j < n_gate, j + n_rest, jnp.where(j < n_gate + n_rest, j - n_gate, 0))

    return pl.pallas_call(
        _w_in_kernel, grid=(d, PROJ_COLS // W_IN_TILE),
        in_specs=[pl.BlockSpec((None, D_MODEL, W_IN_TILE), src)],
        out_specs=pl.BlockSpec((None, D_MODEL, W_IN_TILE), lambda l, j: (l, 0, j)),
        out_shape=jax.ShapeDtypeStruct((d, D_MODEL, PROJ_COLS), BF16),
        compiler_params=_params(("parallel", "parallel")), name="w_in_layout",
    )(w_in)


def _block_diag(w):
    n, r, c = w.shape
    eye = jnp.eye(n, dtype=w.dtype)
    return (eye[:, None, :, None] * w[:, :, None, :]).reshape(n * r, n * c)


def _prep_layer(i, W):
    lora = jnp.zeros((RW_LORA, 3 * BRANCH_W), F32)
    lora = lora.at[0:RW_LORA_W, 0:BRANCH_W].set(W["rw_w_up"][i])
    lora = lora.at[RW_LORA_W:RW_LORA_W + RW_LORA_A, BRANCH_W:2 * BRANCH_W].set(W["rw_a_up"][i])
    lora = lora.at[RW_LORA_W + RW_LORA_A:, 2 * BRANCH_W:].set(W["rw_g_up"][i])
    vec = jnp.stack([W["rw_w0"][i], W["rw_a0"][i], W["rw_k_k"][i], W["rw_k_a"][i],
                     W["rw_r_k"][i].reshape(BRANCH_W), W["rw_ln_g"][i], W["rw_ln_b"][i],
                     jnp.zeros((BRANCH_W,), F32)])
    row = lambda name: W[name][i].reshape(1, -1)
    return dict(
        norm_pre_mix=row("norm_pre_mix"),
        conv_w=W["conv_w"][i], conv_b=row("conv_b"),
        lru_w=jnp.concatenate([_block_diag(W["lru_wa"][i]), _block_diag(W["lru_wx"][i])], axis=1).astype(BF16),
        lru_b=jnp.stack([W["lru_ba"][i], W["lru_bx"][i]]), lru_lambda=row("lru_lambda"),
        hg_norm_g=row("hg_norm_g"),
        rw_mu=row("rw_mu"), rw_lora=lora.astype(BF16), rw_vec=vec,
        norm_post_mix=row("norm_post_mix"), norm_pre_ffn=row("norm_pre_ffn"),
        norm_post_ffn=row("norm_post_ffn"), norm_ple=row("norm_ple"),
    )


def _tiles(nb, s_len):
    t = nb * s_len
    tm_in = min(t, 1024)
    tm_tok = min(t, 512)
    if s_len >= 512:
        return dict(tm_in=tm_in, tn_in=2048, proj_dtype=BF16, tm_mix=tm_tok, tm_ffn=min(t, 1024), ffn_split=2,
                    lru_tt=128,
                    hg=dict(nbb=2, tt_len=256, chunk=64, ub=2),
                    rw=dict(nbb=8, tt_len=64, chunk=64, ua=4, ub=8))
    return dict(tm_in=tm_in, tn_in=2048, proj_dtype=F32, tm_mix=tm_tok, tm_ffn=tm_tok, ffn_split=2,
                lru_tt=s_len,
                hg=dict(nbb=min(nb, 16), tt_len=s_len, chunk=s_len, ub=4),
                rw=dict(nbb=min(nb, 16), tt_len=s_len, chunk=s_len, ua=8, ub=8))


def _run_trunk(x3, p4, states, layers, big, lbraw, *, s_valid, pos0_is_zero):
    nb, s_len, _ = x3.shape
    t = nb * s_len
    plan = _tiles(nb, s_len)
    conv0, lru0, hg0, rw0, sh0 = states
    x = x3.reshape(t, D_MODEL)
    new = ([], [], [])
    nhg = nrw = None
    for i, L in enumerate(layers):
        proj = _in_proj(x, L["norm_pre_mix"], big["w_in"], plan["tm_in"], plan["tn_in"], plan["proj_dtype"], i)
        proj3 = proj.reshape(nb, s_len, PROJ_COLS)
        prev8 = jnp.pad(conv0[i], ((0, 0), (SUBLANES - (CONV_W - 1), 0), (0, 0)))
        ya, nlru = _lru(proj3, prev8, lru0[i], L["conv_w"], L["conv_b"], L["lru_w"], L["lru_b"], L["lru_lambda"],
                        tt_len=plan["lru_tt"], s_valid=s_valid, pos0_is_zero=pos0_is_zero)
        yb, nhg = _hgrn(proj3, hg0, nhg, lbraw, L["hg_norm_g"], s_valid=s_valid, layer=i, **plan["hg"])
        yc, nrw = _rwkv(proj3, sh0[i].reshape(nb, 1, RW_COLS), rw0, nrw, L["rw_mu"], L["rw_lora"], L["rw_vec"],
                        s_valid=s_valid, layer=i, **plan["rw"])
        x = _mix(x, proj, ya.reshape(t, BRANCH_W), yb.reshape(t, BRANCH_W), yc.reshape(t, BRANCH_W),
                 big["w_branch"], big["w_out"], L["norm_post_mix"], plan["tm_mix"], i)
        x = _ffn(x, p4.reshape(-1, t, PLE_DIM), L["norm_pre_ffn"], big["w_ffn_gate"], big["w_ffn_up"],
                 big["w_ffn_down"], L["norm_post_ffn"], big["w_ple"], big["w_ple_gate"], L["norm_ple"], plan["tm_ffn"], plan["ffn_split"], i)
        nconv = proj3[:, s_valid - (CONV_W - 1):s_valid, OFF_LRU:OFF_LRU + BRANCH_W].astype(F32)
        nsh = proj3[:, s_valid - 1, OFF_RW:OFF_RW + RW_COLS].astype(F32)
        for lst, val in zip(new, (nconv, nlru, nsh)):
            lst.append(val)
    nconv, nlru, nsh = (jnp.stack(l) for l in new)
    return x.reshape(nb, s_len, D_MODEL), (nconv, nlru, nhg, nrw, nsh)


def kernel(x_prompt, x_sample, p_prompt, p_sample, state_conv_a, state_lru_a, state_hgrn, state_rwkv, state_shift_c, norm_pre_mix, w_in, conv_w, conv_b, lru_wa, lru_ba, lru_wx, lru_bx, lru_lambda, hg_lower_bounds, hg_norm_g, rw_mu, rw_w0, rw_w_up, rw_a0, rw_a_up, rw_g_up, rw_k_k, rw_k_a, rw_r_k, rw_ln_g, rw_ln_b, w_branch, w_out, norm_post_mix, norm_pre_ffn, w_ffn_gate, w_ffn_up, w_ffn_down, norm_post_ffn, w_ple, w_ple_gate, norm_ple):
    W = dict(norm_pre_mix=norm_pre_mix, w_in=w_in, conv_w=conv_w, conv_b=conv_b, lru_wa=lru_wa, lru_ba=lru_ba,
             lru_wx=lru_wx, lru_bx=lru_bx, lru_lambda=lru_lambda, hg_norm_g=hg_norm_g, rw_mu=rw_mu, rw_w0=rw_w0,
             rw_w_up=rw_w_up, rw_a0=rw_a0, rw_a_up=rw_a_up, rw_g_up=rw_g_up, rw_k_k=rw_k_k, rw_k_a=rw_k_a,
             rw_r_k=rw_r_k, rw_ln_g=rw_ln_g, rw_ln_b=rw_ln_b, w_branch=w_branch, w_out=w_out,
             norm_post_mix=norm_post_mix, norm_pre_ffn=norm_pre_ffn, w_ffn_gate=w_ffn_gate, w_ffn_up=w_ffn_up,
             w_ffn_down=w_ffn_down, norm_post_ffn=norm_post_ffn, w_ple=w_ple, w_ple_gate=w_ple_gate, norm_ple=norm_ple)
    depth = w_in.shape[0]
    d_ff = w_ffn_gate.shape[2]
    big = dict(
        w_in=_w_in_layout(w_in),
        w_branch=_to_bf16(w_branch.reshape(depth, N_BRANCH * BRANCH_W, D_MODEL), N_BRANCH * BRANCH_W // 2),
        w_out=_to_bf16(w_out, D_MODEL),
        w_ffn_gate=_to_bf16(w_ffn_gate, D_MODEL // 2), w_ffn_up=_to_bf16(w_ffn_up, D_MODEL // 2),
        w_ffn_down=_to_bf16(w_ffn_down, d_ff // 2),
        w_ple=_to_bf16(w_ple, PLE_DIM), w_ple_gate=_to_bf16(w_ple_gate, D_MODEL),
    )
    layers = [_prep_layer(i, W) for i in range(depth)]
    lbraw = hg_lower_bounds.astype(F32)

    bp, sp, _ = x_prompt.shape
    zeros = lambda *shape: jnp.zeros((depth, bp) + shape, F32)
    zero_states = (zeros(CONV_W - 1, BRANCH_W), zeros(BRANCH_W), zeros(HG_HEADS, HG_D, HG_D),
                   zeros(RW_HEADS, RW_HD, RW_HD), zeros(RW_COLS))
    y_prompt, st_p = _run_trunk(x_prompt, p_prompt, zero_states, layers, big, lbraw, s_valid=sp, pos0_is_zero=True)

    bs, ss, _ = x_sample.shape
    ss_pad = -(-ss // SUBLANES) * SUBLANES
    xs = jnp.pad(x_sample, ((0, 0), (0, ss_pad - ss), (0, 0)))
    ps = jnp.pad(p_sample, ((0, 0), (0, 0), (0, ss_pad - ss), (0, 0)))
    y_sample, st_s = _run_trunk(xs, ps, (state_conv_a, state_lru_a, state_hgrn, state_rwkv, state_shift_c),
                                layers, big, lbraw, s_valid=ss, pos0_is_zero=False)
    return (y_prompt, y_sample[:, :ss]) + st_p + st_s
```

```python
import functools
import math

import jax
import jax.numpy as jnp
from jax import lax
from jax.experimental import pallas as pl
from jax.experimental.pallas import tpu as pltpu

F32 = jnp.float32
BF16 = jnp.bfloat16

D_MODEL = 1024
BRANCH_W = 512
N_BRANCH = 3
LRU_BLOCKS = 8
LRU_BW = BRANCH_W // LRU_BLOCKS
CONV_W = 4
LRU_C = 8.0
HG_HEADS = 4
HG_D = BRANCH_W // HG_HEADS
HG_F_MIN = 1e-20
RW_HD = 64
RW_HEADS = BRANCH_W // RW_HD
RW_LORA_W = 64
RW_LORA_A = 64
RW_LORA_G = 128
RW_LORA = RW_LORA_W + RW_LORA_A + RW_LORA_G
RW_GN_EPS = 64e-5
RW_COLS = 3 * BRANCH_W + RW_LORA
PLE_DIM = 256
EPS = 1e-6

SUBLANES = 8
LANES = 128
MXU_DIM = 256
VMEM_LIMIT = 56 * 1024 * 1024

PROJ_COLS = 8192
OFF_GATES = 0
OFF_LRU = N_BRANCH * D_MODEL
OFF_HG = OFF_LRU + 2 * BRANCH_W
OFF_RW = OFF_HG + 4 * BRANCH_W
RW_BLOCK = PROJ_COLS - OFF_RW

RW_GROUP = LANES // RW_HD
RW_GW = RW_GROUP * RW_HD


def _params(sem):
    return pltpu.CompilerParams(dimension_semantics=sem, vmem_limit_bytes=VMEM_LIMIT)


def _rms(x, g):
    return x * lax.rsqrt(jnp.mean(x * x, axis=-1, keepdims=True) + EPS) * g


def _sigmoid(x):
    return 1.0 / (1.0 + jnp.exp(-x))


def _sigmoid_t(x):
    return 0.5 * jnp.tanh(0.5 * x) + 0.5


def _softplus(x):
    return jnp.maximum(x, 0.0) + jnp.log1p(jnp.exp(-jnp.abs(x)))


def _dot(a, b):
    return jnp.dot(a.astype(BF16), b.astype(BF16), preferred_element_type=F32)


def _dot_nt(a, b):
    return lax.dot_general(a.astype(BF16), b.astype(BF16), (((1,), (1,)), ((), ())),
                           preferred_element_type=F32)


def _dot_tn(a, b):
    return lax.dot_general(a.astype(BF16), b.astype(BF16), (((0,), (0,)), ((), ())),
                           preferred_element_type=F32)


def _split2(x):
    hi = x.astype(BF16)
    return hi, (x - hi.astype(F32)).astype(BF16)


def _seg_sum(x, seg):
    i = lax.broadcasted_iota(jnp.int32, (MXU_DIM, MXU_DIM), 0)
    j = lax.broadcasted_iota(jnp.int32, (MXU_DIM, MXU_DIM), 1)
    ones = (i // seg == j // seg).astype(BF16)
    xb = x.astype(BF16)
    tiles = [jnp.dot(xb[:, l:l + MXU_DIM], ones, preferred_element_type=F32)
             for l in range(0, x.shape[1], MXU_DIM)]
    return jnp.concatenate(tiles, axis=1)


def _chunk_sums(cum, x, nbb, tt_len):
    hi, lo = _split2(x)
    m = cum.astype(BF16)
    pre, tot = [], []
    for b in range(nbb):
        rs = slice(b * tt_len, (b + 1) * tt_len)
        both = jnp.dot(m, lo[rs], preferred_element_type=F32) + jnp.dot(m, hi[rs], preferred_element_type=F32)
        pre.append(both[:tt_len])
        tot.append(both[tt_len:])
    if nbb == 1:
        return pre[0], tot[0]
    return jnp.concatenate(pre, axis=0), jnp.concatenate(tot, axis=0)


def _in_proj_kernel(x_ref, g_ref, w_ref, o_ref, h_scr):
    @pl.when(pl.program_id(1) == 0)
    def _():
        h_scr[...] = _rms(x_ref[...], g_ref[...]).astype(BF16)

    o_ref[...] = jnp.dot(h_scr[...], w_ref[...], preferred_element_type=F32).astype(o_ref.dtype)


def _in_proj(x, g, w, tm, tn, out_dtype, layer):
    t = x.shape[0]
    return pl.pallas_call(
        _in_proj_kernel,
        grid=(t // tm, PROJ_COLS // tn),
        in_specs=[
            pl.BlockSpec((tm, D_MODEL), lambda i, j: (i, 0)),
            pl.BlockSpec((1, D_MODEL), lambda i, j: (0, 0)),
            pl.BlockSpec((None, D_MODEL, tn), lambda i, j: (layer, 0, j)),
        ],
        out_specs=pl.BlockSpec((tm, tn), lambda i, j: (i, j)),
        out_shape=jax.ShapeDtypeStruct((t, PROJ_COLS), out_dtype),
        scratch_shapes=[pltpu.VMEM((tm, D_MODEL), BF16)],
        compiler_params=_params(("parallel", "arbitrary")),
        name="in_proj",
    )(x, g, w)


def _lru_kernel(xa_ref, ga_ref, prev8_ref, h0_ref, cw_ref, cb_ref, w_ref, bab_ref, lam_ref,
                y_ref, hout_ref, prev_scr, h_scr, *, nb, tt_len, s_valid, pos0_is_zero):
    tt = pl.program_id(0)
    rows = nb * tt_len

    @pl.when(tt == 0)
    def _():
        prev_scr[...] = prev8_ref[...]
        h_scr[...] = h0_ref[...]

    xa = xa_ref[...].astype(F32)
    n8 = tt_len // SUBLANES
    xa4 = xa.reshape(nb, n8, SUBLANES, BRANCH_W)
    prev4 = prev_scr[...][:, None]
    t8 = lax.broadcasted_iota(jnp.int32, (1, 1, SUBLANES, 1), 2)
    cw = cw_ref[...]
    xc4 = cb_ref[...][None, None] + cw[CONV_W - 1][None, None, None] * xa4
    for j in range(1, CONV_W):
        rot = pltpu.roll(xa4, j, 2)
        rot_before = pltpu.roll(prev4, j, 2)
        if n8 > 1:
            rot_before = jnp.concatenate([rot_before, rot[:, :n8 - 1]], axis=1)
        xc4 = xc4 + cw[CONV_W - 1 - j][None, None, None] * jnp.where(t8 < j, rot_before, rot)
    prev_scr[...] = xa[:, tt_len - SUBLANES:, :]

    xc2 = xc4.reshape(rows, BRANCH_W)
    z = _dot(xc2, w_ref[...])
    bab = bab_ref[...]
    r = _sigmoid_t(z[:, :BRANCH_W] + bab[0:1])
    i = _sigmoid_t(z[:, BRANCH_W:] + bab[1:2])
    log_a = (-LRU_C) * r * _softplus(-lam_ref[...])
    a = jnp.exp(log_a)
    m2 = jnp.maximum(1.0 - a * a, 0.0)
    mult = m2 * lax.rsqrt(jnp.maximum(m2, 1e-30))
    t_in = lax.broadcasted_iota(jnp.int32, (rows, 1), 0) % tt_len
    if pos0_is_zero:
        mult = jnp.where(jnp.logical_and(tt == 0, t_in == 0), 1.0, mult)
    b = xc2 * i * mult
    valid = (tt * tt_len + t_in) < s_valid
    a4 = jnp.where(valid, a, 1.0).reshape(nb, n8, SUBLANES, BRANCH_W)
    b4 = jnp.where(valid, b, 0.0).reshape(nb, n8, SUBLANES, BRANCH_W)

    d = 1
    while d < SUBLANES:
        keep = t8 >= d
        b4 = a4 * jnp.where(keep, pltpu.roll(b4, d, 2), 0.0) + b4
        a4 = a4 * jnp.where(keep, pltpu.roll(a4, d, 2), 1.0)
        d *= 2
    h_in = h_scr[...][:, None, :]
    blocks = []
    for blk in range(n8):
        hb = a4[:, blk] * h_in + b4[:, blk]
        blocks.append(hb)
        h_in = hb[:, SUBLANES - 1:SUBLANES, :]
    hh = blocks[0] if n8 == 1 else jnp.concatenate(blocks, axis=1)
    h = h_in.reshape(nb, BRANCH_W)
    h_scr[...] = h
    hout_ref[...] = h

    ga = ga_ref[...].astype(F32)
    gelu = 0.5 * ga * (1.0 + jnp.tanh(math.sqrt(2.0 / math.pi) * (ga + 0.044715 * ga * ga * ga)))
    y_ref[...] = (hh * gelu).astype(y_ref.dtype)


def _lru(proj3, prev8, h0, cw, cb, w, bab, lam, *, tt_len, s_valid, pos0_is_zero):
    nb, s_len, _ = proj3.shape
    blk = OFF_LRU // BRANCH_W
    kern = functools.partial(_lru_kernel, nb=nb, tt_len=tt_len, s_valid=s_valid, pos0_is_zero=pos0_is_zero)
    full2 = lambda t: (0, 0)
    return pl.pallas_call(
        kern,
        grid=(s_len // tt_len,),
        in_specs=[
            pl.BlockSpec((nb, tt_len, BRANCH_W), lambda t: (0, t, blk)),
            pl.BlockSpec((nb, tt_len, BRANCH_W), lambda t: (0, t, blk + 1)),
            pl.BlockSpec((nb, SUBLANES, BRANCH_W), lambda t: (0, 0, 0)),
            pl.BlockSpec((nb, BRANCH_W), full2),
            pl.BlockSpec((CONV_W, BRANCH_W), full2),
            pl.BlockSpec((1, BRANCH_W), full2),
            pl.BlockSpec((BRANCH_W, 2 * BRANCH_W), full2),
            pl.BlockSpec((2, BRANCH_W), full2),
            pl.BlockSpec((1, BRANCH_W), full2),
        ],
        out_specs=[
            pl.BlockSpec((nb, tt_len, BRANCH_W), lambda t: (0, t, 0)),
            pl.BlockSpec((nb, BRANCH_W), full2),
        ],
        out_shape=[
            jax.ShapeDtypeStruct((nb, s_len, BRANCH_W), proj3.dtype),
            jax.ShapeDtypeStruct((nb, BRANCH_W), F32),
        ],
        scratch_shapes=[
            pltpu.VMEM((nb, SUBLANES, BRANCH_W), F32),
            pltpu.VMEM((nb, BRANCH_W), F32),
        ],
        compiler_params=_params(("arbitrary",)),
        name="rglru",
    )(proj3, proj3, prev8, h0, cw, cb, w, bab, lam)


def _own_layer_block(sout_ref, layer):
    if layer > 0:
        return sout_ref
    if sout_ref.shape[0] > 1:
        sout_ref[1:] = jnp.zeros((sout_ref.shape[0] - 1,) + tuple(sout_ref.shape[1:]), sout_ref.dtype)
    return sout_ref.at[0]


def _state_out(s0, sbuf, layer, nbb, tail):
    depth = s0.shape[0]
    zeros = (0,) * len(tail)
    if layer == 0:
        spec = pl.BlockSpec((depth, nbb) + tail, lambda b, t: (0, b) + zeros)
        return spec, [], []
    spec = pl.BlockSpec((None, nbb) + tail, lambda b, t: (layer, b) + zeros)
    return spec, [pl.BlockSpec(memory_space=pl.ANY)], [sbuf]


def _hg_diag_blocks(qh, kh, vh, bch):
    c = qh.shape[0]
    nblk = c // SUBLANES
    q3 = qh.reshape(nblk, SUBLANES, HG_D)
    k3 = kh.reshape(nblk, SUBLANES, HG_D)
    v3 = vh.reshape(nblk, SUBLANES, HG_D)
    b3 = bch.reshape(nblk, SUBLANES, HG_D)
    tin = lax.broadcasted_iota(jnp.int32, (1, SUBLANES, 1), 1)
    o3 = jnp.zeros((nblk, SUBLANES, HG_D), F32)
    for s in range(SUBLANES):
        dec = jnp.exp(jnp.minimum(b3 - b3[:, s:s + 1, :], 0.0))
        w = jnp.sum(q3 * k3[:, s:s + 1, :] * dec, axis=-1, keepdims=True)
        w = jnp.where(tin >= s, w, 0.0)
        o3 = o3 + w * v3[:, s:s + 1, :]
    return o3.reshape(c, HG_D)


def _hg_level_refs(bch, h, c):
    gq, gk = [], []
    zero = jnp.zeros((h, HG_D), F32)
    for j in range(c // h):
        if j % 2 == 1:
            gq.append(jnp.broadcast_to(bch[j * h - 1:j * h, :], (h, HG_D)))
            gk.append(zero)
        else:
            gq.append(zero)
            gk.append(jnp.broadcast_to(bch[(j + 1) * h - 1:(j + 1) * h, :], (h, HG_D)))
    return jnp.concatenate(gq, axis=0), jnp.concatenate(gk, axis=0)


def _hgrn_kernel(q_ref, f_ref, v_ref, g_ref, s0_ref, lbraw_ref, ng_ref, cum_ref, y_ref, sout_ref,
                 st_scr, qs_scr, k_scr, v_scr, bc_scr, qe_scr, kh_scr, et_scr, o_scr,
                 *, nbb, tt_len, chunk, s_valid, s_len, layer, ub):
    tt = pl.program_id(1)
    n_t = pl.num_programs(1)
    c = chunk
    rows = nbb * tt_len
    n_cb = tt_len // c
    kv_major = s_len == c

    raw = lbraw_ref[...]
    ex = jnp.exp(raw - jnp.max(raw, axis=0, keepdims=True))
    sm = ex / jnp.sum(ex, axis=0, keepdims=True)
    lb = jnp.zeros((1, BRANCH_W), F32)
    for l in range(1, layer + 1):
        lb = lb + sm[l:l + 1]

    @pl.when(tt == 0)
    def _():
        def init(bb, carry):
            for hd in range(HG_HEADS):
                st_scr[bb, hd] = s0_ref[bb, hd] if kv_major else s0_ref[bb, hd].T
            return carry

        lax.fori_loop(0, nbb, init, 0)

    q = q_ref[...].astype(F32).reshape(rows, BRANCH_W)
    fp = f_ref[...].astype(F32).reshape(rows, BRANCH_W)
    sg = _sigmoid(fp)
    f = lb + (1.0 - lb) * sg
    k = (1.0 - lb) * (1.0 - sg)
    logf = jnp.log(jnp.maximum(f, HG_F_MIN))
    if s_valid < s_len:
        t_in = lax.broadcasted_iota(jnp.int32, (rows, 1), 0) % tt_len
        valid = (tt * tt_len + t_in) < s_valid
        k = jnp.where(valid, k, 0.0)
        logf = jnp.where(valid, logf, 0.0)
    bc, btot = _chunk_sums(cum_ref[...], logf, nbb, tt_len)
    qs = q * _sigmoid_t(q)
    qs_scr[...] = qs
    k_scr[...] = k
    v_scr[...] = v_ref[...].astype(F32).reshape(rows, BRANCH_W)
    bc_scr[...] = bc
    qe_scr[...] = qs * jnp.exp(bc)
    kh_scr[...] = k * jnp.exp(btot - bc)
    et_scr[...] = jnp.exp(btot)

    ti = lax.broadcasted_iota(jnp.int32, (c, 1), 0)
    ii = lax.broadcasted_iota(jnp.int32, (c, c), 0)
    jj = lax.broadcasted_iota(jnp.int32, (c, c), 1)
    levels = []
    h = c // 2
    while h >= SUBLANES:
        odd = (ti // h) % 2 == 1
        pair = jnp.logical_and(ii // (2 * h) == jj // (2 * h),
                               jnp.logical_and((ii // h) % 2 == 1, (jj // h) % 2 == 0))
        levels.append((h, odd, pair))
        h //= 2

    def step(it, carry):
        cb = it // (nbb // ub)
        b0 = (it % (nbb // ub)) * ub
        chains = []
        for u in range(ub):
            r0 = pl.multiple_of((b0 + u) * tt_len + cb * c, c)
            for hd in range(HG_HEADS):
                ln = slice(hd * HG_D, (hd + 1) * HG_D)
                chains.append(dict(bb=b0 + u, hd=hd, r0=r0, ln=ln, st=st_scr[b0 + u, hd],
                                   qh=qs_scr[pl.ds(r0, c), ln], kh=k_scr[pl.ds(r0, c), ln],
                                   vh=v_scr[pl.ds(r0, c), ln], bch=bc_scr[pl.ds(r0, c), ln]))
        if kv_major:
            outs = [_dot(qe_scr[pl.ds(x["r0"], c), x["ln"]], x["st"]) for x in chains]
        else:
            outs = [_dot_nt(qe_scr[pl.ds(x["r0"], c), x["ln"]], x["st"]) for x in chains]
        if levels:
            amats = []
            for x in chains:
                amat = None
                for (h, odd, pair) in levels:
                    gq, gk = _hg_level_refs(x["bch"], h, c)
                    qt = jnp.where(odd, x["qh"] * jnp.exp(jnp.where(odd, x["bch"] - gq, 0.0)), 0.0)
                    kt = jnp.where(odd, 0.0, x["kh"] * jnp.exp(jnp.where(odd, 0.0, gk - x["bch"])))
                    term = jnp.where(pair, _dot_nt(qt, kt), 0.0)
                    amat = term if amat is None else amat + term
                amats.append(amat)
            outs = [o + _dot(a, x["vh"]) for o, a, x in zip(outs, amats, chains)]
        for o, x in zip(outs, chains):
            o_scr[pl.ds(x["r0"], c), x["ln"]] = o + _hg_diag_blocks(x["qh"], x["kh"], x["vh"], x["bch"])
        for x in chains:
            khat = kh_scr[pl.ds(x["r0"], c), x["ln"]]
            if kv_major:
                last = jnp.where(ti == c - 1, x["bch"], 0.0)
                hi, lo = _split2(last)
                ones = jnp.ones((c, HG_D), BF16)
                col = (lax.dot_general(hi, ones, (((0,), (0,)), ((), ())), preferred_element_type=F32)
                       + lax.dot_general(lo, ones, (((0,), (0,)), ((), ())), preferred_element_type=F32))
                st_scr[x["bb"], x["hd"]] = x["st"] * jnp.exp(col) + _dot_tn(khat, x["vh"])
            else:
                st_scr[x["bb"], x["hd"]] = x["st"] * et_scr[pl.ds(x["r0"], 1), x["ln"]] + _dot_tn(x["vh"], khat)
        return carry

    n_it = n_cb * (nbb // ub)
    if n_it == 1:
        step(0, 0)
    else:
        lax.fori_loop(0, n_it, step, 0)

    g = g_ref[...].astype(F32).reshape(rows, BRANCH_W)
    ng = ng_ref[...]
    outs = []
    for hd in range(HG_HEADS):
        ln = slice(hd * HG_D, (hd + 1) * HG_D)
        o = o_scr[:, ln]
        outs.append(o * lax.rsqrt(jnp.mean(o * o, axis=-1, keepdims=True) + EPS) * ng[:, ln])
    y = jnp.concatenate(outs, axis=-1) * (g * _sigmoid_t(g))
    y_ref[...] = y.reshape(nbb, tt_len, BRANCH_W).astype(y_ref.dtype)

    @pl.when(tt == n_t - 1)
    def _():
        out = _own_layer_block(sout_ref, layer)

        def fin(bb, carry):
            for hd in range(HG_HEADS):
                out[bb, hd] = st_scr[bb, hd] if kv_major else st_scr[bb, hd].T
            return carry

        lax.fori_loop(0, nbb, fin, 0)


def _hgrn_kernel_inplace(q_ref, f_ref, v_ref, g_ref, s0_ref, lbraw_ref, ng_ref, cum_ref, sbuf_ref, *rest, **kw):
    _hgrn_kernel(q_ref, f_ref, v_ref, g_ref, s0_ref, lbraw_ref, ng_ref, cum_ref, *rest, **kw)


def _chunk_cum_matrix(rows, c):
    i = jnp.arange(rows)[:, None]
    j = jnp.arange(rows)[None, :]
    same = i // c == j // c
    return jnp.concatenate([same & (i >= j), same], axis=0).astype(BF16)


def _hgrn(proj3, s0, sbuf, lbraw, ng, *, nbb, tt_len, chunk, s_valid, layer, ub):
    nb, s_len, _ = proj3.shape
    blk = OFF_HG // BRANCH_W
    rows = nbb * tt_len
    kern = functools.partial(_hgrn_kernel if layer == 0 else _hgrn_kernel_inplace, nbb=nbb, tt_len=tt_len,
                             chunk=chunk, s_valid=s_valid, s_len=s_len, layer=layer, ub=ub)
    seq = lambda k: pl.BlockSpec((nbb, tt_len, BRANCH_W), lambda b, t, k=k: (b, t, blk + k))
    st_in = pl.BlockSpec((None, nbb, HG_HEADS, HG_D, HG_D), lambda b, t: (layer, b, 0, 0, 0))
    st_out, extra_specs, extra_args = _state_out(s0, sbuf, layer, nbb, (HG_HEADS, HG_D, HG_D))
    return pl.pallas_call(
        kern,
        grid=(nb // nbb, s_len // tt_len),
        in_specs=[seq(0), seq(1), seq(2), seq(3), st_in,
                  pl.BlockSpec(lbraw.shape, lambda b, t: (0, 0)),
                  pl.BlockSpec((1, BRANCH_W), lambda b, t: (0, 0)),
                  pl.BlockSpec((2 * tt_len, tt_len), lambda b, t: (0, 0))] + extra_specs,
        out_specs=[pl.BlockSpec((nbb, tt_len, BRANCH_W), lambda b, t: (b, t, 0)), st_out],
        out_shape=[jax.ShapeDtypeStruct((nb, s_len, BRANCH_W), proj3.dtype),
                   jax.ShapeDtypeStruct(s0.shape, F32)],
        input_output_aliases={8: 1} if extra_args else {},
        scratch_shapes=[pltpu.VMEM((nbb, HG_HEADS, HG_D, HG_D), F32)]
        + [pltpu.VMEM((rows, BRANCH_W), F32)] * 8,
        compiler_params=_params(("parallel", "arbitrary")),
        name="hgrn2",
    )(proj3, proj3, proj3, proj3, s0, lbraw, ng, _chunk_cum_matrix(tt_len, chunk), *extra_args)


def _rwkv_kernel(c_ref, sh0_ref, s0_ref, mu_ref, wl_ref, vec_ref, cum_ref, y_ref, sout_ref,
                 sbd_scr, carry_scr, at_scr, rt_scr, bt_scr, kt_scr, bp_scr, kp_scr, v_scr, pc_scr,
                 bonus_scr, gate_scr, o_scr, tcat_scr, aak_scr, arb_scr, ark_scr,
                 *, nbb, tt_len, chunk, s_valid, s_len, layer, ua, ub):
    tt = pl.program_id(1)
    n_t = pl.num_programs(1)
    c = chunk
    c4 = RW_GROUP * c
    n_groups = RW_HEADS // RW_GROUP
    rows = nbb * tt_len
    n_cb = tt_len // c
    n_ch = rows // c

    vec = vec_ref[...]
    w0, a0, k_k, k_a, r_k, ln_g, ln_b = [vec[i:i + 1] for i in range(7)]
    gi_ = lax.broadcasted_iota(jnp.int32, (RW_GW, RW_GW), 0)
    gj_ = lax.broadcasted_iota(jnp.int32, (RW_GW, RW_GW), 1)
    bd_state = gi_ // RW_HD == gj_ // RW_HD

    @pl.when(tt == 0)
    def _():
        carry_scr[...] = sh0_ref[...]

        def init(bb, carry):
            for g in range(n_groups):
                blk = jnp.concatenate([s0_ref[bb, g * RW_GROUP + h] for h in range(RW_GROUP)], axis=0)
                sbd_scr[bb, g] = jnp.where(bd_state, jnp.concatenate([blk] * RW_GROUP, axis=1), 0.0)
            return carry

        lax.fori_loop(0, nbb, init, 0)

    cc3 = c_ref[:, :, :RW_COLS].astype(F32)
    t3 = lax.broadcasted_iota(jnp.int32, (1, tt_len, 1), 1)
    prev3 = jnp.where(t3 == 0, carry_scr[...], pltpu.roll(cc3, 1, 1))
    carry_scr[...] = cc3[:, tt_len - 1:tt_len, :]
    xm = (cc3 + (prev3 - cc3) * mu_ref[...][None]).reshape(rows, RW_COLS)
    r = xm[:, 0:BRANCH_W]
    k = xm[:, BRANCH_W:2 * BRANCH_W]
    v = xm[:, 2 * BRANCH_W:3 * BRANCH_W]
    lo = xm[:, 3 * BRANCH_W:]
    lane_l = lax.broadcasted_iota(jnp.int32, (1, RW_LORA), 1)
    act = jnp.where(lane_l < RW_LORA_W, jnp.tanh(lo),
                    jnp.where(lane_l < RW_LORA_W + RW_LORA_A, lo, _sigmoid_t(lo)))
    z = _dot(act, wl_ref[...])
    ld = (-math.exp(-0.5)) * _sigmoid_t(w0 + z[:, 0:BRANCH_W])
    a = _sigmoid_t(a0 + z[:, BRANCH_W:2 * BRANCH_W])
    kk = k * k_k
    kbar = k * (1.0 + (a - 1.0) * k_a)
    sums = _seg_sum(jnp.concatenate([kk * kk, r * kbar * r_k], axis=0), RW_HD)
    kap = kk * lax.rsqrt(jnp.maximum(sums[:rows], 1e-24))
    if s_valid < s_len:
        t_in = lax.broadcasted_iota(jnp.int32, (rows, 1), 0) % tt_len
        valid = (tt * tt_len + t_in) < s_valid
        ld = jnp.where(valid, ld, 0.0)
        kap = jnp.where(valid, kap, 0.0)
        kbar = jnp.where(valid, kbar, 0.0)
    lw, ltot = _chunk_sums(cum_ref[...], ld, nbb, tt_len)
    back = ltot - lw
    e_in = jnp.exp(lw)
    e_neg = jnp.exp(-lw)
    e_back = jnp.exp(back)
    at_scr[...] = -kap * jnp.exp(lw - ld)
    rt_scr[...] = r * e_in
    bt_scr[...] = kap * a * e_neg
    kt_scr[...] = kbar * e_neg
    bp_scr[...] = kap * a * e_back
    kp_scr[...] = kbar * e_back
    v_scr[...] = v
    pc_scr[...] = jnp.exp(ltot)
    bonus_scr[...] = sums[rows:]
    gate_scr[...] = z[:, 2 * BRANCH_W:]

    si = lax.broadcasted_iota(jnp.int32, (c4, RW_GW), 0)
    sj = lax.broadcasted_iota(jnp.int32, (c4, RW_GW), 1)
    head_rows = si // c == sj // RW_HD
    qi = lax.broadcasted_iota(jnp.int32, (c4, c4), 0)
    qj = lax.broadcasted_iota(jnp.int32, (c4, c4), 1)
    same = qi // c == qj // c
    strict = jnp.logical_and(same, qi % c > qj % c)
    incl = jnp.logical_and(same, qi % c >= qj % c)
    eye = (qi == qj).astype(F32)
    fuse_sq = c4 % LANES == 0

    def stack(x):
        return jnp.where(head_rows, jnp.concatenate([x] * RW_GROUP, axis=0), 0.0)

    def unstack(x):
        out = x[0:c]
        for h in range(1, RW_GROUP):
            out = out + x[h * c:(h + 1) * c]
        return out

    def phase_a(it, carry):
        chains = [(it * ua + u, g) for u in range(ua) for g in range(n_groups)]
        nmats = []
        for ch, g in chains:
            r0 = pl.multiple_of(ch * c, c)
            ln = slice(g * RW_GW, (g + 1) * RW_GW)
            lhs = jnp.concatenate([stack(at_scr[pl.ds(r0, c), ln]), stack(rt_scr[pl.ds(r0, c), ln])], axis=0)
            rhs = jnp.concatenate([bt_scr[pl.ds(r0, c), ln]] * RW_GROUP + [kt_scr[pl.ds(r0, c), ln]] * RW_GROUP,
                                  axis=0)
            quad = _dot_nt(lhs, rhs)
            nmats.append(jnp.where(strict, quad[:c4, :c4], 0.0))
            aak_scr[ch, g] = unstack(jnp.where(strict, quad[:c4, c4:], 0.0))
            arb_scr[ch, g] = unstack(jnp.where(incl, quad[c4:, :c4], 0.0))
            ark_scr[ch, g] = unstack(jnp.where(incl, quad[c4:, c4:], 0.0))
        tinvs = [eye + n for n in nmats]
        npows = [_dot(n, n) for n in nmats]
        span = 2
        while 2 * span < c:
            if fuse_sq:
                boths = [_dot(p, jnp.concatenate([p, t], axis=1)) for p, t in zip(npows, tinvs)]
                tinvs = [t + bo[:, c4:] for t, bo in zip(tinvs, boths)]
                npows = [bo[:, :c4] for bo in boths]
            else:
                tinvs = [t + _dot(p, t) for p, t in zip(npows, tinvs)]
                npows = [_dot(p, p) for p in npows]
            span *= 2
        tinvs = [t + _dot(p, t) for p, t in zip(npows, tinvs)]
        for (ch, g), t in zip(chains, tinvs):
            tcat_scr[ch, g] = unstack(t)
        return carry

    if n_ch // ua == 1:
        phase_a(0, 0)
    else:
        lax.fori_loop(0, n_ch // ua, phase_a, 0)

    def phase_b(it, carry):
        cb = it // (nbb // ub)
        b0 = (it % (nbb // ub)) * ub
        chains = [(b0 + u, g) for u in range(ub) for g in range(n_groups)]
        ops = []
        for bb, g in chains:
            ch = bb * n_cb + cb
            r0 = pl.multiple_of(ch * c, c)
            ln = slice(g * RW_GW, (g + 1) * RW_GW)
            ops.append(dict(bb=bb, g=g, ch=ch, r0=r0, ln=ln, sbd=sbd_scr[bb, g],
                            v_bd=stack(v_scr[pl.ds(r0, c), ln])))
        sprods = [_dot_nt(jnp.concatenate([at_scr[pl.ds(q["r0"], c), q["ln"]], rt_scr[pl.ds(q["r0"], c), q["ln"]]],
                                          axis=0), q["sbd"]) for q in ops]
        wmats = [sp[:c] + _dot(aak_scr[q["ch"], q["g"]], q["v_bd"]) for q, sp in zip(ops, sprods)]
        us = [_dot(tcat_scr[q["ch"], q["g"]], stack(w)) for q, w in zip(ops, wmats)]
        for q, u, sp in zip(ops, us, sprods):
            r0, ln = q["r0"], q["ln"]
            o_scr[pl.ds(r0, c), ln] = (sp[c:] + _dot(arb_scr[q["ch"], q["g"]], stack(u))
                                       + _dot(ark_scr[q["ch"], q["g"]], q["v_bd"]))
        for q, u in zip(ops, us):
            r0, ln = q["r0"], q["ln"]
            upd = _dot_tn(jnp.concatenate([u, v_scr[pl.ds(r0, c), ln]], axis=0),
                          jnp.concatenate([bp_scr[pl.ds(r0, c), ln], kp_scr[pl.ds(r0, c), ln]], axis=0))
            sbd_scr[q["bb"], q["g"]] = q["sbd"] * pc_scr[pl.ds(r0, 1), ln] + jnp.where(bd_state, upd, 0.0)
        return carry

    n_it = n_cb * (nbb // ub)
    if n_it == 1:
        phase_b(0, 0)
    else:
        lax.fori_loop(0, n_it, phase_b, 0)

    o = o_scr[...]
    inv_n = 1.0 / RW_HD
    mean = _seg_sum(o, RW_HD) * inv_n
    cen = o - mean
    var = _seg_sum(cen * cen, RW_HD) * inv_n
    on = cen * lax.rsqrt(var + RW_GN_EPS) * ln_g + ln_b
    y = (on + bonus_scr[...] * v_scr[...]) * gate_scr[...]
    y_ref[...] = y.reshape(nbb, tt_len, BRANCH_W).astype(y_ref.dtype)

    @pl.when(tt == n_t - 1)
    def _():
        out = _own_layer_block(sout_ref, layer)

        def fin(bb, carry):
            for g in range(n_groups):
                sbd = sbd_scr[bb, g]
                for h in range(RW_GROUP):
                    out[bb, g * RW_GROUP + h] = sbd[h * RW_HD:(h + 1) * RW_HD, h * RW_HD:(h + 1) * RW_HD]
            return carry

        lax.fori_loop(0, nbb, fin, 0)


def _rwkv_kernel_inplace(c_ref, sh0_ref, s0_ref, mu_ref, wl_ref, vec_ref, cum_ref, sbuf_ref, *rest, **kw):
    _rwkv_kernel(c_ref, sh0_ref, s0_ref, mu_ref, wl_ref, vec_ref, cum_ref, *rest, **kw)


def _rwkv(proj3, sh0, s0, sbuf, mu, wl, vec, *, nbb, tt_len, chunk, s_valid, layer, ua, ub):
    nb, s_len, _ = proj3.shape
    kern = functools.partial(_rwkv_kernel if layer == 0 else _rwkv_kernel_inplace, nbb=nbb, tt_len=tt_len,
                             chunk=chunk, s_valid=s_valid, s_len=s_len, layer=layer, ua=ua, ub=ub)
    st_in = pl.BlockSpec((None, nbb, RW_HEADS, RW_HD, RW_HD), lambda b, t: (layer, b, 0, 0, 0))
    st_out, extra_specs, extra_args = _state_out(s0, sbuf, layer, nbb, (RW_HEADS, RW_HD, RW_HD))
    full2 = lambda b, t: (0, 0)
    rows = nbb * tt_len
    n_groups = RW_HEADS // RW_GROUP
    mats = pltpu.VMEM((rows // chunk, n_groups, chunk, RW_GROUP * chunk), F32)
    return pl.pallas_call(
        kern,
        grid=(nb // nbb, s_len // tt_len),
        in_specs=[
            pl.BlockSpec((nbb, tt_len, RW_BLOCK), lambda b, t: (b, t, OFF_RW // RW_BLOCK)),
            pl.BlockSpec((nbb, 1, RW_COLS), lambda b, t: (b, 0, 0)),
            st_in,
            pl.BlockSpec((1, RW_COLS), full2),
            pl.BlockSpec((RW_LORA, 3 * BRANCH_W), full2),
            pl.BlockSpec((SUBLANES, BRANCH_W), full2),
            pl.BlockSpec((2 * tt_len, tt_len), full2),
        ] + extra_specs,
        out_specs=[pl.BlockSpec((nbb, tt_len, BRANCH_W), lambda b, t: (b, t, 0)), st_out],
        out_shape=[jax.ShapeDtypeStruct((nb, s_len, BRANCH_W), proj3.dtype),
                   jax.ShapeDtypeStruct(s0.shape, F32)],
        input_output_aliases={7: 1} if extra_args else {},
        scratch_shapes=[
            pltpu.VMEM((nbb, n_groups, RW_GW, RW_GW), F32),
            pltpu.VMEM((nbb, 1, RW_COLS), F32),
        ] + [pltpu.VMEM((rows, BRANCH_W), F32)] * 11 + [mats] * 4,
        compiler_params=_params(("parallel", "arbitrary")),
        name="rwkv7",
    )(proj3, sh0, s0, mu, wl, vec, _chunk_cum_matrix(tt_len, chunk), *extra_args)


def _mix_kernel(x_ref, gts_ref, ya_ref, yb_ref, yc_ref, wb_ref, wo_ref, g_ref, o_ref):
    acc = None
    for n, y_ref in enumerate((ya_ref, yb_ref, yc_ref)):
        up = _dot(y_ref[...], wb_ref[n * BRANCH_W:(n + 1) * BRANCH_W, :])
        term = _sigmoid_t(gts_ref[:, n * D_MODEL:(n + 1) * D_MODEL].astype(F32)) * up
        acc = term if acc is None else acc + term
    mix = _dot(acc, wo_ref[...])
    o_ref[...] = x_ref[...] + _rms(mix, g_ref[...])


def _mix(x, proj, ya, yb, yc, wb, wo, g, tm, layer):
    t = x.shape[0]
    row = lambda w: pl.BlockSpec((tm, w), lambda i: (i, 0))
    return pl.pallas_call(
        _mix_kernel,
        grid=(t // tm,),
        in_specs=[row(D_MODEL), row(N_BRANCH * D_MODEL), row(BRANCH_W), row(BRANCH_W), row(BRANCH_W),
                  pl.BlockSpec((None, N_BRANCH * BRANCH_W, D_MODEL), lambda i: (layer, 0, 0)),
                  pl.BlockSpec((None, D_MODEL, D_MODEL), lambda i: (layer, 0, 0)),
                  pl.BlockSpec((1, D_MODEL), lambda i: (0, 0))],
        out_specs=row(D_MODEL),
        out_shape=jax.ShapeDtypeStruct((t, D_MODEL), F32),
        compiler_params=_params(("parallel",)),
        name="branch_mix",
    )(x, proj, ya, yb, yc, wb, wo, g)


def _ffn_kernel(x_ref, p_ref, gpre_ref, wg_ref, wu_ref, wd_ref, gpost_ref, wple_ref, wpg_ref, gple_ref,
                o_ref, h_scr, acc_scr):
    j = pl.program_id(1)

    @pl.when(j == 0)
    def _():
        h_scr[...] = _rms(x_ref[...], gpre_ref[...]).astype(BF16)
        acc_scr[...] = jnp.zeros_like(acc_scr)

    h = h_scr[...]
    gt = jnp.dot(h, wg_ref[...], preferred_element_type=F32)
    up = jnp.dot(h, wu_ref[...], preferred_element_type=F32)
    acc_scr[...] += _dot(gt * _sigmoid_t(gt) * up, wd_ref[...])

    @pl.when(j == pl.num_programs(1) - 1)
    def _():
        x2 = x_ref[...] + _rms(acc_scr[...], gpost_ref[...])
        ple = _dot(p_ref[...], wple_ref[...]) * _sigmoid_t(_dot(x2, wpg_ref[...]))
        o_ref[...] = x2 + _rms(ple, gple_ref[...])


def _ffn(x, p, gpre, wg, wu, wd, gpost, wple, wpg, gple, tm, n_split, layer):
    t = x.shape[0]
    d_ff = wg.shape[2]
    tf = d_ff // n_split
    vecspec = pl.BlockSpec((1, D_MODEL), lambda i, j: (0, 0))
    return pl.pallas_call(
        _ffn_kernel,
        grid=(t // tm, n_split),
        in_specs=[
            pl.BlockSpec((tm, D_MODEL), lambda i, j: (i, 0)),
            pl.BlockSpec((None, tm, PLE_DIM), lambda i, j: (layer, i, 0)),
            vecspec,
            pl.BlockSpec((None, D_MODEL, tf), lambda i, j: (layer, 0, j)),
            pl.BlockSpec((None, D_MODEL, tf), lambda i, j: (layer, 0, j)),
            pl.BlockSpec((None, tf, D_MODEL), lambda i, j: (layer, j, 0)),
            vecspec,
            pl.BlockSpec((None, PLE_DIM, D_MODEL), lambda i, j: (layer, 0, 0)),
            pl.BlockSpec((None, D_MODEL, D_MODEL), lambda i, j: (layer, 0, 0)),
            vecspec,
        ],
        out_specs=pl.BlockSpec((tm, D_MODEL), lambda i, j: (i, 0)),
        out_shape=jax.ShapeDtypeStruct((t, D_MODEL), F32),
        scratch_shapes=[pltpu.VMEM((tm, D_MODEL), BF16), pltpu.VMEM((tm, D_MODEL), F32)],
        compiler_params=_params(("parallel", "arbitrary")),
        name="ffn_ple",
    )(x, p, gpre, wg, wu, wd, gpost, wple, wpg, gple)


def _cast_kernel(x_ref, o_ref):
    o_ref[...] = x_ref[...].astype(o_ref.dtype)


def _to_bf16(w, tr):
    d, r, c = w.shape
    spec = pl.BlockSpec((None, tr, c), lambda l, i: (l, i, 0))
    return pl.pallas_call(
        _cast_kernel, grid=(d, r // tr), in_specs=[spec], out_specs=spec,
        out_shape=jax.ShapeDtypeStruct(w.shape, BF16),
        compiler_params=_params(("parallel", "parallel")), name="to_bf16",
    )(w)


W_IN_TILE = 256


def _w_in_kernel(x_ref, o_ref):
    is_pad = pl.program_id(1) >= (OFF_RW + RW_COLS) // W_IN_TILE
    o_ref[...] = jnp.where(is_pad, 0.0, x_ref[...]).astype(o_ref.dtype)


def _w_in_layout(w_in):
    d = w_in.shape[0]
    n_gate = N_BRANCH * D_MODEL // W_IN_TILE
    n_rest = (OFF_RW + RW_COLS) // W_IN_TILE - n_gate

    def src(l, j):
        return l, 0, jnp.where(j < n_gate, j + n_rest, jnp.where(j < n_gate + n_rest, j - n_gate, 0))

    return pl.pallas_call(
        _w_in_kernel, grid=(d, PROJ_COLS // W_IN_TILE),
        in_specs=[pl.BlockSpec((None, D_MODEL, W_IN_TILE), src)],
        out_specs=pl.BlockSpec((None, D_MODEL, W_IN_TILE), lambda l, j: (l, 0, j)),
        out_shape=jax.ShapeDtypeStruct((d, D_MODEL, PROJ_COLS), BF16),
        compiler_params=_params(("parallel", "parallel")), name="w_in_layout",
    )(w_in)


def _block_diag(w):
    n, r, c = w.shape
    eye = jnp.eye(n, dtype=w.dtype)
    return (eye[:, None, :, None] * w[:, :, None, :]).reshape(n * r, n * c)


def _prep_layer(i, W):
    lora = jnp.zeros((RW_LORA, 3 * BRANCH_W), F32)
    lora = lora.at[0:RW_LORA_W, 0:BRANCH_W].set(W["rw_w_up"][i])
    lora = lora.at[RW_LORA_W:RW_LORA_W + RW_LORA_A, BRANCH_W:2 * BRANCH_W].set(W["rw_a_up"][i])
    lora = lora.at[RW_LORA_W + RW_LORA_A:, 2 * BRANCH_W:].set(W["rw_g_up"][i])
    vec = jnp.stack([W["rw_w0"][i], W["rw_a0"][i], W["rw_k_k"][i], W["rw_k_a"][i],
                     W["rw_r_k"][i].reshape(BRANCH_W), W["rw_ln_g"][i], W["rw_ln_b"][i],
                     jnp.zeros((BRANCH_W,), F32)])
    row = lambda name: W[name][i].reshape(1, -1)
    return dict(
        norm_pre_mix=row("norm_pre_mix"),
        conv_w=W["conv_w"][i], conv_b=row("conv_b"),
        lru_w=jnp.concatenate([_block_diag(W["lru_wa"][i]), _block_diag(W["lru_wx"][i])], axis=1).astype(BF16),
        lru_b=jnp.stack([W["lru_ba"][i], W["lru_bx"][i]]), lru_lambda=row("lru_lambda"),
        hg_norm_g=row("hg_norm_g"),
        rw_mu=row("rw_mu"), rw_lora=lora.astype(BF16), rw_vec=vec,
        norm_post_mix=row("norm_post_mix"), norm_pre_ffn=row("norm_pre_ffn"),
        norm_post_ffn=row("norm_post_ffn"), norm_ple=row("norm_ple"),
    )


def _tiles(nb, s_len):
    t = nb * s_len
    tm_in = min(t, 1024)
    tm_tok = min(t, 512)
    if s_len >= 512:
        return dict(tm_in=tm_in, tn_in=2048, proj_dtype=BF16, tm_mix=tm_tok, tm_ffn=tm_tok, ffn_split=2,
                    lru_tt=128,
                    hg=dict(nbb=2, tt_len=256, chunk=64, ub=2),
                    rw=dict(nbb=8, tt_len=64, chunk=64, ua=4, ub=8))
    return dict(tm_in=tm_in, tn_in=2048, proj_dtype=F32, tm_mix=tm_tok, tm_ffn=tm_tok, ffn_split=2,
                lru_tt=s_len,
                hg=dict(nbb=min(nb, 16), tt_len=s_len, chunk=s_len, ub=4),
                rw=dict(nbb=min(nb, 16), tt_len=s_len, chunk=s_len, ua=8, ub=8))


def _run_trunk(x3, p4, states, layers, big, lbraw, *, s_valid, pos0_is_zero):
    nb, s_len, _ = x3.shape
    t = nb * s_len
    plan = _tiles(nb, s_len)
    conv0, lru0, hg0, rw0, sh0 = states
    x = x3.reshape(t, D_MODEL)
    new = ([], [], [])
    nhg = nrw = None
    for i, L in enumerate(layers):
        proj = _in_proj(x, L["norm_pre_mix"], big["w_in"], plan["tm_in"], plan["tn_in"], plan["proj_dtype"], i)
        proj3 = proj.reshape(nb, s_len, PROJ_COLS)
        prev8 = jnp.pad(conv0[i], ((0, 0), (SUBLANES - (CONV_W - 1), 0), (0, 0)))
        ya, nlru = _lru(proj3, prev8, lru0[i], L["conv_w"], L["conv_b"], L["lru_w"], L["lru_b"], L["lru_lambda"],
                        tt_len=plan["lru_tt"], s_valid=s_valid, pos0_is_zero=pos0_is_zero)
        yb, nhg = _hgrn(proj3, hg0, nhg, lbraw, L["hg_norm_g"], s_valid=s_valid, layer=i, **plan["hg"])
        yc, nrw = _rwkv(proj3, sh0[i].reshape(nb, 1, RW_COLS), rw0, nrw, L["rw_mu"], L["rw_lora"], L["rw_vec"],
                        s_valid=s_valid, layer=i, **plan["rw"])
        x = _mix(x, proj, ya.reshape(t, BRANCH_W), yb.reshape(t, BRANCH_W), yc.reshape(t, BRANCH_W),
                 big["w_branch"], big["w_out"], L["norm_post_mix"], plan["tm_mix"], i)
        x = _ffn(x, p4.reshape(-1, t, PLE_DIM), L["norm_pre_ffn"], big["w_ffn_gate"], big["w_ffn_up"],
                 big["w_ffn_down"], L["norm_post_ffn"], big["w_ple"], big["w_ple_gate"], L["norm_ple"], plan["tm_ffn"], plan["ffn_split"], i)
        nconv = proj3[:, s_valid - (CONV_W - 1):s_valid, OFF_LRU:OFF_LRU + BRANCH_W].astype(F32)
        nsh = proj3[:, s_valid - 1, OFF_RW:OFF_RW + RW_COLS].astype(F32)
        for lst, val in zip(new, (nconv, nlru, nsh)):
            lst.append(val)
    nconv, nlru, nsh = (jnp.stack(l) for l in new)
    return x.reshape(nb, s_len, D_MODEL), (nconv, nlru, nhg, nrw, nsh)


def kernel(x_prompt, x_sample, p_prompt, p_sample, state_conv_a, state_lru_a, state_hgrn, state_rwkv, state_shift_c, norm_pre_mix, w_in, conv_w, conv_b, lru_wa, lru_ba, lru_wx, lru_bx, lru_lambda, hg_lower_bounds, hg_norm_g, rw_mu, rw_w0, rw_w_up, rw_a0, rw_a_up, rw_g_up, rw_k_k, rw_k_a, rw_r_k, rw_ln_g, rw_ln_b, w_branch, w_out, norm_post_mix, norm_pre_ffn, w_ffn_gate, w_ffn_up, w_ffn_down, norm_post_ffn, w_ple, w_ple_gate, norm_ple):
    W = dict(norm_pre_mix=norm_pre_mix, w_in=w_in, conv_w=conv_w, conv_b=conv_b, lru_wa=lru_wa, lru_ba=lru_ba,
             lru_wx=lru_wx, lru_bx=lru_bx, lru_lambda=lru_lambda, hg_norm_g=hg_norm_g, rw_mu=rw_mu, rw_w0=rw_w0,
             rw_w_up=rw_w_up, rw_a0=rw_a0, rw_a_up=rw_a_up, rw_g_up=rw_g_up, rw_k_k=rw_k_k, rw_k_a=rw_k_a,
             rw_r_k=rw_r_k, rw_ln_g=rw_ln_g, rw_ln_b=rw_ln_b, w_branch=w_branch, w_out=w_out,
             norm_post_mix=norm_post_mix, norm_pre_ffn=norm_pre_ffn, w_ffn_gate=w_ffn_gate, w_ffn_up=w_ffn_up,
             w_ffn_down=w_ffn_down, norm_post_ffn=norm_post_ffn, w_ple=w_ple, w_ple_gate=w_ple_gate, norm_ple=norm_ple)
    depth = w_in.shape[0]
    d_ff = w_ffn_gate.shape[2]
    big = dict(
        w_in=_w_in_layout(w_in),
        w_branch=_to_bf16(w_branch.reshape(depth, N_BRANCH * BRANCH_W, D_MODEL), N_BRANCH * BRANCH_W // 2),
        w_out=_to_bf16(w_out, D_MODEL),
        w_ffn_gate=_to_bf16(w_ffn_gate, D_MODEL // 2), w_ffn_up=_to_bf16(w_ffn_up, D_MODEL // 2),
        w_ffn_down=_to_bf16(w_ffn_down, d_ff // 2),
        w_ple=_to_bf16(w_ple, PLE_DIM), w_ple_gate=_to_bf16(w_ple_gate, D_MODEL),
    )
    layers = [_prep_layer(i, W) for i in range(depth)]
    lbraw = hg_lower_bounds.astype(F32)

    bp, sp, _ = x_prompt.shape
    zeros = lambda *shape: jnp.zeros((depth, bp) + shape, F32)
    zero_states = (zeros(CONV_W - 1, BRANCH_W), zeros(BRANCH_W), zeros(HG_HEADS, HG_D, HG_D),
                   zeros(RW_HEADS, RW_HD, RW_HD), zeros(RW_COLS))
    y_prompt, st_p = _run_trunk(x_prompt, p_prompt, zero_states, layers, big, lbraw, s_valid=sp, pos0_is_zero=True)

    bs, ss, _ = x_sample.shape
    ss_pad = -(-ss // SUBLANES) * SUBLANES
    xs = jnp.pad(x_sample, ((0, 0), (0, ss_pad - ss), (0, 0)))
    ps = jnp.pad(p_sample, ((0, 0), (0, 0), (0, ss_pad - ss), (0, 0)))
    y_sample, st_s = _run_trunk(xs, ps, (state_conv_a, state_lru_a, state_hgrn, state_rwkv, state_shift_c),
                                layers, big, lbraw, s_valid=ss, pos0_is_zero=False)
    return (y_prompt, y_sample[:, :ss]) + st_p + st_s
```

```python
import functools
import math

import jax
import jax.numpy as jnp
from jax import lax
from jax.experimental import pallas as pl
from jax.experimental.pallas import tpu as pltpu

F32 = jnp.float32
BF16 = jnp.bfloat16

D_MODEL = 1024
BRANCH_W = 512
N_BRANCH = 3
CONV_W = 4
LRU_C = 8.0
HG_HEADS = 4
HG_D = BRANCH_W // HG_HEADS
HG_F_MIN = 1e-20
RW_HD = 64
RW_HEADS = BRANCH_W // RW_HD
RW_LORA_W = 64
RW_LORA_A = 64
RW_LORA_G = 128
RW_LORA = RW_LORA_W + RW_LORA_A + RW_LORA_G
RW_GN_EPS = 64e-5
RW_COLS = 3 * BRANCH_W + RW_LORA
PLE_DIM = 256
EPS = 1e-6

SUBLANES = 8
LANES = 128
MXU_DIM = 256
VMEM_LIMIT = 56 * 1024 * 1024

PROJ_COLS = 8192
OFF_GATES = 0
OFF_LRU = N_BRANCH * D_MODEL
OFF_HG = OFF_LRU + 2 * BRANCH_W
OFF_RW = OFF_HG + 4 * BRANCH_W
RW_BLOCK = PROJ_COLS - OFF_RW

RW_GROUP = LANES // RW_HD
RW_GW = RW_GROUP * RW_HD


def _params(sem):
    return pltpu.CompilerParams(dimension_semantics=sem, vmem_limit_bytes=VMEM_LIMIT)


def _rms(x, g):
    return x * lax.rsqrt(jnp.mean(x * x, axis=-1, keepdims=True) + EPS) * g


def _sigmoid(x):
    return 1.0 / (1.0 + jnp.exp(-x))


def _sigmoid_t(x):
    return 0.5 * jnp.tanh(0.5 * x) + 0.5


def _softplus(x):
    return jnp.maximum(x, 0.0) + jnp.log1p(jnp.exp(-jnp.abs(x)))


def _dot(a, b):
    return jnp.dot(a.astype(BF16), b.astype(BF16), preferred_element_type=F32)


def _dot_nt(a, b):
    return lax.dot_general(a.astype(BF16), b.astype(BF16), (((1,), (1,)), ((), ())),
                           preferred_element_type=F32)


def _dot_tn(a, b):
    return lax.dot_general(a.astype(BF16), b.astype(BF16), (((0,), (0,)), ((), ())),
                           preferred_element_type=F32)


def _split2(x):
    hi = x.astype(BF16)
    return hi, (x - hi.astype(F32)).astype(BF16)


def _seg_sum(x, seg):
    i = lax.broadcasted_iota(jnp.int32, (MXU_DIM, MXU_DIM), 0)
    j = lax.broadcasted_iota(jnp.int32, (MXU_DIM, MXU_DIM), 1)
    ones = (i // seg == j // seg).astype(BF16)
    xb = x.astype(BF16)
    tiles = [jnp.dot(xb[:, l:l + MXU_DIM], ones, preferred_element_type=F32)
             for l in range(0, x.shape[1], MXU_DIM)]
    return jnp.concatenate(tiles, axis=1)


def _chunk_sums(cum, x, nbb, tt_len):
    hi, lo = _split2(x)
    m = cum.astype(BF16)
    pre, tot = [], []
    for b in range(nbb):
        rs = slice(b * tt_len, (b + 1) * tt_len)
        both = jnp.dot(m, lo[rs], preferred_element_type=F32) + jnp.dot(m, hi[rs], preferred_element_type=F32)
        pre.append(both[:tt_len])
        tot.append(both[tt_len:])
    if nbb == 1:
        return pre[0], tot[0]
    return jnp.concatenate(pre, axis=0), jnp.concatenate(tot, axis=0)


def _in_proj_kernel(x_ref, g_ref, w_ref, o_ref, h_scr):
    @pl.when(pl.program_id(1) == 0)
    def _():
        h_scr[...] = _rms(x_ref[...], g_ref[...]).astype(BF16)

    o_ref[...] = jnp.dot(h_scr[...], w_ref[...], preferred_element_type=F32).astype(o_ref.dtype)


def _in_proj(x, g, w, tm, tn, out_dtype, layer):
    t = x.shape[0]
    return pl.pallas_call(
        _in_proj_kernel,
        grid=(t // tm, PROJ_COLS // tn),
        in_specs=[
            pl.BlockSpec((tm, D_MODEL), lambda i, j: (i, 0)),
            pl.BlockSpec((1, D_MODEL), lambda i, j: (0, 0)),
            pl.BlockSpec((None, D_MODEL, tn), lambda i, j: (layer, 0, j)),
        ],
        out_specs=pl.BlockSpec((tm, tn), lambda i, j: (i, j)),
        out_shape=jax.ShapeDtypeStruct((t, PROJ_COLS), out_dtype),
        scratch_shapes=[pltpu.VMEM((tm, D_MODEL), BF16)],
        compiler_params=_params(("parallel", "arbitrary")),
        name="in_proj",
    )(x, g, w)


def _lru_kernel(xa_ref, ga_ref, prev8_ref, h0_ref, cw_ref, cb_ref, w_ref, bab_ref, lam_ref,
                y_ref, hout_ref, prev_scr, h_scr, *, nb, tt_len, s_valid, pos0_is_zero):
    tt = pl.program_id(0)
    rows = nb * tt_len

    @pl.when(tt == 0)
    def _():
        prev_scr[...] = prev8_ref[...]
        h_scr[...] = h0_ref[...]

    xa = xa_ref[...].astype(F32)
    n8 = tt_len // SUBLANES
    xa4 = xa.reshape(nb, n8, SUBLANES, BRANCH_W)
    prev4 = prev_scr[...][:, None]
    t8 = lax.broadcasted_iota(jnp.int32, (1, 1, SUBLANES, 1), 2)
    cw = cw_ref[...]
    xc4 = cb_ref[...][None, None] + cw[CONV_W - 1][None, None, None] * xa4
    for j in range(1, CONV_W):
        rot = pltpu.roll(xa4, j, 2)
        rot_before = pltpu.roll(prev4, j, 2)
        if n8 > 1:
            rot_before = jnp.concatenate([rot_before, rot[:, :n8 - 1]], axis=1)
        xc4 = xc4 + cw[CONV_W - 1 - j][None, None, None] * jnp.where(t8 < j, rot_before, rot)
    prev_scr[...] = xa[:, tt_len - SUBLANES:, :]

    xc2 = xc4.reshape(rows, BRANCH_W)
    z = _dot(xc2, w_ref[...])
    bab = bab_ref[...]
    r = _sigmoid_t(z[:, :BRANCH_W] + bab[0:1])
    i = _sigmoid_t(z[:, BRANCH_W:] + bab[1:2])
    log_a = (-LRU_C) * r * _softplus(-lam_ref[...])
    a = jnp.exp(log_a)
    m2 = jnp.maximum(1.0 - a * a, 0.0)
    mult = m2 * lax.rsqrt(jnp.maximum(m2, 1e-30))
    t_in = lax.broadcasted_iota(jnp.int32, (rows, 1), 0) % tt_len
    if pos0_is_zero:
        mult = jnp.where(jnp.logical_and(tt == 0, t_in == 0), 1.0, mult)
    b = xc2 * i * mult
    valid = (tt * tt_len + t_in) < s_valid
    a4 = jnp.where(valid, a, 1.0).reshape(nb, n8, SUBLANES, BRANCH_W)
    b4 = jnp.where(valid, b, 0.0).reshape(nb, n8, SUBLANES, BRANCH_W)

    d = 1
    while d < SUBLANES:
        keep = t8 >= d
        b4 = a4 * jnp.where(keep, pltpu.roll(b4, d, 2), 0.0) + b4
        a4 = a4 * jnp.where(keep, pltpu.roll(a4, d, 2), 1.0)
        d *= 2
    h_in = h_scr[...][:, None, :]
    blocks = []
    for blk in range(n8):
        hb = a4[:, blk] * h_in + b4[:, blk]
        blocks.append(hb)
        h_in = hb[:, SUBLANES - 1:SUBLANES, :]
    hh = blocks[0] if n8 == 1 else jnp.concatenate(blocks, axis=1)
    h = h_in.reshape(nb, BRANCH_W)
    h_scr[...] = h
    hout_ref[...] = h

    ga = ga_ref[...].astype(F32)
    gelu = 0.5 * ga * (1.0 + jnp.tanh(math.sqrt(2.0 / math.pi) * (ga + 0.044715 * ga * ga * ga)))
    y_ref[...] = (hh * gelu).astype(y_ref.dtype)


def _lru(proj3, prev8, h0, cw, cb, w, bab, lam, *, tt_len, s_valid, pos0_is_zero):
    nb, s_len, _ = proj3.shape
    blk = OFF_LRU // BRANCH_W
    kern = functools.partial(_lru_kernel, nb=nb, tt_len=tt_len, s_valid=s_valid, pos0_is_zero=pos0_is_zero)
    full2 = lambda t: (0, 0)
    return pl.pallas_call(
        kern,
        grid=(s_len // tt_len,),
        in_specs=[
            pl.BlockSpec((nb, tt_len, BRANCH_W), lambda t: (0, t, blk)),
            pl.BlockSpec((nb, tt_len, BRANCH_W), lambda t: (0, t, blk + 1)),
            pl.BlockSpec((nb, SUBLANES, BRANCH_W), lambda t: (0, 0, 0)),
            pl.BlockSpec((nb, BRANCH_W), full2),
            pl.BlockSpec((CONV_W, BRANCH_W), full2),
            pl.BlockSpec((1, BRANCH_W), full2),
            pl.BlockSpec((BRANCH_W, 2 * BRANCH_W), full2),
            pl.BlockSpec((2, BRANCH_W), full2),
            pl.BlockSpec((1, BRANCH_W), full2),
        ],
        out_specs=[
            pl.BlockSpec((nb, tt_len, BRANCH_W), lambda t: (0, t, 0)),
            pl.BlockSpec((nb, BRANCH_W), full2),
        ],
        out_shape=[
            jax.ShapeDtypeStruct((nb, s_len, BRANCH_W), proj3.dtype),
            jax.ShapeDtypeStruct((nb, BRANCH_W), F32),
        ],
        scratch_shapes=[
            pltpu.VMEM((nb, SUBLANES, BRANCH_W), F32),
            pltpu.VMEM((nb, BRANCH_W), F32),
        ],
        compiler_params=_params(("arbitrary",)),
        name="rglru",
    )(proj3, proj3, prev8, h0, cw, cb, w, bab, lam)


def _own_layer_block(sout_ref, layer):
    if layer > 0:
        return sout_ref
    if sout_ref.shape[0] > 1:
        sout_ref[1:] = jnp.zeros((sout_ref.shape[0] - 1,) + tuple(sout_ref.shape[1:]), sout_ref.dtype)
    return sout_ref.at[0]


def _state_out(s0, sbuf, layer, nbb, tail):
    depth = s0.shape[0]
    zeros = (0,) * len(tail)
    if layer == 0:
        spec = pl.BlockSpec((depth, nbb) + tail, lambda b, t: (0, b) + zeros)
        return spec, [], []
    spec = pl.BlockSpec((None, nbb) + tail, lambda b, t: (layer, b) + zeros)
    return spec, [pl.BlockSpec(memory_space=pl.ANY)], [sbuf]


def _hg_diag_blocks(qh, kh, vh, bch):
    c = qh.shape[0]
    nblk = c // SUBLANES
    q3 = qh.reshape(nblk, SUBLANES, HG_D)
    k3 = kh.reshape(nblk, SUBLANES, HG_D)
    v3 = vh.reshape(nblk, SUBLANES, HG_D)
    b3 = bch.reshape(nblk, SUBLANES, HG_D)
    tin = lax.broadcasted_iota(jnp.int32, (1, SUBLANES, 1), 1)
    o3 = jnp.zeros((nblk, SUBLANES, HG_D), F32)
    for s in range(SUBLANES):
        dec = jnp.exp(jnp.minimum(b3 - b3[:, s:s + 1, :], 0.0))
        w = jnp.sum(q3 * k3[:, s:s + 1, :] * dec, axis=-1, keepdims=True)
        w = jnp.where(tin >= s, w, 0.0)
        o3 = o3 + w * v3[:, s:s + 1, :]
    return o3.reshape(c, HG_D)


def _hg_level_refs(bch, h, c):
    gq, gk = [], []
    zero = jnp.zeros((h, HG_D), F32)
    for j in range(c // h):
        if j % 2 == 1:
            gq.append(jnp.broadcast_to(bch[j * h - 1:j * h, :], (h, HG_D)))
            gk.append(zero)
        else:
            gq.append(zero)
            gk.append(jnp.broadcast_to(bch[(j + 1) * h - 1:(j + 1) * h, :], (h, HG_D)))
    return jnp.concatenate(gq, axis=0), jnp.concatenate(gk, axis=0)


def _hgrn_kernel(q_ref, f_ref, v_ref, g_ref, s0_ref, lbraw_ref, ng_ref, cum_ref, y_ref, sout_ref,
                 st_scr, qs_scr, k_scr, v_scr, bc_scr, qe_scr, kh_scr, et_scr, o_scr,
                 *, nbb, tt_len, chunk, s_valid, s_len, layer, ub):
    tt = pl.program_id(1)
    n_t = pl.num_programs(1)
    c = chunk
    rows = nbb * tt_len
    n_cb = tt_len // c
    kv_major = s_len == c

    raw = lbraw_ref[...]
    ex = jnp.exp(raw - jnp.max(raw, axis=0, keepdims=True))
    sm = ex / jnp.sum(ex, axis=0, keepdims=True)
    lb = jnp.zeros((1, BRANCH_W), F32)
    for l in range(1, layer + 1):
        lb = lb + sm[l:l + 1]

    @pl.when(tt == 0)
    def _():
        def init(bb, carry):
            for hd in range(HG_HEADS):
                st_scr[bb, hd] = s0_ref[bb, hd] if kv_major else s0_ref[bb, hd].T
            return carry

        lax.fori_loop(0, nbb, init, 0)

    q = q_ref[...].astype(F32).reshape(rows, BRANCH_W)
    fp = f_ref[...].astype(F32).reshape(rows, BRANCH_W)
    sg = _sigmoid(fp)
    f = lb + (1.0 - lb) * sg
    k = (1.0 - lb) * (1.0 - sg)
    logf = jnp.log(jnp.maximum(f, HG_F_MIN))
    if s_valid < s_len:
        t_in = lax.broadcasted_iota(jnp.int32, (rows, 1), 0) % tt_len
        valid = (tt * tt_len + t_in) < s_valid
        k = jnp.where(valid, k, 0.0)
        logf = jnp.where(valid, logf, 0.0)
    bc, btot = _chunk_sums(cum_ref[...], logf, nbb, tt_len)
    qs = q * _sigmoid_t(q)
    qs_scr[...] = qs
    k_scr[...] = k
    v_scr[...] = v_ref[...].astype(F32).reshape(rows, BRANCH_W)
    bc_scr[...] = bc
    qe_scr[...] = qs * jnp.exp(bc)
    kh_scr[...] = k * jnp.exp(btot - bc)
    et_scr[...] = jnp.exp(btot)

    ti = lax.broadcasted_iota(jnp.int32, (c, 1), 0)
    ii = lax.broadcasted_iota(jnp.int32, (c, c), 0)
    jj = lax.broadcasted_iota(jnp.int32, (c, c), 1)
    levels = []
    h = c // 2
    while h >= SUBLANES:
        odd = (ti // h) % 2 == 1
        pair = jnp.logical_and(ii // (2 * h) == jj // (2 * h),
                               jnp.logical_and((ii // h) % 2 == 1, (jj // h) % 2 == 0))
        levels.append((h, odd, pair))
        h //= 2

    def step(it, carry):
        cb = it // (nbb // ub)
        b0 = (it % (nbb // ub)) * ub
        chains = []
        for u in range(ub):
            r0 = pl.multiple_of((b0 + u) * tt_len + cb * c, c)
            for hd in range(HG_HEADS):
                ln = slice(hd * HG_D, (hd + 1) * HG_D)
                chains.append(dict(bb=b0 + u, hd=hd, r0=r0, ln=ln, st=st_scr[b0 + u, hd],
                                   qh=qs_scr[pl.ds(r0, c), ln], kh=k_scr[pl.ds(r0, c), ln],
                                   vh=v_scr[pl.ds(r0, c), ln], bch=bc_scr[pl.ds(r0, c), ln]))
        if kv_major:
            outs = [_dot(qe_scr[pl.ds(x["r0"], c), x["ln"]], x["st"]) for x in chains]
        else:
            outs = [_dot_nt(qe_scr[pl.ds(x["r0"], c), x["ln"]], x["st"]) for x in chains]
        if levels:
            amats = []
            for x in chains:
                amat = None
                for (h, odd, pair) in levels:
                    gq, gk = _hg_level_refs(x["bch"], h, c)
                    qt = jnp.where(odd, x["qh"] * jnp.exp(jnp.where(odd, x["bch"] - gq, 0.0)), 0.0)
                    kt = jnp.where(odd, 0.0, x["kh"] * jnp.exp(jnp.where(odd, 0.0, gk - x["bch"])))
                    term = jnp.where(pair, _dot_nt(qt, kt), 0.0)
                    amat = term if amat is None else amat + term
                amats.append(amat)
            outs = [o + _dot(a, x["vh"]) for o, a, x in zip(outs, amats, chains)]
        for o, x in zip(outs, chains):
            o_scr[pl.ds(x["r0"], c), x["ln"]] = o + _hg_diag_blocks(x["qh"], x["kh"], x["vh"], x["bch"])
        for x in chains:
            khat = kh_scr[pl.ds(x["r0"], c), x["ln"]]
            if kv_major:
                last = jnp.where(ti == c - 1, x["bch"], 0.0)
                hi, lo = _split2(last)
                ones = jnp.ones((c, HG_D), BF16)
                col = (lax.dot_general(hi, ones, (((0,), (0,)), ((), ())), preferred_element_type=F32)
                       + lax.dot_general(lo, ones, (((0,), (0,)), ((), ())), preferred_element_type=F32))
                st_scr[x["bb"], x["hd"]] = x["st"] * jnp.exp(col) + _dot_tn(khat, x["vh"])
            else:
                st_scr[x["bb"], x["hd"]] = x["st"] * et_scr[pl.ds(x["r0"], 1), x["ln"]] + _dot_tn(x["vh"], khat)
        return carry

    n_it = n_cb * (nbb // ub)
    if n_it == 1:
        step(0, 0)
    else:
        lax.fori_loop(0, n_it, step, 0)

    g = g_ref[...].astype(F32).reshape(rows, BRANCH_W)
    ng = ng_ref[...]
    outs = []
    for hd in range(HG_HEADS):
        ln = slice(hd * HG_D, (hd + 1) * HG_D)
        o = o_scr[:, ln]
        outs.append(o * lax.rsqrt(jnp.mean(o * o, axis=-1, keepdims=True) + EPS) * ng[:, ln])
    y = jnp.concatenate(outs, axis=-1) * (g * _sigmoid_t(g))
    y_ref[...] = y.reshape(nbb, tt_len, BRANCH_W).astype(y_ref.dtype)

    @pl.when(tt == n_t - 1)
    def _():
        out = _own_layer_block(sout_ref, layer)

        def fin(bb, carry):
            for hd in range(HG_HEADS):
                out[bb, hd] = st_scr[bb, hd] if kv_major else st_scr[bb, hd].T
            return carry

        lax.fori_loop(0, nbb, fin, 0)


def _hgrn_kernel_inplace(q_ref, f_ref, v_ref, g_ref, s0_ref, lbraw_ref, ng_ref, cum_ref, sbuf_ref, *rest, **kw):
    _hgrn_kernel(q_ref, f_ref, v_ref, g_ref, s0_ref, lbraw_ref, ng_ref, cum_ref, *rest, **kw)


def _chunk_cum_matrix(rows, c):
    i = jnp.arange(rows)[:, None]
    j = jnp.arange(rows)[None, :]
    same = i // c == j // c
    return jnp.concatenate([same & (i >= j), same], axis=0).astype(BF16)


def _hgrn(proj3, s0, sbuf, lbraw, ng, *, nbb, tt_len, chunk, s_valid, layer, ub):
    nb, s_len, _ = proj3.shape
    blk = OFF_HG // BRANCH_W
    rows = nbb * tt_len
    kern = functools.partial(_hgrn_kernel if layer == 0 else _hgrn_kernel_inplace, nbb=nbb, tt_len=tt_len,
                             chunk=chunk, s_valid=s_valid, s_len=s_len, layer=layer, ub=ub)
    seq = lambda k: pl.BlockSpec((nbb, tt_len, BRANCH_W), lambda b, t, k=k: (b, t, blk + k))
    st_in = pl.BlockSpec((None, nbb, HG_HEADS, HG_D, HG_D), lambda b, t: (layer, b, 0, 0, 0))
    st_out, extra_specs, extra_args = _state_out(s0, sbuf, layer, nbb, (HG_HEADS, HG_D, HG_D))
    return pl.pallas_call(
        kern,
        grid=(nb // nbb, s_len // tt_len),
        in_specs=[seq(0), seq(1), seq(2), seq(3), st_in,
                  pl.BlockSpec(lbraw.shape, lambda b, t: (0, 0)),
                  pl.BlockSpec((1, BRANCH_W), lambda b, t: (0, 0)),
                  pl.BlockSpec((2 * tt_len, tt_len), lambda b, t: (0, 0))] + extra_specs,
        out_specs=[pl.BlockSpec((nbb, tt_len, BRANCH_W), lambda b, t: (b, t, 0)), st_out],
        out_shape=[jax.ShapeDtypeStruct((nb, s_len, BRANCH_W), proj3.dtype),
                   jax.ShapeDtypeStruct(s0.shape, F32)],
        input_output_aliases={8: 1} if extra_args else {},
        scratch_shapes=[pltpu.VMEM((nbb, HG_HEADS, HG_D, HG_D), F32)]
        + [pltpu.VMEM((rows, BRANCH_W), F32)] * 8,
        compiler_params=_params(("parallel", "arbitrary")),
        name="hgrn2",
    )(proj3, proj3, proj3, proj3, s0, lbraw, ng, _chunk_cum_matrix(tt_len, chunk), *extra_args)


def _rwkv_kernel(c_ref, sh0_ref, s0_ref, mu_ref, wl_ref, vec_ref, cum_ref, y_ref, sout_ref,
                 sbd_scr, carry_scr, at_scr, rt_scr, bt_scr, kt_scr, bp_scr, kp_scr, v_scr, pc_scr,
                 bonus_scr, gate_scr, o_scr, tcat_scr, aak_scr, arb_scr, ark_scr,
                 *, nbb, tt_len, chunk, s_valid, s_len, layer, ua, ub):
    tt = pl.program_id(1)
    n_t = pl.num_programs(1)
    c = chunk
    c4 = RW_GROUP * c
    n_groups = RW_HEADS // RW_GROUP
    rows = nbb * tt_len
    n_cb = tt_len // c
    n_ch = rows // c

    vec = vec_ref[...]
    w0, a0, k_k, k_a, r_k, ln_g, ln_b = [vec[i:i + 1] for i in range(7)]
    gi_ = lax.broadcasted_iota(jnp.int32, (RW_GW, RW_GW), 0)
    gj_ = lax.broadcasted_iota(jnp.int32, (RW_GW, RW_GW), 1)
    bd_state = gi_ // RW_HD == gj_ // RW_HD

    @pl.when(tt == 0)
    def _():
        carry_scr[...] = sh0_ref[...]

        def init(bb, carry):
            zero = jnp.zeros((RW_HD, RW_HD), F32)
            for g in range(n_groups):
                blocks = [jnp.concatenate([s0_ref[bb, g * RW_GROUP + h] if j == h else zero for j in range(RW_GROUP)],
                                          axis=1) for h in range(RW_GROUP)]
                sbd_scr[bb, g] = jnp.concatenate(blocks, axis=0)
            return carry

        lax.fori_loop(0, nbb, init, 0)

    cc3 = c_ref[:, :, :RW_COLS].astype(F32)
    t3 = lax.broadcasted_iota(jnp.int32, (1, tt_len, 1), 1)
    prev3 = jnp.where(t3 == 0, carry_scr[...], pltpu.roll(cc3, 1, 1))
    carry_scr[...] = cc3[:, tt_len - 1:tt_len, :]
    xm = (cc3 + (prev3 - cc3) * mu_ref[...][None]).reshape(rows, RW_COLS)
    r = xm[:, 0:BRANCH_W]
    k = xm[:, BRANCH_W:2 * BRANCH_W]
    v = xm[:, 2 * BRANCH_W:3 * BRANCH_W]
    lo = xm[:, 3 * BRANCH_W:]
    lane_l = lax.broadcasted_iota(jnp.int32, (1, RW_LORA), 1)
    act = jnp.where(lane_l < RW_LORA_W, jnp.tanh(lo),
                    jnp.where(lane_l < RW_LORA_W + RW_LORA_A, lo, _sigmoid_t(lo)))
    z = _dot(act, wl_ref[...])
    ld = (-math.exp(-0.5)) * _sigmoid_t(w0 + z[:, 0:BRANCH_W])
    a = _sigmoid_t(a0 + z[:, BRANCH_W:2 * BRANCH_W])
    kk = k * k_k
    kbar = k * (1.0 + (a - 1.0) * k_a)
    sums = _seg_sum(jnp.concatenate([kk * kk, r * kbar * r_k], axis=0), RW_HD)
    kap = kk * lax.rsqrt(jnp.maximum(sums[:rows], 1e-24))
    if s_valid < s_len:
        t_in = lax.broadcasted_iota(jnp.int32, (rows, 1), 0) % tt_len
        valid = (tt * tt_len + t_in) < s_valid
        ld = jnp.where(valid, ld, 0.0)
        kap = jnp.where(valid, kap, 0.0)
        kbar = jnp.where(valid, kbar, 0.0)
    lw, ltot = _chunk_sums(cum_ref[...], ld, nbb, tt_len)
    back = ltot - lw
    e_in = jnp.exp(lw)
    e_neg = jnp.exp(-lw)
    e_back = jnp.exp(back)
    at_scr[...] = -kap * jnp.exp(lw - ld)
    rt_scr[...] = r * e_in
    bt_scr[...] = kap * a * e_neg
    kt_scr[...] = kbar * e_neg
    bp_scr[...] = kap * a * e_back
    kp_scr[...] = kbar * e_back
    v_scr[...] = v
    pc_scr[...] = jnp.exp(ltot)
    bonus_scr[...] = sums[rows:]
    gate_scr[...] = z[:, 2 * BRANCH_W:]

    si = lax.broadcasted_iota(jnp.int32, (c4, RW_GW), 0)
    sj = lax.broadcasted_iota(jnp.int32, (c4, RW_GW), 1)
    head_rows = si // c == sj // RW_HD
    qi = lax.broadcasted_iota(jnp.int32, (c4, c4), 0)
    qj = lax.broadcasted_iota(jnp.int32, (c4, c4), 1)
    same = qi // c == qj // c
    strict = jnp.logical_and(same, qi % c > qj % c)
    incl = jnp.logical_and(same, qi % c >= qj % c)
    eye = (qi == qj).astype(F32)
    fuse_sq = c4 % LANES == 0

    def stack(x):
        return jnp.where(head_rows, jnp.concatenate([x] * RW_GROUP, axis=0), 0.0)

    def unstack(x):
        out = x[0:c]
        for h in range(1, RW_GROUP):
            out = out + x[h * c:(h + 1) * c]
        return out

    def phase_a(it, carry):
        chains = [(it * ua + u, g) for u in range(ua) for g in range(n_groups)]
        nmats = []
        for ch, g in chains:
            r0 = pl.multiple_of(ch * c, c)
            ln = slice(g * RW_GW, (g + 1) * RW_GW)
            lhs = jnp.concatenate([stack(at_scr[pl.ds(r0, c), ln]), stack(rt_scr[pl.ds(r0, c), ln])], axis=0)
            rhs = jnp.concatenate([bt_scr[pl.ds(r0, c), ln]] * RW_GROUP + [kt_scr[pl.ds(r0, c), ln]] * RW_GROUP,
                                  axis=0)
            quad = _dot_nt(lhs, rhs)
            nmats.append(jnp.where(strict, quad[:c4, :c4], 0.0))
            aak_scr[ch, g] = unstack(jnp.where(strict, quad[:c4, c4:], 0.0))
            arb_scr[ch, g] = unstack(jnp.where(incl, quad[c4:, :c4], 0.0))
            ark_scr[ch, g] = unstack(jnp.where(incl, quad[c4:, c4:], 0.0))
        tinvs = [eye + n for n in nmats]
        npows = [_dot(n, n) for n in nmats]
        span = 2
        while 2 * span < c:
            if fuse_sq:
                boths = [_dot(p, jnp.concatenate([p, t], axis=1)) for p, t in zip(npows, tinvs)]
                tinvs = [t + bo[:, c4:] for t, bo in zip(tinvs, boths)]
                npows = [bo[:, :c4] for bo in boths]
            else:
                tinvs = [t + _dot(p, t) for p, t in zip(npows, tinvs)]
                npows = [_dot(p, p) for p in npows]
            span *= 2
        tinvs = [t + _dot(p, t) for p, t in zip(npows, tinvs)]
        for (ch, g), t in zip(chains, tinvs):
            tcat_scr[ch, g] = unstack(t)
        return carry

    if n_ch // ua == 1:
        phase_a(0, 0)
    else:
        lax.fori_loop(0, n_ch // ua, phase_a, 0)

    def phase_b(it, carry):
        cb = it // (nbb // ub)
        b0 = (it % (nbb // ub)) * ub
        chains = [(b0 + u, g) for u in range(ub) for g in range(n_groups)]
        ops = []
        for bb, g in chains:
            ch = bb * n_cb + cb
            r0 = pl.multiple_of(ch * c, c)
            ln = slice(g * RW_GW, (g + 1) * RW_GW)
            ops.append(dict(bb=bb, g=g, ch=ch, r0=r0, ln=ln, sbd=sbd_scr[bb, g],
                            v_bd=stack(v_scr[pl.ds(r0, c), ln])))
        sprods = [_dot_nt(jnp.concatenate([at_scr[pl.ds(q["r0"], c), q["ln"]], rt_scr[pl.ds(q["r0"], c), q["ln"]]],
                                          axis=0), q["sbd"]) for q in ops]
        wmats = [sp[:c] + _dot(aak_scr[q["ch"], q["g"]], q["v_bd"]) for q, sp in zip(ops, sprods)]
        us = [_dot(tcat_scr[q["ch"], q["g"]], stack(w)) for q, w in zip(ops, wmats)]
        for q, u, sp in zip(ops, us, sprods):
            r0, ln = q["r0"], q["ln"]
            o_scr[pl.ds(r0, c), ln] = (sp[c:] + _dot(arb_scr[q["ch"], q["g"]], stack(u))
                                       + _dot(ark_scr[q["ch"], q["g"]], q["v_bd"]))
        for q, u in zip(ops, us):
            r0, ln = q["r0"], q["ln"]
            upd = _dot_tn(jnp.concatenate([u, v_scr[pl.ds(r0, c), ln]], axis=0),
                          jnp.concatenate([bp_scr[pl.ds(r0, c), ln], kp_scr[pl.ds(r0, c), ln]], axis=0))
            sbd_scr[q["bb"], q["g"]] = q["sbd"] * pc_scr[pl.ds(r0, 1), ln] + jnp.where(bd_state, upd, 0.0)
        return carry

    n_it = n_cb * (nbb // ub)
    if n_it == 1:
        phase_b(0, 0)
    else:
        lax.fori_loop(0, n_it, phase_b, 0)

    o = o_scr[...]
    inv_n = 1.0 / RW_HD
    mean = _seg_sum(o, RW_HD) * inv_n
    cen = o - mean
    var = _seg_sum(cen * cen, RW_HD) * inv_n
    on = cen * lax.rsqrt(var + RW_GN_EPS) * ln_g + ln_b
    y = (on + bonus_scr[...] * v_scr[...]) * gate_scr[...]
    y_ref[...] = y.reshape(nbb, tt_len, BRANCH_W).astype(y_ref.dtype)

    @pl.when(tt == n_t - 1)
    def _():
        out = _own_layer_block(sout_ref, layer)

        def fin(bb, carry):
            for g in range(n_groups):
                sbd = sbd_scr[bb, g]
                for h in range(RW_GROUP):
                    out[bb, g * RW_GROUP + h] = sbd[h * RW_HD:(h + 1) * RW_HD, h * RW_HD:(h + 1) * RW_HD]
            return carry

        lax.fori_loop(0, nbb, fin, 0)


def _rwkv_kernel_inplace(c_ref, sh0_ref, s0_ref, mu_ref, wl_ref, vec_ref, cum_ref, sbuf_ref, *rest, **kw):
    _rwkv_kernel(c_ref, sh0_ref, s0_ref, mu_ref, wl_ref, vec_ref, cum_ref, *rest, **kw)


def _rwkv(proj3, sh0, s0, sbuf, mu, wl, vec, *, nbb, tt_len, chunk, s_valid, layer, ua, ub):
    nb, s_len, _ = proj3.shape
    kern = functools.partial(_rwkv_kernel if layer == 0 else _rwkv_kernel_inplace, nbb=nbb, tt_len=tt_len,
                             chunk=chunk, s_valid=s_valid, s_len=s_len, layer=layer, ua=ua, ub=ub)
    st_in = pl.BlockSpec((None, nbb, RW_HEADS, RW_HD, RW_HD), lambda b, t: (layer, b, 0, 0, 0))
    st_out, extra_specs, extra_args = _state_out(s0, sbuf, layer, nbb, (RW_HEADS, RW_HD, RW_HD))
    full2 = lambda b, t: (0, 0)
    rows = nbb * tt_len
    n_groups = RW_HEADS // RW_GROUP
    mats = pltpu.VMEM((rows // chunk, n_groups, chunk, RW_GROUP * chunk), F32)
    return pl.pallas_call(
        kern,
        grid=(nb // nbb, s_len // tt_len),
        in_specs=[
            pl.BlockSpec((nbb, tt_len, RW_BLOCK), lambda b, t: (b, t, OFF_RW // RW_BLOCK)),
            pl.BlockSpec((nbb, 1, RW_COLS), lambda b, t: (b, 0, 0)),
            st_in,
            pl.BlockSpec((1, RW_COLS), full2),
            pl.BlockSpec((RW_LORA, 3 * BRANCH_W), full2),
            pl.BlockSpec((SUBLANES, BRANCH_W), full2),
            pl.BlockSpec((2 * tt_len, tt_len), full2),
        ] + extra_specs,
        out_specs=[pl.BlockSpec((nbb, tt_len, BRANCH_W), lambda b, t: (b, t, 0)), st_out],
        out_shape=[jax.ShapeDtypeStruct((nb, s_len, BRANCH_W), proj3.dtype),
                   jax.ShapeDtypeStruct(s0.shape, F32)],
        input_output_aliases={7: 1} if extra_args else {},
        scratch_shapes=[
            pltpu.VMEM((nbb, n_groups, RW_GW, RW_GW), F32),
            pltpu.VMEM((nbb, 1, RW_COLS), F32),
        ] + [pltpu.VMEM((rows, BRANCH_W), F32)] * 11 + [mats] * 4,
        compiler_params=_params(("parallel", "arbitrary")),
        name="rwkv7",
    )(proj3, sh0, s0, mu, wl, vec, _chunk_cum_matrix(tt_len, chunk), *extra_args)


def _mix_kernel(x_ref, gts_ref, ya_ref, yb_ref, yc_ref, wb_ref, wo_ref, g_ref, o_ref):
    acc = None
    for n, y_ref in enumerate((ya_ref, yb_ref, yc_ref)):
        up = _dot(y_ref[...], wb_ref[n * BRANCH_W:(n + 1) * BRANCH_W, :])
        term = _sigmoid_t(gts_ref[:, n * D_MODEL:(n + 1) * D_MODEL].astype(F32)) * up
        acc = term if acc is None else acc + term
    mix = _dot(acc, wo_ref[...])
    o_ref[...] = x_ref[...] + _rms(mix, g_ref[...])


def _mix(x, proj, ya, yb, yc, wb, wo, g, tm, layer):
    t = x.shape[0]
    row = lambda w: pl.BlockSpec((tm, w), lambda i: (i, 0))
    return pl.pallas_call(
        _mix_kernel,
        grid=(t // tm,),
        in_specs=[row(D_MODEL), row(N_BRANCH * D_MODEL), row(BRANCH_W), row(BRANCH_W), row(BRANCH_W),
                  pl.BlockSpec((None, N_BRANCH * BRANCH_W, D_MODEL), lambda i: (layer, 0, 0)),
                  pl.BlockSpec((None, D_MODEL, D_MODEL), lambda i: (layer, 0, 0)),
                  pl.BlockSpec((1, D_MODEL), lambda i: (0, 0))],
        out_specs=row(D_MODEL),
        out_shape=jax.ShapeDtypeStruct((t, D_MODEL), F32),
        compiler_params=_params(("parallel",)),
        name="branch_mix",
    )(x, proj, ya, yb, yc, wb, wo, g)


def _ffn_kernel(x_ref, p_ref, gpre_ref, wg_ref, wu_ref, wd_ref, gpost_ref, wple_ref, wpg_ref, gple_ref,
                o_ref, h_scr, acc_scr):
    j = pl.program_id(1)

    @pl.when(j == 0)
    def _():
        h_scr[...] = _rms(x_ref[...], gpre_ref[...]).astype(BF16)
        acc_scr[...] = jnp.zeros_like(acc_scr)

    h = h_scr[...]
    gt = jnp.dot(h, wg_ref[...], preferred_element_type=F32)
    up = jnp.dot(h, wu_ref[...], preferred_element_type=F32)
    acc_scr[...] += _dot(gt * _sigmoid_t(gt) * up, wd_ref[...])

    @pl.when(j == pl.num_programs(1) - 1)
    def _():
        x2 = x_ref[...] + _rms(acc_scr[...], gpost_ref[...])
        ple = _dot(p_ref[...], wple_ref[...]) * _sigmoid_t(_dot(x2, wpg_ref[...]))
        o_ref[...] = x2 + _rms(ple, gple_ref[...])


def _ffn(x, p, gpre, wg, wu, wd, gpost, wple, wpg, gple, tm, n_split, layer):
    t = x.shape[0]
    d_ff = wg.shape[2]
    tf = d_ff // n_split
    vecspec = pl.BlockSpec((1, D_MODEL), lambda i, j: (0, 0))
    return pl.pallas_call(
        _ffn_kernel,
        grid=(t // tm, n_split),
        in_specs=[
            pl.BlockSpec((tm, D_MODEL), lambda i, j: (i, 0)),
            pl.BlockSpec((None, tm, PLE_DIM), lambda i, j: (layer, i, 0)),
            vecspec,
            pl.BlockSpec((None, D_MODEL, tf), lambda i, j: (layer, 0, j)),
            pl.BlockSpec((None, D_MODEL, tf), lambda i, j: (layer, 0, j)),
            pl.BlockSpec((None, tf, D_MODEL), lambda i, j: (layer, j, 0)),
            vecspec,
            pl.BlockSpec((None, PLE_DIM, D_MODEL), lambda i, j: (layer, 0, 0)),
            pl.BlockSpec((None, D_MODEL, D_MODEL), lambda i, j: (layer, 0, 0)),
            vecspec,
        ],
        out_specs=pl.BlockSpec((tm, D_MODEL), lambda i, j: (i, 0)),
        out_shape=jax.ShapeDtypeStruct((t, D_MODEL), F32),
        scratch_shapes=[pltpu.VMEM((tm, D_MODEL), BF16), pltpu.VMEM((tm, D_MODEL), F32)],
        compiler_params=_params(("parallel", "arbitrary")),
        name="ffn_ple",
    )(x, p, gpre, wg, wu, wd, gpost, wple, wpg, gple)


def _cast_kernel(x_ref, o_ref):
    o_ref[...] = x_ref[...].astype(o_ref.dtype)


def _to_bf16(w, tr):
    d, r, c = w.shape
    spec = pl.BlockSpec((None, tr, c), lambda l, i: (l, i, 0))
    return pl.pallas_call(
        _cast_kernel, grid=(d, r // tr), in_specs=[spec], out_specs=spec,
        out_shape=jax.ShapeDtypeStruct(w.shape, BF16),
        compiler_params=_params(("parallel", "parallel")), name="to_bf16",
    )(w)


W_IN_TILE = 1024


def _w_in_kernel(x_ref, o_ref):
    is_pad = lax.broadcasted_iota(jnp.int32, o_ref.shape, 1) + pl.program_id(1) * W_IN_TILE >= OFF_RW + RW_COLS
    o_ref[...] = jnp.where(is_pad, 0.0, x_ref[0]).astype(o_ref.dtype)


def _w_in_layout(w_in):
    d, _, in_cols = w_in.shape
    n_gate_cols = N_BRANCH * D_MODEL
    n_rest = in_cols - n_gate_cols

    def src(l, j):
        col = j * W_IN_TILE
        start = jnp.where(col < n_gate_cols, col + n_rest, col - n_gate_cols)
        start = jnp.minimum(start, in_cols - W_IN_TILE)
        return l, 0, pl.multiple_of(start, MXU_DIM)

    return pl.pallas_call(
        _w_in_kernel, grid=(d, PROJ_COLS // W_IN_TILE),
        in_specs=[pl.BlockSpec((pl.Element(1), pl.Element(D_MODEL), pl.Element(W_IN_TILE)), src)],
        out_specs=pl.BlockSpec((None, D_MODEL, W_IN_TILE), lambda l, j: (l, 0, j)),
        out_shape=jax.ShapeDtypeStruct((d, D_MODEL, PROJ_COLS), BF16),
        compiler_params=_params(("parallel", "parallel")), name="w_in_layout",
    )(w_in)


def _block_diag(w):
    n, r, c = w.shape
    eye = jnp.eye(n, dtype=w.dtype)
    return (eye[:, None, :, None] * w[:, :, None, :]).reshape(n * r, n * c)


def _prep_layer(i, W):
    lora = jnp.zeros((RW_LORA, 3 * BRANCH_W), F32)
    lora = lora.at[0:RW_LORA_W, 0:BRANCH_W].set(W["rw_w_up"][i])
    lora = lora.at[RW_LORA_W:RW_LORA_W + RW_LORA_A, BRANCH_W:2 * BRANCH_W].set(W["rw_a_up"][i])
    lora = lora.at[RW_LORA_W + RW_LORA_A:, 2 * BRANCH_W:].set(W["rw_g_up"][i])
    vec = jnp.stack([W["rw_w0"][i], W["rw_a0"][i], W["rw_k_k"][i], W["rw_k_a"][i],
                     W["rw_r_k"][i].reshape(BRANCH_W), W["rw_ln_g"][i], W["rw_ln_b"][i],
                     jnp.zeros((BRANCH_W,), F32)])
    row = lambda name: W[name][i].reshape(1, -1)
    return dict(
        norm_pre_mix=row("norm_pre_mix"),
        conv_w=W["conv_w"][i], conv_b=row("conv_b"),
        lru_w=jnp.concatenate([_block_diag(W["lru_wa"][i]), _block_diag(W["lru_wx"][i])], axis=1).astype(BF16),
        lru_b=jnp.stack([W["lru_ba"][i], W["lru_bx"][i]]), lru_lambda=row("lru_lambda"),
        hg_norm_g=row("hg_norm_g"),
        rw_mu=row("rw_mu"), rw_lora=lora.astype(BF16), rw_vec=vec,
        norm_post_mix=row("norm_post_mix"), norm_pre_ffn=row("norm_pre_ffn"),
        norm_post_ffn=row("norm_post_ffn"), norm_ple=row("norm_ple"),
    )


def _tiles(nb, s_len):
    t = nb * s_len
    tm_in = min(t, 1024)
    tm_tok = min(t, 512)
    if s_len >= 512:
        return dict(tm_in=tm_in, tn_in=2048, proj_dtype=BF16, tm_mix=tm_tok, tm_ffn=tm_tok, ffn_split=2,
                    lru_tt=128,
                    hg=dict(nbb=2, tt_len=256, chunk=64, ub=2),
                    rw=dict(nbb=8, tt_len=64, chunk=64, ua=4, ub=8))
    return dict(tm_in=tm_in, tn_in=2048, proj_dtype=F32, tm_mix=tm_tok, tm_ffn=tm_tok, ffn_split=2,
                lru_tt=s_len,
                hg=dict(nbb=min(nb, 16), tt_len=s_len, chunk=s_len, ub=4),
                rw=dict(nbb=min(nb, 16), tt_len=s_len, chunk=s_len, ua=8, ub=min(nb, 16)))


def _run_trunk(x3, p4, states, layers, big, lbraw, *, s_valid, pos0_is_zero):
    nb, s_len, _ = x3.shape
    t = nb * s_len
    plan = _tiles(nb, s_len)
    conv0, lru0, hg0, rw0, sh0 = states
    x = x3.reshape(t, D_MODEL)
    new = ([], [], [])
    nhg = nrw = None
    for i, L in enumerate(layers):
        proj = _in_proj(x, L["norm_pre_mix"], big["w_in"], plan["tm_in"], plan["tn_in"], plan["proj_dtype"], i)
        proj3 = proj.reshape(nb, s_len, PROJ_COLS)
        prev8 = jnp.pad(conv0[i], ((0, 0), (SUBLANES - (CONV_W - 1), 0), (0, 0)))
        ya, nlru = _lru(proj3, prev8, lru0[i], L["conv_w"], L["conv_b"], L["lru_w"], L["lru_b"], L["lru_lambda"],
                        tt_len=plan["lru_tt"], s_valid=s_valid, pos0_is_zero=pos0_is_zero)
        yb, nhg = _hgrn(proj3, hg0, nhg, lbraw, L["hg_norm_g"], s_valid=s_valid, layer=i, **plan["hg"])
        yc, nrw = _rwkv(proj3, sh0[i].reshape(nb, 1, RW_COLS), rw0, nrw, L["rw_mu"], L["rw_lora"], L["rw_vec"],
                        s_valid=s_valid, layer=i, **plan["rw"])
        x = _mix(x, proj, ya.reshape(t, BRANCH_W), yb.reshape(t, BRANCH_W), yc.reshape(t, BRANCH_W),
                 big["w_branch"], big["w_out"], L["norm_post_mix"], plan["tm_mix"], i)
        x = _ffn(x, p4.reshape(-1, t, PLE_DIM), L["norm_pre_ffn"], big["w_ffn_gate"], big["w_ffn_up"],
                 big["w_ffn_down"], L["norm_post_ffn"], big["w_ple"], big["w_ple_gate"], L["norm_ple"], plan["tm_ffn"], plan["ffn_split"], i)
        nconv = proj3[:, s_valid - (CONV_W - 1):s_valid, OFF_LRU:OFF_LRU + BRANCH_W].astype(F32)
        nsh = proj3[:, s_valid - 1, OFF_RW:OFF_RW + RW_COLS].astype(F32)
        for lst, val in zip(new, (nconv, nlru, nsh)):
            lst.append(val)
    nconv, nlru, nsh = (jnp.stack(l) for l in new)
    return x.reshape(nb, s_len, D_MODEL), (nconv, nlru, nhg, nrw, nsh)


def kernel(x_prompt, x_sample, p_prompt, p_sample, state_conv_a, state_lru_a, state_hgrn, state_rwkv, state_shift_c, norm_pre_mix, w_in, conv_w, conv_b, lru_wa, lru_ba, lru_wx, lru_bx, lru_lambda, hg_lower_bounds, hg_norm_g, rw_mu, rw_w0, rw_w_up, rw_a0, rw_a_up, rw_g_up, rw_k_k, rw_k_a, rw_r_k, rw_ln_g, rw_ln_b, w_branch, w_out, norm_post_mix, norm_pre_ffn, w_ffn_gate, w_ffn_up, w_ffn_down, norm_post_ffn, w_ple, w_ple_gate, norm_ple):
    W = dict(norm_pre_mix=norm_pre_mix, w_in=w_in, conv_w=conv_w, conv_b=conv_b, lru_wa=lru_wa, lru_ba=lru_ba,
             lru_wx=lru_wx, lru_bx=lru_bx, lru_lambda=lru_lambda, hg_norm_g=hg_norm_g, rw_mu=rw_mu, rw_w0=rw_w0,
             rw_w_up=rw_w_up, rw_a0=rw_a0, rw_a_up=rw_a_up, rw_g_up=rw_g_up, rw_k_k=rw_k_k, rw_k_a=rw_k_a,
             rw_r_k=rw_r_k, rw_ln_g=rw_ln_g, rw_ln_b=rw_ln_b, w_branch=w_branch, w_out=w_out,
             norm_post_mix=norm_post_mix, norm_pre_ffn=norm_pre_ffn, w_ffn_gate=w_ffn_gate, w_ffn_up=w_ffn_up,
             w_ffn_down=w_ffn_down, norm_post_ffn=norm_post_ffn, w_ple=w_ple, w_ple_gate=w_ple_gate, norm_ple=norm_ple)
    depth = w_in.shape[0]
    d_ff = w_ffn_gate.shape[2]
    big = dict(
        w_in=_w_in_layout(w_in),
        w_branch=_to_bf16(w_branch.reshape(depth, N_BRANCH * BRANCH_W, D_MODEL), N_BRANCH * BRANCH_W // 2),
        w_out=_to_bf16(w_out, D_MODEL),
        w_ffn_gate=_to_bf16(w_ffn_gate, D_MODEL // 2), w_ffn_up=_to_bf16(w_ffn_up, D_MODEL // 2),
        w_ffn_down=_to_bf16(w_ffn_down, d_ff // 2),
        w_ple=_to_bf16(w_ple, PLE_DIM), w_ple_gate=_to_bf16(w_ple_gate, D_MODEL),
    )
    layers = [_prep_layer(i, W) for i in range(depth)]
    lbraw = hg_lower_bounds.astype(F32)

    bp, sp, _ = x_prompt.shape
    zeros = lambda *shape: jnp.zeros((depth, bp) + shape, F32)
    zero_states = (zeros(CONV_W - 1, BRANCH_W), zeros(BRANCH_W), zeros(HG_HEADS, HG_D, HG_D),
                   zeros(RW_HEADS, RW_HD, RW_HD), zeros(RW_COLS))
    y_prompt, st_p = _run_trunk(x_prompt, p_prompt, zero_states, layers, big, lbraw, s_valid=sp, pos0_is_zero=True)

    _, ss, _ = x_sample.shape
    ss_pad = -(-ss // SUBLANES) * SUBLANES
    xs = jnp.pad(x_sample, ((0, 0), (0, ss_pad - ss), (0, 0)))
    ps = jnp.pad(p_sample, ((0, 0), (0, 0), (0, ss_pad - ss), (0, 0)))
    y_sample, st_s = _run_trunk(xs, ps, (state_conv_a, state_lru_a, state_hgrn, state_rwkv, state_shift_c),
                                layers, big, lbraw, s_valid=ss, pos0_is_zero=False)
    return (y_prompt, y_sample[:, :ss]) + st_p + st_s
```

```python
import functools
import math

import jax
import jax.numpy as jnp
from jax import lax
from jax.experimental import pallas as pl
from jax.experimental.pallas import tpu as pltpu

F32 = jnp.float32
BF16 = jnp.bfloat16

D_MODEL = 1024
BRANCH_W = 512
N_BRANCH = 3
CONV_W = 4
LRU_C = 8.0
HG_HEADS = 4
HG_D = BRANCH_W // HG_HEADS
HG_F_MIN = 1e-20
RW_HD = 64
RW_HEADS = BRANCH_W // RW_HD
RW_LORA_W = 64
RW_LORA_A = 64
RW_LORA_G = 128
RW_LORA = RW_LORA_W + RW_LORA_A + RW_LORA_G
RW_GN_EPS = 64e-5
RW_COLS = 3 * BRANCH_W + RW_LORA
PLE_DIM = 256
EPS = 1e-6

SUBLANES = 8
LANES = 128
MXU_DIM = 256
VMEM_LIMIT = 56 * 1024 * 1024

PROJ_COLS = 8192
OFF_GATES = 0
OFF_LRU = N_BRANCH * D_MODEL
OFF_HG = OFF_LRU + 2 * BRANCH_W
OFF_RW = OFF_HG + 4 * BRANCH_W
RW_BLOCK = PROJ_COLS - OFF_RW

RW_GROUP = LANES // RW_HD
RW_GW = RW_GROUP * RW_HD


def _params(sem):
    return pltpu.CompilerParams(dimension_semantics=sem, vmem_limit_bytes=VMEM_LIMIT)


def _rms(x, g):
    return x * lax.rsqrt(jnp.mean(x * x, axis=-1, keepdims=True) + EPS) * g


def _sigmoid(x):
    return 1.0 / (1.0 + jnp.exp(-x))


def _sigmoid_t(x):
    return 0.5 * jnp.tanh(0.5 * x) + 0.5


def _softplus(x):
    return jnp.maximum(x, 0.0) + jnp.log1p(jnp.exp(-jnp.abs(x)))


def _dot(a, b):
    return jnp.dot(a.astype(BF16), b.astype(BF16), preferred_element_type=F32)


def _dot_nt(a, b):
    return lax.dot_general(a.astype(BF16), b.astype(BF16), (((1,), (1,)), ((), ())),
                           preferred_element_type=F32)


def _dot_tn(a, b):
    return lax.dot_general(a.astype(BF16), b.astype(BF16), (((0,), (0,)), ((), ())),
                           preferred_element_type=F32)


def _split2(x):
    hi = x.astype(BF16)
    return hi, (x - hi.astype(F32)).astype(BF16)


def _seg_sum(x, seg):
    i = lax.broadcasted_iota(jnp.int32, (MXU_DIM, MXU_DIM), 0)
    j = lax.broadcasted_iota(jnp.int32, (MXU_DIM, MXU_DIM), 1)
    ones = (i // seg == j // seg).astype(BF16)
    xb = x.astype(BF16)
    tiles = [jnp.dot(xb[:, l:l + MXU_DIM], ones, preferred_element_type=F32)
             for l in range(0, x.shape[1], MXU_DIM)]
    return jnp.concatenate(tiles, axis=1)


def _chunk_sums(cum, x, nbb, tt_len):
    hi, lo = _split2(x)
    m = cum.astype(BF16)
    pre, tot = [], []
    for b in range(nbb):
        rs = slice(b * tt_len, (b + 1) * tt_len)
        both = jnp.dot(m, lo[rs], preferred_element_type=F32) + jnp.dot(m, hi[rs], preferred_element_type=F32)
        pre.append(both[:tt_len])
        tot.append(both[tt_len:])
    if nbb == 1:
        return pre[0], tot[0]
    return jnp.concatenate(pre, axis=0), jnp.concatenate(tot, axis=0)


def _in_proj_kernel(x_ref, g_ref, w_ref, o_ref, h_scr):
    @pl.when(pl.program_id(1) == 0)
    def _():
        h_scr[...] = _rms(x_ref[...], g_ref[...]).astype(BF16)

    o_ref[...] = jnp.dot(h_scr[...], w_ref[...], preferred_element_type=F32).astype(o_ref.dtype)


def _in_proj(x, g, w, tm, tn, out_dtype, layer):
    t = x.shape[0]
    return pl.pallas_call(
        _in_proj_kernel,
        grid=(t // tm, PROJ_COLS // tn),
        in_specs=[
            pl.BlockSpec((tm, D_MODEL), lambda i, j: (i, 0)),
            pl.BlockSpec((1, D_MODEL), lambda i, j: (0, 0)),
            pl.BlockSpec((None, D_MODEL, tn), lambda i, j: (layer, 0, j)),
        ],
        out_specs=pl.BlockSpec((tm, tn), lambda i, j: (i, j)),
        out_shape=jax.ShapeDtypeStruct((t, PROJ_COLS), out_dtype),
        scratch_shapes=[pltpu.VMEM((tm, D_MODEL), BF16)],
        compiler_params=_params(("parallel", "arbitrary")),
        name="in_proj",
    )(x, g, w)


def _lru_kernel(xa_ref, ga_ref, prev8_ref, h0_ref, cw_ref, cb_ref, w_ref, bab_ref, lam_ref,
                y_ref, hout_ref, prev_scr, h_scr, *, nb, tt_len, s_valid, s_len, pos0_is_zero):
    tt = pl.program_id(0)
    rows = nb * tt_len

    @pl.when(tt == 0)
    def _():
        prev_scr[...] = prev8_ref[...]
        h_scr[...] = h0_ref[...]

    xa = xa_ref[...].astype(F32)
    n8 = tt_len // SUBLANES
    xa4 = xa.reshape(nb, n8, SUBLANES, BRANCH_W)
    prev4 = prev_scr[...][:, None]
    t8 = lax.broadcasted_iota(jnp.int32, (1, 1, SUBLANES, 1), 2)
    cw = cw_ref[...]
    xc4 = cb_ref[...][None, None] + cw[CONV_W - 1][None, None, None] * xa4
    for j in range(1, CONV_W):
        rot = pltpu.roll(xa4, j, 2)
        rot_before = pltpu.roll(prev4, j, 2)
        if n8 > 1:
            rot_before = jnp.concatenate([rot_before, rot[:, :n8 - 1]], axis=1)
        xc4 = xc4 + cw[CONV_W - 1 - j][None, None, None] * jnp.where(t8 < j, rot_before, rot)
    prev_scr[...] = xa[:, tt_len - SUBLANES:, :]

    xc2 = xc4.reshape(rows, BRANCH_W)
    z = _dot(xc2, w_ref[...])
    bab = bab_ref[...]
    r = _sigmoid_t(z[:, :BRANCH_W] + bab[0:1])
    i = _sigmoid_t(z[:, BRANCH_W:] + bab[1:2])
    log_a = (-LRU_C) * r * _softplus(-lam_ref[...])
    a = jnp.exp(log_a)
    m2 = jnp.maximum(1.0 - a * a, 0.0)
    mult = m2 * lax.rsqrt(jnp.maximum(m2, 1e-30))
    t_in = lax.broadcasted_iota(jnp.int32, (rows, 1), 0) % tt_len
    if pos0_is_zero:
        mult = jnp.where(jnp.logical_and(tt == 0, t_in == 0), 1.0, mult)
    b = xc2 * i * mult
    if s_valid < s_len:
        valid = (tt * tt_len + t_in) < s_valid
        a = jnp.where(valid, a, 1.0)
        b = jnp.where(valid, b, 0.0)
    a4 = a.reshape(nb, n8, SUBLANES, BRANCH_W)
    b4 = b.reshape(nb, n8, SUBLANES, BRANCH_W)

    d = 1
    while d < SUBLANES:
        keep = t8 >= d
        b4 = a4 * jnp.where(keep, pltpu.roll(b4, d, 2), 0.0) + b4
        a4 = a4 * jnp.where(keep, pltpu.roll(a4, d, 2), 1.0)
        d *= 2
    h_in = h_scr[...][:, None, :]
    blocks = []
    for blk in range(n8):
        hb = a4[:, blk] * h_in + b4[:, blk]
        blocks.append(hb)
        h_in = hb[:, SUBLANES - 1:SUBLANES, :]
    hh = blocks[0] if n8 == 1 else jnp.concatenate(blocks, axis=1)
    h = h_in.reshape(nb, BRANCH_W)
    h_scr[...] = h
    hout_ref[...] = h

    ga = ga_ref[...].astype(F32)
    gelu = 0.5 * ga * (1.0 + jnp.tanh(math.sqrt(2.0 / math.pi) * (ga + 0.044715 * ga * ga * ga)))
    y_ref[...] = (hh * gelu).astype(y_ref.dtype)


def _lru(proj3, prev8, h0, cw, cb, w, bab, lam, *, tt_len, s_valid, pos0_is_zero):
    nb, s_len, _ = proj3.shape
    blk = OFF_LRU // BRANCH_W
    kern = functools.partial(_lru_kernel, nb=nb, tt_len=tt_len, s_valid=s_valid, s_len=s_len,
                             pos0_is_zero=pos0_is_zero)
    full2 = lambda t: (0, 0)
    return pl.pallas_call(
        kern,
        grid=(s_len // tt_len,),
        in_specs=[
            pl.BlockSpec((nb, tt_len, BRANCH_W), lambda t: (0, t, blk)),
            pl.BlockSpec((nb, tt_len, BRANCH_W), lambda t: (0, t, blk + 1)),
            pl.BlockSpec((nb, SUBLANES, BRANCH_W), lambda t: (0, 0, 0)),
            pl.BlockSpec((nb, BRANCH_W), full2),
            pl.BlockSpec((CONV_W, BRANCH_W), full2),
            pl.BlockSpec((1, BRANCH_W), full2),
            pl.BlockSpec((BRANCH_W, 2 * BRANCH_W), full2),
            pl.BlockSpec((2, BRANCH_W), full2),
            pl.BlockSpec((1, BRANCH_W), full2),
        ],
        out_specs=[
            pl.BlockSpec((nb, tt_len, BRANCH_W), lambda t: (0, t, 0)),
            pl.BlockSpec((nb, BRANCH_W), full2),
        ],
        out_shape=[
            jax.ShapeDtypeStruct((nb, s_len, BRANCH_W), proj3.dtype),
            jax.ShapeDtypeStruct((nb, BRANCH_W), F32),
        ],
        scratch_shapes=[
            pltpu.VMEM((nb, SUBLANES, BRANCH_W), F32),
            pltpu.VMEM((nb, BRANCH_W), F32),
        ],
        compiler_params=_params(("arbitrary",)),
        name="rglru",
    )(proj3, proj3, prev8, h0, cw, cb, w, bab, lam)


def _own_layer_block(sout_ref, layer):
    if layer > 0:
        return sout_ref
    if sout_ref.shape[0] > 1:
        sout_ref[1:] = jnp.zeros((sout_ref.shape[0] - 1,) + tuple(sout_ref.shape[1:]), sout_ref.dtype)
    return sout_ref.at[0]


def _state_out(s0, sbuf, layer, nbb, tail):
    depth = s0.shape[0]
    zeros = (0,) * len(tail)
    if layer == 0:
        spec = pl.BlockSpec((depth, nbb) + tail, lambda b, t: (0, b) + zeros)
        return spec, [], []
    spec = pl.BlockSpec((None, nbb) + tail, lambda b, t: (layer, b) + zeros)
    return spec, [pl.BlockSpec(memory_space=pl.ANY)], [sbuf]


def _hg_diag_blocks(qh, kh, vh, bch):
    c = qh.shape[0]
    nblk = c // SUBLANES
    q3 = qh.reshape(nblk, SUBLANES, HG_D)
    k3 = kh.reshape(nblk, SUBLANES, HG_D)
    v3 = vh.reshape(nblk, SUBLANES, HG_D)
    b3 = bch.reshape(nblk, SUBLANES, HG_D)
    tin = lax.broadcasted_iota(jnp.int32, (1, SUBLANES, 1), 1)
    o3 = jnp.zeros((nblk, SUBLANES, HG_D), F32)
    for s in range(SUBLANES):
        dec = jnp.exp(jnp.minimum(b3 - b3[:, s:s + 1, :], 0.0))
        w = jnp.sum(q3 * k3[:, s:s + 1, :] * dec, axis=-1, keepdims=True)
        w = jnp.where(tin >= s, w, 0.0)
        o3 = o3 + w * v3[:, s:s + 1, :]
    return o3.reshape(c, HG_D)


def _hg_level_refs(bch, h, c):
    gq, gk = [], []
    zero = jnp.zeros((h, HG_D), F32)
    for j in range(c // h):
        if j % 2 == 1:
            gq.append(jnp.broadcast_to(bch[j * h - 1:j * h, :], (h, HG_D)))
            gk.append(zero)
        else:
            gq.append(zero)
            gk.append(jnp.broadcast_to(bch[(j + 1) * h - 1:(j + 1) * h, :], (h, HG_D)))
    return jnp.concatenate(gq, axis=0), jnp.concatenate(gk, axis=0)


def _hgrn_kernel(q_ref, f_ref, v_ref, g_ref, s0_ref, lbraw_ref, ng_ref, cum_ref, y_ref, sout_ref,
                 st_scr, qs_scr, k_scr, v_scr, bc_scr, qe_scr, kh_scr, et_scr, o_scr,
                 *, nbb, tt_len, chunk, s_valid, s_len, layer, ub):
    tt = pl.program_id(1)
    n_t = pl.num_programs(1)
    c = chunk
    rows = nbb * tt_len
    n_cb = tt_len // c
    kv_major = s_len == c

    raw = lbraw_ref[...]
    ex = jnp.exp(raw - jnp.max(raw, axis=0, keepdims=True))
    sm = ex / jnp.sum(ex, axis=0, keepdims=True)
    lb = jnp.zeros((1, BRANCH_W), F32)
    for l in range(1, layer + 1):
        lb = lb + sm[l:l + 1]

    @pl.when(tt == 0)
    def _():
        def init(bb, carry):
            for hd in range(HG_HEADS):
                st_scr[bb, hd] = s0_ref[bb, hd] if kv_major else s0_ref[bb, hd].T
            return carry

        lax.fori_loop(0, nbb, init, 0)

    q = q_ref[...].astype(F32).reshape(rows, BRANCH_W)
    fp = f_ref[...].astype(F32).reshape(rows, BRANCH_W)
    sg = _sigmoid(fp)
    f = lb + (1.0 - lb) * sg
    k = (1.0 - lb) * (1.0 - sg)
    logf = jnp.log(jnp.maximum(f, HG_F_MIN))
    if s_valid < s_len:
        t_in = lax.broadcasted_iota(jnp.int32, (rows, 1), 0) % tt_len
        valid = (tt * tt_len + t_in) < s_valid
        k = jnp.where(valid, k, 0.0)
        logf = jnp.where(valid, logf, 0.0)
    bc, btot = _chunk_sums(cum_ref[...], logf, nbb, tt_len)
    qs = q * _sigmoid_t(q)
    qs_scr[...] = qs
    k_scr[...] = k
    v_scr[...] = v_ref[...].astype(F32).reshape(rows, BRANCH_W)
    bc_scr[...] = bc
    qe_scr[...] = qs * jnp.exp(bc)
    kh_scr[...] = k * jnp.exp(btot - bc)
    et_scr[...] = jnp.exp(btot)

    ti = lax.broadcasted_iota(jnp.int32, (c, 1), 0)
    ii = lax.broadcasted_iota(jnp.int32, (c, c), 0)
    jj = lax.broadcasted_iota(jnp.int32, (c, c), 1)
    levels = []
    h = c // 2
    while h >= SUBLANES:
        odd = (ti // h) % 2 == 1
        pair = jnp.logical_and(ii // (2 * h) == jj // (2 * h),
                               jnp.logical_and((ii // h) % 2 == 1, (jj // h) % 2 == 0))
        levels.append((h, odd, pair))
        h //= 2

    def step(it, carry):
        cb = it // (nbb // ub)
        b0 = (it % (nbb // ub)) * ub
        chains = []
        for u in range(ub):
            r0 = pl.multiple_of((b0 + u) * tt_len + cb * c, c)
            for hd in range(HG_HEADS):
                ln = slice(hd * HG_D, (hd + 1) * HG_D)
                chains.append(dict(bb=b0 + u, hd=hd, r0=r0, ln=ln, st=st_scr[b0 + u, hd],
                                   qh=qs_scr[pl.ds(r0, c), ln], kh=k_scr[pl.ds(r0, c), ln],
                                   vh=v_scr[pl.ds(r0, c), ln], bch=bc_scr[pl.ds(r0, c), ln]))
        if kv_major:
            outs = [_dot(qe_scr[pl.ds(x["r0"], c), x["ln"]], x["st"]) for x in chains]
        else:
            outs = [_dot_nt(qe_scr[pl.ds(x["r0"], c), x["ln"]], x["st"]) for x in chains]
        if levels:
            amats = []
            for x in chains:
                amat = None
                for (h, odd, pair) in levels:
                    gq, gk = _hg_level_refs(x["bch"], h, c)
                    qt = jnp.where(odd, x["qh"] * jnp.exp(jnp.where(odd, x["bch"] - gq, 0.0)), 0.0)
                    kt = jnp.where(odd, 0.0, x["kh"] * jnp.exp(jnp.where(odd, 0.0, gk - x["bch"])))
                    term = jnp.where(pair, _dot_nt(qt, kt), 0.0)
                    amat = term if amat is None else amat + term
                amats.append(amat)
            outs = [o + _dot(a, x["vh"]) for o, a, x in zip(outs, amats, chains)]
        for o, x in zip(outs, chains):
            o_scr[pl.ds(x["r0"], c), x["ln"]] = o + _hg_diag_blocks(x["qh"], x["kh"], x["vh"], x["bch"])
        for x in chains:
            khat = kh_scr[pl.ds(x["r0"], c), x["ln"]]
            if kv_major:
                last = jnp.where(ti == c - 1, x["bch"], 0.0)
                hi, lo = _split2(last)
                ones = jnp.ones((c, HG_D), BF16)
                col = (lax.dot_general(hi, ones, (((0,), (0,)), ((), ())), preferred_element_type=F32)
                       + lax.dot_general(lo, ones, (((0,), (0,)), ((), ())), preferred_element_type=F32))
                st_scr[x["bb"], x["hd"]] = x["st"] * jnp.exp(col) + _dot_tn(khat, x["vh"])
            else:
                st_scr[x["bb"], x["hd"]] = x["st"] * et_scr[pl.ds(x["r0"], 1), x["ln"]] + _dot_tn(x["vh"], khat)
        return carry

    n_it = n_cb * (nbb // ub)
    if n_it == 1:
        step(0, 0)
    else:
        lax.fori_loop(0, n_it, step, 0)

    g = g_ref[...].astype(F32).reshape(rows, BRANCH_W)
    ng = ng_ref[...]
    outs = []
    for hd in range(HG_HEADS):
        ln = slice(hd * HG_D, (hd + 1) * HG_D)
        o = o_scr[:, ln]
        outs.append(o * lax.rsqrt(jnp.mean(o * o, axis=-1, keepdims=True) + EPS) * ng[:, ln])
    y = jnp.concatenate(outs, axis=-1) * (g * _sigmoid_t(g))
    y_ref[...] = y.reshape(nbb, tt_len, BRANCH_W).astype(y_ref.dtype)

    @pl.when(tt == n_t - 1)
    def _():
        out = _own_layer_block(sout_ref, layer)

        def fin(bb, carry):
            for hd in range(HG_HEADS):
                out[bb, hd] = st_scr[bb, hd] if kv_major else st_scr[bb, hd].T
            return carry

        lax.fori_loop(0, nbb, fin, 0)


def _hgrn_kernel_inplace(q_ref, f_ref, v_ref, g_ref, s0_ref, lbraw_ref, ng_ref, cum_ref, sbuf_ref, *rest, **kw):
    _hgrn_kernel(q_ref, f_ref, v_ref, g_ref, s0_ref, lbraw_ref, ng_ref, cum_ref, *rest, **kw)


def _chunk_cum_matrix(rows, c):
    i = jnp.arange(rows)[:, None]
    j = jnp.arange(rows)[None, :]
    same = i // c == j // c
    return jnp.concatenate([same & (i >= j), same], axis=0).astype(BF16)


def _hgrn(proj3, s0, sbuf, lbraw, ng, *, nbb, tt_len, chunk, s_valid, layer, ub):
    nb, s_len, _ = proj3.shape
    blk = OFF_HG // BRANCH_W
    rows = nbb * tt_len
    kern = functools.partial(_hgrn_kernel if layer == 0 else _hgrn_kernel_inplace, nbb=nbb, tt_len=tt_len,
                             chunk=chunk, s_valid=s_valid, s_len=s_len, layer=layer, ub=ub)
    seq = lambda k: pl.BlockSpec((nbb, tt_len, BRANCH_W), lambda b, t, k=k: (b, t, blk + k))
    st_in = pl.BlockSpec((None, nbb, HG_HEADS, HG_D, HG_D), lambda b, t: (layer, b, 0, 0, 0))
    st_out, extra_specs, extra_args = _state_out(s0, sbuf, layer, nbb, (HG_HEADS, HG_D, HG_D))
    return pl.pallas_call(
        kern,
        grid=(nb // nbb, s_len // tt_len),
        in_specs=[seq(0), seq(1), seq(2), seq(3), st_in,
                  pl.BlockSpec(lbraw.shape, lambda b, t: (0, 0)),
                  pl.BlockSpec((1, BRANCH_W), lambda b, t: (0, 0)),
                  pl.BlockSpec((2 * tt_len, tt_len), lambda b, t: (0, 0))] + extra_specs,
        out_specs=[pl.BlockSpec((nbb, tt_len, BRANCH_W), lambda b, t: (b, t, 0)), st_out],
        out_shape=[jax.ShapeDtypeStruct((nb, s_len, BRANCH_W), proj3.dtype),
                   jax.ShapeDtypeStruct(s0.shape, F32)],
        input_output_aliases={8: 1} if extra_args else {},
        scratch_shapes=[pltpu.VMEM((nbb, HG_HEADS, HG_D, HG_D), F32)]
        + [pltpu.VMEM((rows, BRANCH_W), F32)] * 8,
        compiler_params=_params(("parallel", "arbitrary")),
        name="hgrn2",
    )(proj3, proj3, proj3, proj3, s0, lbraw, ng, _chunk_cum_matrix(tt_len, chunk), *extra_args)


def _rwkv_kernel(c_ref, sh0_ref, s0_ref, mu_ref, wl_ref, vec_ref, cum_ref, y_ref, sout_ref,
                 sbd_scr, carry_scr, at_scr, rt_scr, bt_scr, kt_scr, bp_scr, kp_scr, v_scr, pc_scr,
                 bonus_scr, gate_scr, o_scr, tcat_scr, aak_scr, arb_scr, ark_scr,
                 *, nbb, tt_len, chunk, s_valid, s_len, layer, ua, ub):
    tt = pl.program_id(1)
    n_t = pl.num_programs(1)
    c = chunk
    c4 = RW_GROUP * c
    n_groups = RW_HEADS // RW_GROUP
    rows = nbb * tt_len
    n_cb = tt_len // c
    n_ch = rows // c

    vec = vec_ref[...]
    w0, a0, k_k, k_a, r_k, ln_g, ln_b = [vec[i:i + 1] for i in range(7)]
    gi_ = lax.broadcasted_iota(jnp.int32, (RW_GW, RW_GW), 0)
    gj_ = lax.broadcasted_iota(jnp.int32, (RW_GW, RW_GW), 1)
    bd_state = gi_ // RW_HD == gj_ // RW_HD

    @pl.when(tt == 0)
    def _():
        carry_scr[...] = sh0_ref[...]

        def init(bb, carry):
            zero = jnp.zeros((RW_HD, RW_HD), F32)
            for g in range(n_groups):
                blocks = [jnp.concatenate([s0_ref[bb, g * RW_GROUP + h] if j == h else zero for j in range(RW_GROUP)],
                                          axis=1) for h in range(RW_GROUP)]
                sbd_scr[bb, g] = jnp.concatenate(blocks, axis=0)
            return carry

        lax.fori_loop(0, nbb, init, 0)

    cc3 = c_ref[:, :, :RW_COLS].astype(F32)
    t3 = lax.broadcasted_iota(jnp.int32, (1, tt_len, 1), 1)
    prev3 = jnp.where(t3 == 0, carry_scr[...], pltpu.roll(cc3, 1, 1))
    carry_scr[...] = cc3[:, tt_len - 1:tt_len, :]
    xm = (cc3 + (prev3 - cc3) * mu_ref[...][None]).reshape(rows, RW_COLS)
    r = xm[:, 0:BRANCH_W]
    k = xm[:, BRANCH_W:2 * BRANCH_W]
    v = xm[:, 2 * BRANCH_W:3 * BRANCH_W]
    lo = xm[:, 3 * BRANCH_W:]
    lane_l = lax.broadcasted_iota(jnp.int32, (1, RW_LORA), 1)
    act = jnp.where(lane_l < RW_LORA_W, jnp.tanh(lo),
                    jnp.where(lane_l < RW_LORA_W + RW_LORA_A, lo, _sigmoid_t(lo)))
    z = _dot(act, wl_ref[...])
    ld = (-math.exp(-0.5)) * _sigmoid_t(w0 + z[:, 0:BRANCH_W])
    a = _sigmoid_t(a0 + z[:, BRANCH_W:2 * BRANCH_W])
    kk = k * k_k
    kbar = k * (1.0 + (a - 1.0) * k_a)
    sums = _seg_sum(jnp.concatenate([kk * kk, r * kbar * r_k], axis=0), RW_HD)
    kap = kk * lax.rsqrt(jnp.maximum(sums[:rows], 1e-24))
    if s_valid < s_len:
        t_in = lax.broadcasted_iota(jnp.int32, (rows, 1), 0) % tt_len
        valid = (tt * tt_len + t_in) < s_valid
        ld = jnp.where(valid, ld, 0.0)
        kap = jnp.where(valid, kap, 0.0)
        kbar = jnp.where(valid, kbar, 0.0)
    lw, ltot = _chunk_sums(cum_ref[...], ld, nbb, tt_len)
    back = ltot - lw
    e_in = jnp.exp(lw)
    e_neg = jnp.exp(-lw)
    e_back = jnp.exp(back)
    at_scr[...] = -kap * jnp.exp(lw - ld)
    rt_scr[...] = r * e_in
    bt_scr[...] = kap * a * e_neg
    kt_scr[...] = kbar * e_neg
    bp_scr[...] = kap * a * e_back
    kp_scr[...] = kbar * e_back
    v_scr[...] = v
    pc_scr[...] = jnp.exp(ltot)
    bonus_scr[...] = sums[rows:]
    gate_scr[...] = z[:, 2 * BRANCH_W:]

    si = lax.broadcasted_iota(jnp.int32, (c4, RW_GW), 0)
    sj = lax.broadcasted_iota(jnp.int32, (c4, RW_GW), 1)
    head_rows = si // c == sj // RW_HD
    qi = lax.broadcasted_iota(jnp.int32, (c4, c4), 0)
    qj = lax.broadcasted_iota(jnp.int32, (c4, c4), 1)
    same = qi // c == qj // c
    strict = jnp.logical_and(same, qi % c > qj % c)
    incl = jnp.logical_and(same, qi % c >= qj % c)
    eye = (qi == qj).astype(F32)
    fuse_sq = c4 % LANES == 0

    def stack(x):
        return jnp.where(head_rows, jnp.concatenate([x] * RW_GROUP, axis=0), 0.0)

    def unstack(x):
        out = x[0:c]
        for h in range(1, RW_GROUP):
            out = out + x[h * c:(h + 1) * c]
        return out

    def phase_a(it, carry):
        chains = [(it * ua + u, g) for u in range(ua) for g in range(n_groups)]
        nmats = []
        for ch, g in chains:
            r0 = pl.multiple_of(ch * c, c)
            ln = slice(g * RW_GW, (g + 1) * RW_GW)
            lhs = jnp.concatenate([stack(at_scr[pl.ds(r0, c), ln]), stack(rt_scr[pl.ds(r0, c), ln])], axis=0)
            rhs = jnp.concatenate([bt_scr[pl.ds(r0, c), ln]] * RW_GROUP + [kt_scr[pl.ds(r0, c), ln]] * RW_GROUP,
                                  axis=0)
            quad = _dot_nt(lhs, rhs)
            nmats.append(jnp.where(strict, quad[:c4, :c4], 0.0))
            aak_scr[ch, g] = unstack(jnp.where(strict, quad[:c4, c4:], 0.0))
            arb_scr[ch, g] = unstack(jnp.where(incl, quad[c4:, :c4], 0.0))
            ark_scr[ch, g] = unstack(jnp.where(incl, quad[c4:, c4:], 0.0))
        tinvs = [eye + n for n in nmats]
        npows = [_dot(n, n) for n in nmats]
        span = 2
        while 2 * span < c:
            if fuse_sq:
                boths = [_dot(p, jnp.concatenate([p, t], axis=1)) for p, t in zip(npows, tinvs)]
                tinvs = [t + bo[:, c4:] for t, bo in zip(tinvs, boths)]
                npows = [bo[:, :c4] for bo in boths]
            else:
                tinvs = [t + _dot(p, t) for p, t in zip(npows, tinvs)]
                npows = [_dot(p, p) for p in npows]
            span *= 2
        tinvs = [t + _dot(p, t) for p, t in zip(npows, tinvs)]
        for (ch, g), t in zip(chains, tinvs):
            tcat_scr[ch, g] = unstack(t)
        return carry

    if n_ch // ua == 1:
        phase_a(0, 0)
    else:
        lax.fori_loop(0, n_ch // ua, phase_a, 0)

    def phase_b(it, carry):
        cb = it // (nbb // ub)
        b0 = (it % (nbb // ub)) * ub
        chains = [(b0 + u, g) for u in range(ub) for g in range(n_groups)]
        ops = []
        for bb, g in chains:
            ch = bb * n_cb + cb
            r0 = pl.multiple_of(ch * c, c)
            ln = slice(g * RW_GW, (g + 1) * RW_GW)
            ops.append(dict(bb=bb, g=g, ch=ch, r0=r0, ln=ln, sbd=sbd_scr[bb, g],
                            v_bd=stack(v_scr[pl.ds(r0, c), ln])))
        sprods = [_dot_nt(jnp.concatenate([at_scr[pl.ds(q["r0"], c), q["ln"]], rt_scr[pl.ds(q["r0"], c), q["ln"]]],
                                          axis=0), q["sbd"]) for q in ops]
        wmats = [sp[:c] + _dot(aak_scr[q["ch"], q["g"]], q["v_bd"]) for q, sp in zip(ops, sprods)]
        us = [_dot(tcat_scr[q["ch"], q["g"]], stack(w)) for q, w in zip(ops, wmats)]
        for q, u, sp in zip(ops, us, sprods):
            r0, ln = q["r0"], q["ln"]
            o_scr[pl.ds(r0, c), ln] = (sp[c:] + _dot(arb_scr[q["ch"], q["g"]], stack(u))
                                       + _dot(ark_scr[q["ch"], q["g"]], q["v_bd"]))
        for q, u in zip(ops, us):
            r0, ln = q["r0"], q["ln"]
            upd = _dot_tn(jnp.concatenate([u, v_scr[pl.ds(r0, c), ln]], axis=0),
                          jnp.concatenate([bp_scr[pl.ds(r0, c), ln], kp_scr[pl.ds(r0, c), ln]], axis=0))
            sbd_scr[q["bb"], q["g"]] = q["sbd"] * pc_scr[pl.ds(r0, 1), ln] + jnp.where(bd_state, upd, 0.0)
        return carry

    n_it = n_cb * (nbb // ub)
    if n_it == 1:
        phase_b(0, 0)
    else:
        lax.fori_loop(0, n_it, phase_b, 0)

    o = o_scr[...]
    inv_n = 1.0 / RW_HD
    mean = _seg_sum(o, RW_HD) * inv_n
    cen = o - mean
    var = _seg_sum(cen * cen, RW_HD) * inv_n
    on = cen * lax.rsqrt(var + RW_GN_EPS) * ln_g + ln_b
    y = (on + bonus_scr[...] * v_scr[...]) * gate_scr[...]
    y_ref[...] = y.reshape(nbb, tt_len, BRANCH_W).astype(y_ref.dtype)

    @pl.when(tt == n_t - 1)
    def _():
        out = _own_layer_block(sout_ref, layer)

        def fin(bb, carry):
            for g in range(n_groups):
                sbd = sbd_scr[bb, g]
                for h in range(RW_GROUP):
                    out[bb, g * RW_GROUP + h] = sbd[h * RW_HD:(h + 1) * RW_HD, h * RW_HD:(h + 1) * RW_HD]
            return carry

        lax.fori_loop(0, nbb, fin, 0)


def _rwkv_kernel_inplace(c_ref, sh0_ref, s0_ref, mu_ref, wl_ref, vec_ref, cum_ref, sbuf_ref, *rest, **kw):
    _rwkv_kernel(c_ref, sh0_ref, s0_ref, mu_ref, wl_ref, vec_ref, cum_ref, *rest, **kw)


def _rwkv(proj3, sh0, s0, sbuf, mu, wl, vec, *, nbb, tt_len, chunk, s_valid, layer, ua, ub):
    nb, s_len, _ = proj3.shape
    kern = functools.partial(_rwkv_kernel if layer == 0 else _rwkv_kernel_inplace, nbb=nbb, tt_len=tt_len,
                             chunk=chunk, s_valid=s_valid, s_len=s_len, layer=layer, ua=ua, ub=ub)
    st_in = pl.BlockSpec((None, nbb, RW_HEADS, RW_HD, RW_HD), lambda b, t: (layer, b, 0, 0, 0))
    st_out, extra_specs, extra_args = _state_out(s0, sbuf, layer, nbb, (RW_HEADS, RW_HD, RW_HD))
    full2 = lambda b, t: (0, 0)
    rows = nbb * tt_len
    n_groups = RW_HEADS // RW_GROUP
    mats = pltpu.VMEM((rows // chunk, n_groups, chunk, RW_GROUP * chunk), F32)
    return pl.pallas_call(
        kern,
        grid=(nb // nbb, s_len // tt_len),
        in_specs=[
            pl.BlockSpec((nbb, tt_len, RW_BLOCK), lambda b, t: (b, t, OFF_RW // RW_BLOCK)),
            pl.BlockSpec((nbb, 1, RW_COLS), lambda b, t: (b, 0, 0)),
            st_in,
            pl.BlockSpec((1, RW_COLS), full2),
            pl.BlockSpec((RW_LORA, 3 * BRANCH_W), full2),
            pl.BlockSpec((SUBLANES, BRANCH_W), full2),
            pl.BlockSpec((2 * tt_len, tt_len), full2),
        ] + extra_specs,
        out_specs=[pl.BlockSpec((nbb, tt_len, BRANCH_W), lambda b, t: (b, t, 0)), st_out],
        out_shape=[jax.ShapeDtypeStruct((nb, s_len, BRANCH_W), proj3.dtype),
                   jax.ShapeDtypeStruct(s0.shape, F32)],
        input_output_aliases={7: 1} if extra_args else {},
        scratch_shapes=[
            pltpu.VMEM((nbb, n_groups, RW_GW, RW_GW), F32),
            pltpu.VMEM((nbb, 1, RW_COLS), F32),
        ] + [pltpu.VMEM((rows, BRANCH_W), F32)] * 11 + [mats] * 4,
        compiler_params=_params(("parallel", "arbitrary")),
        name="rwkv7",
    )(proj3, sh0, s0, mu, wl, vec, _chunk_cum_matrix(tt_len, chunk), *extra_args)


def _mix_kernel(x_ref, gts_ref, ya_ref, yb_ref, yc_ref, wb_ref, wo_ref, g_ref, o_ref):
    acc = None
    for n, y_ref in enumerate((ya_ref, yb_ref, yc_ref)):
        up = _dot(y_ref[...], wb_ref[n * BRANCH_W:(n + 1) * BRANCH_W, :])
        term = _sigmoid_t(gts_ref[:, n * D_MODEL:(n + 1) * D_MODEL].astype(F32)) * up
        acc = term if acc is None else acc + term
    mix = _dot(acc, wo_ref[...])
    o_ref[...] = x_ref[...] + _rms(mix, g_ref[...])


def _mix(x, proj, ya, yb, yc, wb, wo, g, tm, layer):
    t = x.shape[0]
    row = lambda w: pl.BlockSpec((tm, w), lambda i: (i, 0))
    return pl.pallas_call(
        _mix_kernel,
        grid=(t // tm,),
        in_specs=[row(D_MODEL), row(N_BRANCH * D_MODEL), row(BRANCH_W), row(BRANCH_W), row(BRANCH_W),
                  pl.BlockSpec((None, N_BRANCH * BRANCH_W, D_MODEL), lambda i: (layer, 0, 0)),
                  pl.BlockSpec((None, D_MODEL, D_MODEL), lambda i: (layer, 0, 0)),
                  pl.BlockSpec((1, D_MODEL), lambda i: (0, 0))],
        out_specs=row(D_MODEL),
        out_shape=jax.ShapeDtypeStruct((t, D_MODEL), F32),
        compiler_params=_params(("parallel",)),
        name="branch_mix",
    )(x, proj, ya, yb, yc, wb, wo, g)


def _ffn_kernel(x_ref, p_ref, gpre_ref, wg_ref, wu_ref, wd_ref, gpost_ref, wple_ref, wpg_ref, gple_ref,
                o_ref, h_scr, acc_scr):
    j = pl.program_id(1)

    @pl.when(j == 0)
    def _():
        h_scr[...] = _rms(x_ref[...], gpre_ref[...]).astype(BF16)
        acc_scr[...] = jnp.zeros_like(acc_scr)

    h = h_scr[...]
    gt = jnp.dot(h, wg_ref[...], preferred_element_type=F32)
    up = jnp.dot(h, wu_ref[...], preferred_element_type=F32)
    acc_scr[...] += _dot(gt * _sigmoid_t(gt) * up, wd_ref[...])

    @pl.when(j == pl.num_programs(1) - 1)
    def _():
        x2 = x_ref[...] + _rms(acc_scr[...], gpost_ref[...])
        ple = _dot(p_ref[...], wple_ref[...]) * _sigmoid_t(_dot(x2, wpg_ref[...]))
        o_ref[...] = x2 + _rms(ple, gple_ref[...])


def _ffn(x, p, gpre, wg, wu, wd, gpost, wple, wpg, gple, tm, n_split, layer):
    t = x.shape[0]
    d_ff = wg.shape[2]
    tf = d_ff // n_split
    vecspec = pl.BlockSpec((1, D_MODEL), lambda i, j: (0, 0))
    return pl.pallas_call(
        _ffn_kernel,
        grid=(t // tm, n_split),
        in_specs=[
            pl.BlockSpec((tm, D_MODEL), lambda i, j: (i, 0)),
            pl.BlockSpec((None, tm, PLE_DIM), lambda i, j: (layer, i, 0)),
            vecspec,
            pl.BlockSpec((None, D_MODEL, tf), lambda i, j: (layer, 0, j)),
            pl.BlockSpec((None, D_MODEL, tf), lambda i, j: (layer, 0, j)),
            pl.BlockSpec((None, tf, D_MODEL), lambda i, j: (layer, j, 0)),
            vecspec,
            pl.BlockSpec((None, PLE_DIM, D_MODEL), lambda i, j: (layer, 0, 0)),
            pl.BlockSpec((None, D_MODEL, D_MODEL), lambda i, j: (layer, 0, 0)),
            vecspec,
        ],
        out_specs=pl.BlockSpec((tm, D_MODEL), lambda i, j: (i, 0)),
        out_shape=jax.ShapeDtypeStruct((t, D_MODEL), F32),
        scratch_shapes=[pltpu.VMEM((tm, D_MODEL), BF16), pltpu.VMEM((tm, D_MODEL), F32)],
        compiler_params=_params(("parallel", "arbitrary")),
        name="ffn_ple",
    )(x, p, gpre, wg, wu, wd, gpost, wple, wpg, gple)


def _cast_kernel(x_ref, o_ref):
    o_ref[...] = x_ref[...].astype(o_ref.dtype)


def _to_bf16(w, tr):
    d, r, c = w.shape
    spec = pl.BlockSpec((None, tr, c), lambda l, i: (l, i, 0))
    return pl.pallas_call(
        _cast_kernel, grid=(d, r // tr), in_specs=[spec], out_specs=spec,
        out_shape=jax.ShapeDtypeStruct(w.shape, BF16),
        compiler_params=_params(("parallel", "parallel")), name="to_bf16",
    )(w)


W_IN_TILE = 1024


def _w_in_kernel(x_ref, o_ref):
    is_pad = lax.broadcasted_iota(jnp.int32, o_ref.shape, 1) + pl.program_id(1) * W_IN_TILE >= OFF_RW + RW_COLS
    o_ref[...] = jnp.where(is_pad, 0.0, x_ref[0]).astype(o_ref.dtype)


def _w_in_layout(w_in):
    d, _, in_cols = w_in.shape
    n_gate_cols = N_BRANCH * D_MODEL
    n_rest = in_cols - n_gate_cols

    def src(l, j):
        col = j * W_IN_TILE
        start = jnp.where(col < n_gate_cols, col + n_rest, col - n_gate_cols)
        start = jnp.minimum(start, in_cols - W_IN_TILE)
        return l, 0, pl.multiple_of(start, MXU_DIM)

    return pl.pallas_call(
        _w_in_kernel, grid=(d, PROJ_COLS // W_IN_TILE),
        in_specs=[pl.BlockSpec((pl.Element(1), pl.Element(D_MODEL), pl.Element(W_IN_TILE)), src)],
        out_specs=pl.BlockSpec((None, D_MODEL, W_IN_TILE), lambda l, j: (l, 0, j)),
        out_shape=jax.ShapeDtypeStruct((d, D_MODEL, PROJ_COLS), BF16),
        compiler_params=_params(("parallel", "parallel")), name="w_in_layout",
    )(w_in)


def _block_diag(w):
    n, r, c = w.shape
    eye = jnp.eye(n, dtype=w.dtype)
    return (eye[:, None, :, None] * w[:, :, None, :]).reshape(n * r, n * c)


def _prep_layer(i, W):
    lora = jnp.zeros((RW_LORA, 3 * BRANCH_W), F32)
    lora = lora.at[0:RW_LORA_W, 0:BRANCH_W].set(W["rw_w_up"][i])
    lora = lora.at[RW_LORA_W:RW_LORA_W + RW_LORA_A, BRANCH_W:2 * BRANCH_W].set(W["rw_a_up"][i])
    lora = lora.at[RW_LORA_W + RW_LORA_A:, 2 * BRANCH_W:].set(W["rw_g_up"][i])
    vec = jnp.stack([W["rw_w0"][i], W["rw_a0"][i], W["rw_k_k"][i], W["rw_k_a"][i],
                     W["rw_r_k"][i].reshape(BRANCH_W), W["rw_ln_g"][i], W["rw_ln_b"][i],
                     jnp.zeros((BRANCH_W,), F32)])
    row = lambda name: W[name][i].reshape(1, -1)
    return dict(
        norm_pre_mix=row("norm_pre_mix"),
        conv_w=W["conv_w"][i], conv_b=row("conv_b"),
        lru_w=jnp.concatenate([_block_diag(W["lru_wa"][i]), _block_diag(W["lru_wx"][i])], axis=1).astype(BF16),
        lru_b=jnp.stack([W["lru_ba"][i], W["lru_bx"][i]]), lru_lambda=row("lru_lambda"),
        hg_norm_g=row("hg_norm_g"),
        rw_mu=row("rw_mu"), rw_lora=lora.astype(BF16), rw_vec=vec,
        norm_post_mix=row("norm_post_mix"), norm_pre_ffn=row("norm_pre_ffn"),
        norm_post_ffn=row("norm_post_ffn"), norm_ple=row("norm_ple"),
    )


def _tiles(nb, s_len):
    t = nb * s_len
    tm_in = min(t, 1024)
    tm_tok = min(t, 512)
    if s_len >= 512:
        return dict(tm_in=tm_in, tn_in=2048, proj_dtype=BF16, tm_mix=tm_tok, tm_ffn=tm_tok, ffn_split=2,
                    lru_tt=128,
                    hg=dict(nbb=2, tt_len=256, chunk=64, ub=2),
                    rw=dict(nbb=8, tt_len=64, chunk=64, ua=4, ub=8))
    return dict(tm_in=tm_in, tn_in=2048, proj_dtype=F32, tm_mix=tm_tok, tm_ffn=tm_tok, ffn_split=2,
                lru_tt=s_len,
                hg=dict(nbb=min(nb, 16), tt_len=s_len, chunk=s_len, ub=4),
                rw=dict(nbb=min(nb, 16), tt_len=s_len, chunk=s_len, ua=8, ub=min(nb, 16)))


def _run_trunk(x3, p4, states, layers, big, lbraw, *, s_valid, pos0_is_zero):
    nb, s_len, _ = x3.shape
    t = nb * s_len
    plan = _tiles(nb, s_len)
    for mix in (plan["hg"], plan["rw"]):
        assert nb % mix["nbb"] == 0 and s_len % mix["tt_len"] == 0 and mix["tt_len"] % mix["chunk"] == 0, mix
    assert t % plan["tm_in"] == 0 and t % plan["tm_mix"] == 0 and t % plan["tm_ffn"] == 0 and s_len % plan["lru_tt"] == 0
    conv0, lru0, hg0, rw0, sh0 = states
    x = x3.reshape(t, D_MODEL)
    new = ([], [], [])
    nhg = nrw = None
    for i, L in enumerate(layers):
        proj = _in_proj(x, L["norm_pre_mix"], big["w_in"], plan["tm_in"], plan["tn_in"], plan["proj_dtype"], i)
        proj3 = proj.reshape(nb, s_len, PROJ_COLS)
        prev8 = jnp.pad(conv0[i], ((0, 0), (SUBLANES - (CONV_W - 1), 0), (0, 0)))
        ya, nlru = _lru(proj3, prev8, lru0[i], L["conv_w"], L["conv_b"], L["lru_w"], L["lru_b"], L["lru_lambda"],
                        tt_len=plan["lru_tt"], s_valid=s_valid, pos0_is_zero=pos0_is_zero)
        yb, nhg = _hgrn(proj3, hg0, nhg, lbraw, L["hg_norm_g"], s_valid=s_valid, layer=i, **plan["hg"])
        yc, nrw = _rwkv(proj3, sh0[i].reshape(nb, 1, RW_COLS), rw0, nrw, L["rw_mu"], L["rw_lora"], L["rw_vec"],
                        s_valid=s_valid, layer=i, **plan["rw"])
        x = _mix(x, proj, ya.reshape(t, BRANCH_W), yb.reshape(t, BRANCH_W), yc.reshape(t, BRANCH_W),
                 big["w_branch"], big["w_out"], L["norm_post_mix"], plan["tm_mix"], i)
        x = _ffn(x, p4.reshape(-1, t, PLE_DIM), L["norm_pre_ffn"], big["w_ffn_gate"], big["w_ffn_up"],
                 big["w_ffn_down"], L["norm_post_ffn"], big["w_ple"], big["w_ple_gate"], L["norm_ple"], plan["tm_ffn"], plan["ffn_split"], i)
        nconv = proj3[:, s_valid - (CONV_W - 1):s_valid, OFF_LRU:OFF_LRU + BRANCH_W].astype(F32)
        nsh = proj3[:, s_valid - 1, OFF_RW:OFF_RW + RW_COLS].astype(F32)
        for lst, val in zip(new, (nconv, nlru, nsh)):
            lst.append(val)
    nconv, nlru, nsh = (jnp.stack(l) for l in new)
    return x.reshape(nb, s_len, D_MODEL), (nconv, nlru, nhg, nrw, nsh)


def kernel(x_prompt, x_sample, p_prompt, p_sample, state_conv_a, state_lru_a, state_hgrn, state_rwkv, state_shift_c, norm_pre_mix, w_in, conv_w, conv_b, lru_wa, lru_ba, lru_wx, lru_bx, lru_lambda, hg_lower_bounds, hg_norm_g, rw_mu, rw_w0, rw_w_up, rw_a0, rw_a_up, rw_g_up, rw_k_k, rw_k_a, rw_r_k, rw_ln_g, rw_ln_b, w_branch, w_out, norm_post_mix, norm_pre_ffn, w_ffn_gate, w_ffn_up, w_ffn_down, norm_post_ffn, w_ple, w_ple_gate, norm_ple):
    W = dict(norm_pre_mix=norm_pre_mix, w_in=w_in, conv_w=conv_w, conv_b=conv_b, lru_wa=lru_wa, lru_ba=lru_ba,
             lru_wx=lru_wx, lru_bx=lru_bx, lru_lambda=lru_lambda, hg_norm_g=hg_norm_g, rw_mu=rw_mu, rw_w0=rw_w0,
             rw_w_up=rw_w_up, rw_a0=rw_a0, rw_a_up=rw_a_up, rw_g_up=rw_g_up, rw_k_k=rw_k_k, rw_k_a=rw_k_a,
             rw_r_k=rw_r_k, rw_ln_g=rw_ln_g, rw_ln_b=rw_ln_b, w_branch=w_branch, w_out=w_out,
             norm_post_mix=norm_post_mix, norm_pre_ffn=norm_pre_ffn, w_ffn_gate=w_ffn_gate, w_ffn_up=w_ffn_up,
             w_ffn_down=w_ffn_down, norm_post_ffn=norm_post_ffn, w_ple=w_ple, w_ple_gate=w_ple_gate, norm_ple=norm_ple)
    depth = w_in.shape[0]
    d_ff = w_ffn_gate.shape[2]
    big = dict(
        w_in=_w_in_layout(w_in),
        w_branch=_to_bf16(w_branch.reshape(depth, N_BRANCH * BRANCH_W, D_MODEL), N_BRANCH * BRANCH_W // 2),
        w_out=_to_bf16(w_out, D_MODEL),
        w_ffn_gate=_to_bf16(w_ffn_gate, D_MODEL // 2), w_ffn_up=_to_bf16(w_ffn_up, D_MODEL // 2),
        w_ffn_down=_to_bf16(w_ffn_down, d_ff // 2),
        w_ple=_to_bf16(w_ple, PLE_DIM), w_ple_gate=_to_bf16(w_ple_gate, D_MODEL),
    )
    layers = [_prep_layer(i, W) for i in range(depth)]
    lbraw = hg_lower_bounds.astype(F32)

    bp, sp, _ = x_prompt.shape
    zeros = lambda *shape: jnp.zeros((depth, bp) + shape, F32)
    zero_states = (zeros(CONV_W - 1, BRANCH_W), zeros(BRANCH_W), zeros(HG_HEADS, HG_D, HG_D),
                   zeros(RW_HEADS, RW_HD, RW_HD), zeros(RW_COLS))
    y_prompt, st_p = _run_trunk(x_prompt, p_prompt, zero_states, layers, big, lbraw, s_valid=sp, pos0_is_zero=True)

    _, ss, _ = x_sample.shape
    ss_pad = -(-ss // SUBLANES) * SUBLANES
    xs = jnp.pad(x_sample, ((0, 0), (0, ss_pad - ss), (0, 0)))
    ps = jnp.pad(p_sample, ((0, 0), (0, 0), (0, ss_pad - ss), (0, 0)))
    y_sample, st_s = _run_trunk(xs, ps, (state_conv_a, state_lru_a, state_hgrn, state_rwkv, state_shift_c),
                                layers, big, lbraw, s_valid=ss, pos0_is_zero=False)
    return (y_prompt, y_sample[:, :ss]) + st_p + st_s
```

```python
import functools
import math

import jax
import jax.numpy as jnp
from jax import lax
from jax.experimental import pallas as pl
from jax.experimental.pallas import tpu as pltpu

F32 = jnp.float32
BF16 = jnp.bfloat16

D_MODEL = 1024
BRANCH_W = 512
N_BRANCH = 3
CONV_W = 4
LRU_C = 8.0
HG_HEADS = 4
HG_D = BRANCH_W // HG_HEADS
HG_F_MIN = 1e-20
RW_HD = 64
RW_HEADS = BRANCH_W // RW_HD
RW_LORA_W = 64
RW_LORA_A = 64
RW_LORA_G = 128
RW_LORA = RW_LORA_W + RW_LORA_A + RW_LORA_G
RW_GN_EPS = 64e-5
RW_COLS = 3 * BRANCH_W + RW_LORA
PLE_DIM = 256
EPS = 1e-6

SUBLANES = 8
LANES = 128
MXU_DIM = 256
VMEM_LIMIT = 56 * 1024 * 1024

PROJ_COLS = 8192
OFF_GATES = 0
OFF_LRU = N_BRANCH * D_MODEL
OFF_HG = OFF_LRU + 2 * BRANCH_W
OFF_RW = OFF_HG + 4 * BRANCH_W
RW_BLOCK = PROJ_COLS - OFF_RW

RW_GROUP = LANES // RW_HD
RW_GW = RW_GROUP * RW_HD


def _params(sem):
    return pltpu.CompilerParams(dimension_semantics=sem, vmem_limit_bytes=VMEM_LIMIT)


def _rms(x, g):
    return x * lax.rsqrt(jnp.mean(x * x, axis=-1, keepdims=True) + EPS) * g


def _sigmoid(x):
    return 1.0 / (1.0 + jnp.exp(-x))


def _sigmoid_t(x):
    return 0.5 * jnp.tanh(0.5 * x) + 0.5


def _softplus(x):
    return jnp.maximum(x, 0.0) + jnp.log1p(jnp.exp(-jnp.abs(x)))


def _dot(a, b):
    return jnp.dot(a.astype(BF16), b.astype(BF16), preferred_element_type=F32)


def _dot_nt(a, b):
    return lax.dot_general(a.astype(BF16), b.astype(BF16), (((1,), (1,)), ((), ())),
                           preferred_element_type=F32)


def _dot_tn(a, b):
    return lax.dot_general(a.astype(BF16), b.astype(BF16), (((0,), (0,)), ((), ())),
                           preferred_element_type=F32)


def _split2(x):
    hi = x.astype(BF16)
    return hi, (x - hi.astype(F32)).astype(BF16)


def _seg_sum(x, seg):
    i = lax.broadcasted_iota(jnp.int32, (MXU_DIM, MXU_DIM), 0)
    j = lax.broadcasted_iota(jnp.int32, (MXU_DIM, MXU_DIM), 1)
    ones = (i // seg == j // seg).astype(BF16)
    xb = x.astype(BF16)
    tiles = [jnp.dot(xb[:, l:l + MXU_DIM], ones, preferred_element_type=F32)
             for l in range(0, x.shape[1], MXU_DIM)]
    return jnp.concatenate(tiles, axis=1)


def _chunk_sums(cum, x, nbb, tt_len):
    hi, lo = _split2(x)
    m = cum.astype(BF16)
    pre, tot = [], []
    for b in range(nbb):
        rs = slice(b * tt_len, (b + 1) * tt_len)
        both = jnp.dot(m, lo[rs], preferred_element_type=F32) + jnp.dot(m, hi[rs], preferred_element_type=F32)
        pre.append(both[:tt_len])
        tot.append(both[tt_len:])
    if nbb == 1:
        return pre[0], tot[0]
    return jnp.concatenate(pre, axis=0), jnp.concatenate(tot, axis=0)


def _in_proj_kernel(x_ref, g_ref, w_ref, o_ref, h_scr):
    @pl.when(pl.program_id(1) == 0)
    def _():
        h_scr[...] = _rms(x_ref[...], g_ref[...]).astype(BF16)

    o_ref[...] = jnp.dot(h_scr[...], w_ref[...], preferred_element_type=F32).astype(o_ref.dtype)


def _in_proj(x, g, w, tm, tn, out_dtype, layer):
    t = x.shape[0]
    return pl.pallas_call(
        _in_proj_kernel,
        grid=(t // tm, PROJ_COLS // tn),
        in_specs=[
            pl.BlockSpec((tm, D_MODEL), lambda i, j: (i, 0)),
            pl.BlockSpec((1, D_MODEL), lambda i, j: (0, 0)),
            pl.BlockSpec((None, D_MODEL, tn), lambda i, j: (layer, 0, j)),
        ],
        out_specs=pl.BlockSpec((tm, tn), lambda i, j: (i, j)),
        out_shape=jax.ShapeDtypeStruct((t, PROJ_COLS), out_dtype),
        scratch_shapes=[pltpu.VMEM((tm, D_MODEL), BF16)],
        compiler_params=_params(("parallel", "arbitrary")),
        name="in_proj",
    )(x, g, w)


def _lru_kernel(xa_ref, ga_ref, prev8_ref, h0_ref, cw_ref, cb_ref, w_ref, bab_ref, lam_ref,
                y_ref, hout_ref, prev_scr, h_scr, *, nb, tt_len, s_valid, s_len, pos0_is_zero):
    tt = pl.program_id(0)
    rows = nb * tt_len

    @pl.when(tt == 0)
    def _():
        prev_scr[...] = prev8_ref[...]
        h_scr[...] = h0_ref[...]

    xa = xa_ref[...].astype(F32)
    n8 = tt_len // SUBLANES
    xa4 = xa.reshape(nb, n8, SUBLANES, BRANCH_W)
    prev4 = prev_scr[...][:, None]
    t8 = lax.broadcasted_iota(jnp.int32, (1, 1, SUBLANES, 1), 2)
    cw = cw_ref[...]
    xc4 = cb_ref[...][None, None] + cw[CONV_W - 1][None, None, None] * xa4
    for j in range(1, CONV_W):
        rot = pltpu.roll(xa4, j, 2)
        rot_before = pltpu.roll(prev4, j, 2)
        if n8 > 1:
            rot_before = jnp.concatenate([rot_before, rot[:, :n8 - 1]], axis=1)
        xc4 = xc4 + cw[CONV_W - 1 - j][None, None, None] * jnp.where(t8 < j, rot_before, rot)
    prev_scr[...] = xa[:, tt_len - SUBLANES:, :]

    xc2 = xc4.reshape(rows, BRANCH_W)
    z = _dot(xc2, w_ref[...])
    bab = bab_ref[...]
    r = _sigmoid_t(z[:, :BRANCH_W] + bab[0:1])
    i = _sigmoid_t(z[:, BRANCH_W:] + bab[1:2])
    log_a = (-LRU_C) * r * _softplus(-lam_ref[...])
    a = jnp.exp(log_a)
    m2 = jnp.maximum(1.0 - a * a, 0.0)
    mult = m2 * lax.rsqrt(jnp.maximum(m2, 1e-30))
    t_in = lax.broadcasted_iota(jnp.int32, (rows, 1), 0) % tt_len
    if pos0_is_zero:
        mult = jnp.where(jnp.logical_and(tt == 0, t_in == 0), 1.0, mult)
    b = xc2 * i * mult
    if s_valid < s_len:
        valid = (tt * tt_len + t_in) < s_valid
        a = jnp.where(valid, a, 1.0)
        b = jnp.where(valid, b, 0.0)
    a4 = a.reshape(nb, n8, SUBLANES, BRANCH_W)
    b4 = b.reshape(nb, n8, SUBLANES, BRANCH_W)

    d = 1
    while d < SUBLANES:
        keep = t8 >= d
        b4 = a4 * jnp.where(keep, pltpu.roll(b4, d, 2), 0.0) + b4
        a4 = a4 * jnp.where(keep, pltpu.roll(a4, d, 2), 1.0)
        d *= 2
    h_in = h_scr[...][:, None, :]
    blocks = []
    for blk in range(n8):
        hb = a4[:, blk] * h_in + b4[:, blk]
        blocks.append(hb)
        h_in = hb[:, SUBLANES - 1:SUBLANES, :]
    hh = blocks[0] if n8 == 1 else jnp.concatenate(blocks, axis=1)
    h = h_in.reshape(nb, BRANCH_W)
    h_scr[...] = h
    hout_ref[...] = h

    ga = ga_ref[...].astype(F32)
    gelu = 0.5 * ga * (1.0 + jnp.tanh(math.sqrt(2.0 / math.pi) * (ga + 0.044715 * ga * ga * ga)))
    y_ref[...] = (hh * gelu).astype(y_ref.dtype)


def _lru(proj3, prev8, h0, cw, cb, w, bab, lam, *, tt_len, s_valid, pos0_is_zero):
    nb, s_len, _ = proj3.shape
    blk = OFF_LRU // BRANCH_W
    kern = functools.partial(_lru_kernel, nb=nb, tt_len=tt_len, s_valid=s_valid, s_len=s_len,
                             pos0_is_zero=pos0_is_zero)
    full2 = lambda t: (0, 0)
    return pl.pallas_call(
        kern,
        grid=(s_len // tt_len,),
        in_specs=[
            pl.BlockSpec((nb, tt_len, BRANCH_W), lambda t: (0, t, blk)),
            pl.BlockSpec((nb, tt_len, BRANCH_W), lambda t: (0, t, blk + 1)),
            pl.BlockSpec((nb, SUBLANES, BRANCH_W), lambda t: (0, 0, 0)),
            pl.BlockSpec((nb, BRANCH_W), full2),
            pl.BlockSpec((CONV_W, BRANCH_W), full2),
            pl.BlockSpec((1, BRANCH_W), full2),
            pl.BlockSpec((BRANCH_W, 2 * BRANCH_W), full2),
            pl.BlockSpec((2, BRANCH_W), full2),
            pl.BlockSpec((1, BRANCH_W), full2),
        ],
        out_specs=[
            pl.BlockSpec((nb, tt_len, BRANCH_W), lambda t: (0, t, 0)),
            pl.BlockSpec((nb, BRANCH_W), full2),
        ],
        out_shape=[
            jax.ShapeDtypeStruct((nb, s_len, BRANCH_W), proj3.dtype),
            jax.ShapeDtypeStruct((nb, BRANCH_W), F32),
        ],
        scratch_shapes=[
            pltpu.VMEM((nb, SUBLANES, BRANCH_W), F32),
            pltpu.VMEM((nb, BRANCH_W), F32),
        ],
        compiler_params=_params(("arbitrary",)),
        name="rglru",
    )(proj3, proj3, prev8, h0, cw, cb, w, bab, lam)


def _own_layer_block(sout_ref, layer):
    if layer > 0:
        return sout_ref
    if sout_ref.shape[0] > 1:
        sout_ref[1:] = jnp.zeros((sout_ref.shape[0] - 1,) + tuple(sout_ref.shape[1:]), sout_ref.dtype)
    return sout_ref.at[0]


def _state_out(s0, sbuf, layer, nbb, tail):
    depth = s0.shape[0]
    zeros = (0,) * len(tail)
    if layer == 0:
        spec = pl.BlockSpec((depth, nbb) + tail, lambda b, t: (0, b) + zeros)
        return spec, [], []
    spec = pl.BlockSpec((None, nbb) + tail, lambda b, t: (layer, b) + zeros)
    return spec, [pl.BlockSpec(memory_space=pl.ANY)], [sbuf]


def _hg_diag_blocks(qh, kh, vh, bch):
    c = qh.shape[0]
    nblk = c // SUBLANES
    q3 = qh.reshape(nblk, SUBLANES, HG_D)
    k3 = kh.reshape(nblk, SUBLANES, HG_D)
    v3 = vh.reshape(nblk, SUBLANES, HG_D)
    b3 = bch.reshape(nblk, SUBLANES, HG_D)
    tin = lax.broadcasted_iota(jnp.int32, (1, SUBLANES, 1), 1)
    o3 = jnp.zeros((nblk, SUBLANES, HG_D), F32)
    for s in range(SUBLANES):
        dec = jnp.exp(jnp.minimum(b3 - b3[:, s:s + 1, :], 0.0))
        w = jnp.sum(q3 * k3[:, s:s + 1, :] * dec, axis=-1, keepdims=True)
        w = jnp.where(tin >= s, w, 0.0)
        o3 = o3 + w * v3[:, s:s + 1, :]
    return o3.reshape(c, HG_D)


def _hg_level_refs(bch, h, c):
    gq, gk = [], []
    zero = jnp.zeros((h, HG_D), F32)
    for j in range(c // h):
        if j % 2 == 1:
            gq.append(jnp.broadcast_to(bch[j * h - 1:j * h, :], (h, HG_D)))
            gk.append(zero)
        else:
            gq.append(zero)
            gk.append(jnp.broadcast_to(bch[(j + 1) * h - 1:(j + 1) * h, :], (h, HG_D)))
    return jnp.concatenate(gq, axis=0), jnp.concatenate(gk, axis=0)


def _hgrn_kernel(q_ref, f_ref, v_ref, g_ref, s0_ref, lbraw_ref, ng_ref, cum_ref, y_ref, sout_ref,
                 st_scr, qs_scr, k_scr, v_scr, bc_scr, qe_scr, kh_scr, et_scr, o_scr,
                 *, nbb, tt_len, chunk, s_valid, s_len, layer, ub):
    tt = pl.program_id(1)
    n_t = pl.num_programs(1)
    c = chunk
    rows = nbb * tt_len
    n_cb = tt_len // c
    kv_major = s_len == c

    raw = lbraw_ref[...]
    ex = jnp.exp(raw - jnp.max(raw, axis=0, keepdims=True))
    sm = ex / jnp.sum(ex, axis=0, keepdims=True)
    lb = jnp.zeros((1, BRANCH_W), F32)
    for l in range(1, layer + 1):
        lb = lb + sm[l:l + 1]

    @pl.when(tt == 0)
    def _():
        def init(bb, carry):
            for hd in range(HG_HEADS):
                st_scr[bb, hd] = s0_ref[bb, hd] if kv_major else s0_ref[bb, hd].T
            return carry

        lax.fori_loop(0, nbb, init, 0)

    q = q_ref[...].astype(F32).reshape(rows, BRANCH_W)
    fp = f_ref[...].astype(F32).reshape(rows, BRANCH_W)
    sg = _sigmoid(fp)
    f = lb + (1.0 - lb) * sg
    k = (1.0 - lb) * (1.0 - sg)
    logf = jnp.log(jnp.maximum(f, HG_F_MIN))
    if s_valid < s_len:
        t_in = lax.broadcasted_iota(jnp.int32, (rows, 1), 0) % tt_len
        valid = (tt * tt_len + t_in) < s_valid
        k = jnp.where(valid, k, 0.0)
        logf = jnp.where(valid, logf, 0.0)
    bc, btot = _chunk_sums(cum_ref[...], logf, nbb, tt_len)
    qs = q * _sigmoid_t(q)
    qs_scr[...] = qs
    k_scr[...] = k
    v_scr[...] = v_ref[...].astype(F32).reshape(rows, BRANCH_W)
    bc_scr[...] = bc
    qe_scr[...] = qs * jnp.exp(bc)
    kh_scr[...] = k * jnp.exp(btot - bc)
    et_scr[...] = jnp.exp(btot)

    ti = lax.broadcasted_iota(jnp.int32, (c, 1), 0)
    ii = lax.broadcasted_iota(jnp.int32, (c, c), 0)
    jj = lax.broadcasted_iota(jnp.int32, (c, c), 1)
    levels = []
    h = c // 2
    while h >= SUBLANES:
        odd = (ti // h) % 2 == 1
        pair = jnp.logical_and(ii // (2 * h) == jj // (2 * h),
                               jnp.logical_and((ii // h) % 2 == 1, (jj // h) % 2 == 0))
        levels.append((h, odd, pair))
        h //= 2

    def step(it, carry):
        cb = it // (nbb // ub)
        b0 = (it % (nbb // ub)) * ub
        chains = []
        for u in range(ub):
            r0 = pl.multiple_of((b0 + u) * tt_len + cb * c, c)
            for hd in range(HG_HEADS):
                ln = slice(hd * HG_D, (hd + 1) * HG_D)
                chains.append(dict(bb=b0 + u, hd=hd, r0=r0, ln=ln, st=st_scr[b0 + u, hd],
                                   qh=qs_scr[pl.ds(r0, c), ln], kh=k_scr[pl.ds(r0, c), ln],
                                   vh=v_scr[pl.ds(r0, c), ln], bch=bc_scr[pl.ds(r0, c), ln]))
        if kv_major:
            outs = [_dot(qe_scr[pl.ds(x["r0"], c), x["ln"]], x["st"]) for x in chains]
        else:
            outs = [_dot_nt(qe_scr[pl.ds(x["r0"], c), x["ln"]], x["st"]) for x in chains]
        if levels:
            amats = []
            for x in chains:
                amat = None
                for (h, odd, pair) in levels:
                    gq, gk = _hg_level_refs(x["bch"], h, c)
                    qt = jnp.where(odd, x["qh"] * jnp.exp(jnp.where(odd, x["bch"] - gq, 0.0)), 0.0)
                    kt = jnp.where(odd, 0.0, x["kh"] * jnp.exp(jnp.where(odd, 0.0, gk - x["bch"])))
                    term = jnp.where(pair, _dot_nt(qt, kt), 0.0)
                    amat = term if amat is None else amat + term
                amats.append(amat)
            outs = [o + _dot(a, x["vh"]) for o, a, x in zip(outs, amats, chains)]
        for o, x in zip(outs, chains):
            o_scr[pl.ds(x["r0"], c), x["ln"]] = o + _hg_diag_blocks(x["qh"], x["kh"], x["vh"], x["bch"])
        for x in chains:
            khat = kh_scr[pl.ds(x["r0"], c), x["ln"]]
            if kv_major:
                last = jnp.where(ti == c - 1, x["bch"], 0.0)
                hi, lo = _split2(last)
                ones = jnp.ones((c, HG_D), BF16)
                col = (lax.dot_general(hi, ones, (((0,), (0,)), ((), ())), preferred_element_type=F32)
                       + lax.dot_general(lo, ones, (((0,), (0,)), ((), ())), preferred_element_type=F32))
                st_scr[x["bb"], x["hd"]] = x["st"] * jnp.exp(col) + _dot_tn(khat, x["vh"])
            else:
                st_scr[x["bb"], x["hd"]] = x["st"] * et_scr[pl.ds(x["r0"], 1), x["ln"]] + _dot_tn(x["vh"], khat)
        return carry

    n_it = n_cb * (nbb // ub)
    if n_it == 1:
        step(0, 0)
    else:
        lax.fori_loop(0, n_it, step, 0)

    g = g_ref[...].astype(F32).reshape(rows, BRANCH_W)
    ng = ng_ref[...]
    outs = []
    for hd in range(HG_HEADS):
        ln = slice(hd * HG_D, (hd + 1) * HG_D)
        o = o_scr[:, ln]
        outs.append(o * lax.rsqrt(jnp.mean(o * o, axis=-1, keepdims=True) + EPS) * ng[:, ln])
    y = jnp.concatenate(outs, axis=-1) * (g * _sigmoid_t(g))
    y_ref[...] = y.reshape(nbb, tt_len, BRANCH_W).astype(y_ref.dtype)

    @pl.when(tt == n_t - 1)
    def _():
        out = _own_layer_block(sout_ref, layer)

        def fin(bb, carry):
            for hd in range(HG_HEADS):
                out[bb, hd] = st_scr[bb, hd] if kv_major else st_scr[bb, hd].T
            return carry

        lax.fori_loop(0, nbb, fin, 0)


def _hgrn_kernel_inplace(q_ref, f_ref, v_ref, g_ref, s0_ref, lbraw_ref, ng_ref, cum_ref, sbuf_ref, *rest, **kw):
    _hgrn_kernel(q_ref, f_ref, v_ref, g_ref, s0_ref, lbraw_ref, ng_ref, cum_ref, *rest, **kw)


def _chunk_cum_matrix(rows, c):
    i = jnp.arange(rows)[:, None]
    j = jnp.arange(rows)[None, :]
    same = i // c == j // c
    return jnp.concatenate([same & (i >= j), same], axis=0).astype(BF16)


def _hgrn(proj3, s0, sbuf, lbraw, ng, *, nbb, tt_len, chunk, s_valid, layer, ub):
    nb, s_len, _ = proj3.shape
    blk = OFF_HG // BRANCH_W
    rows = nbb * tt_len
    kern = functools.partial(_hgrn_kernel if layer == 0 else _hgrn_kernel_inplace, nbb=nbb, tt_len=tt_len,
                             chunk=chunk, s_valid=s_valid, s_len=s_len, layer=layer, ub=ub)
    seq = lambda k: pl.BlockSpec((nbb, tt_len, BRANCH_W), lambda b, t, k=k: (b, t, blk + k))
    st_in = pl.BlockSpec((None, nbb, HG_HEADS, HG_D, HG_D), lambda b, t: (layer, b, 0, 0, 0))
    st_out, extra_specs, extra_args = _state_out(s0, sbuf, layer, nbb, (HG_HEADS, HG_D, HG_D))
    return pl.pallas_call(
        kern,
        grid=(nb // nbb, s_len // tt_len),
        in_specs=[seq(0), seq(1), seq(2), seq(3), st_in,
                  pl.BlockSpec(lbraw.shape, lambda b, t: (0, 0)),
                  pl.BlockSpec((1, BRANCH_W), lambda b, t: (0, 0)),
                  pl.BlockSpec((2 * tt_len, tt_len), lambda b, t: (0, 0))] + extra_specs,
        out_specs=[pl.BlockSpec((nbb, tt_len, BRANCH_W), lambda b, t: (b, t, 0)), st_out],
        out_shape=[jax.ShapeDtypeStruct((nb, s_len, BRANCH_W), proj3.dtype),
                   jax.ShapeDtypeStruct(s0.shape, F32)],
        input_output_aliases={8: 1} if extra_args else {},
        scratch_shapes=[pltpu.VMEM((nbb, HG_HEADS, HG_D, HG_D), F32)]
        + [pltpu.VMEM((rows, BRANCH_W), F32)] * 8,
        compiler_params=_params(("parallel", "arbitrary")),
        name="hgrn2",
    )(proj3, proj3, proj3, proj3, s0, lbraw, ng, _chunk_cum_matrix(tt_len, chunk), *extra_args)


def _rwkv_kernel(c_ref, sh0_ref, s0_ref, mu_ref, wl_ref, vec_ref, cum_ref, y_ref, sout_ref,
                 sbd_scr, carry_scr, at_scr, rt_scr, bt_scr, kt_scr, bp_scr, kp_scr, v_scr, pc_scr,
                 bonus_scr, gate_scr, o_scr, tcat_scr, aak_scr, arb_scr, ark_scr,
                 *, nbb, tt_len, chunk, s_valid, s_len, layer, ua, ub):
    tt = pl.program_id(1)
    n_t = pl.num_programs(1)
    c = chunk
    c4 = RW_GROUP * c
    n_groups = RW_HEADS // RW_GROUP
    rows = nbb * tt_len
    n_cb = tt_len // c
    n_ch = rows // c

    vec = vec_ref[...]
    w0, a0, k_k, k_a, r_k, ln_g, ln_b = [vec[i:i + 1] for i in range(7)]
    gi_ = lax.broadcasted_iota(jnp.int32, (RW_GW, RW_GW), 0)
    gj_ = lax.broadcasted_iota(jnp.int32, (RW_GW, RW_GW), 1)
    bd_state = gi_ // RW_HD == gj_ // RW_HD

    @pl.when(tt == 0)
    def _():
        carry_scr[...] = sh0_ref[...]

        def init(bb, carry):
            zero = jnp.zeros((RW_HD, RW_HD), F32)
            for g in range(n_groups):
                blocks = [jnp.concatenate([s0_ref[bb, g * RW_GROUP + h] if j == h else zero for j in range(RW_GROUP)],
                                          axis=1) for h in range(RW_GROUP)]
                sbd_scr[bb, g] = jnp.concatenate(blocks, axis=0)
            return carry

        lax.fori_loop(0, nbb, init, 0)

    cc3 = c_ref[:, :, :RW_COLS].astype(F32)
    t3 = lax.broadcasted_iota(jnp.int32, (1, tt_len, 1), 1)
    prev3 = jnp.where(t3 == 0, carry_scr[...], pltpu.roll(cc3, 1, 1))
    carry_scr[...] = cc3[:, tt_len - 1:tt_len, :]
    xm = (cc3 + (prev3 - cc3) * mu_ref[...][None]).reshape(rows, RW_COLS)
    r = xm[:, 0:BRANCH_W]
    k = xm[:, BRANCH_W:2 * BRANCH_W]
    v = xm[:, 2 * BRANCH_W:3 * BRANCH_W]
    lo = xm[:, 3 * BRANCH_W:]
    lane_l = lax.broadcasted_iota(jnp.int32, (1, RW_LORA), 1)
    act = jnp.where(lane_l < RW_LORA_W, jnp.tanh(lo),
                    jnp.where(lane_l < RW_LORA_W + RW_LORA_A, lo, _sigmoid_t(lo)))
    z = _dot(act, wl_ref[...])
    ld = (-math.exp(-0.5)) * _sigmoid_t(w0 + z[:, 0:BRANCH_W])
    a = _sigmoid_t(a0 + z[:, BRANCH_W:2 * BRANCH_W])
    kk = k * k_k
    kbar = k * (1.0 + (a - 1.0) * k_a)
    sums = _seg_sum(jnp.concatenate([kk * kk, r * kbar * r_k], axis=0), RW_HD)
    kap = kk * lax.rsqrt(jnp.maximum(sums[:rows], 1e-24))
    if s_valid < s_len:
        t_in = lax.broadcasted_iota(jnp.int32, (rows, 1), 0) % tt_len
        valid = (tt * tt_len + t_in) < s_valid
        ld = jnp.where(valid, ld, 0.0)
        kap = jnp.where(valid, kap, 0.0)
        kbar = jnp.where(valid, kbar, 0.0)
    lw, ltot = _chunk_sums(cum_ref[...], ld, nbb, tt_len)
    back = ltot - lw
    e_in = jnp.exp(lw)
    e_neg = jnp.exp(-lw)
    e_back = jnp.exp(back)
    at_scr[...] = -kap * jnp.exp(lw - ld)
    rt_scr[...] = r * e_in
    bt_scr[...] = kap * a * e_neg
    kt_scr[...] = kbar * e_neg
    bp_scr[...] = kap * a * e_back
    kp_scr[...] = kbar * e_back
    v_scr[...] = v
    pc_scr[...] = jnp.exp(ltot)
    bonus_scr[...] = sums[rows:]
    gate_scr[...] = z[:, 2 * BRANCH_W:]

    si = lax.broadcasted_iota(jnp.int32, (c4, RW_GW), 0)
    sj = lax.broadcasted_iota(jnp.int32, (c4, RW_GW), 1)
    head_rows = si // c == sj // RW_HD
    qi = lax.broadcasted_iota(jnp.int32, (c4, c4), 0)
    qj = lax.broadcasted_iota(jnp.int32, (c4, c4), 1)
    same = qi // c == qj // c
    strict = jnp.logical_and(same, qi % c > qj % c)
    incl = jnp.logical_and(same, qi % c >= qj % c)
    eye = (qi == qj).astype(F32)
    fuse_sq = c4 % LANES == 0

    def stack(x):
        return jnp.where(head_rows, jnp.concatenate([x] * RW_GROUP, axis=0), 0.0)

    def unstack(x):
        out = x[0:c]
        for h in range(1, RW_GROUP):
            out = out + x[h * c:(h + 1) * c]
        return out

    def phase_a(it, carry):
        chains = [(it * ua + u, g) for u in range(ua) for g in range(n_groups)]
        nmats = []
        for ch, g in chains:
            r0 = pl.multiple_of(ch * c, c)
            ln = slice(g * RW_GW, (g + 1) * RW_GW)
            lhs = jnp.concatenate([stack(at_scr[pl.ds(r0, c), ln]), stack(rt_scr[pl.ds(r0, c), ln])], axis=0)
            rhs = jnp.concatenate([bt_scr[pl.ds(r0, c), ln]] * RW_GROUP + [kt_scr[pl.ds(r0, c), ln]] * RW_GROUP,
                                  axis=0)
            quad = _dot_nt(lhs, rhs)
            nmats.append(jnp.where(strict, quad[:c4, :c4], 0.0))
            aak_scr[ch, g] = unstack(jnp.where(strict, quad[:c4, c4:], 0.0))
            arb_scr[ch, g] = unstack(jnp.where(incl, quad[c4:, :c4], 0.0))
            ark_scr[ch, g] = unstack(jnp.where(incl, quad[c4:, c4:], 0.0))
        tinvs = [eye + n for n in nmats]
        npows = [_dot(n, n) for n in nmats]
        span = 2
        while 2 * span < c:
            if fuse_sq:
                boths = [_dot(p, jnp.concatenate([p, t], axis=1)) for p, t in zip(npows, tinvs)]
                tinvs = [t + bo[:, c4:] for t, bo in zip(tinvs, boths)]
                npows = [bo[:, :c4] for bo in boths]
            else:
                tinvs = [t + _dot(p, t) for p, t in zip(npows, tinvs)]
                npows = [_dot(p, p) for p in npows]
            span *= 2
        tinvs = [t + _dot(p, t) for p, t in zip(npows, tinvs)]
        for (ch, g), t in zip(chains, tinvs):
            tcat_scr[ch, g] = unstack(t)
        return carry

    if n_ch // ua == 1:
        phase_a(0, 0)
    else:
        lax.fori_loop(0, n_ch // ua, phase_a, 0)

    def phase_b(it, carry):
        cb = it // (nbb // ub)
        b0 = (it % (nbb // ub)) * ub
        chains = [(b0 + u, g) for u in range(ub) for g in range(n_groups)]
        ops = []
        for bb, g in chains:
            ch = bb * n_cb + cb
            r0 = pl.multiple_of(ch * c, c)
            ln = slice(g * RW_GW, (g + 1) * RW_GW)
            ops.append(dict(bb=bb, g=g, ch=ch, r0=r0, ln=ln, sbd=sbd_scr[bb, g],
                            v_bd=stack(v_scr[pl.ds(r0, c), ln])))
        sprods = [_dot_nt(jnp.concatenate([at_scr[pl.ds(q["r0"], c), q["ln"]], rt_scr[pl.ds(q["r0"], c), q["ln"]]],
                                          axis=0), q["sbd"]) for q in ops]
        wmats = [sp[:c] + _dot(aak_scr[q["ch"], q["g"]], q["v_bd"]) for q, sp in zip(ops, sprods)]
        us = [_dot(tcat_scr[q["ch"], q["g"]], stack(w)) for q, w in zip(ops, wmats)]
        for q, u, sp in zip(ops, us, sprods):
            r0, ln = q["r0"], q["ln"]
            o_scr[pl.ds(r0, c), ln] = (sp[c:] + _dot(arb_scr[q["ch"], q["g"]], stack(u))
                                       + _dot(ark_scr[q["ch"], q["g"]], q["v_bd"]))
        for q, u in zip(ops, us):
            r0, ln = q["r0"], q["ln"]
            upd = _dot_tn(jnp.concatenate([u, v_scr[pl.ds(r0, c), ln]], axis=0),
                          jnp.concatenate([bp_scr[pl.ds(r0, c), ln], kp_scr[pl.ds(r0, c), ln]], axis=0))
            sbd_scr[q["bb"], q["g"]] = q["sbd"] * pc_scr[pl.ds(r0, 1), ln] + jnp.where(bd_state, upd, 0.0)
        return carry

    n_it = n_cb * (nbb // ub)
    if n_it == 1:
        phase_b(0, 0)
    else:
        lax.fori_loop(0, n_it, phase_b, 0)

    o = o_scr[...]
    inv_n = 1.0 / RW_HD
    mean = _seg_sum(o, RW_HD) * inv_n
    cen = o - mean
    var = _seg_sum(cen * cen, RW_HD) * inv_n
    on = cen * lax.rsqrt(var + RW_GN_EPS) * ln_g + ln_b
    y = (on + bonus_scr[...] * v_scr[...]) * gate_scr[...]
    y_ref[...] = y.reshape(nbb, tt_len, BRANCH_W).astype(y_ref.dtype)

    @pl.when(tt == n_t - 1)
    def _():
        out = _own_layer_block(sout_ref, layer)

        def fin(bb, carry):
            for g in range(n_groups):
                sbd = sbd_scr[bb, g]
                for h in range(RW_GROUP):
                    out[bb, g * RW_GROUP + h] = sbd[h * RW_HD:(h + 1) * RW_HD, h * RW_HD:(h + 1) * RW_HD]
            return carry

        lax.fori_loop(0, nbb, fin, 0)


def _rwkv_kernel_inplace(c_ref, sh0_ref, s0_ref, mu_ref, wl_ref, vec_ref, cum_ref, sbuf_ref, *rest, **kw):
    _rwkv_kernel(c_ref, sh0_ref, s0_ref, mu_ref, wl_ref, vec_ref, cum_ref, *rest, **kw)


def _rwkv(proj3, sh0, s0, sbuf, mu, wl, vec, *, nbb, tt_len, chunk, s_valid, layer, ua, ub):
    nb, s_len, _ = proj3.shape
    kern = functools.partial(_rwkv_kernel if layer == 0 else _rwkv_kernel_inplace, nbb=nbb, tt_len=tt_len,
                             chunk=chunk, s_valid=s_valid, s_len=s_len, layer=layer, ua=ua, ub=ub)
    st_in = pl.BlockSpec((None, nbb, RW_HEADS, RW_HD, RW_HD), lambda b, t: (layer, b, 0, 0, 0))
    st_out, extra_specs, extra_args = _state_out(s0, sbuf, layer, nbb, (RW_HEADS, RW_HD, RW_HD))
    full2 = lambda b, t: (0, 0)
    rows = nbb * tt_len
    n_groups = RW_HEADS // RW_GROUP
    mats = pltpu.VMEM((rows // chunk, n_groups, chunk, RW_GROUP * chunk), F32)
    return pl.pallas_call(
        kern,
        grid=(nb // nbb, s_len // tt_len),
        in_specs=[
            pl.BlockSpec((nbb, tt_len, RW_BLOCK), lambda b, t: (b, t, OFF_RW // RW_BLOCK)),
            pl.BlockSpec((nbb, 1, RW_COLS), lambda b, t: (b, 0, 0)),
            st_in,
            pl.BlockSpec((1, RW_COLS), full2),
            pl.BlockSpec((RW_LORA, 3 * BRANCH_W), full2),
            pl.BlockSpec((SUBLANES, BRANCH_W), full2),
            pl.BlockSpec((2 * tt_len, tt_len), full2),
        ] + extra_specs,
        out_specs=[pl.BlockSpec((nbb, tt_len, BRANCH_W), lambda b, t: (b, t, 0)), st_out],
        out_shape=[jax.ShapeDtypeStruct((nb, s_len, BRANCH_W), proj3.dtype),
                   jax.ShapeDtypeStruct(s0.shape, F32)],
        input_output_aliases={7: 1} if extra_args else {},
        scratch_shapes=[
            pltpu.VMEM((nbb, n_groups, RW_GW, RW_GW), F32),
            pltpu.VMEM((nbb, 1, RW_COLS), F32),
        ] + [pltpu.VMEM((rows, BRANCH_W), F32)] * 11 + [mats] * 4,
        compiler_params=_params(("parallel", "arbitrary")),
        name="rwkv7",
    )(proj3, sh0, s0, mu, wl, vec, _chunk_cum_matrix(tt_len, chunk), *extra_args)


def _mix_kernel(x_ref, gts_ref, ya_ref, yb_ref, yc_ref, wb_ref, wo_ref, g_ref, o_ref):
    acc = None
    for n, y_ref in enumerate((ya_ref, yb_ref, yc_ref)):
        up = _dot(y_ref[...], wb_ref[n * BRANCH_W:(n + 1) * BRANCH_W, :])
        term = _sigmoid_t(gts_ref[:, n * D_MODEL:(n + 1) * D_MODEL].astype(F32)) * up
        acc = term if acc is None else acc + term
    mix = _dot(acc, wo_ref[...])
    o_ref[...] = x_ref[...] + _rms(mix, g_ref[...])


def _mix(x, proj, ya, yb, yc, wb, wo, g, tm, layer):
    t = x.shape[0]
    row = lambda w: pl.BlockSpec((tm, w), lambda i: (i, 0))
    return pl.pallas_call(
        _mix_kernel,
        grid=(t // tm,),
        in_specs=[row(D_MODEL), row(N_BRANCH * D_MODEL), row(BRANCH_W), row(BRANCH_W), row(BRANCH_W),
                  pl.BlockSpec((None, N_BRANCH * BRANCH_W, D_MODEL), lambda i: (layer, 0, 0),
                               pipeline_mode=pl.Buffered(1)),
                  pl.BlockSpec((None, D_MODEL, D_MODEL), lambda i: (layer, 0, 0), pipeline_mode=pl.Buffered(1)),
                  pl.BlockSpec((1, D_MODEL), lambda i: (0, 0))],
        out_specs=row(D_MODEL),
        out_shape=jax.ShapeDtypeStruct((t, D_MODEL), F32),
        compiler_params=_params(("parallel",)),
        name="branch_mix",
    )(x, proj, ya, yb, yc, wb, wo, g)


def _ffn_kernel(x_ref, p_ref, gpre_ref, wg_ref, wu_ref, wd_ref, gpost_ref, wple_ref, wpg_ref, gple_ref,
                o_ref, h_scr, acc_scr):
    j = pl.program_id(1)

    @pl.when(j == 0)
    def _():
        h_scr[...] = _rms(x_ref[...], gpre_ref[...]).astype(BF16)
        acc_scr[...] = jnp.zeros_like(acc_scr)

    h = h_scr[...]
    gt = jnp.dot(h, wg_ref[...], preferred_element_type=F32)
    up = jnp.dot(h, wu_ref[...], preferred_element_type=F32)
    acc_scr[...] += _dot(gt * _sigmoid_t(gt) * up, wd_ref[...])

    @pl.when(j == pl.num_programs(1) - 1)
    def _():
        x2 = x_ref[...] + _rms(acc_scr[...], gpost_ref[...])
        ple = _dot(p_ref[...], wple_ref[...]) * _sigmoid_t(_dot(x2, wpg_ref[...]))
        o_ref[...] = x2 + _rms(ple, gple_ref[...])


def _ffn(x, p, gpre, wg, wu, wd, gpost, wple, wpg, gple, tm, n_split, layer):
    t = x.shape[0]
    d_ff = wg.shape[2]
    tf = d_ff // n_split
    vecspec = pl.BlockSpec((1, D_MODEL), lambda i, j: (0, 0))
    return pl.pallas_call(
        _ffn_kernel,
        grid=(t // tm, n_split),
        in_specs=[
            pl.BlockSpec((tm, D_MODEL), lambda i, j: (i, 0)),
            pl.BlockSpec((None, tm, PLE_DIM), lambda i, j: (layer, i, 0)),
            vecspec,
            pl.BlockSpec((None, D_MODEL, tf), lambda i, j: (layer, 0, j)),
            pl.BlockSpec((None, D_MODEL, tf), lambda i, j: (layer, 0, j)),
            pl.BlockSpec((None, tf, D_MODEL), lambda i, j: (layer, j, 0)),
            vecspec,
            pl.BlockSpec((None, PLE_DIM, D_MODEL), lambda i, j: (layer, 0, 0)),
            pl.BlockSpec((None, D_MODEL, D_MODEL), lambda i, j: (layer, 0, 0)),
            vecspec,
        ],
        out_specs=pl.BlockSpec((tm, D_MODEL), lambda i, j: (i, 0)),
        out_shape=jax.ShapeDtypeStruct((t, D_MODEL), F32),
        scratch_shapes=[pltpu.VMEM((tm, D_MODEL), BF16), pltpu.VMEM((tm, D_MODEL), F32)],
        compiler_params=_params(("parallel", "arbitrary")),
        name="ffn_ple",
    )(x, p, gpre, wg, wu, wd, gpost, wple, wpg, gple)


def _cast_kernel(x_ref, o_ref):
    o_ref[...] = x_ref[...].astype(o_ref.dtype)


def _to_bf16(w, tr):
    d, r, c = w.shape
    spec = pl.BlockSpec((None, tr, c), lambda l, i: (l, i, 0))
    return pl.pallas_call(
        _cast_kernel, grid=(d, r // tr), in_specs=[spec], out_specs=spec,
        out_shape=jax.ShapeDtypeStruct(w.shape, BF16),
        compiler_params=_params(("parallel", "parallel")), name="to_bf16",
    )(w)


W_IN_TILE = 1024


def _w_in_kernel(x_ref, o_ref):
    is_pad = lax.broadcasted_iota(jnp.int32, o_ref.shape, 1) + pl.program_id(1) * W_IN_TILE >= OFF_RW + RW_COLS
    o_ref[...] = jnp.where(is_pad, 0.0, x_ref[0]).astype(o_ref.dtype)


def _w_in_layout(w_in):
    d, _, in_cols = w_in.shape
    n_gate_cols = N_BRANCH * D_MODEL
    n_rest = in_cols - n_gate_cols

    def src(l, j):
        col = j * W_IN_TILE
        start = jnp.where(col < n_gate_cols, col + n_rest, col - n_gate_cols)
        start = jnp.minimum(start, in_cols - W_IN_TILE)
        return l, 0, pl.multiple_of(start, MXU_DIM)

    return pl.pallas_call(
        _w_in_kernel, grid=(d, PROJ_COLS // W_IN_TILE),
        in_specs=[pl.BlockSpec((pl.Element(1), pl.Element(D_MODEL), pl.Element(W_IN_TILE)), src)],
        out_specs=pl.BlockSpec((None, D_MODEL, W_IN_TILE), lambda l, j: (l, 0, j)),
        out_shape=jax.ShapeDtypeStruct((d, D_MODEL, PROJ_COLS), BF16),
        compiler_params=_params(("parallel", "parallel")), name="w_in_layout",
    )(w_in)


def _block_diag(w):
    n, r, c = w.shape
    eye = jnp.eye(n, dtype=w.dtype)
    return (eye[:, None, :, None] * w[:, :, None, :]).reshape(n * r, n * c)


def _prep_layer(i, W):
    lora = jnp.zeros((RW_LORA, 3 * BRANCH_W), F32)
    lora = lora.at[0:RW_LORA_W, 0:BRANCH_W].set(W["rw_w_up"][i])
    lora = lora.at[RW_LORA_W:RW_LORA_W + RW_LORA_A, BRANCH_W:2 * BRANCH_W].set(W["rw_a_up"][i])
    lora = lora.at[RW_LORA_W + RW_LORA_A:, 2 * BRANCH_W:].set(W["rw_g_up"][i])
    vec = jnp.stack([W["rw_w0"][i], W["rw_a0"][i], W["rw_k_k"][i], W["rw_k_a"][i],
                     W["rw_r_k"][i].reshape(BRANCH_W), W["rw_ln_g"][i], W["rw_ln_b"][i],
                     jnp.zeros((BRANCH_W,), F32)])
    row = lambda name: W[name][i].reshape(1, -1)
    return dict(
        norm_pre_mix=row("norm_pre_mix"),
        conv_w=W["conv_w"][i], conv_b=row("conv_b"),
        lru_w=jnp.concatenate([_block_diag(W["lru_wa"][i]), _block_diag(W["lru_wx"][i])], axis=1).astype(BF16),
        lru_b=jnp.stack([W["lru_ba"][i], W["lru_bx"][i]]), lru_lambda=row("lru_lambda"),
        hg_norm_g=row("hg_norm_g"),
        rw_mu=row("rw_mu"), rw_lora=lora.astype(BF16), rw_vec=vec,
        norm_post_mix=row("norm_post_mix"), norm_pre_ffn=row("norm_pre_ffn"),
        norm_post_ffn=row("norm_post_ffn"), norm_ple=row("norm_ple"),
    )


def _tiles(nb, s_len):
    t = nb * s_len
    tm_in = min(t, 1024)
    tm_tok = min(t, 512)
    if s_len >= 512:
        return dict(tm_in=tm_in, tn_in=2048, proj_dtype=BF16, tm_mix=min(t, 1024), tm_ffn=tm_tok, ffn_split=2,
                    lru_tt=128,
                    hg=dict(nbb=2, tt_len=256, chunk=64, ub=2),
                    rw=dict(nbb=8, tt_len=64, chunk=64, ua=4, ub=8))
    return dict(tm_in=tm_in, tn_in=2048, proj_dtype=F32, tm_mix=tm_tok, tm_ffn=tm_tok, ffn_split=2,
                lru_tt=s_len,
                hg=dict(nbb=min(nb, 16), tt_len=s_len, chunk=s_len, ub=4),
                rw=dict(nbb=min(nb, 16), tt_len=s_len, chunk=s_len, ua=8, ub=min(nb, 16)))


def _run_trunk(x3, p4, states, layers, big, lbraw, *, s_valid, pos0_is_zero):
    nb, s_len, _ = x3.shape
    t = nb * s_len
    plan = _tiles(nb, s_len)
    for mix in (plan["hg"], plan["rw"]):
        assert nb % mix["nbb"] == 0 and s_len % mix["tt_len"] == 0 and mix["tt_len"] % mix["chunk"] == 0, mix
    assert t % plan["tm_in"] == 0 and t % plan["tm_mix"] == 0 and t % plan["tm_ffn"] == 0 and s_len % plan["lru_tt"] == 0
    conv0, lru0, hg0, rw0, sh0 = states
    x = x3.reshape(t, D_MODEL)
    new = ([], [], [])
    nhg = nrw = None
    for i, L in enumerate(layers):
        proj = _in_proj(x, L["norm_pre_mix"], big["w_in"], plan["tm_in"], plan["tn_in"], plan["proj_dtype"], i)
        proj3 = proj.reshape(nb, s_len, PROJ_COLS)
        prev8 = jnp.pad(conv0[i], ((0, 0), (SUBLANES - (CONV_W - 1), 0), (0, 0)))
        ya, nlru = _lru(proj3, prev8, lru0[i], L["conv_w"], L["conv_b"], L["lru_w"], L["lru_b"], L["lru_lambda"],
                        tt_len=plan["lru_tt"], s_valid=s_valid, pos0_is_zero=pos0_is_zero)
        yb, nhg = _hgrn(proj3, hg0, nhg, lbraw, L["hg_norm_g"], s_valid=s_valid, layer=i, **plan["hg"])
        yc, nrw = _rwkv(proj3, sh0[i].reshape(nb, 1, RW_COLS), rw0, nrw, L["rw_mu"], L["rw_lora"], L["rw_vec"],
                        s_valid=s_valid, layer=i, **plan["rw"])
        x = _mix(x, proj, ya.reshape(t, BRANCH_W), yb.reshape(t, BRANCH_W), yc.reshape(t, BRANCH_W),
                 big["w_branch"], big["w_out"], L["norm_post_mix"], plan["tm_mix"], i)
        x = _ffn(x, p4.reshape(-1, t, PLE_DIM), L["norm_pre_ffn"], big["w_ffn_gate"], big["w_ffn_up"],
                 big["w_ffn_down"], L["norm_post_ffn"], big["w_ple"], big["w_ple_gate"], L["norm_ple"], plan["tm_ffn"], plan["ffn_split"], i)
        nconv = proj3[:, s_valid - (CONV_W - 1):s_valid, OFF_LRU:OFF_LRU + BRANCH_W].astype(F32)
        nsh = proj3[:, s_valid - 1, OFF_RW:OFF_RW + RW_COLS].astype(F32)
        for lst, val in zip(new, (nconv, nlru, nsh)):
            lst.append(val)
    nconv, nlru, nsh = (jnp.stack(l) for l in new)
    return x.reshape(nb, s_len, D_MODEL), (nconv, nlru, nhg, nrw, nsh)


def kernel(x_prompt, x_sample, p_prompt, p_sample, state_conv_a, state_lru_a, state_hgrn, state_rwkv, state_shift_c, norm_pre_mix, w_in, conv_w, conv_b, lru_wa, lru_ba, lru_wx, lru_bx, lru_lambda, hg_lower_bounds, hg_norm_g, rw_mu, rw_w0, rw_w_up, rw_a0, rw_a_up, rw_g_up, rw_k_k, rw_k_a, rw_r_k, rw_ln_g, rw_ln_b, w_branch, w_out, norm_post_mix, norm_pre_ffn, w_ffn_gate, w_ffn_up, w_ffn_down, norm_post_ffn, w_ple, w_ple_gate, norm_ple):
    W = dict(norm_pre_mix=norm_pre_mix, w_in=w_in, conv_w=conv_w, conv_b=conv_b, lru_wa=lru_wa, lru_ba=lru_ba,
             lru_wx=lru_wx, lru_bx=lru_bx, lru_lambda=lru_lambda, hg_norm_g=hg_norm_g, rw_mu=rw_mu, rw_w0=rw_w0,
             rw_w_up=rw_w_up, rw_a0=rw_a0, rw_a_up=rw_a_up, rw_g_up=rw_g_up, rw_k_k=rw_k_k, rw_k_a=rw_k_a,
             rw_r_k=rw_r_k, rw_ln_g=rw_ln_g, rw_ln_b=rw_ln_b, w_branch=w_branch, w_out=w_out,
             norm_post_mix=norm_post_mix, norm_pre_ffn=norm_pre_ffn, w_ffn_gate=w_ffn_gate, w_ffn_up=w_ffn_up,
             w_ffn_down=w_ffn_down, norm_post_ffn=norm_post_ffn, w_ple=w_ple, w_ple_gate=w_ple_gate, norm_ple=norm_ple)
    depth = w_in.shape[0]
    d_ff = w_ffn_gate.shape[2]
    big = dict(
        w_in=_w_in_layout(w_in),
        w_branch=_to_bf16(w_branch.reshape(depth, N_BRANCH * BRANCH_W, D_MODEL), N_BRANCH * BRANCH_W // 2),
        w_out=_to_bf16(w_out, D_MODEL),
        w_ffn_gate=_to_bf16(w_ffn_gate, D_MODEL // 2), w_ffn_up=_to_bf16(w_ffn_up, D_MODEL // 2),
        w_ffn_down=_to_bf16(w_ffn_down, d_ff // 2),
        w_ple=_to_bf16(w_ple, PLE_DIM), w_ple_gate=_to_bf16(w_ple_gate, D_MODEL),
    )
    layers = [_prep_layer(i, W) for i in range(depth)]
    lbraw = hg_lower_bounds.astype(F32)

    bp, sp, _ = x_prompt.shape
    zeros = lambda *shape: jnp.zeros((depth, bp) + shape, F32)
    zero_states = (zeros(CONV_W - 1, BRANCH_W), zeros(BRANCH_W), zeros(HG_HEADS, HG_D, HG_D),
                   zeros(RW_HEADS, RW_HD, RW_HD), zeros(RW_COLS))
    y_prompt, st_p = _run_trunk(x_prompt, p_prompt, zero_states, layers, big, lbraw, s_valid=sp, pos0_is_zero=True)

    _, ss, _ = x_sample.shape
    ss_pad = -(-ss // SUBLANES) * SUBLANES
    xs = jnp.pad(x_sample, ((0, 0), (0, ss_pad - ss), (0, 0)))
    ps = jnp.pad(p_sample, ((0, 0), (0, 0), (0, ss_pad - ss), (0, 0)))
    y_sample, st_s = _run_trunk(xs, ps, (state_conv_a, state_lru_a, state_hgrn, state_rwkv, state_shift_c),
                                layers, big, lbraw, s_valid=ss, pos0_is_zero=False)
    return (y_prompt, y_sample[:, :ss]) + st_p + st_s
```

```python
import functools
import math

import jax
import jax.numpy as jnp
from jax import lax
from jax.experimental import pallas as pl
from jax.experimental.pallas import tpu as pltpu

F32 = jnp.float32
BF16 = jnp.bfloat16

D_MODEL = 1024
BRANCH_W = 512
N_BRANCH = 3
CONV_W = 4
LRU_C = 8.0
HG_HEADS = 4
HG_D = BRANCH_W // HG_HEADS
HG_F_MIN = 1e-20
RW_HD = 64
RW_HEADS = BRANCH_W // RW_HD
RW_LORA_W = 64
RW_LORA_A = 64
RW_LORA_G = 128
RW_LORA = RW_LORA_W + RW_LORA_A + RW_LORA_G
RW_GN_EPS = 64e-5
RW_COLS = 3 * BRANCH_W + RW_LORA
PLE_DIM = 256
EPS = 1e-6

SUBLANES = 8
LANES = 128
MXU_DIM = 256
VMEM_LIMIT = 56 * 1024 * 1024

PROJ_COLS = 8192
OFF_GATES = 0
OFF_LRU = N_BRANCH * D_MODEL
OFF_HG = OFF_LRU + 2 * BRANCH_W
OFF_RW = OFF_HG + 4 * BRANCH_W
RW_BLOCK = PROJ_COLS - OFF_RW

RW_GROUP = LANES // RW_HD
RW_GW = RW_GROUP * RW_HD


def _params(sem):
    return pltpu.CompilerParams(dimension_semantics=sem, vmem_limit_bytes=VMEM_LIMIT)


def _rms(x, g):
    return x * lax.rsqrt(jnp.mean(x * x, axis=-1, keepdims=True) + EPS) * g


def _sigmoid(x):
    return 1.0 / (1.0 + jnp.exp(-x))


def _sigmoid_t(x):
    return 0.5 * jnp.tanh(0.5 * x) + 0.5


def _softplus(x):
    return jnp.maximum(x, 0.0) + jnp.log1p(jnp.exp(-jnp.abs(x)))


def _dot(a, b):
    return jnp.dot(a.astype(BF16), b.astype(BF16), preferred_element_type=F32)


def _dot_nt(a, b):
    return lax.dot_general(a.astype(BF16), b.astype(BF16), (((1,), (1,)), ((), ())),
                           preferred_element_type=F32)


def _dot_tn(a, b):
    return lax.dot_general(a.astype(BF16), b.astype(BF16), (((0,), (0,)), ((), ())),
                           preferred_element_type=F32)


def _split2(x):
    hi = x.astype(BF16)
    return hi, (x - hi.astype(F32)).astype(BF16)


def _seg_sum(x, seg):
    i = lax.broadcasted_iota(jnp.int32, (MXU_DIM, MXU_DIM), 0)
    j = lax.broadcasted_iota(jnp.int32, (MXU_DIM, MXU_DIM), 1)
    ones = (i // seg == j // seg).astype(BF16)
    xb = x.astype(BF16)
    tiles = [jnp.dot(xb[:, l:l + MXU_DIM], ones, preferred_element_type=F32)
             for l in range(0, x.shape[1], MXU_DIM)]
    return jnp.concatenate(tiles, axis=1)


def _chunk_sums(cum, x, nbb, tt_len):
    hi, lo = _split2(x)
    m = cum.astype(BF16)
    pre, tot = [], []
    for b in range(nbb):
        rs = slice(b * tt_len, (b + 1) * tt_len)
        both = jnp.dot(m, lo[rs], preferred_element_type=F32) + jnp.dot(m, hi[rs], preferred_element_type=F32)
        pre.append(both[:tt_len])
        tot.append(both[tt_len:])
    if nbb == 1:
        return pre[0], tot[0]
    return jnp.concatenate(pre, axis=0), jnp.concatenate(tot, axis=0)


def _in_proj_kernel(x_ref, g_ref, w_ref, o_ref, h_scr):
    @pl.when(pl.program_id(1) == 0)
    def _():
        h_scr[...] = _rms(x_ref[...], g_ref[...]).astype(BF16)

    o_ref[...] = jnp.dot(h_scr[...], w_ref[...], preferred_element_type=F32).astype(o_ref.dtype)


def _in_proj(x, g, w, tm, tn, out_dtype, layer):
    t = x.shape[0]
    return pl.pallas_call(
        _in_proj_kernel,
        grid=(t // tm, PROJ_COLS // tn),
        in_specs=[
            pl.BlockSpec((tm, D_MODEL), lambda i, j: (i, 0)),
            pl.BlockSpec((1, D_MODEL), lambda i, j: (0, 0)),
            pl.BlockSpec((None, D_MODEL, tn), lambda i, j: (layer, 0, j)),
        ],
        out_specs=pl.BlockSpec((tm, tn), lambda i, j: (i, j)),
        out_shape=jax.ShapeDtypeStruct((t, PROJ_COLS), out_dtype),
        scratch_shapes=[pltpu.VMEM((tm, D_MODEL), BF16)],
        compiler_params=_params(("parallel", "arbitrary")),
        name="in_proj",
    )(x, g, w)


def _lru_kernel(xa_ref, ga_ref, prev8_ref, h0_ref, cw_ref, cb_ref, w_ref, bab_ref, lam_ref,
                y_ref, hout_ref, prev_scr, h_scr, *, nb, tt_len, s_valid, s_len, pos0_is_zero):
    tt = pl.program_id(0)
    rows = nb * tt_len

    @pl.when(tt == 0)
    def _():
        prev_scr[...] = prev8_ref[...]
        h_scr[...] = h0_ref[...]

    xa = xa_ref[...].astype(F32)
    n8 = tt_len // SUBLANES
    xa4 = xa.reshape(nb, n8, SUBLANES, BRANCH_W)
    prev4 = prev_scr[...][:, None]
    t8 = lax.broadcasted_iota(jnp.int32, (1, 1, SUBLANES, 1), 2)
    cw = cw_ref[...]
    xc4 = cb_ref[...][None, None] + cw[CONV_W - 1][None, None, None] * xa4
    for j in range(1, CONV_W):
        rot = pltpu.roll(xa4, j, 2)
        rot_before = pltpu.roll(prev4, j, 2)
        if n8 > 1:
            rot_before = jnp.concatenate([rot_before, rot[:, :n8 - 1]], axis=1)
        xc4 = xc4 + cw[CONV_W - 1 - j][None, None, None] * jnp.where(t8 < j, rot_before, rot)
    prev_scr[...] = xa[:, tt_len - SUBLANES:, :]

    xc2 = xc4.reshape(rows, BRANCH_W)
    z = _dot(xc2, w_ref[...])
    bab = bab_ref[...]
    r = _sigmoid_t(z[:, :BRANCH_W] + bab[0:1])
    i = _sigmoid_t(z[:, BRANCH_W:] + bab[1:2])
    log_a = (-LRU_C) * r * _softplus(-lam_ref[...])
    a = jnp.exp(log_a)
    m2 = jnp.maximum(1.0 - a * a, 0.0)
    mult = m2 * lax.rsqrt(jnp.maximum(m2, 1e-30))
    t_in = lax.broadcasted_iota(jnp.int32, (rows, 1), 0) % tt_len
    if pos0_is_zero:
        mult = jnp.where(jnp.logical_and(tt == 0, t_in == 0), 1.0, mult)
    b = xc2 * i * mult
    if s_valid < s_len:
        valid = (tt * tt_len + t_in) < s_valid
        a = jnp.where(valid, a, 1.0)
        b = jnp.where(valid, b, 0.0)
    a4 = a.reshape(nb, n8, SUBLANES, BRANCH_W)
    b4 = b.reshape(nb, n8, SUBLANES, BRANCH_W)

    d = 1
    while d < SUBLANES:
        keep = t8 >= d
        b4 = a4 * jnp.where(keep, pltpu.roll(b4, d, 2), 0.0) + b4
        a4 = a4 * jnp.where(keep, pltpu.roll(a4, d, 2), 1.0)
        d *= 2
    h_in = h_scr[...][:, None, :]
    blocks = []
    for blk in range(n8):
        hb = a4[:, blk] * h_in + b4[:, blk]
        blocks.append(hb)
        h_in = hb[:, SUBLANES - 1:SUBLANES, :]
    hh = blocks[0] if n8 == 1 else jnp.concatenate(blocks, axis=1)
    h = h_in.reshape(nb, BRANCH_W)
    h_scr[...] = h
    hout_ref[...] = h

    ga = ga_ref[...].astype(F32)
    gelu = 0.5 * ga * (1.0 + jnp.tanh(math.sqrt(2.0 / math.pi) * (ga + 0.044715 * ga * ga * ga)))
    y_ref[...] = (hh * gelu).astype(y_ref.dtype)


def _lru(proj3, prev8, h0, cw, cb, w, bab, lam, *, tt_len, s_valid, pos0_is_zero):
    nb, s_len, _ = proj3.shape
    blk = OFF_LRU // BRANCH_W
    kern = functools.partial(_lru_kernel, nb=nb, tt_len=tt_len, s_valid=s_valid, s_len=s_len,
                             pos0_is_zero=pos0_is_zero)
    full2 = lambda t: (0, 0)
    return pl.pallas_call(
        kern,
        grid=(s_len // tt_len,),
        in_specs=[
            pl.BlockSpec((nb, tt_len, BRANCH_W), lambda t: (0, t, blk)),
            pl.BlockSpec((nb, tt_len, BRANCH_W), lambda t: (0, t, blk + 1)),
            pl.BlockSpec((nb, SUBLANES, BRANCH_W), lambda t: (0, 0, 0)),
            pl.BlockSpec((nb, BRANCH_W), full2),
            pl.BlockSpec((CONV_W, BRANCH_W), full2),
            pl.BlockSpec((1, BRANCH_W), full2),
            pl.BlockSpec((BRANCH_W, 2 * BRANCH_W), full2),
            pl.BlockSpec((2, BRANCH_W), full2),
            pl.BlockSpec((1, BRANCH_W), full2),
        ],
        out_specs=[
            pl.BlockSpec((nb, tt_len, BRANCH_W), lambda t: (0, t, 0)),
            pl.BlockSpec((nb, BRANCH_W), full2),
        ],
        out_shape=[
            jax.ShapeDtypeStruct((nb, s_len, BRANCH_W), proj3.dtype),
            jax.ShapeDtypeStruct((nb, BRANCH_W), F32),
        ],
        scratch_shapes=[
            pltpu.VMEM((nb, SUBLANES, BRANCH_W), F32),
            pltpu.VMEM((nb, BRANCH_W), F32),
        ],
        compiler_params=_params(("arbitrary",)),
        name="rglru",
    )(proj3, proj3, prev8, h0, cw, cb, w, bab, lam)


def _own_layer_block(sout_ref, layer):
    if layer > 0:
        return sout_ref
    if sout_ref.shape[0] > 1:
        sout_ref[1:] = jnp.zeros((sout_ref.shape[0] - 1,) + tuple(sout_ref.shape[1:]), sout_ref.dtype)
    return sout_ref.at[0]


def _state_out(s0, sbuf, layer, nbb, tail):
    depth = s0.shape[0]
    zeros = (0,) * len(tail)
    if layer == 0:
        spec = pl.BlockSpec((depth, nbb) + tail, lambda b, t: (0, b) + zeros)
        return spec, [], []
    spec = pl.BlockSpec((None, nbb) + tail, lambda b, t: (layer, b) + zeros)
    return spec, [pl.BlockSpec(memory_space=pl.ANY)], [sbuf]


def _hg_diag_blocks(qh, kh, vh, bch):
    c = qh.shape[0]
    nblk = c // SUBLANES
    q3 = qh.reshape(nblk, SUBLANES, HG_D)
    k3 = kh.reshape(nblk, SUBLANES, HG_D)
    v3 = vh.reshape(nblk, SUBLANES, HG_D)
    b3 = bch.reshape(nblk, SUBLANES, HG_D)
    tin = lax.broadcasted_iota(jnp.int32, (1, SUBLANES, 1), 1)
    o3 = jnp.zeros((nblk, SUBLANES, HG_D), F32)
    for s in range(SUBLANES):
        dec = jnp.exp(jnp.minimum(b3 - b3[:, s:s + 1, :], 0.0))
        w = jnp.sum(q3 * k3[:, s:s + 1, :] * dec, axis=-1, keepdims=True)
        w = jnp.where(tin >= s, w, 0.0)
        o3 = o3 + w * v3[:, s:s + 1, :]
    return o3.reshape(c, HG_D)


def _hg_level_refs(bch, h, c):
    gq, gk = [], []
    zero = jnp.zeros((h, HG_D), F32)
    for j in range(c // h):
        if j % 2 == 1:
            gq.append(jnp.broadcast_to(bch[j * h - 1:j * h, :], (h, HG_D)))
            gk.append(zero)
        else:
            gq.append(zero)
            gk.append(jnp.broadcast_to(bch[(j + 1) * h - 1:(j + 1) * h, :], (h, HG_D)))
    return jnp.concatenate(gq, axis=0), jnp.concatenate(gk, axis=0)


def _hgrn_kernel(q_ref, f_ref, v_ref, g_ref, s0_ref, lbraw_ref, ng_ref, cum_ref, y_ref, sout_ref,
                 st_scr, qs_scr, k_scr, v_scr, bc_scr, qe_scr, kh_scr, et_scr, o_scr,
                 *, nbb, tt_len, chunk, s_valid, s_len, layer, ub):
    tt = pl.program_id(1)
    n_t = pl.num_programs(1)
    c = chunk
    rows = nbb * tt_len
    n_cb = tt_len // c
    kv_major = s_len == c

    raw = lbraw_ref[...]
    ex = jnp.exp(raw - jnp.max(raw, axis=0, keepdims=True))
    sm = ex / jnp.sum(ex, axis=0, keepdims=True)
    lb = jnp.zeros((1, BRANCH_W), F32)
    for l in range(1, layer + 1):
        lb = lb + sm[l:l + 1]

    @pl.when(tt == 0)
    def _():
        def init(bb, carry):
            for hd in range(HG_HEADS):
                st_scr[bb, hd] = s0_ref[bb, hd] if kv_major else s0_ref[bb, hd].T
            return carry

        lax.fori_loop(0, nbb, init, 0)

    q = q_ref[...].astype(F32).reshape(rows, BRANCH_W)
    fp = f_ref[...].astype(F32).reshape(rows, BRANCH_W)
    sg = _sigmoid(fp)
    f = lb + (1.0 - lb) * sg
    k = (1.0 - lb) * (1.0 - sg)
    logf = jnp.log(jnp.maximum(f, HG_F_MIN))
    if s_valid < s_len:
        t_in = lax.broadcasted_iota(jnp.int32, (rows, 1), 0) % tt_len
        valid = (tt * tt_len + t_in) < s_valid
        k = jnp.where(valid, k, 0.0)
        logf = jnp.where(valid, logf, 0.0)
    bc, btot = _chunk_sums(cum_ref[...], logf, nbb, tt_len)
    qs = q * _sigmoid_t(q)
    qs_scr[...] = qs
    k_scr[...] = k
    v_scr[...] = v_ref[...].astype(F32).reshape(rows, BRANCH_W)
    bc_scr[...] = bc
    qe_scr[...] = qs * jnp.exp(bc)
    kh_scr[...] = k * jnp.exp(btot - bc)
    et_scr[...] = jnp.exp(btot)

    ti = lax.broadcasted_iota(jnp.int32, (c, 1), 0)
    ii = lax.broadcasted_iota(jnp.int32, (c, c), 0)
    jj = lax.broadcasted_iota(jnp.int32, (c, c), 1)
    levels = []
    h = c // 2
    while h >= SUBLANES:
        odd = (ti // h) % 2 == 1
        pair = jnp.logical_and(ii // (2 * h) == jj // (2 * h),
                               jnp.logical_and((ii // h) % 2 == 1, (jj // h) % 2 == 0))
        levels.append((h, odd, pair))
        h //= 2

    def step(it, carry):
        cb = it // (nbb // ub)
        b0 = (it % (nbb // ub)) * ub
        chains = []
        for u in range(ub):
            r0 = pl.multiple_of((b0 + u) * tt_len + cb * c, c)
            for hd in range(HG_HEADS):
                ln = slice(hd * HG_D, (hd + 1) * HG_D)
                chains.append(dict(bb=b0 + u, hd=hd, r0=r0, ln=ln, st=st_scr[b0 + u, hd],
                                   qh=qs_scr[pl.ds(r0, c), ln], kh=k_scr[pl.ds(r0, c), ln],
                                   vh=v_scr[pl.ds(r0, c), ln], bch=bc_scr[pl.ds(r0, c), ln]))
        if kv_major:
            outs = [_dot(qe_scr[pl.ds(x["r0"], c), x["ln"]], x["st"]) for x in chains]
        else:
            outs = [_dot_nt(qe_scr[pl.ds(x["r0"], c), x["ln"]], x["st"]) for x in chains]
        if levels:
            amats = []
            for x in chains:
                amat = None
                for (h, odd, pair) in levels:
                    gq, gk = _hg_level_refs(x["bch"], h, c)
                    qt = jnp.where(odd, x["qh"] * jnp.exp(jnp.where(odd, x["bch"] - gq, 0.0)), 0.0)
                    kt = jnp.where(odd, 0.0, x["kh"] * jnp.exp(jnp.where(odd, 0.0, gk - x["bch"])))
                    term = jnp.where(pair, _dot_nt(qt, kt), 0.0)
                    amat = term if amat is None else amat + term
                amats.append(amat)
            outs = [o + _dot(a, x["vh"]) for o, a, x in zip(outs, amats, chains)]
        for o, x in zip(outs, chains):
            o_scr[pl.ds(x["r0"], c), x["ln"]] = o + _hg_diag_blocks(x["qh"], x["kh"], x["vh"], x["bch"])
        for x in chains:
            khat = kh_scr[pl.ds(x["r0"], c), x["ln"]]
            if kv_major:
                last = jnp.where(ti == c - 1, x["bch"], 0.0)
                hi, lo = _split2(last)
                ones = jnp.ones((c, HG_D), BF16)
                col = (lax.dot_general(hi, ones, (((0,), (0,)), ((), ())), preferred_element_type=F32)
                       + lax.dot_general(lo, ones, (((0,), (0,)), ((), ())), preferred_element_type=F32))
                st_scr[x["bb"], x["hd"]] = x["st"] * jnp.exp(col) + _dot_tn(khat, x["vh"])
            else:
                st_scr[x["bb"], x["hd"]] = x["st"] * et_scr[pl.ds(x["r0"], 1), x["ln"]] + _dot_tn(x["vh"], khat)
        return carry

    n_it = n_cb * (nbb // ub)
    if n_it == 1:
        step(0, 0)
    else:
        lax.fori_loop(0, n_it, step, 0)

    g = g_ref[...].astype(F32).reshape(rows, BRANCH_W)
    ng = ng_ref[...]
    outs = []
    for hd in range(HG_HEADS):
        ln = slice(hd * HG_D, (hd + 1) * HG_D)
        o = o_scr[:, ln]
        outs.append(o * lax.rsqrt(jnp.mean(o * o, axis=-1, keepdims=True) + EPS) * ng[:, ln])
    y = jnp.concatenate(outs, axis=-1) * (g * _sigmoid_t(g))
    y_ref[...] = y.reshape(nbb, tt_len, BRANCH_W).astype(y_ref.dtype)

    @pl.when(tt == n_t - 1)
    def _():
        out = _own_layer_block(sout_ref, layer)

        def fin(bb, carry):
            for hd in range(HG_HEADS):
                out[bb, hd] = st_scr[bb, hd] if kv_major else st_scr[bb, hd].T
            return carry

        lax.fori_loop(0, nbb, fin, 0)


def _hgrn_kernel_inplace(q_ref, f_ref, v_ref, g_ref, s0_ref, lbraw_ref, ng_ref, cum_ref, sbuf_ref, *rest, **kw):
    _hgrn_kernel(q_ref, f_ref, v_ref, g_ref, s0_ref, lbraw_ref, ng_ref, cum_ref, *rest, **kw)


def _chunk_cum_matrix(rows, c):
    i = jnp.arange(rows)[:, None]
    j = jnp.arange(rows)[None, :]
    same = i // c == j // c
    return jnp.concatenate([same & (i >= j), same], axis=0).astype(BF16)


def _hgrn(proj3, s0, sbuf, lbraw, ng, *, nbb, tt_len, chunk, s_valid, layer, ub):
    nb, s_len, _ = proj3.shape
    blk = OFF_HG // BRANCH_W
    rows = nbb * tt_len
    kern = functools.partial(_hgrn_kernel if layer == 0 else _hgrn_kernel_inplace, nbb=nbb, tt_len=tt_len,
                             chunk=chunk, s_valid=s_valid, s_len=s_len, layer=layer, ub=ub)
    seq = lambda k: pl.BlockSpec((nbb, tt_len, BRANCH_W), lambda b, t, k=k: (b, t, blk + k))
    st_in = pl.BlockSpec((None, nbb, HG_HEADS, HG_D, HG_D), lambda b, t: (layer, b, 0, 0, 0))
    st_out, extra_specs, extra_args = _state_out(s0, sbuf, layer, nbb, (HG_HEADS, HG_D, HG_D))
    return pl.pallas_call(
        kern,
        grid=(nb // nbb, s_len // tt_len),
        in_specs=[seq(0), seq(1), seq(2), seq(3), st_in,
                  pl.BlockSpec(lbraw.shape, lambda b, t: (0, 0)),
                  pl.BlockSpec((1, BRANCH_W), lambda b, t: (0, 0)),
                  pl.BlockSpec((2 * tt_len, tt_len), lambda b, t: (0, 0))] + extra_specs,
        out_specs=[pl.BlockSpec((nbb, tt_len, BRANCH_W), lambda b, t: (b, t, 0)), st_out],
        out_shape=[jax.ShapeDtypeStruct((nb, s_len, BRANCH_W), proj3.dtype),
                   jax.ShapeDtypeStruct(s0.shape, F32)],
        input_output_aliases={8: 1} if extra_args else {},
        scratch_shapes=[pltpu.VMEM((nbb, HG_HEADS, HG_D, HG_D), F32)]
        + [pltpu.VMEM((rows, BRANCH_W), F32)] * 8,
        compiler_params=_params(("parallel", "arbitrary")),
        name="hgrn2",
    )(proj3, proj3, proj3, proj3, s0, lbraw, ng, _chunk_cum_matrix(tt_len, chunk), *extra_args)


def _rwkv_kernel(c_ref, sh0_ref, s0_ref, mu_ref, wl_ref, vec_ref, cum_ref, y_ref, sout_ref,
                 sbd_scr, carry_scr, at_scr, rt_scr, bt_scr, kt_scr, bp_scr, kp_scr, v_scr, pc_scr,
                 bonus_scr, gate_scr, o_scr, tcat_scr, aak_scr, arb_scr, ark_scr,
                 *, nbb, tt_len, chunk, s_valid, s_len, layer, ua, ub):
    tt = pl.program_id(1)
    n_t = pl.num_programs(1)
    c = chunk
    c4 = RW_GROUP * c
    n_groups = RW_HEADS // RW_GROUP
    rows = nbb * tt_len
    n_cb = tt_len // c
    n_ch = rows // c

    vec = vec_ref[...]
    w0, a0, k_k, k_a, r_k, ln_g, ln_b = [vec[i:i + 1] for i in range(7)]
    gi_ = lax.broadcasted_iota(jnp.int32, (RW_GW, RW_GW), 0)
    gj_ = lax.broadcasted_iota(jnp.int32, (RW_GW, RW_GW), 1)
    bd_state = gi_ // RW_HD == gj_ // RW_HD

    @pl.when(tt == 0)
    def _():
        carry_scr[...] = sh0_ref[...]

        def init(bb, carry):
            zero = jnp.zeros((RW_HD, RW_HD), F32)
            for g in range(n_groups):
                blocks = [jnp.concatenate([s0_ref[bb, g * RW_GROUP + h] if j == h else zero for j in range(RW_GROUP)],
                                          axis=1) for h in range(RW_GROUP)]
                sbd_scr[bb, g] = jnp.concatenate(blocks, axis=0)
            return carry

        lax.fori_loop(0, nbb, init, 0)

    cc3 = c_ref[:, :, :RW_COLS].astype(F32)
    t3 = lax.broadcasted_iota(jnp.int32, (1, tt_len, 1), 1)
    prev3 = jnp.where(t3 == 0, carry_scr[...], pltpu.roll(cc3, 1, 1))
    carry_scr[...] = cc3[:, tt_len - 1:tt_len, :]
    xm = (cc3 + (prev3 - cc3) * mu_ref[...][None]).reshape(rows, RW_COLS)
    r = xm[:, 0:BRANCH_W]
    k = xm[:, BRANCH_W:2 * BRANCH_W]
    v = xm[:, 2 * BRANCH_W:3 * BRANCH_W]
    lo = xm[:, 3 * BRANCH_W:]
    lane_l = lax.broadcasted_iota(jnp.int32, (1, RW_LORA), 1)
    act = jnp.where(lane_l < RW_LORA_W, jnp.tanh(lo),
                    jnp.where(lane_l < RW_LORA_W + RW_LORA_A, lo, _sigmoid_t(lo)))
    z = _dot(act, wl_ref[...])
    ld = (-math.exp(-0.5)) * _sigmoid_t(w0 + z[:, 0:BRANCH_W])
    a = _sigmoid_t(a0 + z[:, BRANCH_W:2 * BRANCH_W])
    kk = k * k_k
    kbar = k * (1.0 + (a - 1.0) * k_a)
    sums = _seg_sum(jnp.concatenate([kk * kk, r * kbar * r_k], axis=0), RW_HD)
    kap = kk * lax.rsqrt(jnp.maximum(sums[:rows], 1e-24))
    if s_valid < s_len:
        t_in = lax.broadcasted_iota(jnp.int32, (rows, 1), 0) % tt_len
        valid = (tt * tt_len + t_in) < s_valid
        ld = jnp.where(valid, ld, 0.0)
        kap = jnp.where(valid, kap, 0.0)
        kbar = jnp.where(valid, kbar, 0.0)
    lw, ltot = _chunk_sums(cum_ref[...], ld, nbb, tt_len)
    back = ltot - lw
    e_in = jnp.exp(lw)
    e_neg = jnp.exp(-lw)
    e_back = jnp.exp(back)
    at_scr[...] = -kap * jnp.exp(lw - ld)
    rt_scr[...] = r * e_in
    bt_scr[...] = kap * a * e_neg
    kt_scr[...] = kbar * e_neg
    bp_scr[...] = kap * a * e_back
    kp_scr[...] = kbar * e_back
    v_scr[...] = v
    pc_scr[...] = jnp.exp(ltot)
    bonus_scr[...] = sums[rows:]
    gate_scr[...] = z[:, 2 * BRANCH_W:]

    si = lax.broadcasted_iota(jnp.int32, (c4, RW_GW), 0)
    sj = lax.broadcasted_iota(jnp.int32, (c4, RW_GW), 1)
    head_rows = si // c == sj // RW_HD
    qi = lax.broadcasted_iota(jnp.int32, (c4, c4), 0)
    qj = lax.broadcasted_iota(jnp.int32, (c4, c4), 1)
    same = qi // c == qj // c
    strict = jnp.logical_and(same, qi % c > qj % c)
    incl = jnp.logical_and(same, qi % c >= qj % c)
    eye = (qi == qj).astype(F32)
    fuse_sq = c4 % LANES == 0

    def stack(x):
        return jnp.where(head_rows, jnp.concatenate([x] * RW_GROUP, axis=0), 0.0)

    def unstack(x):
        out = x[0:c]
        for h in range(1, RW_GROUP):
            out = out + x[h * c:(h + 1) * c]
        return out

    def phase_a(it, carry):
        chains = [(it * ua + u, g) for u in range(ua) for g in range(n_groups)]
        nmats = []
        for ch, g in chains:
            r0 = pl.multiple_of(ch * c, c)
            ln = slice(g * RW_GW, (g + 1) * RW_GW)
            lhs = jnp.concatenate([stack(at_scr[pl.ds(r0, c), ln]), stack(rt_scr[pl.ds(r0, c), ln])], axis=0)
            rhs = jnp.concatenate([bt_scr[pl.ds(r0, c), ln]] * RW_GROUP + [kt_scr[pl.ds(r0, c), ln]] * RW_GROUP,
                                  axis=0)
            quad = _dot_nt(lhs, rhs)
            nmats.append(jnp.where(strict, quad[:c4, :c4], 0.0))
            aak_scr[ch, g] = unstack(jnp.where(strict, quad[:c4, c4:], 0.0))
            arb_scr[ch, g] = unstack(jnp.where(incl, quad[c4:, :c4], 0.0))
            ark_scr[ch, g] = unstack(jnp.where(incl, quad[c4:, c4:], 0.0))
        tinvs = [eye + n for n in nmats]
        npows = [_dot(n, n) for n in nmats]
        span = 2
        while 2 * span < c:
            if fuse_sq:
                boths = [_dot(p, jnp.concatenate([p, t], axis=1)) for p, t in zip(npows, tinvs)]
                tinvs = [t + bo[:, c4:] for t, bo in zip(tinvs, boths)]
                npows = [bo[:, :c4] for bo in boths]
            else:
                tinvs = [t + _dot(p, t) for p, t in zip(npows, tinvs)]
                npows = [_dot(p, p) for p in npows]
            span *= 2
        tinvs = [t + _dot(p, t) for p, t in zip(npows, tinvs)]
        for (ch, g), t in zip(chains, tinvs):
            tcat_scr[ch, g] = unstack(t)
        return carry

    if n_ch // ua == 1:
        phase_a(0, 0)
    else:
        lax.fori_loop(0, n_ch // ua, phase_a, 0)

    def phase_b(it, carry):
        cb = it // (nbb // ub)
        b0 = (it % (nbb // ub)) * ub
        chains = [(b0 + u, g) for u in range(ub) for g in range(n_groups)]
        ops = []
        for bb, g in chains:
            ch = bb * n_cb + cb
            r0 = pl.multiple_of(ch * c, c)
            ln = slice(g * RW_GW, (g + 1) * RW_GW)
            ops.append(dict(bb=bb, g=g, ch=ch, r0=r0, ln=ln, sbd=sbd_scr[bb, g],
                            v_bd=stack(v_scr[pl.ds(r0, c), ln])))
        sprods = [_dot_nt(jnp.concatenate([at_scr[pl.ds(q["r0"], c), q["ln"]], rt_scr[pl.ds(q["r0"], c), q["ln"]]],
                                          axis=0), q["sbd"]) for q in ops]
        wmats = [sp[:c] + _dot(aak_scr[q["ch"], q["g"]], q["v_bd"]) for q, sp in zip(ops, sprods)]
        us = [_dot(tcat_scr[q["ch"], q["g"]], stack(w)) for q, w in zip(ops, wmats)]
        for q, u, sp in zip(ops, us, sprods):
            r0, ln = q["r0"], q["ln"]
            o_scr[pl.ds(r0, c), ln] = (sp[c:] + _dot(arb_scr[q["ch"], q["g"]], stack(u))
                                       + _dot(ark_scr[q["ch"], q["g"]], q["v_bd"]))
        for q, u in zip(ops, us):
            r0, ln = q["r0"], q["ln"]
            upd = _dot_tn(jnp.concatenate([u, v_scr[pl.ds(r0, c), ln]], axis=0),
                          jnp.concatenate([bp_scr[pl.ds(r0, c), ln], kp_scr[pl.ds(r0, c), ln]], axis=0))
            sbd_scr[q["bb"], q["g"]] = q["sbd"] * pc_scr[pl.ds(r0, 1), ln] + jnp.where(bd_state, upd, 0.0)
        return carry

    n_it = n_cb * (nbb // ub)
    if n_it == 1:
        phase_b(0, 0)
    else:
        lax.fori_loop(0, n_it, phase_b, 0)

    o = o_scr[...]
    inv_n = 1.0 / RW_HD
    mean = _seg_sum(o, RW_HD) * inv_n
    cen = o - mean
    var = _seg_sum(cen * cen, RW_HD) * inv_n
    on = cen * lax.rsqrt(var + RW_GN_EPS) * ln_g + ln_b
    y = (on + bonus_scr[...] * v_scr[...]) * gate_scr[...]
    y_ref[...] = y.reshape(nbb, tt_len, BRANCH_W).astype(y_ref.dtype)

    @pl.when(tt == n_t - 1)
    def _():
        out = _own_layer_block(sout_ref, layer)

        def fin(bb, carry):
            for g in range(n_groups):
                sbd = sbd_scr[bb, g]
                for h in range(RW_GROUP):
                    out[bb, g * RW_GROUP + h] = sbd[h * RW_HD:(h + 1) * RW_HD, h * RW_HD:(h + 1) * RW_HD]
            return carry

        lax.fori_loop(0, nbb, fin, 0)


def _rwkv_kernel_inplace(c_ref, sh0_ref, s0_ref, mu_ref, wl_ref, vec_ref, cum_ref, sbuf_ref, *rest, **kw):
    _rwkv_kernel(c_ref, sh0_ref, s0_ref, mu_ref, wl_ref, vec_ref, cum_ref, *rest, **kw)


def _rwkv(proj3, sh0, s0, sbuf, mu, wl, vec, *, nbb, tt_len, chunk, s_valid, layer, ua, ub):
    nb, s_len, _ = proj3.shape
    kern = functools.partial(_rwkv_kernel if layer == 0 else _rwkv_kernel_inplace, nbb=nbb, tt_len=tt_len,
                             chunk=chunk, s_valid=s_valid, s_len=s_len, layer=layer, ua=ua, ub=ub)
    st_in = pl.BlockSpec((None, nbb, RW_HEADS, RW_HD, RW_HD), lambda b, t: (layer, b, 0, 0, 0))
    st_out, extra_specs, extra_args = _state_out(s0, sbuf, layer, nbb, (RW_HEADS, RW_HD, RW_HD))
    full2 = lambda b, t: (0, 0)
    rows = nbb * tt_len
    n_groups = RW_HEADS // RW_GROUP
    mats = pltpu.VMEM((rows // chunk, n_groups, chunk, RW_GROUP * chunk), F32)
    return pl.pallas_call(
        kern,
        grid=(nb // nbb, s_len // tt_len),
        in_specs=[
            pl.BlockSpec((nbb, tt_len, RW_BLOCK), lambda b, t: (b, t, OFF_RW // RW_BLOCK)),
            pl.BlockSpec((nbb, 1, RW_COLS), lambda b, t: (b, 0, 0)),
            st_in,
            pl.BlockSpec((1, RW_COLS), full2),
            pl.BlockSpec((RW_LORA, 3 * BRANCH_W), full2),
            pl.BlockSpec((SUBLANES, BRANCH_W), full2),
            pl.BlockSpec((2 * tt_len, tt_len), full2),
        ] + extra_specs,
        out_specs=[pl.BlockSpec((nbb, tt_len, BRANCH_W), lambda b, t: (b, t, 0)), st_out],
        out_shape=[jax.ShapeDtypeStruct((nb, s_len, BRANCH_W), proj3.dtype),
                   jax.ShapeDtypeStruct(s0.shape, F32)],
        input_output_aliases={7: 1} if extra_args else {},
        scratch_shapes=[
            pltpu.VMEM((nbb, n_groups, RW_GW, RW_GW), F32),
            pltpu.VMEM((nbb, 1, RW_COLS), F32),
        ] + [pltpu.VMEM((rows, BRANCH_W), F32)] * 11 + [mats] * 4,
        compiler_params=_params(("parallel", "arbitrary")),
        name="rwkv7",
    )(proj3, sh0, s0, mu, wl, vec, _chunk_cum_matrix(tt_len, chunk), *extra_args)


def _mix_kernel(x_ref, gts_ref, ya_ref, yb_ref, yc_ref, wb_ref, wo_ref, g_ref, o_ref):
    acc = None
    for n, y_ref in enumerate((ya_ref, yb_ref, yc_ref)):
        up = _dot(y_ref[...], wb_ref[n * BRANCH_W:(n + 1) * BRANCH_W, :])
        term = _sigmoid_t(gts_ref[:, n * D_MODEL:(n + 1) * D_MODEL].astype(F32)) * up
        acc = term if acc is None else acc + term
    mix = _dot(acc, wo_ref[...])
    o_ref[...] = x_ref[...] + _rms(mix, g_ref[...])


def _mix(x, proj, ya, yb, yc, wb, wo, g, tm, layer):
    t = x.shape[0]
    row = lambda w: pl.BlockSpec((tm, w), lambda i: (i, 0))
    return pl.pallas_call(
        _mix_kernel,
        grid=(t // tm,),
        in_specs=[row(D_MODEL), row(N_BRANCH * D_MODEL), row(BRANCH_W), row(BRANCH_W), row(BRANCH_W),
                  pl.BlockSpec((None, N_BRANCH * BRANCH_W, D_MODEL), lambda i: (layer, 0, 0),
                               pipeline_mode=pl.Buffered(1)),
                  pl.BlockSpec((None, D_MODEL, D_MODEL), lambda i: (layer, 0, 0), pipeline_mode=pl.Buffered(1)),
                  pl.BlockSpec((1, D_MODEL), lambda i: (0, 0))],
        out_specs=row(D_MODEL),
        out_shape=jax.ShapeDtypeStruct((t, D_MODEL), F32),
        compiler_params=_params(("parallel",)),
        name="branch_mix",
    )(x, proj, ya, yb, yc, wb, wo, g)


def _ffn_kernel(x_ref, p_ref, gpre_ref, wg_ref, wu_ref, wd_ref, gpost_ref, wple_ref, wpg_ref, gple_ref,
                o_ref, h_scr, acc_scr):
    j = pl.program_id(1)

    @pl.when(j == 0)
    def _():
        h_scr[...] = _rms(x_ref[...], gpre_ref[...]).astype(BF16)
        acc_scr[...] = jnp.zeros_like(acc_scr)

    h = h_scr[...]
    gt = jnp.dot(h, wg_ref[...], preferred_element_type=F32)
    up = jnp.dot(h, wu_ref[...], preferred_element_type=F32)
    acc_scr[...] += _dot(gt * _sigmoid_t(gt) * up, wd_ref[...])

    @pl.when(j == pl.num_programs(1) - 1)
    def _():
        x2 = x_ref[...] + _rms(acc_scr[...], gpost_ref[...])
        ple = _dot(p_ref[...], wple_ref[...]) * _sigmoid_t(_dot(x2, wpg_ref[...]))
        o_ref[...] = x2 + _rms(ple, gple_ref[...])


def _ffn(x, p, gpre, wg, wu, wd, gpost, wple, wpg, gple, tm, n_split, layer):
    t = x.shape[0]
    d_ff = wg.shape[2]
    tf = d_ff // n_split
    vecspec = pl.BlockSpec((1, D_MODEL), lambda i, j: (0, 0))
    return pl.pallas_call(
        _ffn_kernel,
        grid=(t // tm, n_split),
        in_specs=[
            pl.BlockSpec((tm, D_MODEL), lambda i, j: (i, 0)),
            pl.BlockSpec((None, tm, PLE_DIM), lambda i, j: (layer, i, 0)),
            vecspec,
            pl.BlockSpec((None, D_MODEL, tf), lambda i, j: (layer, 0, j)),
            pl.BlockSpec((None, D_MODEL, tf), lambda i, j: (layer, 0, j)),
            pl.BlockSpec((None, tf, D_MODEL), lambda i, j: (layer, j, 0)),
            vecspec,
            pl.BlockSpec((None, PLE_DIM, D_MODEL), lambda i, j: (layer, 0, 0)),
            pl.BlockSpec((None, D_MODEL, D_MODEL), lambda i, j: (layer, 0, 0)),
            vecspec,
        ],
        out_specs=pl.BlockSpec((tm, D_MODEL), lambda i, j: (i, 0)),
        out_shape=jax.ShapeDtypeStruct((t, D_MODEL), F32),
        scratch_shapes=[pltpu.VMEM((tm, D_MODEL), BF16), pltpu.VMEM((tm, D_MODEL), F32)],
        compiler_params=_params(("parallel", "arbitrary")),
        name="ffn_ple",
    )(x, p, gpre, wg, wu, wd, gpost, wple, wpg, gple)


def _cast_kernel(x_ref, o_ref):
    o_ref[...] = x_ref[...].astype(o_ref.dtype)


def _to_bf16(w, tr):
    d, r, c = w.shape
    spec = pl.BlockSpec((None, tr, c), lambda l, i: (l, i, 0))
    return pl.pallas_call(
        _cast_kernel, grid=(d, r // tr), in_specs=[spec], out_specs=spec,
        out_shape=jax.ShapeDtypeStruct(w.shape, BF16),
        compiler_params=_params(("parallel", "parallel")), name="to_bf16",
    )(w)


W_IN_TILE = 1024


def _w_in_kernel(x_ref, o_ref):
    is_pad = lax.broadcasted_iota(jnp.int32, o_ref.shape, 1) + pl.program_id(1) * W_IN_TILE >= OFF_RW + RW_COLS
    o_ref[...] = jnp.where(is_pad, 0.0, x_ref[0]).astype(o_ref.dtype)


def _w_in_layout(w_in):
    d, _, in_cols = w_in.shape
    n_gate_cols = N_BRANCH * D_MODEL
    n_rest = in_cols - n_gate_cols

    def src(l, j):
        col = j * W_IN_TILE
        start = jnp.where(col < n_gate_cols, col + n_rest, col - n_gate_cols)
        start = jnp.minimum(start, in_cols - W_IN_TILE)
        return l, 0, pl.multiple_of(start, MXU_DIM)

    return pl.pallas_call(
        _w_in_kernel, grid=(d, PROJ_COLS // W_IN_TILE),
        in_specs=[pl.BlockSpec((pl.Element(1), pl.Element(D_MODEL), pl.Element(W_IN_TILE)), src)],
        out_specs=pl.BlockSpec((None, D_MODEL, W_IN_TILE), lambda l, j: (l, 0, j)),
        out_shape=jax.ShapeDtypeStruct((d, D_MODEL, PROJ_COLS), BF16),
        compiler_params=_params(("parallel", "parallel")), name="w_in_layout",
    )(w_in)


def _block_diag(w):
    n, r, c = w.shape
    eye = jnp.eye(n, dtype=w.dtype)
    return (eye[:, None, :, None] * w[:, :, None, :]).reshape(n * r, n * c)


def _prep_layer(i, W):
    lora = jnp.zeros((RW_LORA, 3 * BRANCH_W), F32)
    lora = lora.at[0:RW_LORA_W, 0:BRANCH_W].set(W["rw_w_up"][i])
    lora = lora.at[RW_LORA_W:RW_LORA_W + RW_LORA_A, BRANCH_W:2 * BRANCH_W].set(W["rw_a_up"][i])
    lora = lora.at[RW_LORA_W + RW_LORA_A:, 2 * BRANCH_W:].set(W["rw_g_up"][i])
    vec = jnp.stack([W["rw_w0"][i], W["rw_a0"][i], W["rw_k_k"][i], W["rw_k_a"][i],
                     W["rw_r_k"][i].reshape(BRANCH_W), W["rw_ln_g"][i], W["rw_ln_b"][i],
                     jnp.zeros((BRANCH_W,), F32)])
    row = lambda name: W[name][i].reshape(1, -1)
    return dict(
        norm_pre_mix=row("norm_pre_mix"),
        conv_w=W["conv_w"][i], conv_b=row("conv_b"),
        lru_w=jnp.concatenate([_block_diag(W["lru_wa"][i]), _block_diag(W["lru_wx"][i])], axis=1).astype(BF16),
        lru_b=jnp.stack([W["lru_ba"][i], W["lru_bx"][i]]), lru_lambda=row("lru_lambda"),
        hg_norm_g=row("hg_norm_g"),
        rw_mu=row("rw_mu"), rw_lora=lora.astype(BF16), rw_vec=vec,
        norm_post_mix=row("norm_post_mix"), norm_pre_ffn=row("norm_pre_ffn"),
        norm_post_ffn=row("norm_post_ffn"), norm_ple=row("norm_ple"),
    )


def _tiles(nb, s_len):
    t = nb * s_len
    tm_in = min(t, 2048)
    tm_tok = min(t, 512)
    if s_len >= 512:
        return dict(tm_in=tm_in, tn_in=2048, proj_dtype=BF16, tm_mix=min(t, 1024), tm_ffn=tm_tok, ffn_split=2,
                    lru_tt=128,
                    hg=dict(nbb=2, tt_len=256, chunk=64, ub=2),
                    rw=dict(nbb=8, tt_len=64, chunk=64, ua=4, ub=8))
    return dict(tm_in=tm_in, tn_in=2048, proj_dtype=F32, tm_mix=tm_tok, tm_ffn=tm_tok, ffn_split=2,
                lru_tt=s_len,
                hg=dict(nbb=min(nb, 16), tt_len=s_len, chunk=s_len, ub=4),
                rw=dict(nbb=min(nb, 16), tt_len=s_len, chunk=s_len, ua=8, ub=min(nb, 16)))


def _run_trunk(x3, p4, states, layers, big, lbraw, *, s_valid, pos0_is_zero):
    nb, s_len, _ = x3.shape
    t = nb * s_len
    plan = _tiles(nb, s_len)
    for mix in (plan["hg"], plan["rw"]):
        assert nb % mix["nbb"] == 0 and s_len % mix["tt_len"] == 0 and mix["tt_len"] % mix["chunk"] == 0, mix
    assert t % plan["tm_in"] == 0 and t % plan["tm_mix"] == 0 and t % plan["tm_ffn"] == 0 and s_len % plan["lru_tt"] == 0
    conv0, lru0, hg0, rw0, sh0 = states
    x = x3.reshape(t, D_MODEL)
    new = ([], [], [])
    nhg = nrw = None
    for i, L in enumerate(layers):
        proj = _in_proj(x, L["norm_pre_mix"], big["w_in"], plan["tm_in"], plan["tn_in"], plan["proj_dtype"], i)
        proj3 = proj.reshape(nb, s_len, PROJ_COLS)
        prev8 = jnp.pad(conv0[i], ((0, 0), (SUBLANES - (CONV_W - 1), 0), (0, 0)))
        ya, nlru = _lru(proj3, prev8, lru0[i], L["conv_w"], L["conv_b"], L["lru_w"], L["lru_b"], L["lru_lambda"],
                        tt_len=plan["lru_tt"], s_valid=s_valid, pos0_is_zero=pos0_is_zero)
        yb, nhg = _hgrn(proj3, hg0, nhg, lbraw, L["hg_norm_g"], s_valid=s_valid, layer=i, **plan["hg"])
        yc, nrw = _rwkv(proj3, sh0[i].reshape(nb, 1, RW_COLS), rw0, nrw, L["rw_mu"], L["rw_lora"], L["rw_vec"],
                        s_valid=s_valid, layer=i, **plan["rw"])
        x = _mix(x, proj, ya.reshape(t, BRANCH_W), yb.reshape(t, BRANCH_W), yc.reshape(t, BRANCH_W),
                 big["w_branch"], big["w_out"], L["norm_post_mix"], plan["tm_mix"], i)
        x = _ffn(x, p4.reshape(-1, t, PLE_DIM), L["norm_pre_ffn"], big["w_ffn_gate"], big["w_ffn_up"],
                 big["w_ffn_down"], L["norm_post_ffn"], big["w_ple"], big["w_ple_gate"], L["norm_ple"], plan["tm_ffn"], plan["ffn_split"], i)
        nconv = proj3[:, s_valid - (CONV_W - 1):s_valid, OFF_LRU:OFF_LRU + BRANCH_W].astype(F32)
        nsh = proj3[:, s_valid - 1, OFF_RW:OFF_RW + RW_COLS].astype(F32)
        for lst, val in zip(new, (nconv, nlru, nsh)):
            lst.append(val)
    nconv, nlru, nsh = (jnp.stack(l) for l in new)
    return x.reshape(nb, s_len, D_MODEL), (nconv, nlru, nhg, nrw, nsh)


def kernel(x_prompt, x_sample, p_prompt, p_sample, state_conv_a, state_lru_a, state_hgrn, state_rwkv, state_shift_c, norm_pre_mix, w_in, conv_w, conv_b, lru_wa, lru_ba, lru_wx, lru_bx, lru_lambda, hg_lower_bounds, hg_norm_g, rw_mu, rw_w0, rw_w_up, rw_a0, rw_a_up, rw_g_up, rw_k_k, rw_k_a, rw_r_k, rw_ln_g, rw_ln_b, w_branch, w_out, norm_post_mix, norm_pre_ffn, w_ffn_gate, w_ffn_up, w_ffn_down, norm_post_ffn, w_ple, w_ple_gate, norm_ple):
    W = dict(norm_pre_mix=norm_pre_mix, w_in=w_in, conv_w=conv_w, conv_b=conv_b, lru_wa=lru_wa, lru_ba=lru_ba,
             lru_wx=lru_wx, lru_bx=lru_bx, lru_lambda=lru_lambda, hg_norm_g=hg_norm_g, rw_mu=rw_mu, rw_w0=rw_w0,
             rw_w_up=rw_w_up, rw_a0=rw_a0, rw_a_up=rw_a_up, rw_g_up=rw_g_up, rw_k_k=rw_k_k, rw_k_a=rw_k_a,
             rw_r_k=rw_r_k, rw_ln_g=rw_ln_g, rw_ln_b=rw_ln_b, w_branch=w_branch, w_out=w_out,
             norm_post_mix=norm_post_mix, norm_pre_ffn=norm_pre_ffn, w_ffn_gate=w_ffn_gate, w_ffn_up=w_ffn_up,
             w_ffn_down=w_ffn_down, norm_post_ffn=norm_post_ffn, w_ple=w_ple, w_ple_gate=w_ple_gate, norm_ple=norm_ple)
    depth = w_in.shape[0]
    d_ff = w_ffn_gate.shape[2]
    big = dict(
        w_in=_w_in_layout(w_in),
        w_branch=_to_bf16(w_branch.reshape(depth, N_BRANCH * BRANCH_W, D_MODEL), N_BRANCH * BRANCH_W // 2),
        w_out=_to_bf16(w_out, D_MODEL),
        w_ffn_gate=_to_bf16(w_ffn_gate, D_MODEL // 2), w_ffn_up=_to_bf16(w_ffn_up, D_MODEL // 2),
        w_ffn_down=_to_bf16(w_ffn_down, d_ff // 2),
        w_ple=_to_bf16(w_ple, PLE_DIM), w_ple_gate=_to_bf16(w_ple_gate, D_MODEL),
    )
    layers = [_prep_layer(i, W) for i in range(depth)]
    lbraw = hg_lower_bounds.astype(F32)

    bp, sp, _ = x_prompt.shape
    zeros = lambda *shape: jnp.zeros((depth, bp) + shape, F32)
    zero_states = (zeros(CONV_W - 1, BRANCH_W), zeros(BRANCH_W), zeros(HG_HEADS, HG_D, HG_D),
                   zeros(RW_HEADS, RW_HD, RW_HD), zeros(RW_COLS))
    y_prompt, st_p = _run_trunk(x_prompt, p_prompt, zero_states, layers, big, lbraw, s_valid=sp, pos0_is_zero=True)

    _, ss, _ = x_sample.shape
    ss_pad = -(-ss // SUBLANES) * SUBLANES
    xs = jnp.pad(x_sample, ((0, 0), (0, ss_pad - ss), (0, 0)))
    ps = jnp.pad(p_sample, ((0, 0), (0, 0), (0, ss_pad - ss), (0, 0)))
    y_sample, st_s = _run_trunk(xs, ps, (state_conv_a, state_lru_a, state_hgrn, state_rwkv, state_shift_c),
                                layers, big, lbraw, s_valid=ss, pos0_is_zero=False)
    return (y_prompt, y_sample[:, :ss]) + st_p + st_s
```

```python
import functools
import math

import jax
import jax.numpy as jnp
from jax import lax
from jax.experimental import pallas as pl
from jax.experimental.pallas import tpu as pltpu

F32 = jnp.float32
BF16 = jnp.bfloat16

D_MODEL = 1024
BRANCH_W = 512
N_BRANCH = 3
CONV_W = 4
LRU_C = 8.0
HG_HEADS = 4
HG_D = BRANCH_W // HG_HEADS
HG_F_MIN = 1e-20
RW_HD = 64
RW_HEADS = BRANCH_W // RW_HD
RW_LORA_W = 64
RW_LORA_A = 64
RW_LORA_G = 128
RW_LORA = RW_LORA_W + RW_LORA_A + RW_LORA_G
RW_GN_EPS = 64e-5
RW_COLS = 3 * BRANCH_W + RW_LORA
PLE_DIM = 256
EPS = 1e-6

SUBLANES = 8
LANES = 128
MXU_DIM = 256
VMEM_LIMIT = 56 * 1024 * 1024

PROJ_COLS = 8192
OFF_GATES = 0
OFF_LRU = N_BRANCH * D_MODEL
OFF_HG = OFF_LRU + 2 * BRANCH_W
OFF_RW = OFF_HG + 4 * BRANCH_W
RW_BLOCK = PROJ_COLS - OFF_RW

RW_GROUP = LANES // RW_HD
RW_GW = RW_GROUP * RW_HD


def _params(sem):
    return pltpu.CompilerParams(dimension_semantics=sem, vmem_limit_bytes=VMEM_LIMIT)


def _rms(x, g):
    return x * lax.rsqrt(jnp.mean(x * x, axis=-1, keepdims=True) + EPS) * g


def _sigmoid(x):
    return 1.0 / (1.0 + jnp.exp(-x))


def _sigmoid_t(x):
    return 0.5 * jnp.tanh(0.5 * x) + 0.5


def _softplus(x):
    return jnp.maximum(x, 0.0) + jnp.log1p(jnp.exp(-jnp.abs(x)))


def _dot(a, b):
    return jnp.dot(a.astype(BF16), b.astype(BF16), preferred_element_type=F32)


def _dot_nt(a, b):
    return lax.dot_general(a.astype(BF16), b.astype(BF16), (((1,), (1,)), ((), ())),
                           preferred_element_type=F32)


def _dot_tn(a, b):
    return lax.dot_general(a.astype(BF16), b.astype(BF16), (((0,), (0,)), ((), ())),
                           preferred_element_type=F32)


def _split2(x):
    hi = x.astype(BF16)
    return hi, (x - hi.astype(F32)).astype(BF16)


def _seg_sum(x, seg):
    i = lax.broadcasted_iota(jnp.int32, (MXU_DIM, MXU_DIM), 0)
    j = lax.broadcasted_iota(jnp.int32, (MXU_DIM, MXU_DIM), 1)
    ones = (i // seg == j // seg).astype(BF16)
    xb = x.astype(BF16)
    tiles = [jnp.dot(xb[:, l:l + MXU_DIM], ones, preferred_element_type=F32)
             for l in range(0, x.shape[1], MXU_DIM)]
    return jnp.concatenate(tiles, axis=1)


def _chunk_sums(cum, x, nbb, tt_len):
    hi, lo = _split2(x)
    m = cum.astype(BF16)
    pre, tot = [], []
    for b in range(nbb):
        rs = slice(b * tt_len, (b + 1) * tt_len)
        both = jnp.dot(m, lo[rs], preferred_element_type=F32) + jnp.dot(m, hi[rs], preferred_element_type=F32)
        pre.append(both[:tt_len])
        tot.append(both[tt_len:])
    if nbb == 1:
        return pre[0], tot[0]
    return jnp.concatenate(pre, axis=0), jnp.concatenate(tot, axis=0)


def _in_proj_kernel(x_ref, g_ref, w_ref, o_ref, h_scr):
    @pl.when(pl.program_id(1) == 0)
    def _():
        h_scr[...] = _rms(x_ref[...], g_ref[...]).astype(BF16)

    o_ref[...] = jnp.dot(h_scr[...], w_ref[...], preferred_element_type=F32).astype(o_ref.dtype)


def _in_proj(x, g, w, tm, tn, out_dtype, layer):
    t = x.shape[0]
    return pl.pallas_call(
        _in_proj_kernel,
        grid=(t // tm, PROJ_COLS // tn),
        in_specs=[
            pl.BlockSpec((tm, D_MODEL), lambda i, j: (i, 0)),
            pl.BlockSpec((1, D_MODEL), lambda i, j: (0, 0)),
            pl.BlockSpec((None, D_MODEL, tn), lambda i, j: (layer, 0, j)),
        ],
        out_specs=pl.BlockSpec((tm, tn), lambda i, j: (i, j)),
        out_shape=jax.ShapeDtypeStruct((t, PROJ_COLS), out_dtype),
        scratch_shapes=[pltpu.VMEM((tm, D_MODEL), BF16)],
        compiler_params=_params(("parallel", "arbitrary")),
        name="in_proj",
    )(x, g, w)


def _lru_kernel(xa_ref, ga_ref, prev8_ref, h0_ref, cw_ref, cb_ref, w_ref, bab_ref, lam_ref,
                y_ref, hout_ref, prev_scr, h_scr, *, nb, tt_len, s_valid, s_len, pos0_is_zero):
    tt = pl.program_id(0)
    rows = nb * tt_len

    @pl.when(tt == 0)
    def _():
        prev_scr[...] = prev8_ref[...]
        h_scr[...] = h0_ref[...]

    xa = xa_ref[...].astype(F32)
    n8 = tt_len // SUBLANES
    xa4 = xa.reshape(nb, n8, SUBLANES, BRANCH_W)
    prev4 = prev_scr[...][:, None]
    t8 = lax.broadcasted_iota(jnp.int32, (1, 1, SUBLANES, 1), 2)
    cw = cw_ref[...]
    xc4 = cb_ref[...][None, None] + cw[CONV_W - 1][None, None, None] * xa4
    for j in range(1, CONV_W):
        rot = pltpu.roll(xa4, j, 2)
        rot_before = pltpu.roll(prev4, j, 2)
        if n8 > 1:
            rot_before = jnp.concatenate([rot_before, rot[:, :n8 - 1]], axis=1)
        xc4 = xc4 + cw[CONV_W - 1 - j][None, None, None] * jnp.where(t8 < j, rot_before, rot)
    prev_scr[...] = xa[:, tt_len - SUBLANES:, :]

    xc2 = xc4.reshape(rows, BRANCH_W)
    z = _dot(xc2, w_ref[...])
    bab = bab_ref[...]
    r = _sigmoid_t(z[:, :BRANCH_W] + bab[0:1])
    i = _sigmoid_t(z[:, BRANCH_W:] + bab[1:2])
    log_a = (-LRU_C) * r * _softplus(-lam_ref[...])
    a = jnp.exp(log_a)
    m2 = jnp.maximum(1.0 - a * a, 0.0)
    mult = m2 * lax.rsqrt(jnp.maximum(m2, 1e-30))
    t_in = lax.broadcasted_iota(jnp.int32, (rows, 1), 0) % tt_len
    if pos0_is_zero:
        mult = jnp.where(jnp.logical_and(tt == 0, t_in == 0), 1.0, mult)
    b = xc2 * i * mult
    if s_valid < s_len:
        valid = (tt * tt_len + t_in) < s_valid
        a = jnp.where(valid, a, 1.0)
        b = jnp.where(valid, b, 0.0)
    a4 = a.reshape(nb, n8, SUBLANES, BRANCH_W)
    b4 = b.reshape(nb, n8, SUBLANES, BRANCH_W)

    d = 1
    while d < SUBLANES:
        keep = t8 >= d
        b4 = a4 * jnp.where(keep, pltpu.roll(b4, d, 2), 0.0) + b4
        a4 = a4 * jnp.where(keep, pltpu.roll(a4, d, 2), 1.0)
        d *= 2
    h_in = h_scr[...][:, None, :]
    blocks = []
    for blk in range(n8):
        hb = a4[:, blk] * h_in + b4[:, blk]
        blocks.append(hb)
        h_in = hb[:, SUBLANES - 1:SUBLANES, :]
    hh = blocks[0] if n8 == 1 else jnp.concatenate(blocks, axis=1)
    h = h_in.reshape(nb, BRANCH_W)
    h_scr[...] = h
    hout_ref[...] = h

    ga = ga_ref[...].astype(F32)
    gelu = 0.5 * ga * (1.0 + jnp.tanh(math.sqrt(2.0 / math.pi) * (ga + 0.044715 * ga * ga * ga)))
    y_ref[...] = (hh * gelu).astype(y_ref.dtype)


def _lru(proj3, prev8, h0, cw, cb, w, bab, lam, *, tt_len, s_valid, pos0_is_zero):
    nb, s_len, _ = proj3.shape
    blk = OFF_LRU // BRANCH_W
    kern = functools.partial(_lru_kernel, nb=nb, tt_len=tt_len, s_valid=s_valid, s_len=s_len,
                             pos0_is_zero=pos0_is_zero)
    full2 = lambda t: (0, 0)
    return pl.pallas_call(
        kern,
        grid=(s_len // tt_len,),
        in_specs=[
            pl.BlockSpec((nb, tt_len, BRANCH_W), lambda t: (0, t, blk)),
            pl.BlockSpec((nb, tt_len, BRANCH_W), lambda t: (0, t, blk + 1)),
            pl.BlockSpec((nb, SUBLANES, BRANCH_W), lambda t: (0, 0, 0)),
            pl.BlockSpec((nb, BRANCH_W), full2),
            pl.BlockSpec((CONV_W, BRANCH_W), full2),
            pl.BlockSpec((1, BRANCH_W), full2),
            pl.BlockSpec((BRANCH_W, 2 * BRANCH_W), full2),
            pl.BlockSpec((2, BRANCH_W), full2),
            pl.BlockSpec((1, BRANCH_W), full2),
        ],
        out_specs=[
            pl.BlockSpec((nb, tt_len, BRANCH_W), lambda t: (0, t, 0)),
            pl.BlockSpec((nb, BRANCH_W), full2),
        ],
        out_shape=[
            jax.ShapeDtypeStruct((nb, s_len, BRANCH_W), proj3.dtype),
            jax.ShapeDtypeStruct((nb, BRANCH_W), F32),
        ],
        scratch_shapes=[
            pltpu.VMEM((nb, SUBLANES, BRANCH_W), F32),
            pltpu.VMEM((nb, BRANCH_W), F32),
        ],
        compiler_params=_params(("arbitrary",)),
        name="rglru",
    )(proj3, proj3, prev8, h0, cw, cb, w, bab, lam)


def _own_layer_block(sout_ref, layer):
    if layer > 0:
        return sout_ref
    if sout_ref.shape[0] > 1:
        sout_ref[1:] = jnp.zeros((sout_ref.shape[0] - 1,) + tuple(sout_ref.shape[1:]), sout_ref.dtype)
    return sout_ref.at[0]


def _state_out(s0, sbuf, layer, nbb, tail):
    depth = s0.shape[0]
    zeros = (0,) * len(tail)
    if layer == 0:
        spec = pl.BlockSpec((depth, nbb) + tail, lambda b, t: (0, b) + zeros)
        return spec, [], []
    spec = pl.BlockSpec((None, nbb) + tail, lambda b, t: (layer, b) + zeros)
    return spec, [pl.BlockSpec(memory_space=pl.ANY)], [sbuf]


def _hg_diag_blocks(qh, kh, vh, bch):
    c = qh.shape[0]
    nblk = c // SUBLANES
    q3 = qh.reshape(nblk, SUBLANES, HG_D)
    k3 = kh.reshape(nblk, SUBLANES, HG_D)
    v3 = vh.reshape(nblk, SUBLANES, HG_D)
    b3 = bch.reshape(nblk, SUBLANES, HG_D)
    tin = lax.broadcasted_iota(jnp.int32, (1, SUBLANES, 1), 1)
    o3 = jnp.zeros((nblk, SUBLANES, HG_D), F32)
    for s in range(SUBLANES):
        dec = jnp.exp(jnp.minimum(b3 - b3[:, s:s + 1, :], 0.0))
        w = jnp.sum(q3 * k3[:, s:s + 1, :] * dec, axis=-1, keepdims=True)
        w = jnp.where(tin >= s, w, 0.0)
        o3 = o3 + w * v3[:, s:s + 1, :]
    return o3.reshape(c, HG_D)


def _hg_level_refs(bch, h, c):
    gq, gk = [], []
    zero = jnp.zeros((h, HG_D), F32)
    for j in range(c // h):
        if j % 2 == 1:
            gq.append(jnp.broadcast_to(bch[j * h - 1:j * h, :], (h, HG_D)))
            gk.append(zero)
        else:
            gq.append(zero)
            gk.append(jnp.broadcast_to(bch[(j + 1) * h - 1:(j + 1) * h, :], (h, HG_D)))
    return jnp.concatenate(gq, axis=0), jnp.concatenate(gk, axis=0)


def _hgrn_kernel(q_ref, f_ref, v_ref, g_ref, s0_ref, lbraw_ref, ng_ref, cum_ref, y_ref, sout_ref,
                 st_scr, qs_scr, k_scr, v_scr, bc_scr, qe_scr, kh_scr, et_scr, o_scr,
                 *, nbb, tt_len, chunk, s_valid, s_len, layer, ub):
    tt = pl.program_id(1)
    n_t = pl.num_programs(1)
    c = chunk
    rows = nbb * tt_len
    n_cb = tt_len // c
    kv_major = s_len == c

    raw = lbraw_ref[...]
    ex = jnp.exp(raw - jnp.max(raw, axis=0, keepdims=True))
    sm = ex / jnp.sum(ex, axis=0, keepdims=True)
    lb = jnp.zeros((1, BRANCH_W), F32)
    for l in range(1, layer + 1):
        lb = lb + sm[l:l + 1]

    @pl.when(tt == 0)
    def _():
        def init(bb, carry):
            for hd in range(HG_HEADS):
                st_scr[bb, hd] = s0_ref[bb, hd] if kv_major else s0_ref[bb, hd].T
            return carry

        lax.fori_loop(0, nbb, init, 0)

    q = q_ref[...].astype(F32).reshape(rows, BRANCH_W)
    fp = f_ref[...].astype(F32).reshape(rows, BRANCH_W)
    sg = _sigmoid(fp)
    f = lb + (1.0 - lb) * sg
    k = (1.0 - lb) * (1.0 - sg)
    logf = jnp.log(jnp.maximum(f, HG_F_MIN))
    if s_valid < s_len:
        t_in = lax.broadcasted_iota(jnp.int32, (rows, 1), 0) % tt_len
        valid = (tt * tt_len + t_in) < s_valid
        k = jnp.where(valid, k, 0.0)
        logf = jnp.where(valid, logf, 0.0)
    bc, btot = _chunk_sums(cum_ref[...], logf, nbb, tt_len)
    qs = q * _sigmoid_t(q)
    qs_scr[...] = qs
    k_scr[...] = k
    v_scr[...] = v_ref[...].astype(F32).reshape(rows, BRANCH_W)
    bc_scr[...] = bc
    qe_scr[...] = qs * jnp.exp(bc)
    kh_scr[...] = k * jnp.exp(btot - bc)
    et_scr[...] = jnp.exp(btot)

    ti = lax.broadcasted_iota(jnp.int32, (c, 1), 0)
    ii = lax.broadcasted_iota(jnp.int32, (c, c), 0)
    jj = lax.broadcasted_iota(jnp.int32, (c, c), 1)
    levels = []
    h = c // 2
    while h >= SUBLANES:
        odd = (ti // h) % 2 == 1
        pair = jnp.logical_and(ii // (2 * h) == jj // (2 * h),
                               jnp.logical_and((ii // h) % 2 == 1, (jj // h) % 2 == 0))
        levels.append((h, odd, pair))
        h //= 2

    def step(it, carry):
        cb = it // (nbb // ub)
        b0 = (it % (nbb // ub)) * ub
        chains = []
        for u in range(ub):
            r0 = pl.multiple_of((b0 + u) * tt_len + cb * c, c)
            for hd in range(HG_HEADS):
                ln = slice(hd * HG_D, (hd + 1) * HG_D)
                chains.append(dict(bb=b0 + u, hd=hd, r0=r0, ln=ln, st=st_scr[b0 + u, hd],
                                   qh=qs_scr[pl.ds(r0, c), ln], kh=k_scr[pl.ds(r0, c), ln],
                                   vh=v_scr[pl.ds(r0, c), ln], bch=bc_scr[pl.ds(r0, c), ln]))
        if kv_major:
            outs = [_dot(qe_scr[pl.ds(x["r0"], c), x["ln"]], x["st"]) for x in chains]
        else:
            outs = [_dot_nt(qe_scr[pl.ds(x["r0"], c), x["ln"]], x["st"]) for x in chains]
        if levels:
            amats = []
            for x in chains:
                amat = None
                for (h, odd, pair) in levels:
                    gq, gk = _hg_level_refs(x["bch"], h, c)
                    qt = jnp.where(odd, x["qh"] * jnp.exp(jnp.where(odd, x["bch"] - gq, 0.0)), 0.0)
                    kt = jnp.where(odd, 0.0, x["kh"] * jnp.exp(jnp.where(odd, 0.0, gk - x["bch"])))
                    term = jnp.where(pair, _dot_nt(qt, kt), 0.0)
                    amat = term if amat is None else amat + term
                amats.append(amat)
            outs = [o + _dot(a, x["vh"]) for o, a, x in zip(outs, amats, chains)]
        for o, x in zip(outs, chains):
            o_scr[pl.ds(x["r0"], c), x["ln"]] = o + _hg_diag_blocks(x["qh"], x["kh"], x["vh"], x["bch"])
        for x in chains:
            khat = kh_scr[pl.ds(x["r0"], c), x["ln"]]
            if kv_major:
                last = jnp.where(ti == c - 1, x["bch"], 0.0)
                hi, lo = _split2(last)
                ones = jnp.ones((c, HG_D), BF16)
                col = (lax.dot_general(hi, ones, (((0,), (0,)), ((), ())), preferred_element_type=F32)
                       + lax.dot_general(lo, ones, (((0,), (0,)), ((), ())), preferred_element_type=F32))
                st_scr[x["bb"], x["hd"]] = x["st"] * jnp.exp(col) + _dot_tn(khat, x["vh"])
            else:
                st_scr[x["bb"], x["hd"]] = x["st"] * et_scr[pl.ds(x["r0"], 1), x["ln"]] + _dot_tn(x["vh"], khat)
        return carry

    n_it = n_cb * (nbb // ub)
    if n_it == 1:
        step(0, 0)
    else:
        lax.fori_loop(0, n_it, step, 0)

    g = g_ref[...].astype(F32).reshape(rows, BRANCH_W)
    ng = ng_ref[...]
    outs = []
    for hd in range(HG_HEADS):
        ln = slice(hd * HG_D, (hd + 1) * HG_D)
        o = o_scr[:, ln]
        outs.append(o * lax.rsqrt(jnp.mean(o * o, axis=-1, keepdims=True) + EPS) * ng[:, ln])
    y = jnp.concatenate(outs, axis=-1) * (g * _sigmoid_t(g))
    y_ref[...] = y.reshape(nbb, tt_len, BRANCH_W).astype(y_ref.dtype)

    @pl.when(tt == n_t - 1)
    def _():
        out = _own_layer_block(sout_ref, layer)

        def fin(bb, carry):
            for hd in range(HG_HEADS):
                out[bb, hd] = st_scr[bb, hd] if kv_major else st_scr[bb, hd].T
            return carry

        lax.fori_loop(0, nbb, fin, 0)


def _hgrn_kernel_inplace(q_ref, f_ref, v_ref, g_ref, s0_ref, lbraw_ref, ng_ref, cum_ref, sbuf_ref, *rest, **kw):
    _hgrn_kernel(q_ref, f_ref, v_ref, g_ref, s0_ref, lbraw_ref, ng_ref, cum_ref, *rest, **kw)


def _chunk_cum_matrix(rows, c):
    i = jnp.arange(rows)[:, None]
    j = jnp.arange(rows)[None, :]
    same = i // c == j // c
    return jnp.concatenate([same & (i >= j), same], axis=0).astype(BF16)


def _hgrn(proj3, s0, sbuf, lbraw, ng, *, nbb, tt_len, chunk, s_valid, layer, ub):
    nb, s_len, _ = proj3.shape
    blk = OFF_HG // BRANCH_W
    rows = nbb * tt_len
    kern = functools.partial(_hgrn_kernel if layer == 0 else _hgrn_kernel_inplace, nbb=nbb, tt_len=tt_len,
                             chunk=chunk, s_valid=s_valid, s_len=s_len, layer=layer, ub=ub)
    seq = lambda k: pl.BlockSpec((nbb, tt_len, BRANCH_W), lambda b, t, k=k: (b, t, blk + k))
    st_in = pl.BlockSpec((None, nbb, HG_HEADS, HG_D, HG_D), lambda b, t: (layer, b, 0, 0, 0))
    st_out, extra_specs, extra_args = _state_out(s0, sbuf, layer, nbb, (HG_HEADS, HG_D, HG_D))
    return pl.pallas_call(
        kern,
        grid=(nb // nbb, s_len // tt_len),
        in_specs=[seq(0), seq(1), seq(2), seq(3), st_in,
                  pl.BlockSpec(lbraw.shape, lambda b, t: (0, 0)),
                  pl.BlockSpec((1, BRANCH_W), lambda b, t: (0, 0)),
                  pl.BlockSpec((2 * tt_len, tt_len), lambda b, t: (0, 0))] + extra_specs,
        out_specs=[pl.BlockSpec((nbb, tt_len, BRANCH_W), lambda b, t: (b, t, 0)), st_out],
        out_shape=[jax.ShapeDtypeStruct((nb, s_len, BRANCH_W), proj3.dtype),
                   jax.ShapeDtypeStruct(s0.shape, F32)],
        input_output_aliases={8: 1} if extra_args else {},
        scratch_shapes=[pltpu.VMEM((nbb, HG_HEADS, HG_D, HG_D), F32)]
        + [pltpu.VMEM((rows, BRANCH_W), F32)] * 8,
        compiler_params=_params(("parallel", "arbitrary")),
        name="hgrn2",
    )(proj3, proj3, proj3, proj3, s0, lbraw, ng, _chunk_cum_matrix(tt_len, chunk), *extra_args)


def _rwkv_kernel(c_ref, sh0_ref, s0_ref, mu_ref, wl_ref, vec_ref, cum_ref, y_ref, sout_ref,
                 sbd_scr, carry_scr, at_scr, rt_scr, bt_scr, kt_scr, bp_scr, kp_scr, v_scr, pc_scr,
                 bonus_scr, gate_scr, o_scr, tcat_scr, aak_scr, arb_scr, ark_scr,
                 *, nbb, tt_len, chunk, s_valid, s_len, layer, ua, ub):
    tt = pl.program_id(1)
    n_t = pl.num_programs(1)
    c = chunk
    c4 = RW_GROUP * c
    n_groups = RW_HEADS // RW_GROUP
    rows = nbb * tt_len
    n_cb = tt_len // c
    n_ch = rows // c

    vec = vec_ref[...]
    w0, a0, k_k, k_a, r_k, ln_g, ln_b = [vec[i:i + 1] for i in range(7)]
    gi_ = lax.broadcasted_iota(jnp.int32, (RW_GW, RW_GW), 0)
    gj_ = lax.broadcasted_iota(jnp.int32, (RW_GW, RW_GW), 1)
    bd_state = gi_ // RW_HD == gj_ // RW_HD

    @pl.when(tt == 0)
    def _():
        carry_scr[...] = sh0_ref[...]

        def init(bb, carry):
            zero = jnp.zeros((RW_HD, RW_HD), F32)
            for g in range(n_groups):
                blocks = [jnp.concatenate([s0_ref[bb, g * RW_GROUP + h] if j == h else zero for j in range(RW_GROUP)],
                                          axis=1) for h in range(RW_GROUP)]
                sbd_scr[bb, g] = jnp.concatenate(blocks, axis=0)
            return carry

        lax.fori_loop(0, nbb, init, 0)

    cc3 = c_ref[:, :, :RW_COLS].astype(F32)
    t3 = lax.broadcasted_iota(jnp.int32, (1, tt_len, 1), 1)
    prev3 = jnp.where(t3 == 0, carry_scr[...], pltpu.roll(cc3, 1, 1))
    carry_scr[...] = cc3[:, tt_len - 1:tt_len, :]
    xm = (cc3 + (prev3 - cc3) * mu_ref[...][None]).reshape(rows, RW_COLS)
    r = xm[:, 0:BRANCH_W]
    k = xm[:, BRANCH_W:2 * BRANCH_W]
    v = xm[:, 2 * BRANCH_W:3 * BRANCH_W]
    lo = xm[:, 3 * BRANCH_W:]
    lane_l = lax.broadcasted_iota(jnp.int32, (1, RW_LORA), 1)
    act = jnp.where(lane_l < RW_LORA_W, jnp.tanh(lo),
                    jnp.where(lane_l < RW_LORA_W + RW_LORA_A, lo, _sigmoid_t(lo)))
    z = _dot(act, wl_ref[...])
    ld = (-math.exp(-0.5)) * _sigmoid_t(w0 + z[:, 0:BRANCH_W])
    a = _sigmoid_t(a0 + z[:, BRANCH_W:2 * BRANCH_W])
    kk = k * k_k
    kbar = k * (1.0 + (a - 1.0) * k_a)
    sums = _seg_sum(jnp.concatenate([kk * kk, r * kbar * r_k], axis=0), RW_HD)
    kap = kk * lax.rsqrt(jnp.maximum(sums[:rows], 1e-24))
    if s_valid < s_len:
        t_in = lax.broadcasted_iota(jnp.int32, (rows, 1), 0) % tt_len
        valid = (tt * tt_len + t_in) < s_valid
        ld = jnp.where(valid, ld, 0.0)
        kap = jnp.where(valid, kap, 0.0)
        kbar = jnp.where(valid, kbar, 0.0)
    lw, ltot = _chunk_sums(cum_ref[...], ld, nbb, tt_len)
    back = ltot - lw
    e_in = jnp.exp(lw)
    e_neg = jnp.exp(-lw)
    e_back = jnp.exp(back)
    at_scr[...] = -kap * jnp.exp(lw - ld)
    rt_scr[...] = r * e_in
    bt_scr[...] = kap * a * e_neg
    kt_scr[...] = kbar * e_neg
    bp_scr[...] = kap * a * e_back
    kp_scr[...] = kbar * e_back
    v_scr[...] = v
    pc_scr[...] = jnp.exp(ltot)
    bonus_scr[...] = sums[rows:]
    gate_scr[...] = z[:, 2 * BRANCH_W:]

    si = lax.broadcasted_iota(jnp.int32, (c4, RW_GW), 0)
    sj = lax.broadcasted_iota(jnp.int32, (c4, RW_GW), 1)
    head_rows = si // c == sj // RW_HD
    qi = lax.broadcasted_iota(jnp.int32, (c4, c4), 0)
    qj = lax.broadcasted_iota(jnp.int32, (c4, c4), 1)
    same = qi // c == qj // c
    strict = jnp.logical_and(same, qi % c > qj % c)
    incl = jnp.logical_and(same, qi % c >= qj % c)
    eye = (qi == qj).astype(F32)
    fuse_sq = c4 % LANES == 0

    def stack(x):
        return jnp.where(head_rows, jnp.concatenate([x] * RW_GROUP, axis=0), 0.0)

    def unstack(x):
        out = x[0:c]
        for h in range(1, RW_GROUP):
            out = out + x[h * c:(h + 1) * c]
        return out

    def phase_a(it, carry):
        chains = [(it * ua + u, g) for u in range(ua) for g in range(n_groups)]
        nmats = []
        for ch, g in chains:
            r0 = pl.multiple_of(ch * c, c)
            ln = slice(g * RW_GW, (g + 1) * RW_GW)
            lhs = jnp.concatenate([stack(at_scr[pl.ds(r0, c), ln]), stack(rt_scr[pl.ds(r0, c), ln])], axis=0)
            rhs = jnp.concatenate([bt_scr[pl.ds(r0, c), ln]] * RW_GROUP + [kt_scr[pl.ds(r0, c), ln]] * RW_GROUP,
                                  axis=0)
            quad = _dot_nt(lhs, rhs)
            nmats.append(jnp.where(strict, quad[:c4, :c4], 0.0))
            aak_scr[ch, g] = unstack(jnp.where(strict, quad[:c4, c4:], 0.0))
            arb_scr[ch, g] = unstack(jnp.where(incl, quad[c4:, :c4], 0.0))
            ark_scr[ch, g] = unstack(jnp.where(incl, quad[c4:, c4:], 0.0))
        tinvs = [eye + n for n in nmats]
        npows = [_dot(n, n) for n in nmats]
        span = 2
        while 2 * span < c:
            if fuse_sq:
                boths = [_dot(p, jnp.concatenate([p, t], axis=1)) for p, t in zip(npows, tinvs)]
                tinvs = [t + bo[:, c4:] for t, bo in zip(tinvs, boths)]
                npows = [bo[:, :c4] for bo in boths]
            else:
                tinvs = [t + _dot(p, t) for p, t in zip(npows, tinvs)]
                npows = [_dot(p, p) for p in npows]
            span *= 2
        tinvs = [t + _dot(p, t) for p, t in zip(npows, tinvs)]
        for (ch, g), t in zip(chains, tinvs):
            tcat_scr[ch, g] = unstack(t)
        return carry

    if n_ch // ua == 1:
        phase_a(0, 0)
    else:
        lax.fori_loop(0, n_ch // ua, phase_a, 0)

    def phase_b(it, carry):
        cb = it // (nbb // ub)
        b0 = (it % (nbb // ub)) * ub
        chains = [(b0 + u, g) for u in range(ub) for g in range(n_groups)]
        ops = []
        for bb, g in chains:
            ch = bb * n_cb + cb
            r0 = pl.multiple_of(ch * c, c)
            ln = slice(g * RW_GW, (g + 1) * RW_GW)
            ops.append(dict(bb=bb, g=g, ch=ch, r0=r0, ln=ln, sbd=sbd_scr[bb, g],
                            v_bd=stack(v_scr[pl.ds(r0, c), ln])))
        sprods = [_dot_nt(jnp.concatenate([at_scr[pl.ds(q["r0"], c), q["ln"]], rt_scr[pl.ds(q["r0"], c), q["ln"]]],
                                          axis=0), q["sbd"]) for q in ops]
        wmats = [sp[:c] + _dot(aak_scr[q["ch"], q["g"]], q["v_bd"]) for q, sp in zip(ops, sprods)]
        us = [_dot(tcat_scr[q["ch"], q["g"]], stack(w)) for q, w in zip(ops, wmats)]
        for q, u, sp in zip(ops, us, sprods):
            r0, ln = q["r0"], q["ln"]
            o_scr[pl.ds(r0, c), ln] = (sp[c:] + _dot(arb_scr[q["ch"], q["g"]], stack(u))
                                       + _dot(ark_scr[q["ch"], q["g"]], q["v_bd"]))
        for q, u in zip(ops, us):
            r0, ln = q["r0"], q["ln"]
            upd = _dot_tn(jnp.concatenate([u, v_scr[pl.ds(r0, c), ln]], axis=0),
                          jnp.concatenate([bp_scr[pl.ds(r0, c), ln], kp_scr[pl.ds(r0, c), ln]], axis=0))
            sbd_scr[q["bb"], q["g"]] = q["sbd"] * pc_scr[pl.ds(r0, 1), ln] + jnp.where(bd_state, upd, 0.0)
        return carry

    n_it = n_cb * (nbb // ub)
    if n_it == 1:
        phase_b(0, 0)
    else:
        lax.fori_loop(0, n_it, phase_b, 0)

    o = o_scr[...]
    inv_n = 1.0 / RW_HD
    mean = _seg_sum(o, RW_HD) * inv_n
    cen = o - mean
    var = _seg_sum(cen * cen, RW_HD) * inv_n
    on = cen * lax.rsqrt(var + RW_GN_EPS) * ln_g + ln_b
    y = (on + bonus_scr[...] * v_scr[...]) * gate_scr[...]
    y_ref[...] = y.reshape(nbb, tt_len, BRANCH_W).astype(y_ref.dtype)

    @pl.when(tt == n_t - 1)
    def _():
        out = _own_layer_block(sout_ref, layer)

        def fin(bb, carry):
            for g in range(n_groups):
                sbd = sbd_scr[bb, g]
                for h in range(RW_GROUP):
                    out[bb, g * RW_GROUP + h] = sbd[h * RW_HD:(h + 1) * RW_HD, h * RW_HD:(h + 1) * RW_HD]
            return carry

        lax.fori_loop(0, nbb, fin, 0)


def _rwkv_kernel_inplace(c_ref, sh0_ref, s0_ref, mu_ref, wl_ref, vec_ref, cum_ref, sbuf_ref, *rest, **kw):
    _rwkv_kernel(c_ref, sh0_ref, s0_ref, mu_ref, wl_ref, vec_ref, cum_ref, *rest, **kw)


def _rwkv(proj3, sh0, s0, sbuf, mu, wl, vec, *, nbb, tt_len, chunk, s_valid, layer, ua, ub):
    nb, s_len, _ = proj3.shape
    kern = functools.partial(_rwkv_kernel if layer == 0 else _rwkv_kernel_inplace, nbb=nbb, tt_len=tt_len,
                             chunk=chunk, s_valid=s_valid, s_len=s_len, layer=layer, ua=ua, ub=ub)
    st_in = pl.BlockSpec((None, nbb, RW_HEADS, RW_HD, RW_HD), lambda b, t: (layer, b, 0, 0, 0))
    st_out, extra_specs, extra_args = _state_out(s0, sbuf, layer, nbb, (RW_HEADS, RW_HD, RW_HD))
    full2 = lambda b, t: (0, 0)
    rows = nbb * tt_len
    n_groups = RW_HEADS // RW_GROUP
    mats = pltpu.VMEM((rows // chunk, n_groups, chunk, RW_GROUP * chunk), F32)
    return pl.pallas_call(
        kern,
        grid=(nb // nbb, s_len // tt_len),
        in_specs=[
            pl.BlockSpec((nbb, tt_len, RW_BLOCK), lambda b, t: (b, t, OFF_RW // RW_BLOCK)),
            pl.BlockSpec((nbb, 1, RW_COLS), lambda b, t: (b, 0, 0)),
            st_in,
            pl.BlockSpec((1, RW_COLS), full2),
            pl.BlockSpec((RW_LORA, 3 * BRANCH_W), full2),
            pl.BlockSpec((SUBLANES, BRANCH_W), full2),
            pl.BlockSpec((2 * tt_len, tt_len), full2),
        ] + extra_specs,
        out_specs=[pl.BlockSpec((nbb, tt_len, BRANCH_W), lambda b, t: (b, t, 0)), st_out],
        out_shape=[jax.ShapeDtypeStruct((nb, s_len, BRANCH_W), proj3.dtype),
                   jax.ShapeDtypeStruct(s0.shape, F32)],
        input_output_aliases={7: 1} if extra_args else {},
        scratch_shapes=[
            pltpu.VMEM((nbb, n_groups, RW_GW, RW_GW), F32),
            pltpu.VMEM((nbb, 1, RW_COLS), F32),
        ] + [pltpu.VMEM((rows, BRANCH_W), F32)] * 11 + [mats] * 4,
        compiler_params=_params(("parallel", "arbitrary")),
        name="rwkv7",
    )(proj3, sh0, s0, mu, wl, vec, _chunk_cum_matrix(tt_len, chunk), *extra_args)


def _mix_kernel(x_ref, gts_ref, ya_ref, yb_ref, yc_ref, wb_ref, wo_ref, g_ref, gn_ref, o_ref, h_ref):
    acc = None
    for n, y_ref in enumerate((ya_ref, yb_ref, yc_ref)):
        up = _dot(y_ref[...], wb_ref[n * BRANCH_W:(n + 1) * BRANCH_W, :])
        term = _sigmoid_t(gts_ref[:, n * D_MODEL:(n + 1) * D_MODEL].astype(F32)) * up
        acc = term if acc is None else acc + term
    mix = _dot(acc, wo_ref[...])
    x1 = x_ref[...] + _rms(mix, g_ref[...])
    o_ref[...] = x1
    h_ref[...] = _rms(x1, gn_ref[...]).astype(BF16)


def _mix(x, proj, ya, yb, yc, wb, wo, g, gn, tm, layer):
    t = x.shape[0]
    row = lambda w: pl.BlockSpec((tm, w), lambda i: (i, 0))
    return pl.pallas_call(
        _mix_kernel,
        grid=(t // tm,),
        in_specs=[row(D_MODEL), row(N_BRANCH * D_MODEL), row(BRANCH_W), row(BRANCH_W), row(BRANCH_W),
                  pl.BlockSpec((None, N_BRANCH * BRANCH_W, D_MODEL), lambda i: (layer, 0, 0),
                               pipeline_mode=pl.Buffered(1)),
                  pl.BlockSpec((None, D_MODEL, D_MODEL), lambda i: (layer, 0, 0), pipeline_mode=pl.Buffered(1)),
                  pl.BlockSpec((1, D_MODEL), lambda i: (0, 0)),
                  pl.BlockSpec((1, D_MODEL), lambda i: (0, 0))],
        out_specs=[row(D_MODEL), row(D_MODEL)],
        out_shape=[jax.ShapeDtypeStruct((t, D_MODEL), F32), jax.ShapeDtypeStruct((t, D_MODEL), BF16)],
        compiler_params=_params(("parallel",)),
        name="branch_mix",
    )(x, proj, ya, yb, yc, wb, wo, g, gn)


def _ffn_kernel(x_ref, p_ref, h_ref, wg_ref, wu_ref, wd_ref, gpost_ref, wple_ref, wpg_ref, gple_ref,
                o_ref, acc_scr):
    j = pl.program_id(1)

    @pl.when(j == 0)
    def _():
        acc_scr[...] = jnp.zeros_like(acc_scr)

    h = h_ref[...]
    gt = jnp.dot(h, wg_ref[...], preferred_element_type=F32)
    up = jnp.dot(h, wu_ref[...], preferred_element_type=F32)
    acc_scr[...] += _dot(gt * _sigmoid_t(gt) * up, wd_ref[...])

    @pl.when(j == pl.num_programs(1) - 1)
    def _():
        x2 = x_ref[...] + _rms(acc_scr[...], gpost_ref[...])
        ple = _dot(p_ref[...], wple_ref[...]) * _sigmoid_t(_dot(x2, wpg_ref[...]))
        o_ref[...] = x2 + _rms(ple, gple_ref[...])


def _ffn(x, p, h, wg, wu, wd, gpost, wple, wpg, gple, tm, n_split, layer):
    t = x.shape[0]
    d_ff = wg.shape[2]
    tf = d_ff // n_split
    vecspec = pl.BlockSpec((1, D_MODEL), lambda i, j: (0, 0))
    return pl.pallas_call(
        _ffn_kernel,
        grid=(t // tm, n_split),
        in_specs=[
            pl.BlockSpec((tm, D_MODEL), lambda i, j: (i, 0)),
            pl.BlockSpec((None, tm, PLE_DIM), lambda i, j: (layer, i, 0)),
            pl.BlockSpec((tm, D_MODEL), lambda i, j: (i, 0)),
            pl.BlockSpec((None, D_MODEL, tf), lambda i, j: (layer, 0, j)),
            pl.BlockSpec((None, D_MODEL, tf), lambda i, j: (layer, 0, j)),
            pl.BlockSpec((None, tf, D_MODEL), lambda i, j: (layer, j, 0)),
            vecspec,
            pl.BlockSpec((None, PLE_DIM, D_MODEL), lambda i, j: (layer, 0, 0)),
            pl.BlockSpec((None, D_MODEL, D_MODEL), lambda i, j: (layer, 0, 0)),
            vecspec,
        ],
        out_specs=pl.BlockSpec((tm, D_MODEL), lambda i, j: (i, 0)),
        out_shape=jax.ShapeDtypeStruct((t, D_MODEL), F32),
        scratch_shapes=[pltpu.VMEM((tm, D_MODEL), F32)],
        compiler_params=_params(("parallel", "arbitrary")),
        name="ffn_ple",
    )(x, p, h, wg, wu, wd, gpost, wple, wpg, gple)


def _cast_kernel(x_ref, o_ref):
    o_ref[...] = x_ref[...].astype(o_ref.dtype)


def _to_bf16(w, tr):
    d, r, c = w.shape
    spec = pl.BlockSpec((None, tr, c), lambda l, i: (l, i, 0))
    return pl.pallas_call(
        _cast_kernel, grid=(d, r // tr), in_specs=[spec], out_specs=spec,
        out_shape=jax.ShapeDtypeStruct(w.shape, BF16),
        compiler_params=_params(("parallel", "parallel")), name="to_bf16",
    )(w)


W_IN_TILE = 1024


def _w_in_kernel(x_ref, o_ref):
    is_pad = lax.broadcasted_iota(jnp.int32, o_ref.shape, 1) + pl.program_id(1) * W_IN_TILE >= OFF_RW + RW_COLS
    o_ref[...] = jnp.where(is_pad, 0.0, x_ref[0]).astype(o_ref.dtype)


def _w_in_layout(w_in):
    d, _, in_cols = w_in.shape
    n_gate_cols = N_BRANCH * D_MODEL
    n_rest = in_cols - n_gate_cols

    def src(l, j):
        col = j * W_IN_TILE
        start = jnp.where(col < n_gate_cols, col + n_rest, col - n_gate_cols)
        start = jnp.minimum(start, in_cols - W_IN_TILE)
        return l, 0, pl.multiple_of(start, MXU_DIM)

    return pl.pallas_call(
        _w_in_kernel, grid=(d, PROJ_COLS // W_IN_TILE),
        in_specs=[pl.BlockSpec((pl.Element(1), pl.Element(D_MODEL), pl.Element(W_IN_TILE)), src)],
        out_specs=pl.BlockSpec((None, D_MODEL, W_IN_TILE), lambda l, j: (l, 0, j)),
        out_shape=jax.ShapeDtypeStruct((d, D_MODEL, PROJ_COLS), BF16),
        compiler_params=_params(("parallel", "parallel")), name="w_in_layout",
    )(w_in)


def _block_diag(w):
    n, r, c = w.shape
    eye = jnp.eye(n, dtype=w.dtype)
    return (eye[:, None, :, None] * w[:, :, None, :]).reshape(n * r, n * c)


def _prep_layer(i, W):
    lora = jnp.zeros((RW_LORA, 3 * BRANCH_W), F32)
    lora = lora.at[0:RW_LORA_W, 0:BRANCH_W].set(W["rw_w_up"][i])
    lora = lora.at[RW_LORA_W:RW_LORA_W + RW_LORA_A, BRANCH_W:2 * BRANCH_W].set(W["rw_a_up"][i])
    lora = lora.at[RW_LORA_W + RW_LORA_A:, 2 * BRANCH_W:].set(W["rw_g_up"][i])
    vec = jnp.stack([W["rw_w0"][i], W["rw_a0"][i], W["rw_k_k"][i], W["rw_k_a"][i],
                     W["rw_r_k"][i].reshape(BRANCH_W), W["rw_ln_g"][i], W["rw_ln_b"][i],
                     jnp.zeros((BRANCH_W,), F32)])
    row = lambda name: W[name][i].reshape(1, -1)
    return dict(
        norm_pre_mix=row("norm_pre_mix"),
        conv_w=W["conv_w"][i], conv_b=row("conv_b"),
        lru_w=jnp.concatenate([_block_diag(W["lru_wa"][i]), _block_diag(W["lru_wx"][i])], axis=1).astype(BF16),
        lru_b=jnp.stack([W["lru_ba"][i], W["lru_bx"][i]]), lru_lambda=row("lru_lambda"),
        hg_norm_g=row("hg_norm_g"),
        rw_mu=row("rw_mu"), rw_lora=lora.astype(BF16), rw_vec=vec,
        norm_post_mix=row("norm_post_mix"), norm_pre_ffn=row("norm_pre_ffn"),
        norm_post_ffn=row("norm_post_ffn"), norm_ple=row("norm_ple"),
    )


def _tiles(nb, s_len):
    t = nb * s_len
    tm_in = min(t, 2048)
    tm_tok = min(t, 512)
    if s_len >= 512:
        return dict(tm_in=tm_in, tn_in=2048, proj_dtype=BF16, tm_mix=min(t, 1024), tm_ffn=tm_tok, ffn_split=2,
                    lru_tt=128,
                    hg=dict(nbb=2, tt_len=256, chunk=64, ub=2),
                    rw=dict(nbb=8, tt_len=64, chunk=64, ua=4, ub=8))
    return dict(tm_in=tm_in, tn_in=2048, proj_dtype=F32, tm_mix=tm_tok, tm_ffn=tm_tok, ffn_split=2,
                lru_tt=s_len,
                hg=dict(nbb=min(nb, 16), tt_len=s_len, chunk=s_len, ub=4),
                rw=dict(nbb=min(nb, 16), tt_len=s_len, chunk=s_len, ua=8, ub=min(nb, 16)))


def _run_trunk(x3, p4, states, layers, big, lbraw, *, s_valid, pos0_is_zero):
    nb, s_len, _ = x3.shape
    t = nb * s_len
    plan = _tiles(nb, s_len)
    for mix in (plan["hg"], plan["rw"]):
        assert nb % mix["nbb"] == 0 and s_len % mix["tt_len"] == 0 and mix["tt_len"] % mix["chunk"] == 0, mix
    assert t % plan["tm_in"] == 0 and t % plan["tm_mix"] == 0 and t % plan["tm_ffn"] == 0 and s_len % plan["lru_tt"] == 0
    conv0, lru0, hg0, rw0, sh0 = states
    x = x3.reshape(t, D_MODEL)
    new = ([], [], [])
    nhg = nrw = None
    for i, L in enumerate(layers):
        proj = _in_proj(x, L["norm_pre_mix"], big["w_in"], plan["tm_in"], plan["tn_in"], plan["proj_dtype"], i)
        proj3 = proj.reshape(nb, s_len, PROJ_COLS)
        prev8 = jnp.pad(conv0[i], ((0, 0), (SUBLANES - (CONV_W - 1), 0), (0, 0)))
        ya, nlru = _lru(proj3, prev8, lru0[i], L["conv_w"], L["conv_b"], L["lru_w"], L["lru_b"], L["lru_lambda"],
                        tt_len=plan["lru_tt"], s_valid=s_valid, pos0_is_zero=pos0_is_zero)
        yb, nhg = _hgrn(proj3, hg0, nhg, lbraw, L["hg_norm_g"], s_valid=s_valid, layer=i, **plan["hg"])
        yc, nrw = _rwkv(proj3, sh0[i].reshape(nb, 1, RW_COLS), rw0, nrw, L["rw_mu"], L["rw_lora"], L["rw_vec"],
                        s_valid=s_valid, layer=i, **plan["rw"])
        x, h = _mix(x, proj, ya.reshape(t, BRANCH_W), yb.reshape(t, BRANCH_W), yc.reshape(t, BRANCH_W),
                    big["w_branch"], big["w_out"], L["norm_post_mix"], L["norm_pre_ffn"], plan["tm_mix"], i)
        x = _ffn(x, p4.reshape(-1, t, PLE_DIM), h, big["w_ffn_gate"], big["w_ffn_up"],
                 big["w_ffn_down"], L["norm_post_ffn"], big["w_ple"], big["w_ple_gate"], L["norm_ple"], plan["tm_ffn"], plan["ffn_split"], i)
        nconv = proj3[:, s_valid - (CONV_W - 1):s_valid, OFF_LRU:OFF_LRU + BRANCH_W].astype(F32)
        nsh = proj3[:, s_valid - 1, OFF_RW:OFF_RW + RW_COLS].astype(F32)
        for lst, val in zip(new, (nconv, nlru, nsh)):
            lst.append(val)
    nconv, nlru, nsh = (jnp.stack(l) for l in new)
    return x.reshape(nb, s_len, D_MODEL), (nconv, nlru, nhg, nrw, nsh)


def kernel(x_prompt, x_sample, p_prompt, p_sample, state_conv_a, state_lru_a, state_hgrn, state_rwkv, state_shift_c, norm_pre_mix, w_in, conv_w, conv_b, lru_wa, lru_ba, lru_wx, lru_bx, lru_lambda, hg_lower_bounds, hg_norm_g, rw_mu, rw_w0, rw_w_up, rw_a0, rw_a_up, rw_g_up, rw_k_k, rw_k_a, rw_r_k, rw_ln_g, rw_ln_b, w_branch, w_out, norm_post_mix, norm_pre_ffn, w_ffn_gate, w_ffn_up, w_ffn_down, norm_post_ffn, w_ple, w_ple_gate, norm_ple):
    W = dict(norm_pre_mix=norm_pre_mix, w_in=w_in, conv_w=conv_w, conv_b=conv_b, lru_wa=lru_wa, lru_ba=lru_ba,
             lru_wx=lru_wx, lru_bx=lru_bx, lru_lambda=lru_lambda, hg_norm_g=hg_norm_g, rw_mu=rw_mu, rw_w0=rw_w0,
             rw_w_up=rw_w_up, rw_a0=rw_a0, rw_a_up=rw_a_up, rw_g_up=rw_g_up, rw_k_k=rw_k_k, rw_k_a=rw_k_a,
             rw_r_k=rw_r_k, rw_ln_g=rw_ln_g, rw_ln_b=rw_ln_b, w_branch=w_branch, w_out=w_out,
             norm_post_mix=norm_post_mix, norm_pre_ffn=norm_pre_ffn, w_ffn_gate=w_ffn_gate, w_ffn_up=w_ffn_up,
             w_ffn_down=w_ffn_down, norm_post_ffn=norm_post_ffn, w_ple=w_ple, w_ple_gate=w_ple_gate, norm_ple=norm_ple)
    depth = w_in.shape[0]
    d_ff = w_ffn_gate.shape[2]
    big = dict(
        w_in=_w_in_layout(w_in),
        w_branch=_to_bf16(w_branch.reshape(depth, N_BRANCH * BRANCH_W, D_MODEL), N_BRANCH * BRANCH_W // 2),
        w_out=_to_bf16(w_out, D_MODEL),
        w_ffn_gate=_to_bf16(w_ffn_gate, D_MODEL // 2), w_ffn_up=_to_bf16(w_ffn_up, D_MODEL // 2),
        w_ffn_down=_to_bf16(w_ffn_down, d_ff // 2),
        w_ple=_to_bf16(w_ple, PLE_DIM), w_ple_gate=_to_bf16(w_ple_gate, D_MODEL),
    )
    layers = [_prep_layer(i, W) for i in range(depth)]
    lbraw = hg_lower_bounds.astype(F32)

    bp, sp, _ = x_prompt.shape
    zeros = lambda *shape: jnp.zeros((depth, bp) + shape, F32)
    zero_states = (zeros(CONV_W - 1, BRANCH_W), zeros(BRANCH_W), zeros(HG_HEADS, HG_D, HG_D),
                   zeros(RW_HEADS, RW_HD, RW_HD), zeros(RW_COLS))
    y_prompt, st_p = _run_trunk(x_prompt, p_prompt, zero_states, layers, big, lbraw, s_valid=sp, pos0_is_zero=True)

    _, ss, _ = x_sample.shape
    ss_pad = -(-ss // SUBLANES) * SUBLANES
    xs = jnp.pad(x_sample, ((0, 0), (0, ss_pad - ss), (0, 0)))
    ps = jnp.pad(p_sample, ((0, 0), (0, 0), (0, ss_pad - ss), (0, 0)))
    y_sample, st_s = _run_trunk(xs, ps, (state_conv_a, state_lru_a, state_hgrn, state_rwkv, state_shift_c),
                                layers, big, lbraw, s_valid=ss, pos0_is_zero=False)
    return (y_prompt, y_sample[:, :ss]) + st_p + st_s
```
